```python
import math
import jax, jax.numpy as jnp
from jax import lax
import numpy as np

D_MODEL = 1024
BATCH = 4
SEQ = 4096
DEPTH = 1
DEC_BATCH = 128
DEC_SEQ = 1
PAST_LEN = 2048
PAGE_SIZE = 128

A_HEADS = 8
A_HEAD_DIM = 64
A_WIDTH = A_HEADS * A_HEAD_DIM
BRANCHES = ((128, 1), (512, 4), (2048, 16))
MAX_WINDOW = 2048
REL_BUCKETS = 32
REL_MAX_DIST = 2048
B_HEADS = 4
B_HEAD_DIM = 128
B_WIDTH = B_HEADS * B_HEAD_DIM
CONV_WIDTH = 4
CONV_DIM = 3 * B_WIDTH
CHUNK = 64
MIX_WIDTH = A_WIDTH + B_WIDTH
COL_QA = 0
COL_KA = A_WIDTH
COL_VA = 2 * A_WIDTH
COL_UB = 3 * A_WIDTH
COL_ZB = COL_UB + CONV_DIM
COL_AB = COL_ZB + B_WIDTH
COL_BB = COL_AB + B_HEADS
IN_COLS = COL_BB + B_HEADS
N_GROUPS = 4
EXPERTS_PER_GROUP = 8
N_EXPERTS = N_GROUPS * EXPERTS_PER_GROUP
TOP_K = 2
D_EXPERT = 512
MOE_BLOCK = 128
DN_ALPHA = (2 * DEPTH) ** 0.25
DN_BETA = (8 * DEPTH) ** -0.25
LN_EPS = 1e-5
RMS_EPS = 1e-6

kernel_name = 'hybrid_dilated_deltanet_hmoe_step'


def layer_norm(x, g, b):
    xf = x.astype(jnp.float32)
    mu = jnp.mean(xf, axis=-1, keepdims=True)
    var = jnp.mean(jnp.square(xf - mu), axis=-1, keepdims=True)
    y = (xf - mu) * lax.rsqrt(var + LN_EPS) * g.astype(jnp.float32) + b.astype(jnp.float32)
    return y.astype(x.dtype)


def l2_normalize(t):
    return t * lax.rsqrt(jnp.sum(t * t, axis=-1, keepdims=True) + 1e-6)


def rel_bucket(dist):
    max_exact = REL_BUCKETS // 2
    n = jnp.maximum(dist, 0)
    ratio = jnp.maximum(n, 1).astype(jnp.float32) / max_exact
    large = max_exact + (jnp.log(ratio) / math.log(REL_MAX_DIST / max_exact)
                         * (REL_BUCKETS - max_exact)).astype(jnp.int32)
    return jnp.where(n < max_exact, n, jnp.minimum(large, REL_BUCKETS - 1))


def split_proj(p):
    N, T = p.shape[0], p.shape[1]
    heads = lambda t: t.reshape(N, T, A_HEADS, A_HEAD_DIM)
    return (heads(p[..., COL_QA:COL_KA]), heads(p[..., COL_KA:COL_VA]), heads(p[..., COL_VA:COL_UB]),
            p[..., COL_UB:COL_ZB], p[..., COL_ZB:COL_AB], p[..., COL_AB:COL_BB], p[..., COL_BB:IN_COLS])


def dilated_branch_prompt(q, k, v, rel_bias, window, dil):
    B, S, H, Dh = q.shape
    band = window // dil
    L = S // dil
    nb = -(-L // band)
    Lp = nb * band

    def to_blocks(t):
        t = jnp.swapaxes(t.reshape(B, L, dil, H, Dh), 1, 2)
        t = jnp.pad(t, ((0, 0), (0, 0), (0, Lp - L), (0, 0), (0, 0)))
        return t.reshape(B, dil, nb, band, H, Dh)

    def with_prev(t):
        prev = jnp.pad(t, ((0, 0), (0, 0), (1, 0), (0, 0), (0, 0), (0, 0)))[:, :, :nb]
        return jnp.concatenate([prev, t], axis=3)

    def from_blocks(t):
        t = t.reshape(B, dil, Lp, *t.shape[4:])[:, :, :L]
        t = jnp.swapaxes(t, 1, 2)
        return t.reshape(B, S, *t.shape[3:])

    qb = to_blocks(q)
    kk = with_prev(to_blocks(k))
    vv = with_prev(to_blocks(v))
    i = jnp.arange(band)[:, None]
    j = jnp.arange(2 * band)[None, :]
    off = band + i - j
    in_band = (off >= 0) & (off <= band)
    first = (jnp.arange(nb)[:, None, None] == 0) & (j[None] < band)
    mask = in_band[None] & ~first
    bias = jnp.transpose(rel_bias[rel_bucket(jnp.maximum(off, 0) * dil)], (2, 0, 1)).astype(jnp.float32)
    s = jnp.einsum('brnqhd,brnkhd->brnhqk', qb, kk, preferred_element_type=jnp.float32) * (Dh ** -0.5) + bias
    s = jnp.where(mask[:, None], s, -jnp.inf)
    m = jnp.max(s, axis=-1, keepdims=True)
    pr = jnp.exp(s - m)
    l = jnp.sum(pr, axis=-1, keepdims=True)
    o = jnp.einsum('brnhqk,brnkhd->brnqhd', pr, vv.astype(jnp.float32)) / jnp.transpose(l, (0, 1, 2, 4, 3, 5))
    lse = jnp.transpose((m + jnp.log(l))[..., 0], (0, 1, 2, 4, 3))
    return from_blocks(o), from_blocks(lse)


def dilated_branch_cached(q, k_all, v_all, rel_bias, window, dil):
    N, T, H, Dh = q.shape
    P = k_all.shape[1] - T
    dist = jnp.arange(window // dil + 1) * dil
    idx = P + jnp.arange(T)[:, None] - dist[None, :]
    valid = idx >= 0
    idx = jnp.maximum(idx, 0)
    kg = k_all[:, idx]
    vg = v_all[:, idx]
    bias = jnp.transpose(rel_bias[rel_bucket(dist)]).astype(jnp.float32)
    s = jnp.einsum('nthd,ntjhd->nthj', q, kg, preferred_element_type=jnp.float32) * (Dh ** -0.5) + bias
    s = jnp.where(valid[None, :, None, :], s, -jnp.inf)
    m = jnp.max(s, axis=-1, keepdims=True)
    pr = jnp.exp(s - m)
    l = jnp.sum(pr, axis=-1, keepdims=True)
    o = jnp.einsum('nthj,ntjhd->nthd', pr, vg.astype(jnp.float32)) / l
    return o, (m + jnp.log(l))[..., 0]


def merge_branches(branches):
    outs = jnp.stack([b[0] for b in branches], axis=0)
    w = jax.nn.softmax(jnp.stack([b[1] for b in branches], axis=0), axis=0)
    return jnp.sum(w[..., None] * outs, axis=0)


def short_conv(u_ext, conv_w, T):
    acc = u_ext[:, 0:T] * conv_w[0]
    for i in range(1, CONV_WIDTH):
        acc = acc + u_ext[:, i:i + T] * conv_w[i]
    return jax.nn.silu(acc)


def deltanet_features(conv_out, a_raw, b_raw, a_log, dt_bias):
    N, T = conv_out.shape[0], conv_out.shape[1]
    c = conv_out.astype(jnp.float32).reshape(N, T, 3, B_HEADS, B_HEAD_DIM)
    q = l2_normalize(c[:, :, 0]) * (B_HEAD_DIM ** -0.5)
    k = l2_normalize(c[:, :, 1])
    v = c[:, :, 2]
    g = -jnp.exp(a_log.astype(jnp.float32)) * jax.nn.softplus(a_raw.astype(jnp.float32) + dt_bias.astype(jnp.float32))
    beta = jax.nn.sigmoid(b_raw.astype(jnp.float32))
    return q, k, v, g, beta


def gated_delta_chunked(q, k, v, g, beta):
    B, S, H, Dk = q.shape
    Dv = v.shape[-1]
    n = S // CHUNK

    def blocks(t):
        return jnp.moveaxis(t.reshape(B, n, CHUNK, H, *t.shape[3:]), 3, 1)

    qc, kc, vc, gr, bc = blocks(q), blocks(k), blocks(v), blocks(g), blocks(beta)
    gc = jnp.cumsum(gr, axis=-1)
    causal = jnp.tril(jnp.ones((CHUNK, CHUNK), dtype=bool))
    strict = jnp.tril(jnp.ones((CHUNK, CHUNK), dtype=bool), -1)
    decay = jnp.exp(jnp.where(causal, gc[..., :, None] - gc[..., None, :], -jnp.inf))
    kk = jnp.einsum('bhnid,bhnjd->bhnij', kc, kc)
    tmat = jnp.where(strict, bc[..., :, None] * kk * decay, 0.0) + jnp.eye(CHUNK, dtype=jnp.float32)
    rhs = jnp.concatenate([vc * bc[..., None], kc * (bc * jnp.exp(gc))[..., None]], axis=-1)
    sol = lax.linalg.triangular_solve(tmat, rhs, left_side=True, lower=True, unit_diagonal=True)
    u_val, w_key = sol[..., :Dv], sol[..., Dv:]
    qk = jnp.einsum('bhnid,bhnjd->bhnij', qc, kc) * decay
    q_dec = qc * jnp.exp(gc)[..., None]
    k_tail = kc * jnp.exp(gc[..., -1:] - gc)[..., None]
    g_last = jnp.exp(gc[..., -1])

    def step(state, xs):
        u_i, w_i, qk_i, qd_i, kt_i, gl_i = xs
        v_new = u_i - jnp.einsum('bhcd,bhde->bhce', w_i, state)
        out = jnp.einsum('bhcd,bhde->bhce', qd_i, state) + jnp.einsum('bhcs,bhse->bhce', qk_i, v_new)
        state = state * gl_i[..., None, None] + jnp.einsum('bhcd,bhce->bhde', kt_i, v_new)
        return state, out

    xs = tuple(jnp.moveaxis(t, 2, 0) for t in (u_val, w_key, qk, q_dec, k_tail, g_last))
    state, out = lax.scan(step, jnp.zeros((B, H, Dk, Dv), jnp.float32), xs)
    out = jnp.transpose(out, (1, 0, 3, 2, 4)).reshape(B, S, H, Dv)
    return out, state


def gated_delta_recurrent(state, q, k, v, g, beta):
    def step(st, xs):
        q_t, k_t, v_t, g_t, b_t = xs
        st = st * jnp.exp(g_t)[..., None, None]
        mem = jnp.einsum('nhd,nhde->nhe', k_t, st)
        st = st + jnp.einsum('nhd,nhe->nhde', k_t, (v_t - mem) * b_t[..., None])
        return st, jnp.einsum('nhd,nhde->nhe', q_t, st)

    xs = tuple(jnp.moveaxis(t, 1, 0) for t in (q, k, v, g, beta))
    state, out = lax.scan(step, state.astype(jnp.float32), xs)
    return jnp.moveaxis(out, 0, 1), state


def deltanet_out(o, z, o_norm_g):
    N, T = o.shape[0], o.shape[1]
    o = o * lax.rsqrt(jnp.mean(o * o, axis=-1, keepdims=True) + RMS_EPS) * o_norm_g.astype(jnp.float32)
    z = z.astype(jnp.float32).reshape(N, T, B_HEADS, B_HEAD_DIM)
    return (o * jax.nn.silu(z)).reshape(N, T, B_WIDTH)


def hier_moe(h, w_group, b_group, w_router, b_router, w_gate, w_up, w_down):
    T, D = h.shape
    hf = h.astype(jnp.float32)
    g_logits = hf @ w_group.astype(jnp.float32) + b_group.astype(jnp.float32)
    g_idx = jnp.argmax(g_logits, axis=-1)
    p_group = jnp.take_along_axis(jax.nn.softmax(g_logits, axis=-1), g_idx[:, None], axis=-1)[:, 0]
    e_logits = (hf @ w_router.astype(jnp.float32)).reshape(T, N_GROUPS, EXPERTS_PER_GROUP) + b_router.astype(jnp.float32)
    e_logits = jnp.take_along_axis(e_logits, g_idx[:, None, None], axis=1)[:, 0]
    top_v, top_i = lax.top_k(e_logits, TOP_K)
    gate = (p_group[:, None] * jax.nn.softmax(top_v, axis=-1)).reshape(-1)
    expert = (g_idx[:, None] * EXPERTS_PER_GROUP + top_i).reshape(-1).astype(jnp.int32)
    tok = jnp.repeat(jnp.arange(T, dtype=jnp.int32), TOP_K)
    n_assign = T * TOP_K
    order = jnp.argsort(expert)
    e_s, tok_s, gate_s = expert[order], tok[order], gate[order]
    counts = jax.ops.segment_sum(jnp.ones_like(expert), expert, num_segments=N_EXPERTS)
    padded = (counts + MOE_BLOCK - 1) // MOE_BLOCK * MOE_BLOCK
    pad_end = jnp.cumsum(padded)
    pad_start = pad_end - padded
    cnt_start = jnp.cumsum(counts) - counts
    dest = pad_start[e_s] + jnp.arange(n_assign, dtype=jnp.int32) - cnt_start[e_s]
    n_blocks = -(-(n_assign + N_EXPERTS * (MOE_BLOCK - 1)) // MOE_BLOCK)
    slots = n_blocks * MOE_BLOCK
    slot_tok = jnp.full((slots,), T, jnp.int32).at[dest].set(tok_s)
    slot_gate = jnp.zeros((slots,), jnp.float32).at[dest].set(gate_s)
    blk_start = jnp.arange(n_blocks, dtype=jnp.int32) * MOE_BLOCK
    blk_expert = jnp.minimum(jnp.sum(pad_end[None, :] <= blk_start[:, None], axis=1), N_EXPERTS - 1)
    h_pad = jnp.concatenate([h, jnp.zeros((1, D), h.dtype)], axis=0)

    def expert_block(args):
        toks, e = args
        xb = h_pad[toks]
        return (jax.nn.silu(xb @ w_gate[e]) * (xb @ w_up[e])) @ w_down[e]

    y_slots = lax.map(expert_block, (slot_tok.reshape(n_blocks, MOE_BLOCK), blk_expert))
    y_slots = y_slots.reshape(slots, D).astype(jnp.float32) * slot_gate[:, None]
    y = jnp.zeros((T + 1, D), jnp.float32).at[slot_tok].add(y_slots)[:T]
    return y.astype(h.dtype)


def finish_layer(x, mix, w_out, ln1_g, ln1_b, w_group, b_group, w_router, b_router, w_gate, w_up, w_down, ln2_g, ln2_b):
    N, T, D = x.shape
    h = layer_norm(DN_ALPHA * x + mix.astype(x.dtype) @ w_out, ln1_g, ln1_b)
    f = hier_moe(h.reshape(N * T, D), w_group, b_group, w_router, b_router, w_gate, w_up, w_down).reshape(N, T, D)
    return layer_norm(DN_ALPHA * h + f, ln2_g, ln2_b)


def setup_inputs(seed: int = 0) -> dict:
    key = jax.random.key(seed)
    ks = jax.random.split(key, 28)
    nrm = lambda k, shape, s: jax.random.normal(k, shape, jnp.float32) * s
    win_buf = min(MAX_WINDOW, PAST_LEN)
    x_prompt = nrm(ks[0], (BATCH, SEQ, D_MODEL), 1.0)
    x_sample = nrm(ks[1], (DEC_BATCH, DEC_SEQ, D_MODEL), 1.0)
    cache_a_k = nrm(ks[2], (DEPTH, DEC_BATCH, win_buf, A_HEADS, A_HEAD_DIM), 1.0)
    cache_a_v = nrm(ks[3], (DEPTH, DEC_BATCH, win_buf, A_HEADS, A_HEAD_DIM), DN_BETA)
    state_b_ssm = nrm(ks[4], (DEPTH, DEC_BATCH, B_HEADS, B_HEAD_DIM, B_HEAD_DIM), 0.1)
    state_b_conv = nrm(ks[5], (DEPTH, DEC_BATCH, CONV_WIDTH - 1, CONV_DIM), 1.0)
    v_cols = jnp.zeros((IN_COLS,), bool).at[COL_VA:COL_UB].set(True).at[COL_UB + 2 * B_WIDTH:COL_ZB].set(True)
    w_in = nrm(ks[6], (DEPTH, D_MODEL, IN_COLS), D_MODEL ** -0.5) * jnp.where(v_cols, DN_BETA, 1.0)
    rel_bias = nrm(ks[7], (REL_BUCKETS, A_HEADS), 0.5)
    conv_w = nrm(ks[8], (DEPTH, CONV_WIDTH, CONV_DIM), CONV_WIDTH ** -0.5)
    a_log = jnp.log(jax.random.uniform(ks[9], (DEPTH, B_HEADS), jnp.float32, 1.0, 16.0))
    dt = jnp.exp(jax.random.uniform(ks[10], (DEPTH, B_HEADS), jnp.float32, math.log(1e-3), math.log(1e-1)))
    dt_bias = dt + jnp.log(-jnp.expm1(-dt))
    o_norm_g = 1.0 + nrm(ks[11], (DEPTH, B_HEAD_DIM), 0.02)
    w_out = nrm(ks[12], (DEPTH, MIX_WIDTH, D_MODEL), MIX_WIDTH ** -0.5 * DN_BETA)
    ln1_g = 1.0 + nrm(ks[13], (DEPTH, D_MODEL), 0.02)
    ln1_b = nrm(ks[14], (DEPTH, D_MODEL), 0.02)
    w_group = nrm(ks[15], (DEPTH, D_MODEL, N_GROUPS), D_MODEL ** -0.5)
    b_group = nrm(ks[16], (DEPTH, N_GROUPS), 0.01)
    w_router = nrm(ks[17], (DEPTH, D_MODEL, N_EXPERTS), D_MODEL ** -0.5)
    b_router = nrm(ks[18], (DEPTH, N_GROUPS, EXPERTS_PER_GROUP), 0.01)
    w_gate = nrm(ks[19], (DEPTH, N_EXPERTS, D_MODEL, D_EXPERT), D_MODEL ** -0.5)
    w_up = nrm(ks[20], (DEPTH, N_EXPERTS, D_MODEL, D_EXPERT), D_MODEL ** -0.5 * DN_BETA)
    w_down = nrm(ks[21], (DEPTH, N_EXPERTS, D_EXPERT, D_MODEL), D_EXPERT ** -0.5 * DN_BETA)
    ln2_g = 1.0 + nrm(ks[22], (DEPTH, D_MODEL), 0.02)
    ln2_b = nrm(ks[23], (DEPTH, D_MODEL), 0.02)
    return {'x_prompt': x_prompt, 'x_sample': x_sample, 'cache_a_k': cache_a_k, 'cache_a_v': cache_a_v,
            'state_b_ssm': state_b_ssm, 'state_b_conv': state_b_conv, 'w_in': w_in, 'rel_bias': rel_bias,
            'conv_w': conv_w, 'a_log': a_log, 'dt_bias': dt_bias, 'o_norm_g': o_norm_g, 'w_out': w_out,
            'ln1_g': ln1_g, 'ln1_b': ln1_b, 'w_group': w_group, 'b_group': b_group, 'w_router': w_router,
            'b_router': b_router, 'w_gate': w_gate, 'w_up': w_up, 'w_down': w_down, 'ln2_g': ln2_g, 'ln2_b': ln2_b}


def reference(x_prompt, x_sample, cache_a_k, cache_a_v, state_b_ssm, state_b_conv, w_in, rel_bias, conv_w,
              a_log, dt_bias, o_norm_g, w_out, ln1_g, ln1_b, w_group, b_group, w_router, b_router,
              w_gate, w_up, w_down, ln2_g, ln2_b):
    hp, hs = x_prompt, x_sample
    B, S = hp.shape[0], hp.shape[1]
    N, T = hs.shape[0], hs.shape[1]
    win_p = min(MAX_WINDOW, S)
    kp_l, vp_l, ks_l, vs_l, sp_l, ss_l, cp_l, cs_l = [], [], [], [], [], [], [], []
    for l in range(DEPTH):
        lw = (w_out[l], ln1_g[l], ln1_b[l], w_group[l], b_group[l], w_router[l], b_router[l],
              w_gate[l], w_up[l], w_down[l], ln2_g[l], ln2_b[l])
        qa, ka, va, ub, zb, ab, bb = split_proj(hp @ w_in[l])
        oa = merge_branches([dilated_branch_prompt(qa, ka, va, rel_bias, w, d) for (w, d) in BRANCHES])
        u_ext = jnp.pad(ub, ((0, 0), (CONV_WIDTH - 1, 0), (0, 0)))
        q, k, v, g, beta = deltanet_features(short_conv(u_ext, conv_w[l], S), ab, bb, a_log[l], dt_bias[l])
        ob, st_p = gated_delta_chunked(q, k, v, g, beta)
        mix = jnp.concatenate([oa.reshape(B, S, A_WIDTH), deltanet_out(ob, zb, o_norm_g[l])], axis=-1)
        hp_next = finish_layer(hp, mix, *lw)
        kp_l.append(ka[:, S - win_p:])
        vp_l.append(va[:, S - win_p:])
        sp_l.append(st_p)
        cp_l.append(u_ext[:, S:])
        qa, ka, va, ub, zb, ab, bb = split_proj(hs @ w_in[l])
        k_all = jnp.concatenate([cache_a_k[l].astype(ka.dtype), ka], axis=1)
        v_all = jnp.concatenate([cache_a_v[l].astype(va.dtype), va], axis=1)
        oa = merge_branches([dilated_branch_cached(qa, k_all, v_all, rel_bias, w, d) for (w, d) in BRANCHES])
        u_ext = jnp.concatenate([state_b_conv[l].astype(ub.dtype), ub], axis=1)
        q, k, v, g, beta = deltanet_features(short_conv(u_ext, conv_w[l], T), ab, bb, a_log[l], dt_bias[l])
        ob, st_s = gated_delta_recurrent(state_b_ssm[l], q, k, v, g, beta)
        mix = jnp.concatenate([oa.reshape(N, T, A_WIDTH), deltanet_out(ob, zb, o_norm_g[l])], axis=-1)
        hs_next = finish_layer(hs, mix, *lw)
        ks_l.append(ka)
        vs_l.append(va)
        ss_l.append(st_s)
        cs_l.append(u_ext[:, T:])
        hp, hs = hp_next, hs_next
    return (hp, hs, jnp.stack(kp_l), jnp.stack(vp_l), jnp.stack(ks_l), jnp.stack(vs_l),
            jnp.stack(sp_l), jnp.stack(ss_l), jnp.stack(cp_l), jnp.stack(cs_l))
```

```python
import functools
import math

import jax
import jax.numpy as jnp
from jax import lax
from jax.experimental import pallas as pl
from jax.experimental.pallas import tpu as pltpu

F32 = jnp.float32
BF16 = jnp.bfloat16
HIGHEST = lax.Precision.HIGHEST

LANES = 128
A_HEADS = 8
A_HEAD_DIM = 64
A_WIDTH = A_HEADS * A_HEAD_DIM
BRANCHES = ((128, 1), (512, 4), (2048, 16))
BAND = 128
ATT_TILE = BAND * 16
REL_BUCKETS = 32
REL_MAX_DIST = 2048
B_HEADS = 4
B_HEAD_DIM = 128
B_WIDTH = B_HEADS * B_HEAD_DIM
CONV_WIDTH = 4
CONV_DIM = 3 * B_WIDTH
CHUNK = 64
COL_UB = 3 * A_WIDTH
COL_ZB = COL_UB + CONV_DIM
COL_AB = COL_ZB + B_WIDTH
IN_COLS = COL_AB + 2 * B_HEADS
IN_COLS_PAD = COL_AB + LANES
N_GROUPS = 4
EXPERTS_PER_GROUP = 8
N_EXPERTS = N_GROUPS * EXPERTS_PER_GROUP
TOP_K = 2
DN_ALPHA = 2.0 ** 0.25
LN_EPS = 1e-5
RMS_EPS = 1e-6
MASKED = -1e30
MOE_BLK = 256
VMEM_LIMIT = 56 * 1024 * 1024


def _bdot(a, b):
    return jnp.dot(a.astype(BF16), b.astype(BF16), preferred_element_type=F32)


def _bdot_nt(a, b):
    return lax.dot_general(a.astype(BF16), b.astype(BF16), (((1,), (1,)), ((), ())), preferred_element_type=F32)


def _bdot_tn(a, b):
    return lax.dot_general(a.astype(BF16), b.astype(BF16), (((0,), (0,)), ((), ())), preferred_element_type=F32)


def _fdot(a, b):
    return jnp.dot(a, b, precision=HIGHEST, preferred_element_type=F32)


def _sigmoid(x):
    return 1.0 / (1.0 + jnp.exp(-x))


def _silu(x):
    return x * _sigmoid(x)


def _softplus(x):
    return jnp.maximum(x, 0.0) + jnp.log(1.0 + jnp.exp(-jnp.abs(x)))


def _params(sem):
    return pltpu.CompilerParams(dimension_semantics=sem, vmem_limit_bytes=VMEM_LIMIT)


def _proj_kernel(x_ref, w_ref, qkv_ref, ub_ref, zb_ref, ab_ref):
    xb = x_ref[...].astype(BF16)
    qkv_ref[...] = jnp.dot(xb, w_ref[:, 0:COL_UB], preferred_element_type=F32)
    ub_ref[...] = jnp.dot(xb, w_ref[:, COL_UB:COL_ZB], preferred_element_type=F32)
    zb_ref[...] = jnp.dot(xb, w_ref[:, COL_ZB:COL_AB], preferred_element_type=F32)
    ab_ref[...] = jnp.dot(xb, w_ref[:, COL_AB:IN_COLS_PAD], preferred_element_type=F32)


def _proj(x2d, w_pad, tm):
    T, D = x2d.shape
    row = lambda i: (i, 0)
    return pl.pallas_call(
        _proj_kernel,
        grid=(T // tm,),
        in_specs=[pl.BlockSpec((tm, D), row), pl.BlockSpec((D, IN_COLS_PAD), lambda i: (0, 0))],
        out_specs=[pl.BlockSpec((tm, COL_UB), row), pl.BlockSpec((tm, CONV_DIM), row),
                   pl.BlockSpec((tm, B_WIDTH), row), pl.BlockSpec((tm, LANES), row)],
        out_shape=[jax.ShapeDtypeStruct((T, COL_UB), F32), jax.ShapeDtypeStruct((T, CONV_DIM), F32),
                   jax.ShapeDtypeStruct((T, B_WIDTH), F32), jax.ShapeDtypeStruct((T, LANES), F32)],
        compiler_params=_params(("parallel",)),
        name="proj",
    )(x2d, w_pad)


def _rel_bucket(dist):
    max_exact = REL_BUCKETS // 2
    n = jnp.maximum(dist, 0)
    ratio = jnp.maximum(n, 1).astype(F32) / max_exact
    large = max_exact + (jnp.log(ratio) / math.log(REL_MAX_DIST / max_exact)
                         * (REL_BUCKETS - max_exact)).astype(jnp.int32)
    return jnp.where(n < max_exact, n, jnp.minimum(large, REL_BUCKETS - 1))


def _band_bias(rel_bias):
    i = jnp.arange(BAND)[:, None]
    j = jnp.arange(2 * BAND)[None, :]
    off = BAND + i - j
    in_band = (off >= 0) & (off <= BAND)
    tabs = []
    for _, dil in BRANCHES:
        b = jnp.transpose(rel_bias[_rel_bucket(jnp.maximum(off, 0) * dil)], (2, 0, 1)).astype(F32)
        tabs.append(jnp.where(in_band[None], b, MASKED))
    return jnp.stack(tabs)


def _cache_bias(rel_bias):
    tabs = []
    for _, dil in BRANCHES:
        dist = jnp.concatenate([(BAND - jnp.arange(BAND)) * dil, jnp.zeros((8,), jnp.int32)])
        b = rel_bias[_rel_bucket(dist)].astype(F32)
        tabs.append(jnp.pad(b, ((0, 0), (0, LANES - A_HEADS))))
    return jnp.stack(tabs)


def _attn_kernel(q_ref, k_ref, v_ref, bias_ref, o_ref, acc_ref, m_ref, l_ref):
    t = pl.program_id(2)
    tile0 = t * ATT_TILE
    lane = lax.broadcasted_iota(jnp.int32, (BAND, LANES), 1)
    head0 = lane < A_HEAD_DIM
    col = lax.broadcasted_iota(jnp.int32, (BAND, 2 * BAND), 1)
    prev_cols = col < BAND

    def rows(start, dil):
        return pl.ds(pl.multiple_of(start, BAND), BAND) if dil == 1 else pl.ds(start, BAND, stride=dil)

    for br, (_, dil) in enumerate(BRANCHES):
        span = BAND * dil

        def block(i, carry, br=br, dil=dil, span=span):
            start = (i % dil) + (i // dil) * span
            cur = tile0 + start
            first = cur < span
            prev = jnp.where(first, cur, cur - span)
            q = q_ref[0, rows(start, dil), :] * (A_HEAD_DIM ** -0.5)
            kk = jnp.concatenate([k_ref[0, rows(prev, dil), :], k_ref[0, rows(cur, dil), :]], axis=0).astype(BF16)
            vv = jnp.concatenate([v_ref[0, rows(prev, dil), :], v_ref[0, rows(cur, dil), :]], axis=0).astype(BF16)
            pen = jnp.where(first, MASKED, 0.0)
            outs = []
            for hh in range(2):
                qh = jnp.where(head0 if hh == 0 else ~head0, q, 0.0).astype(BF16)
                s = lax.dot_general(qh, kk, (((1,), (1,)), ((), ())), preferred_element_type=F32)
                s = s + bias_ref[br, hh] + jnp.where(prev_cols, pen, 0.0)
                m = jnp.max(s, axis=1, keepdims=True)
                p = jnp.exp(s - m)
                l = jnp.sum(p, axis=1, keepdims=True)
                pv = jnp.dot(p.astype(BF16), vv, preferred_element_type=F32)
                outs.append((pv, m, l))
            acc_ref[br, rows(start, dil), :] = jnp.where(head0, outs[0][0], outs[1][0])
            m_ref[br, rows(start, dil), :] = jnp.where(head0, outs[0][1], outs[1][1])
            l_ref[br, rows(start, dil), :] = jnp.where(head0, outs[0][2], outs[1][2])
            return carry

        lax.fori_loop(0, ATT_TILE // BAND, block, 0)

    def merge(c, carry):
        r = pl.ds(pl.multiple_of(c * 256, 256), 256)
        m0, m1, m2 = m_ref[0, r, :], m_ref[1, r, :], m_ref[2, r, :]
        mx = jnp.maximum(jnp.maximum(m0, m1), m2)
        w0, w1, w2 = jnp.exp(m0 - mx), jnp.exp(m1 - mx), jnp.exp(m2 - mx)
        num = w0 * acc_ref[0, r, :] + w1 * acc_ref[1, r, :] + w2 * acc_ref[2, r, :]
        den = w0 * l_ref[0, r, :] + w1 * l_ref[1, r, :] + w2 * l_ref[2, r, :]
        o_ref[0, r, :] = num / den
        return carry

    lax.fori_loop(0, ATT_TILE // 256, merge, 0)


def _attn(qkv, bias, B, S):
    n_pairs = A_HEADS // 2
    qkv3 = qkv.reshape(B, S, 3 * A_WIDTH)
    return pl.pallas_call(
        _attn_kernel,
        grid=(B, n_pairs, S // ATT_TILE),
        in_specs=[pl.BlockSpec((1, ATT_TILE, LANES), lambda b, hp, t: (b, t, hp)),
                  pl.BlockSpec((1, S, LANES), lambda b, hp, t: (b, 0, n_pairs + hp)),
                  pl.BlockSpec((1, S, LANES), lambda b, hp, t: (b, 0, 2 * n_pairs + hp)),
                  pl.BlockSpec((3, 2, BAND, 2 * BAND), lambda b, hp, t: (0, hp, 0, 0))],
        out_specs=pl.BlockSpec((1, ATT_TILE, LANES), lambda b, hp, t: (b, t, hp)),
        out_shape=jax.ShapeDtypeStruct((B, S, A_WIDTH), F32),
        scratch_shapes=[pltpu.VMEM((3, ATT_TILE, LANES), F32)] * 3,
        compiler_params=_params(("parallel", "parallel", "arbitrary")),
        name="attn",
    )(qkv3, qkv3, qkv3, bias)


def _attn_dec_kernel(qkv_ref, k1_ref, k2_ref, k3_ref, v1_ref, v2_ref, v3_ref, bias_ref, o_ref, *, nb):
    lane_r = lax.broadcasted_iota(jnp.int32, (A_WIDTH, LANES), 0) // A_HEAD_DIM
    lane_c = lax.broadcasted_iota(jnp.int32, (A_WIDTH, LANES), 1)
    seg = (lane_r == lane_c).astype(F32)
    seg_r = lax.broadcasted_iota(jnp.int32, (LANES, A_WIDTH), 0)
    seg_c = lax.broadcasted_iota(jnp.int32, (LANES, A_WIDTH), 1) // A_HEAD_DIM
    seg_t = (seg_r == seg_c).astype(F32)
    head_lane = lax.broadcasted_iota(jnp.int32, (1, LANES), 1) < A_HEADS
    k_refs = (k1_ref, k2_ref, k3_ref)
    v_refs = (v1_ref, v2_ref, v3_ref)

    qkv = qkv_ref[...]
    row_q = lax.broadcasted_iota(jnp.int32, (nb, 3 * A_WIDTH), 0)
    row_o = lax.broadcasted_iota(jnp.int32, (nb, A_WIDTH), 0)

    def sample(n, out):
        row = jnp.sum(jnp.where(row_q == n, qkv, 0.0), axis=0, keepdims=True)
        q = row[:, 0:A_WIDTH] * (A_HEAD_DIM ** -0.5)
        k_new = row[:, A_WIDTH:2 * A_WIDTH]
        v_new = row[:, 2 * A_WIDTH:3 * A_WIDTH]
        s_new = _fdot(q * k_new, seg)
        parts = []
        for br in range(3):
            kc = k_refs[br][n]
            s = _fdot(kc * q, seg) + bias_ref[br, 0:BAND, :]
            s0 = s_new + bias_ref[br, BAND:BAND + 1, :]
            m = jnp.maximum(jnp.max(s, axis=0, keepdims=True), s0)
            p = jnp.exp(s - m)
            p0 = jnp.exp(s0 - m)
            l = jnp.sum(p, axis=0, keepdims=True) + p0
            pe = _fdot(p, seg_t)
            acc = jnp.sum(pe * v_refs[br][n], axis=0, keepdims=True) + _fdot(p0, seg_t) * v_new
            parts.append((acc, jnp.where(head_lane, m, 0.0), jnp.where(head_lane, l, 0.0)))
        mx = jnp.maximum(jnp.maximum(parts[0][1], parts[1][1]), parts[2][1])
        num = jnp.zeros((1, A_WIDTH), F32)
        den = jnp.zeros((1, LANES), F32)
        for acc, m, l in parts:
            w = jnp.where(head_lane, jnp.exp(m - mx), 0.0)
            num = num + _fdot(w, seg_t) * acc
            den = den + w * l
        return jnp.where(row_o == n, num / _fdot(den, seg_t), out)

    o_ref[...] = lax.fori_loop(0, nb, sample, jnp.zeros((nb, A_WIDTH), F32))


def _attn_dec(qkv_s, cache_k, cache_v, bias, nb):
    N, P = cache_k.shape[0], cache_k.shape[1]
    specs = []
    views = []
    for cache in (cache_k, cache_v):
        for window, dil in BRANCHES:
            views.append(cache.reshape(N, P // dil, dil * A_WIDTH))
            specs.append(pl.BlockSpec((nb, BAND, A_WIDTH), functools.partial(
                lambda i, blk: (i, blk, 0), blk=(P // dil) // BAND - 1)))
    return pl.pallas_call(
        functools.partial(_attn_dec_kernel, nb=nb),
        grid=(N // nb,),
        in_specs=[pl.BlockSpec((nb, 3 * A_WIDTH), lambda i: (i, 0))] + specs
                 + [pl.BlockSpec((3, BAND + 8, LANES), lambda i: (0, 0, 0))],
        out_specs=pl.BlockSpec((nb, A_WIDTH), lambda i: (i, 0)),
        out_shape=jax.ShapeDtypeStruct((N, A_WIDTH), F32),
        compiler_params=_params(("parallel",)),
        name="attn_dec",
    )(qkv_s, *views, bias)


def _delta_kernel(alog_ref, dtb_ref, uq_ref, uk_ref, uv_ref, z_ref, ab_ref, wq_ref, wk_ref, wv_ref, og_ref,
                  o_ref, st_ref, pq_ref, pk_ref, pv_ref, *, S):
    h = pl.program_id(1)
    hdr = 8
    for src, dst in ((uq_ref, pq_ref), (uk_ref, pk_ref), (uv_ref, pv_ref)):
        dst[0:hdr, :] = jnp.zeros((hdr, LANES), F32)
        dst[hdr:hdr + S, :] = src[0]

    neg_a = -jnp.exp(jnp.full((1, LANES), alog_ref[h], F32))
    dtb = jnp.full((1, LANES), dtb_ref[h], F32)
    ri = lax.broadcasted_iota(jnp.int32, (CHUNK, CHUNK), 0)
    ci = lax.broadcasted_iota(jnp.int32, (CHUNK, CHUNK), 1)
    incl = ri >= ci
    strict = ri > ci
    tril = incl.astype(F32)
    eye = (ri == ci).astype(F32)
    lane = lax.broadcasted_iota(jnp.int32, (CHUNK, LANES), 1)

    def conv(p_ref, w_ref, base):
        win = p_ref[pl.ds(base, CHUNK + hdr), :]
        acc = win[hdr - 3:hdr - 3 + CHUNK] * w_ref[0:1, :]
        for i in range(1, CONV_WIDTH):
            acc = acc + win[hdr - 3 + i:hdr - 3 + i + CHUNK] * w_ref[i:i + 1, :]
        return _silu(acc)

    def chunk(c, state):
        base = pl.multiple_of(c * CHUNK, CHUNK)
        cq, ck, v = conv(pq_ref, wq_ref, base), conv(pk_ref, wk_ref, base), conv(pv_ref, wv_ref, base)
        q = cq * lax.rsqrt(jnp.sum(cq * cq, axis=1, keepdims=True) + 1e-6) * (B_HEAD_DIM ** -0.5)
        k = ck * lax.rsqrt(jnp.sum(ck * ck, axis=1, keepdims=True) + 1e-6)
        ab = ab_ref[0, pl.ds(base, CHUNK), :]
        a_raw = jnp.sum(jnp.where(lane == h, ab, 0.0), axis=1, keepdims=True)
        b_raw = jnp.sum(jnp.where(lane == h + B_HEADS, ab, 0.0), axis=1, keepdims=True)
        g = neg_a * _softplus(a_raw + dtb)
        beta = _sigmoid(b_raw)
        gc = _fdot(tril, g)
        dmat = _fdot(tril, jnp.where(strict, g[:, 0:CHUNK], 0.0))
        decay = jnp.where(incl, jnp.exp(dmat), 0.0)
        a = jnp.where(strict, beta * _bdot_nt(k, k) * decay, 0.0)
        x = eye - a
        p = _fdot(a, a)
        for _ in range(int(math.log2(CHUNK)) - 2):
            x = x + _fdot(x, p)
            p = _fdot(p, p)
        x = x + _fdot(x, p)
        e_gc = jnp.exp(gc)
        u_val = _bdot(x, v * beta)
        w_key = _bdot(x, k * (beta * e_gc))
        qk = _bdot_nt(q, k) * decay
        gc_last = gc[CHUNK - 1:CHUNK, :]
        v_new = u_val - _bdot(w_key, state)
        out = _bdot(q * e_gc, state) + _bdot(qk, v_new)
        state = state * jnp.exp(gc_last) + _bdot_tn(k * jnp.exp(gc_last - gc), v_new)
        out = out * lax.rsqrt(jnp.mean(out * out, axis=1, keepdims=True) + RMS_EPS) * og_ref[...]
        o_ref[0, pl.ds(base, CHUNK), :] = out * _silu(z_ref[0, pl.ds(base, CHUNK), :])
        return state

    st_ref[0, 0] = lax.fori_loop(0, S // CHUNK, chunk, jnp.zeros((B_HEAD_DIM, B_HEAD_DIM), F32))


def _delta(ub, zb, ab, conv_w, a_log, dt_bias, o_norm_g, B, S):
    seq = lambda off: pl.BlockSpec((1, S, LANES), lambda b, h, *_: (b, 0, off + h))
    wsp = lambda off: pl.BlockSpec((CONV_WIDTH, LANES), lambda b, h, *_: (0, off + h))
    grid_spec = pltpu.PrefetchScalarGridSpec(
        num_scalar_prefetch=2,
        grid=(B, B_HEADS),
        in_specs=[seq(0), seq(B_HEADS), seq(2 * B_HEADS), seq(0),
                  pl.BlockSpec((1, S, LANES), lambda b, h, *_: (b, 0, 0)),
                  wsp(0), wsp(B_HEADS), wsp(2 * B_HEADS),
                  pl.BlockSpec((1, LANES), lambda b, h, *_: (0, 0))],
        out_specs=[pl.BlockSpec((1, S, LANES), lambda b, h, *_: (b, 0, h)),
                   pl.BlockSpec((1, 1, B_HEAD_DIM, B_HEAD_DIM), lambda b, h, *_: (b, h, 0, 0))],
        scratch_shapes=[pltpu.VMEM((S + 8, LANES), F32)] * 3,
    )
    ub3 = ub.reshape(B, S, CONV_DIM)
    return pl.pallas_call(
        functools.partial(_delta_kernel, S=S),
        grid_spec=grid_spec,
        out_shape=[jax.ShapeDtypeStruct((B, S, B_WIDTH), F32),
                   jax.ShapeDtypeStruct((B, B_HEADS, B_HEAD_DIM, B_HEAD_DIM), F32)],
        compiler_params=_params(("parallel", "parallel")),
        name="delta",
    )(a_log, dt_bias, ub3, ub3, ub3, zb.reshape(B, S, B_WIDTH), ab.reshape(B, S, LANES),
      conv_w, conv_w, conv_w, o_norm_g.reshape(1, LANES))


def _delta_dec_kernel(alog_ref, dtb_ref, ub_ref, cs_ref, zb_ref, ab_ref, w_ref, og_ref, st_ref,
                      o_ref, so_ref, *, nb):
    i = pl.program_id(0)
    N = ub_ref.shape[0]
    acc = ub_ref[...] * w_ref[CONV_WIDTH - 1:CONV_WIDTH, :]
    for t in range(CONV_WIDTH - 1):
        acc = acc + cs_ref[t] * w_ref[t:t + 1, :]
    c = _silu(acc)
    ab = ab_ref[...]
    lane = lax.broadcasted_iota(jnp.int32, (N, LANES), 1)
    samp = lax.broadcasted_iota(jnp.int32, (B_HEAD_DIM, N), 1)
    row_id = lax.broadcasted_iota(jnp.int32, (N, LANES), 0)
    out_row = lax.broadcasted_iota(jnp.int32, (nb, LANES), 0)
    for h in range(B_HEADS):
        cq = c[:, h * LANES:(h + 1) * LANES]
        ck = c[:, B_WIDTH + h * LANES:B_WIDTH + (h + 1) * LANES]
        v = c[:, 2 * B_WIDTH + h * LANES:2 * B_WIDTH + (h + 1) * LANES]
        q = cq * lax.rsqrt(jnp.sum(cq * cq, axis=1, keepdims=True) + 1e-6) * (B_HEAD_DIM ** -0.5)
        k = ck * lax.rsqrt(jnp.sum(ck * ck, axis=1, keepdims=True) + 1e-6)
        a_raw = jnp.sum(jnp.where(lane == h, ab, 0.0), axis=1, keepdims=True)
        b_raw = jnp.sum(jnp.where(lane == h + B_HEADS, ab, 0.0), axis=1, keepdims=True)
        neg_a = -jnp.exp(jnp.full((1, 1), alog_ref[h], F32))
        dec = jnp.exp(neg_a * _softplus(a_raw + dtb_ref[h]))
        beta = _sigmoid(b_raw)
        q_t, k_t = q.T, k.T
        z = zb_ref[:, h * LANES:(h + 1) * LANES]

        def sample(j, o_acc, h=h, q_t=q_t, k_t=k_t, v=v, dec=dec, beta=beta, z=z):
            n = i * nb + j
            pick = samp == n
            k_col = jnp.sum(jnp.where(pick, k_t, 0.0), axis=1, keepdims=True)
            q_col = jnp.sum(jnp.where(pick, q_t, 0.0), axis=1, keepdims=True)
            pick_r = row_id == n
            row = lambda t: jnp.sum(jnp.where(pick_r, t, 0.0), axis=0, keepdims=True)
            st = st_ref[j, h] * row(jnp.broadcast_to(dec, (N, LANES)))
            mem = jnp.sum(k_col * st, axis=0, keepdims=True)
            st = st + k_col * ((row(v) - mem) * row(jnp.broadcast_to(beta, (N, LANES))))
            so_ref[j, h] = st
            o = jnp.sum(q_col * st, axis=0, keepdims=True)
            o = o * lax.rsqrt(jnp.mean(o * o, axis=1, keepdims=True) + RMS_EPS) * og_ref[...]
            return jnp.where(out_row == j, o * _silu(row(z)), o_acc)

        o_ref[:, h * LANES:(h + 1) * LANES] = lax.fori_loop(0, nb, sample, jnp.zeros((nb, LANES), F32))


def _delta_dec(ub_s, conv_state, zb_s, ab_s, conv_w, a_log, dt_bias, o_norm_g, state, nb):
    N = ub_s.shape[0]
    full2 = lambda shape: pl.BlockSpec(shape, lambda i, *_: (0, 0))
    grid_spec = pltpu.PrefetchScalarGridSpec(
        num_scalar_prefetch=2,
        grid=(N // nb,),
        in_specs=[full2((N, CONV_DIM)),
                  pl.BlockSpec((CONV_WIDTH - 1, N, CONV_DIM), lambda i, *_: (0, 0, 0)),
                  full2((N, B_WIDTH)), full2((N, LANES)), full2((CONV_WIDTH, CONV_DIM)), full2((1, LANES)),
                  pl.BlockSpec((nb, B_HEADS, B_HEAD_DIM, B_HEAD_DIM), lambda i, *_: (i, 0, 0, 0))],
        out_specs=[pl.BlockSpec((nb, B_WIDTH), lambda i, *_: (i, 0)),
                   pl.BlockSpec((nb, B_HEADS, B_HEAD_DIM, B_HEAD_DIM), lambda i, *_: (i, 0, 0, 0))],
    )
    return pl.pallas_call(
        functools.partial(_delta_dec_kernel, nb=nb),
        grid_spec=grid_spec,
        out_shape=[jax.ShapeDtypeStruct((N, B_WIDTH), F32), jax.ShapeDtypeStruct(state.shape, F32)],
        compiler_params=_params(("parallel",)),
        name="delta_dec",
    )(a_log, dt_bias, ub_s, jnp.swapaxes(conv_state, 0, 1), zb_s, ab_s, conv_w, o_norm_g.reshape(1, LANES), state)


def _layer_norm(r, g, b):
    mu = jnp.mean(r, axis=1, keepdims=True)
    d = r - mu
    var = jnp.mean(d * d, axis=1, keepdims=True)
    return d * lax.rsqrt(var + LN_EPS) * g + b


def _mix_ln_kernel(oa_ref, ob_ref, x_ref, wo_ref, g_ref, b_ref, wr_ref, br_ref, h_ref, route_ref):
    y = (jnp.dot(oa_ref[...].astype(BF16), wo_ref[0:A_WIDTH, :], preferred_element_type=F32)
         + jnp.dot(ob_ref[...].astype(BF16), wo_ref[A_WIDTH:A_WIDTH + B_WIDTH, :], preferred_element_type=F32))
    hcur = _layer_norm(DN_ALPHA * x_ref[...] + y, g_ref[...], b_ref[...])
    h_ref[...] = hcur
    logits = _fdot(hcur, wr_ref[...]) + br_ref[...]
    lane = lax.broadcasted_iota(jnp.int32, logits.shape, 1)
    lane_f = lane.astype(F32)
    ninf = -jnp.inf
    big = 1e9
    gl = jnp.where(lane < N_GROUPS, logits, ninf)
    gmax = jnp.max(gl, axis=1, keepdims=True)
    g_idx = jnp.min(jnp.where(gl == gmax, lane_f, big), axis=1, keepdims=True)
    p_group = 1.0 / jnp.sum(jnp.exp(gl - gmax), axis=1, keepdims=True)
    grp_of_lane = ((lane - N_GROUPS) >> 3).astype(F32)
    sel = (lane >= N_GROUPS) & (lane < N_GROUPS + N_EXPERTS) & (grp_of_lane == g_idx)
    el = jnp.where(sel, logits, ninf)
    v1 = jnp.max(el, axis=1, keepdims=True)
    i1 = jnp.min(jnp.where(el == v1, lane_f, big), axis=1, keepdims=True)
    el2 = jnp.where(lane_f == i1, ninf, el)
    v2 = jnp.max(el2, axis=1, keepdims=True)
    i2 = jnp.min(jnp.where(el2 == v2, lane_f, big), axis=1, keepdims=True)
    t = jnp.exp(v2 - v1)
    gate1 = p_group / (1.0 + t)
    gate2 = p_group * t / (1.0 + t)
    route_ref[...] = jnp.where(lane == 0, gate1, jnp.where(lane == 1, gate2, jnp.where(
        lane == 2, i1 - N_GROUPS, jnp.where(lane == 3, i2 - N_GROUPS, 0.0))))


def _mix_ln(oa, ob, x2d, wo_b, g, b, wr, br, tm):
    T, D = x2d.shape
    row = lambda i: (i, 0)
    fix = lambda i: (0, 0)
    return pl.pallas_call(
        _mix_ln_kernel,
        grid=(T // tm,),
        in_specs=[pl.BlockSpec((tm, A_WIDTH), row), pl.BlockSpec((tm, B_WIDTH), row), pl.BlockSpec((tm, D), row),
                  pl.BlockSpec((A_WIDTH + B_WIDTH, D), fix), pl.BlockSpec((1, D), fix), pl.BlockSpec((1, D), fix),
                  pl.BlockSpec((D, LANES), fix), pl.BlockSpec((1, LANES), fix)],
        out_specs=[pl.BlockSpec((tm, D), row), pl.BlockSpec((tm, LANES), row)],
        out_shape=[jax.ShapeDtypeStruct((T, D), F32), jax.ShapeDtypeStruct((T, LANES), F32)],
        compiler_params=_params(("parallel",)),
        name="mix_ln",
    )(oa, ob, x2d, wo_b, g, b, wr, br)


def _moe_kernel(blk_e_ref, blk_n_ref, tok_ref, dst_ref, h_hbm, gate_ref, wg_ref, wu_ref, wd_ref, out_hbm,
                xbuf, ybuf, gsem, ssem):
    i = pl.program_id(0)
    n_valid = blk_n_ref[i]
    base = i * MOE_BLK

    def row_in(j):
        return pltpu.make_async_copy(h_hbm.at[pl.ds(tok_ref[base + j], 1)], xbuf.at[pl.ds(j, 1)], gsem)

    def row_out(j):
        return pltpu.make_async_copy(ybuf.at[pl.ds(j, 1)], out_hbm.at[pl.ds(dst_ref[base + j], 1)], ssem)

    @pl.when(n_valid > 0)
    def _():
        def start_in(j, c):
            row_in(j).start()
            return c

        def wait_in(j, c):
            row_in(j).wait()
            return c

        lax.fori_loop(0, MOE_BLK, start_in, 0)
        lax.fori_loop(0, MOE_BLK, wait_in, 0)
        x = xbuf[...].astype(BF16)
        a = jnp.dot(x, wg_ref[0].astype(BF16), preferred_element_type=F32)
        u = jnp.dot(x, wu_ref[0].astype(BF16), preferred_element_type=F32)
        y = jnp.dot((_silu(a) * u).astype(BF16), wd_ref[0].astype(BF16), preferred_element_type=F32)
        ybuf[...] = y * gate_ref[...]

        def start_out(j, c):
            row_out(j).start()
            return c

        def wait_out(j, c):
            row_out(j).wait()
            return c

        lax.fori_loop(0, n_valid, start_out, 0)
        lax.fori_loop(0, n_valid, wait_out, 0)


def _moe(h_all, route, w_gate, w_up, w_down):
    T, D = h_all.shape
    De = w_gate.shape[-1]
    n_assign = T * TOP_K
    n_blocks = -(-(n_assign + N_EXPERTS * (MOE_BLK - 1)) // MOE_BLK)
    slots = n_blocks * MOE_BLK
    expert = route[:, 2:4].astype(jnp.int32).reshape(-1)
    gate = route[:, 0:2].reshape(-1)
    onehot = (expert[:, None] == jnp.arange(N_EXPERTS, dtype=jnp.int32)[None, :]).astype(jnp.int32)
    rank = jnp.sum((jnp.cumsum(onehot, axis=0) - onehot) * onehot, axis=1)
    counts = jnp.sum(onehot, axis=0)
    padded = (counts + MOE_BLK - 1) // MOE_BLK * MOE_BLK
    pad_end = jnp.cumsum(padded)
    pad_start = pad_end - padded
    dest = pad_start[expert] + rank
    assign = jnp.arange(n_assign, dtype=jnp.int32)
    slot_tok = jnp.zeros((slots,), jnp.int32).at[dest].set(assign // TOP_K)
    slot_dst = jnp.zeros((slots,), jnp.int32).at[dest].set(assign)
    slot_gate = jnp.zeros((slots,), F32).at[dest].set(gate).reshape(slots, 1)
    blk_start = jnp.arange(n_blocks, dtype=jnp.int32) * MOE_BLK
    blk_e = jnp.minimum(jnp.sum(pad_end[None, :] <= blk_start[:, None], axis=1), N_EXPERTS - 1).astype(jnp.int32)
    blk_n = jnp.clip(pad_start[blk_e] + counts[blk_e] - blk_start, 0, MOE_BLK).astype(jnp.int32)
    last_e = blk_e[jnp.maximum(pad_end[-1] // MOE_BLK - 1, 0)]
    blk_e = jnp.where(blk_start < pad_end[-1], blk_e, last_e)

    wspec = lambda shape: pl.BlockSpec((1,) + shape, lambda i, be, *_: (be[i], 0, 0))
    grid_spec = pltpu.PrefetchScalarGridSpec(
        num_scalar_prefetch=4,
        grid=(n_blocks,),
        in_specs=[pl.BlockSpec(memory_space=pl.ANY),
                  pl.BlockSpec((MOE_BLK, 1), lambda i, *_: (i, 0)),
                  wspec((D, De)), wspec((D, De)), wspec((De, D))],
        out_specs=pl.BlockSpec(memory_space=pl.ANY),
        scratch_shapes=[pltpu.VMEM((MOE_BLK, D), F32), pltpu.VMEM((MOE_BLK, D), F32),
                        pltpu.SemaphoreType.DMA, pltpu.SemaphoreType.DMA],
    )
    out = pl.pallas_call(
        _moe_kernel,
        grid_spec=grid_spec,
        out_shape=jax.ShapeDtypeStruct((n_assign, D), F32),
        compiler_params=_params(("arbitrary",)),
        name="moe",
    )(blk_e, blk_n, slot_tok, slot_dst, h_all, slot_gate, w_gate, w_up, w_down)
    return out.reshape(T, TOP_K * D)


def _final_ln_kernel(h_ref, y_ref, g_ref, b_ref, o_ref):
    D = h_ref.shape[1]
    f = y_ref[:, 0:D] + y_ref[:, D:2 * D]
    o_ref[...] = _layer_norm(DN_ALPHA * h_ref[...] + f, g_ref[...], b_ref[...])


def _final_ln(h_all, y2, g, b, row0, rows, tm):
    D = h_all.shape[1]
    off = row0 // tm
    row = lambda i: (off + i, 0)
    fix = lambda i: (0, 0)
    return pl.pallas_call(
        _final_ln_kernel,
        grid=(rows // tm,),
        in_specs=[pl.BlockSpec((tm, D), row), pl.BlockSpec((tm, TOP_K * D), row),
                  pl.BlockSpec((1, D), fix), pl.BlockSpec((1, D), fix)],
        out_specs=pl.BlockSpec((tm, D), lambda i: (i, 0)),
        out_shape=jax.ShapeDtypeStruct((rows, D), F32),
        compiler_params=_params(("parallel",)),
        name="final_ln",
    )(h_all, y2, g, b)


def kernel(x_prompt, x_sample, cache_a_k, cache_a_v, state_b_ssm, state_b_conv, w_in, rel_bias, conv_w, a_log, dt_bias, o_norm_g, w_out, ln1_g, ln1_b, w_group, b_group, w_router, b_router, w_gate, w_up, w_down, ln2_g, ln2_b):
    B, S, D = x_prompt.shape
    N, T = x_sample.shape[0], x_sample.shape[1]
    depth = w_in.shape[0]
    assert depth == 1 and T == 1 and S % ATT_TILE == 0 and N % 8 == 0
    assert cache_a_k.shape[2] == BRANCHES[-1][0]
    l = 0
    win_p = min(BRANCHES[-1][0], S)

    w_pad = jnp.pad(w_in[l], ((0, 0), (0, IN_COLS_PAD - IN_COLS))).astype(BF16)
    wo_b = w_out[l].astype(BF16)
    wr = jnp.pad(jnp.concatenate([w_group[l], w_router[l]], axis=1), ((0, 0), (0, LANES - N_GROUPS - N_EXPERTS)))
    br = jnp.pad(jnp.concatenate([b_group[l], b_router[l].reshape(-1)]), (0, LANES - N_GROUPS - N_EXPERTS))[None, :]
    g1, b1 = ln1_g[l][None, :], ln1_b[l][None, :]
    g2, b2 = ln2_g[l][None, :], ln2_b[l][None, :]

    xp = x_prompt.reshape(B * S, D)
    qkv_p, ub_p, zb_p, ab_p = _proj(xp, w_pad, 256)
    oa_p = _attn(qkv_p, _band_bias(rel_bias), B, S)
    ob_p, st_p = _delta(ub_p, zb_p, ab_p, conv_w[l], a_log[l], dt_bias[l], o_norm_g[l], B, S)
    h_p, route_p = _mix_ln(oa_p.reshape(B * S, A_WIDTH), ob_p.reshape(B * S, B_WIDTH), xp, wo_b, g1, b1, wr, br, 256)

    xs = x_sample.reshape(N, D)
    qkv_s, ub_s, zb_s, ab_s = _proj(xs, w_pad, N)
    oa_s = _attn_dec(qkv_s, cache_a_k[l], cache_a_v[l], _cache_bias(rel_bias), 8)
    ob_s, st_s = _delta_dec(ub_s, state_b_conv[l], zb_s, ab_s, conv_w[l], a_log[l], dt_bias[l], o_norm_g[l],
                            state_b_ssm[l], 8)
    h_s, route_s = _mix_ln(oa_s, ob_s, xs, wo_b, g1, b1, wr, br, N)

    h_all = jnp.concatenate([h_p, h_s], axis=0)
    y2 = _moe(h_all, jnp.concatenate([route_p, route_s], axis=0), w_gate[l], w_up[l], w_down[l])
    y_p = _final_ln(h_all, y2, g2, b2, 0, B * S, N)
    y_s = _final_ln(h_all, y2, g2, b2, B * S, N, N)

    qkv_p4 = qkv_p.reshape(B, S, 3, A_HEADS, A_HEAD_DIM)
    qkv_s4 = qkv_s.reshape(N, T, 3, A_HEADS, A_HEAD_DIM)
    conv_p = ub_p.reshape(B, S, CONV_DIM)[:, S - (CONV_WIDTH - 1):]
    conv_s = jnp.concatenate([state_b_conv[l], ub_s[:, None, :]], axis=1)[:, T:]
    return (y_p.reshape(B, S, D), y_s.reshape(N, T, D),
            qkv_p4[:, S - win_p:, 1][None], qkv_p4[:, S - win_p:, 2][None],
            qkv_s4[:, :, 1][None], qkv_s4[:, :, 2][None],
            st_p[None], st_s[None], conv_p[None], conv_s[None])
```

```python
import functools
import math

import jax
import jax.numpy as jnp
from jax import lax
from jax.experimental import pallas as pl
from jax.experimental.pallas import tpu as pltpu

F32 = jnp.float32
BF16 = jnp.bfloat16
HIGHEST = lax.Precision.HIGHEST

LANES = 128
A_HEADS = 8
A_HEAD_DIM = 64
A_WIDTH = A_HEADS * A_HEAD_DIM
BRANCHES = ((128, 1), (512, 4), (2048, 16))
BAND = 128
ATT_TILE = BAND * 16
REL_BUCKETS = 32
REL_MAX_DIST = 2048
B_HEADS = 4
B_HEAD_DIM = 128
B_WIDTH = B_HEADS * B_HEAD_DIM
CONV_WIDTH = 4
CONV_DIM = 3 * B_WIDTH
CHUNK = 64
COL_UB = 3 * A_WIDTH
COL_ZB = COL_UB + CONV_DIM
COL_AB = COL_ZB + B_WIDTH
IN_COLS = COL_AB + 2 * B_HEADS
IN_COLS_PAD = COL_AB + LANES
N_GROUPS = 4
EXPERTS_PER_GROUP = 8
N_EXPERTS = N_GROUPS * EXPERTS_PER_GROUP
TOP_K = 2
DN_ALPHA = 2.0 ** 0.25
LN_EPS = 1e-5
RMS_EPS = 1e-6
MASKED = -1e30
MOE_BLK = 256
ROUTE_TILE = 128
VMEM_LIMIT = 56 * 1024 * 1024


def _bdot(a, b):
    return jnp.dot(a.astype(BF16), b.astype(BF16), preferred_element_type=F32)


def _bdot_nt(a, b):
    return lax.dot_general(a.astype(BF16), b.astype(BF16), (((1,), (1,)), ((), ())), preferred_element_type=F32)


def _bdot_tn(a, b):
    return lax.dot_general(a.astype(BF16), b.astype(BF16), (((0,), (0,)), ((), ())), preferred_element_type=F32)


def _fdot(a, b):
    return jnp.dot(a, b, precision=HIGHEST, preferred_element_type=F32)


def _sigmoid(x):
    return 1.0 / (1.0 + jnp.exp(-x))


def _silu(x):
    return x * _sigmoid(x)


def _softplus(x):
    return jnp.maximum(x, 0.0) + jnp.log(1.0 + jnp.exp(-jnp.abs(x)))


def _params(sem):
    return pltpu.CompilerParams(dimension_semantics=sem, vmem_limit_bytes=VMEM_LIMIT)


def _proj_kernel(x_ref, w_ref, qkv_ref, ub_ref, zb_ref, ab_ref):
    xb = x_ref[...].astype(BF16)
    qkv_ref[...] = jnp.dot(xb, w_ref[:, 0:COL_UB], preferred_element_type=F32)
    ub_ref[...] = jnp.dot(xb, w_ref[:, COL_UB:COL_ZB], preferred_element_type=F32)
    zb_ref[...] = jnp.dot(xb, w_ref[:, COL_ZB:COL_AB], preferred_element_type=F32)
    ab_ref[...] = jnp.dot(xb, w_ref[:, COL_AB:IN_COLS_PAD], preferred_element_type=F32)


def _proj(x2d, w_pad, tm):
    T, D = x2d.shape
    row = lambda i: (i, 0)
    return pl.pallas_call(
        _proj_kernel,
        grid=(T // tm,),
        in_specs=[pl.BlockSpec((tm, D), row), pl.BlockSpec((D, IN_COLS_PAD), lambda i: (0, 0))],
        out_specs=[pl.BlockSpec((tm, COL_UB), row), pl.BlockSpec((tm, CONV_DIM), row),
                   pl.BlockSpec((tm, B_WIDTH), row), pl.BlockSpec((tm, LANES), row)],
        out_shape=[jax.ShapeDtypeStruct((T, COL_UB), F32), jax.ShapeDtypeStruct((T, CONV_DIM), F32),
                   jax.ShapeDtypeStruct((T, B_WIDTH), F32), jax.ShapeDtypeStruct((T, LANES), F32)],
        compiler_params=_params(("parallel",)),
        name="proj",
    )(x2d, w_pad)


def _rel_bucket(dist):
    max_exact = REL_BUCKETS // 2
    n = jnp.maximum(dist, 0)
    ratio = jnp.maximum(n, 1).astype(F32) / max_exact
    large = max_exact + (jnp.log(ratio) / math.log(REL_MAX_DIST / max_exact)
                         * (REL_BUCKETS - max_exact)).astype(jnp.int32)
    return jnp.where(n < max_exact, n, jnp.minimum(large, REL_BUCKETS - 1))


def _band_bias(rel_bias):
    i = jnp.arange(BAND)[:, None]
    j = jnp.arange(2 * BAND)[None, :]
    off = BAND + i - j
    in_band = (off >= 0) & (off <= BAND)
    tabs = []
    for _, dil in BRANCHES:
        b = jnp.transpose(rel_bias[_rel_bucket(jnp.maximum(off, 0) * dil)], (2, 0, 1)).astype(F32)
        tabs.append(jnp.where(in_band[None], b, MASKED))
    return jnp.stack(tabs)


def _cache_bias(rel_bias):
    tabs = []
    for _, dil in BRANCHES:
        dist = (BAND - jnp.arange(BAND + 1)) * dil
        tabs.append(rel_bias[_rel_bucket(dist)].astype(F32)[..., None])
    return jnp.stack(tabs)


def _attn_kernel(q_ref, k_ref, v_ref, bias_ref, o_ref, acc_ref, m_ref, l_ref):
    t = pl.program_id(2)
    tile0 = t * ATT_TILE
    lane = lax.broadcasted_iota(jnp.int32, (BAND, LANES), 1)
    head0 = lane < A_HEAD_DIM
    col = lax.broadcasted_iota(jnp.int32, (BAND, 2 * BAND), 1)
    prev_cols = col < BAND

    def rows(start, dil):
        return pl.ds(pl.multiple_of(start, BAND), BAND) if dil == 1 else pl.ds(start, BAND, stride=dil)

    for br, (_, dil) in enumerate(BRANCHES):
        span = BAND * dil

        def block(i, carry, br=br, dil=dil, span=span):
            start = (i % dil) + (i // dil) * span
            cur = tile0 + start
            first = cur < span
            prev = jnp.where(first, cur, cur - span)
            q = q_ref[0, rows(start, dil), :] * (A_HEAD_DIM ** -0.5)
            kk = jnp.concatenate([k_ref[0, rows(prev, dil), :], k_ref[0, rows(cur, dil), :]], axis=0).astype(BF16)
            vv = jnp.concatenate([v_ref[0, rows(prev, dil), :], v_ref[0, rows(cur, dil), :]], axis=0).astype(BF16)
            pen = jnp.where(first, MASKED, 0.0)
            outs = []
            for hh in range(2):
                qh = jnp.where(head0 if hh == 0 else ~head0, q, 0.0).astype(BF16)
                s = lax.dot_general(qh, kk, (((1,), (1,)), ((), ())), preferred_element_type=F32)
                s = s + bias_ref[br, hh] + jnp.where(prev_cols, pen, 0.0)
                m = jnp.max(s, axis=1, keepdims=True)
                p = jnp.exp(s - m)
                l = jnp.sum(p, axis=1, keepdims=True)
                pv = jnp.dot(p.astype(BF16), vv, preferred_element_type=F32)
                outs.append((pv, m, l))
            acc_ref[br, rows(start, dil), :] = jnp.where(head0, outs[0][0], outs[1][0])
            m_ref[br, rows(start, dil), :] = jnp.where(head0, outs[0][1], outs[1][1])
            l_ref[br, rows(start, dil), :] = jnp.where(head0, outs[0][2], outs[1][2])
            return carry

        lax.fori_loop(0, ATT_TILE // BAND, block, 0)

    def merge(c, carry):
        r = pl.ds(pl.multiple_of(c * 256, 256), 256)
        m0, m1, m2 = m_ref[0, r, :], m_ref[1, r, :], m_ref[2, r, :]
        mx = jnp.maximum(jnp.maximum(m0, m1), m2)
        w0, w1, w2 = jnp.exp(m0 - mx), jnp.exp(m1 - mx), jnp.exp(m2 - mx)
        num = w0 * acc_ref[0, r, :] + w1 * acc_ref[1, r, :] + w2 * acc_ref[2, r, :]
        den = w0 * l_ref[0, r, :] + w1 * l_ref[1, r, :] + w2 * l_ref[2, r, :]
        o_ref[0, r, :] = num / den
        return carry

    lax.fori_loop(0, ATT_TILE // 256, merge, 0)


def _attn(qkv, bias, B, S):
    n_pairs = A_HEADS // 2
    qkv3 = qkv.reshape(B, S, 3 * A_WIDTH)
    return pl.pallas_call(
        _attn_kernel,
        grid=(B, n_pairs, S // ATT_TILE),
        in_specs=[pl.BlockSpec((1, ATT_TILE, LANES), lambda b, hp, t: (b, t, hp)),
                  pl.BlockSpec((1, S, LANES), lambda b, hp, t: (b, 0, n_pairs + hp)),
                  pl.BlockSpec((1, S, LANES), lambda b, hp, t: (b, 0, 2 * n_pairs + hp)),
                  pl.BlockSpec((3, 2, BAND, 2 * BAND), lambda b, hp, t: (0, hp, 0, 0))],
        out_specs=pl.BlockSpec((1, ATT_TILE, LANES), lambda b, hp, t: (b, t, hp)),
        out_shape=jax.ShapeDtypeStruct((B, S, A_WIDTH), F32),
        scratch_shapes=[pltpu.VMEM((3, ATT_TILE, LANES), F32)] * 3,
        compiler_params=_params(("parallel", "parallel", "arbitrary")),
        name="attn",
    )(qkv3, qkv3, qkv3, bias)


def _attn_dec_kernel(qkv_ref, k1_ref, k2_ref, k3_ref, v1_ref, v2_ref, v3_ref, bias_ref, o_ref, *, nb):
    k_refs = (k1_ref, k2_ref, k3_ref)
    v_refs = (v1_ref, v2_ref, v3_ref)

    def sample(n, carry):
        q = qkv_ref[n, 0] * (A_HEAD_DIM ** -0.5)
        k_new = qkv_ref[n, 1]
        v_new = qkv_ref[n, 2]
        s_new = jnp.sum(q * k_new, axis=-1, keepdims=True)
        parts = []
        for br in range(3):
            s = jnp.sum(k_refs[br][n] * q[None], axis=-1, keepdims=True) + bias_ref[br, 0:BAND]
            s0 = s_new + bias_ref[br, BAND]
            m = jnp.maximum(jnp.max(s, axis=0), s0)
            p = jnp.exp(s - m[None])
            p0 = jnp.exp(s0 - m)
            l = jnp.sum(p, axis=0) + p0
            acc = jnp.sum(p * v_refs[br][n], axis=0) + p0 * v_new
            parts.append((acc, m, l))
        mx = jnp.maximum(jnp.maximum(parts[0][1], parts[1][1]), parts[2][1])
        num = jnp.zeros((A_HEADS, A_HEAD_DIM), F32)
        den = jnp.zeros((A_HEADS, 1), F32)
        for acc, m, l in parts:
            w = jnp.exp(m - mx)
            num = num + w * acc
            den = den + w * l
        o_ref[n] = num / den
        return carry

    lax.fori_loop(0, nb, sample, 0)


def _attn_dec(qkv_s, cache_k, cache_v, bias, nb):
    N, P = cache_k.shape[0], cache_k.shape[1]
    tile = (A_HEADS, A_HEAD_DIM)
    specs = []
    views = []
    for cache in (cache_k, cache_v):
        for window, dil in BRANCHES:
            views.append(cache.reshape((N, P // dil, dil) + tile))
            specs.append(pl.BlockSpec((nb, BAND, None) + tile, functools.partial(
                lambda i, blk: (i, blk, 0, 0, 0), blk=(P // dil) // BAND - 1)))
    return pl.pallas_call(
        functools.partial(_attn_dec_kernel, nb=nb),
        grid=(N // nb,),
        in_specs=[pl.BlockSpec((nb, 3) + tile, lambda i: (i, 0, 0, 0))] + specs
                 + [pl.BlockSpec((3, BAND + 1, A_HEADS, 1), lambda i: (0, 0, 0, 0))],
        out_specs=pl.BlockSpec((nb,) + tile, lambda i: (i, 0, 0)),
        out_shape=jax.ShapeDtypeStruct((N,) + tile, F32),
        compiler_params=_params(("parallel",)),
        name="attn_dec",
    )(qkv_s, *views, bias)


def _delta_kernel(alog_ref, dtb_ref, uq_ref, uk_ref, uv_ref, z_ref, ab_ref, wq_ref, wk_ref, wv_ref, og_ref,
                  o_ref, st_ref, pq_ref, pk_ref, pv_ref, *, S):
    h = pl.program_id(1)
    hdr = 8
    for src, dst in ((uq_ref, pq_ref), (uk_ref, pk_ref), (uv_ref, pv_ref)):
        dst[0:hdr, :] = jnp.zeros((hdr, LANES), F32)
        dst[hdr:hdr + S, :] = src[0]

    neg_a = -jnp.exp(jnp.full((1, LANES), alog_ref[h], F32))
    dtb = jnp.full((1, LANES), dtb_ref[h], F32)
    ri = lax.broadcasted_iota(jnp.int32, (CHUNK, CHUNK), 0)
    ci = lax.broadcasted_iota(jnp.int32, (CHUNK, CHUNK), 1)
    incl = ri >= ci
    strict = ri > ci
    tril = incl.astype(F32)
    eye = (ri == ci).astype(F32)
    lane = lax.broadcasted_iota(jnp.int32, (CHUNK, LANES), 1)

    def conv(p_ref, w_ref, base):
        win = p_ref[pl.ds(base, CHUNK + hdr), :]
        acc = win[hdr - 3:hdr - 3 + CHUNK] * w_ref[0:1, :]
        for i in range(1, CONV_WIDTH):
            acc = acc + win[hdr - 3 + i:hdr - 3 + i + CHUNK] * w_ref[i:i + 1, :]
        return _silu(acc)

    def chunk(c, state):
        base = pl.multiple_of(c * CHUNK, CHUNK)
        cq, ck, v = conv(pq_ref, wq_ref, base), conv(pk_ref, wk_ref, base), conv(pv_ref, wv_ref, base)
        q = cq * lax.rsqrt(jnp.sum(cq * cq, axis=1, keepdims=True) + 1e-6) * (B_HEAD_DIM ** -0.5)
        k = ck * lax.rsqrt(jnp.sum(ck * ck, axis=1, keepdims=True) + 1e-6)
        ab = ab_ref[0, pl.ds(base, CHUNK), :]
        a_raw = jnp.sum(jnp.where(lane == h, ab, 0.0), axis=1, keepdims=True)
        b_raw = jnp.sum(jnp.where(lane == h + B_HEADS, ab, 0.0), axis=1, keepdims=True)
        g = neg_a * _softplus(a_raw + dtb)
        beta = _sigmoid(b_raw)
        gc = _fdot(tril, g)
        dmat = _fdot(tril, jnp.where(strict, g[:, 0:CHUNK], 0.0))
        decay = jnp.where(incl, jnp.exp(dmat), 0.0)
        a = jnp.where(strict, beta * _bdot_nt(k, k) * decay, 0.0)
        x = eye - a
        p = _fdot(a, a)
        for _ in range(int(math.log2(CHUNK)) - 2):
            x = x + _fdot(x, p)
            p = _fdot(p, p)
        x = x + _fdot(x, p)
        e_gc = jnp.exp(gc)
        u_val = _bdot(x, v * beta)
        w_key = _bdot(x, k * (beta * e_gc))
        qk = _bdot_nt(q, k) * decay
        gc_last = gc[CHUNK - 1:CHUNK, :]
        v_new = u_val - _bdot(w_key, state)
        out = _bdot(q * e_gc, state) + _bdot(qk, v_new)
        state = state * jnp.exp(gc_last) + _bdot_tn(k * jnp.exp(gc_last - gc), v_new)
        out = out * lax.rsqrt(jnp.mean(out * out, axis=1, keepdims=True) + RMS_EPS) * og_ref[...]
        o_ref[0, pl.ds(base, CHUNK), :] = out * _silu(z_ref[0, pl.ds(base, CHUNK), :])
        return state

    st_ref[0, 0] = lax.fori_loop(0, S // CHUNK, chunk, jnp.zeros((B_HEAD_DIM, B_HEAD_DIM), F32))


def _delta(ub, zb, ab, conv_w, a_log, dt_bias, o_norm_g, B, S):
    seq = lambda off: pl.BlockSpec((1, S, LANES), lambda b, h, *_: (b, 0, off + h))
    wsp = lambda off: pl.BlockSpec((CONV_WIDTH, LANES), lambda b, h, *_: (0, off + h))
    grid_spec = pltpu.PrefetchScalarGridSpec(
        num_scalar_prefetch=2,
        grid=(B, B_HEADS),
        in_specs=[seq(0), seq(B_HEADS), seq(2 * B_HEADS), seq(0),
                  pl.BlockSpec((1, S, LANES), lambda b, h, *_: (b, 0, 0)),
                  wsp(0), wsp(B_HEADS), wsp(2 * B_HEADS),
                  pl.BlockSpec((1, LANES), lambda b, h, *_: (0, 0))],
        out_specs=[pl.BlockSpec((1, S, LANES), lambda b, h, *_: (b, 0, h)),
                   pl.BlockSpec((1, 1, B_HEAD_DIM, B_HEAD_DIM), lambda b, h, *_: (b, h, 0, 0))],
        scratch_shapes=[pltpu.VMEM((S + 8, LANES), F32)] * 3,
    )
    ub3 = ub.reshape(B, S, CONV_DIM)
    return pl.pallas_call(
        functools.partial(_delta_kernel, S=S),
        grid_spec=grid_spec,
        out_shape=[jax.ShapeDtypeStruct((B, S, B_WIDTH), F32),
                   jax.ShapeDtypeStruct((B, B_HEADS, B_HEAD_DIM, B_HEAD_DIM), F32)],
        compiler_params=_params(("parallel", "parallel")),
        name="delta",
    )(a_log, dt_bias, ub3, ub3, ub3, zb.reshape(B, S, B_WIDTH), ab.reshape(B, S, LANES),
      conv_w, conv_w, conv_w, o_norm_g.reshape(1, LANES))


def _delta_dec_kernel(alog_ref, dtb_ref, ub_ref, cs_ref, zb_ref, ab_ref, w_ref, og_ref, st_ref,
                      o_ref, so_ref, *, nb):
    i = pl.program_id(0)
    N = ub_ref.shape[0]
    acc = ub_ref[...] * w_ref[CONV_WIDTH - 1:CONV_WIDTH, :]
    for t in range(CONV_WIDTH - 1):
        acc = acc + cs_ref[t] * w_ref[t:t + 1, :]
    c = _silu(acc)
    ab = ab_ref[...]
    lane = lax.broadcasted_iota(jnp.int32, (N, LANES), 1)
    samp = lax.broadcasted_iota(jnp.int32, (B_HEAD_DIM, N), 1)
    row_id = lax.broadcasted_iota(jnp.int32, (N, LANES), 0)
    out_row = lax.broadcasted_iota(jnp.int32, (nb, LANES), 0)
    for h in range(B_HEADS):
        cq = c[:, h * LANES:(h + 1) * LANES]
        ck = c[:, B_WIDTH + h * LANES:B_WIDTH + (h + 1) * LANES]
        v = c[:, 2 * B_WIDTH + h * LANES:2 * B_WIDTH + (h + 1) * LANES]
        q = cq * lax.rsqrt(jnp.sum(cq * cq, axis=1, keepdims=True) + 1e-6) * (B_HEAD_DIM ** -0.5)
        k = ck * lax.rsqrt(jnp.sum(ck * ck, axis=1, keepdims=True) + 1e-6)
        a_raw = jnp.sum(jnp.where(lane == h, ab, 0.0), axis=1, keepdims=True)
        b_raw = jnp.sum(jnp.where(lane == h + B_HEADS, ab, 0.0), axis=1, keepdims=True)
        neg_a = -jnp.exp(jnp.full((1, 1), alog_ref[h], F32))
        dec = jnp.exp(neg_a * _softplus(a_raw + dtb_ref[h]))
        beta = _sigmoid(b_raw)
        q_t, k_t = q.T, k.T
        z = zb_ref[:, h * LANES:(h + 1) * LANES]

        def sample(j, o_acc, h=h, q_t=q_t, k_t=k_t, v=v, dec=dec, beta=beta, z=z):
            n = i * nb + j
            pick = samp == n
            k_col = jnp.sum(jnp.where(pick, k_t, 0.0), axis=1, keepdims=True)
            q_col = jnp.sum(jnp.where(pick, q_t, 0.0), axis=1, keepdims=True)
            pick_r = row_id == n
            row = lambda t: jnp.sum(jnp.where(pick_r, t, 0.0), axis=0, keepdims=True)
            st = st_ref[j, h] * row(jnp.broadcast_to(dec, (N, LANES)))
            mem = jnp.sum(k_col * st, axis=0, keepdims=True)
            st = st + k_col * ((row(v) - mem) * row(jnp.broadcast_to(beta, (N, LANES))))
            so_ref[j, h] = st
            o = jnp.sum(q_col * st, axis=0, keepdims=True)
            o = o * lax.rsqrt(jnp.mean(o * o, axis=1, keepdims=True) + RMS_EPS) * og_ref[...]
            return jnp.where(out_row == j, o * _silu(row(z)), o_acc)

        o_ref[:, h * LANES:(h + 1) * LANES] = lax.fori_loop(0, nb, sample, jnp.zeros((nb, LANES), F32))


def _delta_dec(ub_s, conv_state, zb_s, ab_s, conv_w, a_log, dt_bias, o_norm_g, state, nb):
    N = ub_s.shape[0]
    full2 = lambda shape: pl.BlockSpec(shape, lambda i, *_: (0, 0))
    grid_spec = pltpu.PrefetchScalarGridSpec(
        num_scalar_prefetch=2,
        grid=(N // nb,),
        in_specs=[full2((N, CONV_DIM)),
                  pl.BlockSpec((CONV_WIDTH - 1, N, CONV_DIM), lambda i, *_: (0, 0, 0)),
                  full2((N, B_WIDTH)), full2((N, LANES)), full2((CONV_WIDTH, CONV_DIM)), full2((1, LANES)),
                  pl.BlockSpec((nb, B_HEADS, B_HEAD_DIM, B_HEAD_DIM), lambda i, *_: (i, 0, 0, 0))],
        out_specs=[pl.BlockSpec((nb, B_WIDTH), lambda i, *_: (i, 0)),
                   pl.BlockSpec((nb, B_HEADS, B_HEAD_DIM, B_HEAD_DIM), lambda i, *_: (i, 0, 0, 0))],
    )
    return pl.pallas_call(
        functools.partial(_delta_dec_kernel, nb=nb),
        grid_spec=grid_spec,
        out_shape=[jax.ShapeDtypeStruct((N, B_WIDTH), F32), jax.ShapeDtypeStruct(state.shape, F32)],
        compiler_params=_params(("parallel",)),
        name="delta_dec",
    )(a_log, dt_bias, ub_s, jnp.swapaxes(conv_state, 0, 1), zb_s, ab_s, conv_w, o_norm_g.reshape(1, LANES), state)


def _layer_norm(r, g, b):
    mu = jnp.mean(r, axis=1, keepdims=True)
    d = r - mu
    var = jnp.mean(d * d, axis=1, keepdims=True)
    return d * lax.rsqrt(var + LN_EPS) * g + b


def _mix_ln_kernel(oa_ref, ob_ref, x_ref, wo_ref, g_ref, b_ref, wr_ref, br_ref, h_ref, route_ref, cnt_ref):
    y = (jnp.dot(oa_ref[...].astype(BF16), wo_ref[0:A_WIDTH, :], preferred_element_type=F32)
         + jnp.dot(ob_ref[...].astype(BF16), wo_ref[A_WIDTH:A_WIDTH + B_WIDTH, :], preferred_element_type=F32))
    hcur = _layer_norm(DN_ALPHA * x_ref[...] + y, g_ref[...], b_ref[...])
    h_ref[...] = hcur
    logits = _fdot(hcur, wr_ref[...]) + br_ref[...]
    lane = lax.broadcasted_iota(jnp.int32, logits.shape, 1)
    lane_f = lane.astype(F32)
    ninf = -jnp.inf
    big = 1e9
    gl = jnp.where(lane < N_GROUPS, logits, ninf)
    gmax = jnp.max(gl, axis=1, keepdims=True)
    g_idx = jnp.min(jnp.where(gl == gmax, lane_f, big), axis=1, keepdims=True)
    p_group = 1.0 / jnp.sum(jnp.exp(gl - gmax), axis=1, keepdims=True)
    grp_of_lane = ((lane - N_GROUPS) >> 3).astype(F32)
    sel = (lane >= N_GROUPS) & (lane < N_GROUPS + N_EXPERTS) & (grp_of_lane == g_idx)
    el = jnp.where(sel, logits, ninf)
    v1 = jnp.max(el, axis=1, keepdims=True)
    i1 = jnp.min(jnp.where(el == v1, lane_f, big), axis=1, keepdims=True)
    el2 = jnp.where(lane_f == i1, ninf, el)
    v2 = jnp.max(el2, axis=1, keepdims=True)
    i2 = jnp.min(jnp.where(el2 == v2, lane_f, big), axis=1, keepdims=True)
    t = jnp.exp(v2 - v1)
    gate1 = p_group / (1.0 + t)
    gate2 = p_group * t / (1.0 + t)
    e1, e2 = i1 - N_GROUPS, i2 - N_GROUPS
    route_ref[...] = jnp.where(lane == 0, gate1, jnp.where(lane == 1, gate2, jnp.where(
        lane == 2, e1, jnp.where(lane == 3, e2, 0.0))))
    chosen = ((lane_f == e1) | (lane_f == e2)).astype(F32)
    tm = chosen.shape[0]
    cnt_ref[...] = jnp.sum(chosen.reshape(tm // ROUTE_TILE, ROUTE_TILE, LANES), axis=1)[:, None, :]


def _mix_ln(oa, ob, x2d, wo_b, g, b, wr, br, tm):
    T, D = x2d.shape
    row = lambda i: (i, 0)
    fix = lambda i: (0, 0)
    sub = tm // ROUTE_TILE
    return pl.pallas_call(
        _mix_ln_kernel,
        grid=(T // tm,),
        in_specs=[pl.BlockSpec((tm, A_WIDTH), row), pl.BlockSpec((tm, B_WIDTH), row), pl.BlockSpec((tm, D), row),
                  pl.BlockSpec((A_WIDTH + B_WIDTH, D), fix), pl.BlockSpec((1, D), fix), pl.BlockSpec((1, D), fix),
                  pl.BlockSpec((D, LANES), fix), pl.BlockSpec((1, LANES), fix)],
        out_specs=[pl.BlockSpec((tm, D), row), pl.BlockSpec((tm, LANES), row),
                   pl.BlockSpec((sub, 1, LANES), lambda i: (i, 0, 0))],
        out_shape=[jax.ShapeDtypeStruct((T, D), F32), jax.ShapeDtypeStruct((T, LANES), F32),
                   jax.ShapeDtypeStruct((T // ROUTE_TILE, 1, LANES), F32)],
        compiler_params=_params(("parallel",)),
        name="mix_ln",
    )(oa, ob, x2d, wo_b, g, b, wr, br)


def _slot_layout(counts):
    tiles = counts.shape[0]
    n_assign = tiles * ROUTE_TILE * TOP_K
    n_blocks = -(-(n_assign + N_EXPERTS * (MOE_BLK - 1)) // MOE_BLK)
    per_tile = counts.reshape(tiles, LANES).astype(jnp.int32)
    before = jnp.cumsum(per_tile, axis=0) - per_tile
    total = jnp.sum(per_tile, axis=0)[:N_EXPERTS]
    padded = (total + MOE_BLK - 1) // MOE_BLK * MOE_BLK
    pad_end = jnp.cumsum(padded)
    pad_start = pad_end - padded
    base = (before + jnp.pad(pad_start, (0, LANES - N_EXPERTS))[None, :]).astype(F32).reshape(tiles, 1, LANES)
    blk_start = jnp.arange(n_blocks, dtype=jnp.int32) * MOE_BLK
    blk_e = jnp.minimum(jnp.sum(pad_end[None, :] <= blk_start[:, None], axis=1), N_EXPERTS - 1).astype(jnp.int32)
    blk_n = jnp.clip(pad_start[blk_e] + total[blk_e] - blk_start, 0, MOE_BLK).astype(jnp.int32)
    last_e = blk_e[jnp.maximum(pad_end[-1] // MOE_BLK - 1, 0)]
    blk_e = jnp.where(blk_start < pad_end[-1], blk_e, last_e)
    return base, blk_e, blk_n, n_blocks


def _slot_kernel(route_ref, base_ref, dest_ref):
    route = route_ref[...]
    lane = lax.broadcasted_iota(jnp.int32, route.shape, 1)
    lane_f = lane.astype(F32)
    e1 = jnp.sum(jnp.where(lane == 2, route, 0.0), axis=1, keepdims=True)
    e2 = jnp.sum(jnp.where(lane == 3, route, 0.0), axis=1, keepdims=True)
    oh1, oh2 = lane_f == e1, lane_f == e2
    ri = lax.broadcasted_iota(jnp.int32, (ROUTE_TILE, ROUTE_TILE), 0)
    ci = lax.broadcasted_iota(jnp.int32, (ROUTE_TILE, ROUTE_TILE), 1)
    earlier = _bdot((ri > ci).astype(F32), (oh1 | oh2).astype(F32))
    slot = base_ref[0] + earlier
    d1 = jnp.sum(jnp.where(oh1, slot, 0.0), axis=1, keepdims=True)
    d2 = jnp.sum(jnp.where(oh2, slot, 0.0), axis=1, keepdims=True)
    dest_ref[...] = jnp.where(lane == 0, d1, jnp.where(lane == 1, d2, 0.0)).astype(jnp.int32)


def _slots(route, base):
    T = route.shape[0]
    return pl.pallas_call(
        _slot_kernel,
        grid=(T // ROUTE_TILE,),
        in_specs=[pl.BlockSpec((ROUTE_TILE, LANES), lambda i: (i, 0)), pl.BlockSpec((1, 1, LANES), lambda i: (i, 0, 0))],
        out_specs=pl.BlockSpec((ROUTE_TILE, LANES), lambda i: (i, 0)),
        out_shape=jax.ShapeDtypeStruct((T, LANES), jnp.int32),
        compiler_params=_params(("parallel",)),
        name="slots",
    )(route, base)


def _dispatch_kernel(dest_ref, h_ref, xs_in, xs_hbm, sem):
    del xs_in
    a0 = pl.program_id(0) * (ROUTE_TILE * TOP_K)

    def row_copy(a):
        return pltpu.make_async_copy(h_ref.at[pl.ds(a // TOP_K, 1)], xs_hbm.at[pl.ds(dest_ref[a0 + a], 1)], sem)

    def start(a, c):
        row_copy(a).start()
        return c

    def wait(a, c):
        row_copy(a).wait()
        return c

    lax.fori_loop(0, ROUTE_TILE * TOP_K, start, 0)
    lax.fori_loop(0, ROUTE_TILE * TOP_K, wait, 0)


def _dispatch(h_all, dest, slots):
    T, D = h_all.shape
    grid_spec = pltpu.PrefetchScalarGridSpec(
        num_scalar_prefetch=1,
        grid=(T // ROUTE_TILE,),
        in_specs=[pl.BlockSpec((ROUTE_TILE, D), lambda i, *_: (i, 0)), pl.BlockSpec(memory_space=pl.ANY)],
        out_specs=pl.BlockSpec(memory_space=pl.ANY),
        scratch_shapes=[pltpu.SemaphoreType.DMA],
    )
    return pl.pallas_call(
        _dispatch_kernel,
        grid_spec=grid_spec,
        out_shape=jax.ShapeDtypeStruct((slots, D), F32),
        input_output_aliases={2: 0},
        compiler_params=_params(("arbitrary",)),
        name="dispatch",
    )(dest, h_all, jnp.zeros((slots, D), F32))


def _moe_kernel(blk_e_ref, blk_n_ref, x_ref, wg_ref, wu_ref, wd_ref, y_ref):
    del blk_e_ref
    n_valid = blk_n_ref[pl.program_id(0)]

    @pl.when(n_valid > 0)
    def _():
        x = x_ref[...].astype(BF16)
        a = jnp.dot(x, wg_ref[0].astype(BF16), preferred_element_type=F32)
        u = jnp.dot(x, wu_ref[0].astype(BF16), preferred_element_type=F32)
        y_ref[...] = jnp.dot((_silu(a) * u).astype(BF16), wd_ref[0].astype(BF16), preferred_element_type=F32)

    @pl.when(n_valid == 0)
    def _():
        y_ref[...] = jnp.zeros(y_ref.shape, F32)


def _moe(xs, blk_e, blk_n, w_gate, w_up, w_down):
    slots, D = xs.shape
    De = w_gate.shape[-1]
    wspec = lambda shape: pl.BlockSpec((1,) + shape, lambda i, be, *_: (be[i], 0, 0))
    grid_spec = pltpu.PrefetchScalarGridSpec(
        num_scalar_prefetch=2,
        grid=(slots // MOE_BLK,),
        in_specs=[pl.BlockSpec((MOE_BLK, D), lambda i, *_: (i, 0)), wspec((D, De)), wspec((D, De)), wspec((De, D))],
        out_specs=pl.BlockSpec((MOE_BLK, D), lambda i, *_: (i, 0)),
    )
    return pl.pallas_call(
        _moe_kernel,
        grid_spec=grid_spec,
        out_shape=jax.ShapeDtypeStruct((slots, D), F32),
        compiler_params=_params(("arbitrary",)),
        name="moe",
    )(blk_e, blk_n, xs, w_gate, w_up, w_down)


def _final_ln_kernel(dest_ref, h_ref, route_ref, g_ref, b_ref, ys_hbm, o_ref, ybuf, sem, *, tile0):
    a0 = (tile0 + pl.program_id(0)) * (ROUTE_TILE * TOP_K)

    def row_copy(a):
        return pltpu.make_async_copy(ys_hbm.at[pl.ds(dest_ref[a0 + a], 1)],
                                     ybuf.at[a % TOP_K, pl.ds(a // TOP_K, 1)], sem)

    def start(a, c):
        row_copy(a).start()
        return c

    def wait(a, c):
        row_copy(a).wait()
        return c

    lax.fori_loop(0, ROUTE_TILE * TOP_K, start, 0)
    lax.fori_loop(0, ROUTE_TILE * TOP_K, wait, 0)
    route = route_ref[...]
    lane = lax.broadcasted_iota(jnp.int32, route.shape, 1)
    gate1 = jnp.sum(jnp.where(lane == 0, route, 0.0), axis=1, keepdims=True)
    gate2 = jnp.sum(jnp.where(lane == 1, route, 0.0), axis=1, keepdims=True)
    f = ybuf[0] * gate1 + ybuf[1] * gate2
    o_ref[...] = _layer_norm(DN_ALPHA * h_ref[...] + f, g_ref[...], b_ref[...])


def _final_ln(h_all, route, dest, ys, g, b, row0, rows):
    D = h_all.shape[1]
    tile0 = row0 // ROUTE_TILE
    row = lambda i, *_: (tile0 + i, 0)
    fix = lambda i, *_: (0, 0)
    grid_spec = pltpu.PrefetchScalarGridSpec(
        num_scalar_prefetch=1,
        grid=(rows // ROUTE_TILE,),
        in_specs=[pl.BlockSpec((ROUTE_TILE, D), row), pl.BlockSpec((ROUTE_TILE, LANES), row),
                  pl.BlockSpec((1, D), fix), pl.BlockSpec((1, D), fix), pl.BlockSpec(memory_space=pl.ANY)],
        out_specs=pl.BlockSpec((ROUTE_TILE, D), lambda i, *_: (i, 0)),
        scratch_shapes=[pltpu.VMEM((TOP_K, ROUTE_TILE, D), F32), pltpu.SemaphoreType.DMA],
    )
    return pl.pallas_call(
        functools.partial(_final_ln_kernel, tile0=tile0),
        grid_spec=grid_spec,
        out_shape=jax.ShapeDtypeStruct((rows, D), F32),
        compiler_params=_params(("arbitrary",)),
        name="final_ln",
    )(dest, h_all, route, g, b, ys)


def kernel(x_prompt, x_sample, cache_a_k, cache_a_v, state_b_ssm, state_b_conv, w_in, rel_bias, conv_w, a_log, dt_bias, o_norm_g, w_out, ln1_g, ln1_b, w_group, b_group, w_router, b_router, w_gate, w_up, w_down, ln2_g, ln2_b):
    B, S, D = x_prompt.shape
    N, T = x_sample.shape[0], x_sample.shape[1]
    depth = w_in.shape[0]
    assert depth == 1 and T == 1 and S % ATT_TILE == 0 and N % 8 == 0
    assert cache_a_k.shape[2] == BRANCHES[-1][0]
    l = 0
    win_p = min(BRANCHES[-1][0], S)

    w_pad = jnp.pad(w_in[l], ((0, 0), (0, IN_COLS_PAD - IN_COLS))).astype(BF16)
    wo_b = w_out[l].astype(BF16)
    wr = jnp.pad(jnp.concatenate([w_group[l], w_router[l]], axis=1), ((0, 0), (0, LANES - N_GROUPS - N_EXPERTS)))
    br = jnp.pad(jnp.concatenate([b_group[l], b_router[l].reshape(-1)]), (0, LANES - N_GROUPS - N_EXPERTS))[None, :]
    g1, b1 = ln1_g[l][None, :], ln1_b[l][None, :]
    g2, b2 = ln2_g[l][None, :], ln2_b[l][None, :]

    xp = x_prompt.reshape(B * S, D)
    qkv_p, ub_p, zb_p, ab_p = _proj(xp, w_pad, 256)
    oa_p = _attn(qkv_p, _band_bias(rel_bias), B, S)
    ob_p, st_p = _delta(ub_p, zb_p, ab_p, conv_w[l], a_log[l], dt_bias[l], o_norm_g[l], B, S)
    h_p, route_p, cnt_p = _mix_ln(oa_p.reshape(B * S, A_WIDTH), ob_p.reshape(B * S, B_WIDTH), xp, wo_b, g1, b1,
                                  wr, br, 256)

    xs = x_sample.reshape(N, D)
    qkv_s, ub_s, zb_s, ab_s = _proj(xs, w_pad, N)
    oa_s = _attn_dec(qkv_s.reshape(N, 3, A_HEADS, A_HEAD_DIM), cache_a_k[l], cache_a_v[l], _cache_bias(rel_bias),
                     4).reshape(N, A_WIDTH)
    ob_s, st_s = _delta_dec(ub_s, state_b_conv[l], zb_s, ab_s, conv_w[l], a_log[l], dt_bias[l], o_norm_g[l],
                            state_b_ssm[l], 8)
    h_s, route_s, cnt_s = _mix_ln(oa_s, ob_s, xs, wo_b, g1, b1, wr, br, N)

    h_all = jnp.concatenate([h_p, h_s], axis=0)
    route = jnp.concatenate([route_p, route_s], axis=0)
    base, blk_e, blk_n, n_blocks = _slot_layout(jnp.concatenate([cnt_p, cnt_s], axis=0))
    dest = _slots(route, base)[:, 0:TOP_K].reshape(-1)
    ys = _moe(_dispatch(h_all, dest, n_blocks * MOE_BLK), blk_e, blk_n, w_gate[l], w_up[l], w_down[l])
    y_p = _final_ln(h_all, route, dest, ys, g2, b2, 0, B * S)
    y_s = _final_ln(h_all, route, dest, ys, g2, b2, B * S, N)

    qkv_p4 = qkv_p.reshape(B, S, 3, A_HEADS, A_HEAD_DIM)
    qkv_s4 = qkv_s.reshape(N, T, 3, A_HEADS, A_HEAD_DIM)
    conv_p = ub_p.reshape(B, S, CONV_DIM)[:, S - (CONV_WIDTH - 1):]
    conv_s = jnp.concatenate([state_b_conv[l], ub_s[:, None, :]], axis=1)[:, T:]
    return (y_p.reshape(B, S, D), y_s.reshape(N, T, D),
            qkv_p4[:, S - win_p:, 1][None], qkv_p4[:, S - win_p:, 2][None],
            qkv_s4[:, :, 1][None], qkv_s4[:, :, 2][None],
            st_p[None], st_s[None], conv_p[None], conv_s[None])
```

```python
import functools
import math

import jax
import jax.numpy as jnp
from jax import lax
from jax.experimental import pallas as pl
from jax.experimental.pallas import tpu as pltpu

F32 = jnp.float32
BF16 = jnp.bfloat16
HIGHEST = lax.Precision.HIGHEST

LANES = 128
A_HEADS = 8
A_HEAD_DIM = 64
A_WIDTH = A_HEADS * A_HEAD_DIM
BRANCHES = ((128, 1), (512, 4), (2048, 16))
BAND = 128
ATT_TILE = BAND * 16
REL_BUCKETS = 32
REL_MAX_DIST = 2048
B_HEADS = 4
B_HEAD_DIM = 128
B_WIDTH = B_HEADS * B_HEAD_DIM
CONV_WIDTH = 4
CONV_DIM = 3 * B_WIDTH
CHUNK = 64
COL_UB = 3 * A_WIDTH
COL_ZB = COL_UB + CONV_DIM
COL_AB = COL_ZB + B_WIDTH
IN_COLS = COL_AB + 2 * B_HEADS
IN_COLS_PAD = COL_AB + LANES
N_GROUPS = 4
EXPERTS_PER_GROUP = 8
N_EXPERTS = N_GROUPS * EXPERTS_PER_GROUP
TOP_K = 2
DN_ALPHA = 2.0 ** 0.25
LN_EPS = 1e-5
RMS_EPS = 1e-6
MASKED = -1e30
MOE_BLK = 256
ROUTE_TILE = 128
VMEM_LIMIT = 56 * 1024 * 1024


def _bdot(a, b):
    return jnp.dot(a.astype(BF16), b.astype(BF16), preferred_element_type=F32)


def _bdot_nt(a, b):
    return lax.dot_general(a.astype(BF16), b.astype(BF16), (((1,), (1,)), ((), ())), preferred_element_type=F32)


def _bdot_tn(a, b):
    return lax.dot_general(a.astype(BF16), b.astype(BF16), (((0,), (0,)), ((), ())), preferred_element_type=F32)


def _fdot(a, b):
    return jnp.dot(a, b, precision=HIGHEST, preferred_element_type=F32)


def _sigmoid(x):
    return 1.0 / (1.0 + jnp.exp(-x))


def _silu(x):
    return x * _sigmoid(x)


def _softplus(x):
    return jnp.maximum(x, 0.0) + jnp.log(1.0 + jnp.exp(-jnp.abs(x)))


def _params(sem):
    return pltpu.CompilerParams(dimension_semantics=sem, vmem_limit_bytes=VMEM_LIMIT)


def _proj_kernel(x_ref, w_ref, qkv_ref, ub_ref, zb_ref, ab_ref):
    if w_ref.dtype == BF16:
        xb = x_ref[...].astype(BF16)
        dot = lambda w: jnp.dot(xb, w, preferred_element_type=F32)
    else:
        dot = lambda w: _fdot(x_ref[...], w)
    qkv_ref[...] = dot(w_ref[:, 0:COL_UB])
    ub_ref[...] = dot(w_ref[:, COL_UB:COL_ZB])
    zb_ref[...] = dot(w_ref[:, COL_ZB:COL_AB])
    ab_ref[...] = dot(w_ref[:, COL_AB:IN_COLS_PAD])


def _proj(x2d, w_pad, tm):
    T, D = x2d.shape
    row = lambda i: (i, 0)
    return pl.pallas_call(
        _proj_kernel,
        grid=(T // tm,),
        in_specs=[pl.BlockSpec((tm, D), row), pl.BlockSpec((D, IN_COLS_PAD), lambda i: (0, 0))],
        out_specs=[pl.BlockSpec((tm, COL_UB), row), pl.BlockSpec((tm, CONV_DIM), row),
                   pl.BlockSpec((tm, B_WIDTH), row), pl.BlockSpec((tm, LANES), row)],
        out_shape=[jax.ShapeDtypeStruct((T, COL_UB), F32), jax.ShapeDtypeStruct((T, CONV_DIM), F32),
                   jax.ShapeDtypeStruct((T, B_WIDTH), F32), jax.ShapeDtypeStruct((T, LANES), F32)],
        compiler_params=_params(("parallel",)),
        name="proj",
    )(x2d, w_pad)


def _kv_win_kernel(x_ref, wt_ref, k_ref, v_ref):
    tm = x_ref.shape[1]
    kv = lax.dot_general(wt_ref[...], x_ref[0].astype(BF16), (((1,), (1,)), ((), ())), preferred_element_type=F32)
    k_ref[0] = kv[0:A_WIDTH].reshape(A_HEADS, A_HEAD_DIM, tm)
    v_ref[0] = kv[A_WIDTH:2 * A_WIDTH].reshape(A_HEADS, A_HEAD_DIM, tm)


def _kv_win(x_prompt, w_kv_t, win, tm):
    B, S, D = x_prompt.shape
    t0 = (S - win) // tm
    out = jax.ShapeDtypeStruct((B, A_HEADS, A_HEAD_DIM, win), F32)
    ospec = pl.BlockSpec((1, A_HEADS, A_HEAD_DIM, tm), lambda b, t: (b, 0, 0, t))
    return pl.pallas_call(
        _kv_win_kernel,
        grid=(B, win // tm),
        in_specs=[pl.BlockSpec((1, tm, D), lambda b, t: (b, t0 + t, 0)),
                  pl.BlockSpec((2 * A_WIDTH, D), lambda b, t: (0, 0))],
        out_specs=[ospec, ospec],
        out_shape=[out, out],
        compiler_params=_params(("parallel", "parallel")),
        name="kv_win",
    )(x_prompt, w_kv_t)


def _rel_bucket(dist):
    max_exact = REL_BUCKETS // 2
    n = jnp.maximum(dist, 0)
    ratio = jnp.maximum(n, 1).astype(F32) / max_exact
    large = max_exact + (jnp.log(ratio) / math.log(REL_MAX_DIST / max_exact)
                         * (REL_BUCKETS - max_exact)).astype(jnp.int32)
    return jnp.where(n < max_exact, n, jnp.minimum(large, REL_BUCKETS - 1))


def _bias_of(rel_bias, dist):
    onehot = (_rel_bucket(dist)[None, :] == jnp.arange(REL_BUCKETS)[:, None]).astype(F32)
    return jnp.dot(rel_bias.astype(F32).T, onehot, precision=HIGHEST)


def _band_bias(rel_bias):
    period = 3 * BAND
    tabs = []
    for _, dil in BRANCHES:
        g = jnp.concatenate([_bias_of(rel_bias, (BAND - jnp.arange(BAND + 1)) * dil),
                             jnp.full((A_HEADS, period - BAND - 1), MASKED, F32)], axis=1)
        skew = jnp.tile(g, (1, BAND))[:, :BAND * (period - 1)].reshape(A_HEADS, BAND, period - 1)
        tabs.append(skew[:, :, :2 * BAND])
    return jnp.stack(tabs)


def _cache_bias(rel_bias, P):
    dist = P - jnp.arange(P + LANES)
    tabs = []
    for window, dil in BRANCHES:
        ok = (dist >= 0) & (dist <= window) & (dist % dil == 0)
        tabs.append(jnp.where(ok[None, :], _bias_of(rel_bias, dist), MASKED)[:, None, :])
    return jnp.stack(tabs)


def _attn_kernel(q_ref, k_ref, v_ref, bias_ref, o_ref, acc_ref, m_ref, l_ref):
    t = pl.program_id(2)
    tile0 = t * ATT_TILE
    lane = lax.broadcasted_iota(jnp.int32, (BAND, LANES), 1)
    head0 = lane < A_HEAD_DIM
    col = lax.broadcasted_iota(jnp.int32, (BAND, 2 * BAND), 1)
    prev_cols = col < BAND

    def rows(start, dil):
        return pl.ds(pl.multiple_of(start, BAND), BAND) if dil == 1 else pl.ds(start, BAND, stride=dil)

    for br, (_, dil) in enumerate(BRANCHES):
        span = BAND * dil

        def block(i, carry, br=br, dil=dil, span=span):
            start = (i % dil) + (i // dil) * span
            cur = tile0 + start
            first = cur < span
            prev = jnp.where(first, cur, cur - span)
            q = q_ref[0, rows(start, dil), :] * (A_HEAD_DIM ** -0.5)
            kk = jnp.concatenate([k_ref[0, rows(prev, dil), :], k_ref[0, rows(cur, dil), :]], axis=0).astype(BF16)
            vv = jnp.concatenate([v_ref[0, rows(prev, dil), :], v_ref[0, rows(cur, dil), :]], axis=0).astype(BF16)
            pen = jnp.where(first, MASKED, 0.0)
            outs = []
            for hh in range(2):
                qh = jnp.where(head0 if hh == 0 else ~head0, q, 0.0).astype(BF16)
                s = lax.dot_general(qh, kk, (((1,), (1,)), ((), ())), preferred_element_type=F32)
                s = s + bias_ref[br, hh] + jnp.where(prev_cols, pen, 0.0)
                m = jnp.max(s, axis=1, keepdims=True)
                p = jnp.exp(s - m)
                l = jnp.sum(p, axis=1, keepdims=True)
                pv = jnp.dot(p.astype(BF16), vv, preferred_element_type=F32)
                outs.append((pv, m, l))
            acc_ref[br, rows(start, dil), :] = jnp.where(head0, outs[0][0], outs[1][0])
            m_ref[br, rows(start, dil), :] = jnp.where(head0, outs[0][1], outs[1][1])
            l_ref[br, rows(start, dil), :] = jnp.where(head0, outs[0][2], outs[1][2])
            return carry

        lax.fori_loop(0, ATT_TILE // BAND, block, 0)

    def merge(c, carry):
        r = pl.ds(pl.multiple_of(c * 256, 256), 256)
        m0, m1, m2 = m_ref[0, r, :], m_ref[1, r, :], m_ref[2, r, :]
        mx = jnp.maximum(jnp.maximum(m0, m1), m2)
        w0, w1, w2 = jnp.exp(m0 - mx), jnp.exp(m1 - mx), jnp.exp(m2 - mx)
        num = w0 * acc_ref[0, r, :] + w1 * acc_ref[1, r, :] + w2 * acc_ref[2, r, :]
        den = w0 * l_ref[0, r, :] + w1 * l_ref[1, r, :] + w2 * l_ref[2, r, :]
        o_ref[0, r, :] = num / den
        return carry

    lax.fori_loop(0, ATT_TILE // 256, merge, 0)


def _attn(qkv, bias, B, S):
    n_pairs = A_HEADS // 2
    qkv3 = qkv.reshape(B, S, 3 * A_WIDTH)
    return pl.pallas_call(
        _attn_kernel,
        grid=(B, n_pairs, S // ATT_TILE),
        in_specs=[pl.BlockSpec((1, ATT_TILE, LANES), lambda b, hp, t: (b, t, hp)),
                  pl.BlockSpec((1, S, LANES), lambda b, hp, t: (b, 0, n_pairs + hp)),
                  pl.BlockSpec((1, S, LANES), lambda b, hp, t: (b, 0, 2 * n_pairs + hp)),
                  pl.BlockSpec((3, 2, BAND, 2 * BAND), lambda b, hp, t: (0, hp, 0, 0))],
        out_specs=pl.BlockSpec((1, ATT_TILE, LANES), lambda b, hp, t: (b, t, hp)),
        out_shape=jax.ShapeDtypeStruct((B, S, A_WIDTH), F32),
        scratch_shapes=[pltpu.VMEM((3, ATT_TILE, LANES), F32)] * 3,
        compiler_params=_params(("parallel", "parallel", "arbitrary")),
        name="attn",
    )(qkv3, qkv3, qkv3, bias)


def _attn_dec_kernel(qkv_ref, kt_ref, vt_ref, bias_ref, o_ref, *, nb):
    i = pl.program_id(0)
    N = qkv_ref.shape[-1]
    P = kt_ref.shape[-1]
    lane_n = lax.broadcasted_iota(jnp.int32, (A_HEAD_DIM, N), 1)
    lane_o = lax.broadcasted_iota(jnp.int32, (A_HEAD_DIM, nb), 1)
    lane_t = lax.broadcasted_iota(jnp.int32, (1, LANES), 1)

    def head(h, carry):
        slab = jnp.zeros((A_HEAD_DIM, nb), F32)
        for j in range(nb):
            pick = lane_n == i * nb + j
            col = lambda t: jnp.sum(jnp.where(pick, t, 0.0), axis=1, keepdims=True)
            q = col(qkv_ref[0, h]) * (A_HEAD_DIM ** -0.5)
            k_new, v_new = col(qkv_ref[1, h]), col(qkv_ref[2, h])
            s_new = jnp.sum(q * k_new, axis=0, keepdims=True)
            s = jnp.concatenate([jnp.sum(kt_ref[j, h] * q, axis=0, keepdims=True),
                                 jnp.where(lane_t == 0, s_new, 0.0)], axis=1)
            ps, ms, ls = [], [], []
            for br in range(3):
                sb = s + bias_ref[br, h]
                m = jnp.max(sb, axis=1, keepdims=True)
                p = jnp.exp(sb - m)
                ps.append(p)
                ms.append(m)
                ls.append(jnp.sum(p, axis=1, keepdims=True))
            mx = jnp.maximum(jnp.maximum(ms[0], ms[1]), ms[2])
            w = jnp.zeros((1, P + LANES), F32)
            den = jnp.zeros((1, 1), F32)
            for p, m, l in zip(ps, ms, ls):
                e = jnp.exp(m - mx)
                w = w + e * p
                den = den + e * l
            o = jnp.sum(vt_ref[j, h] * w[:, 0:P], axis=1, keepdims=True) + v_new * w[:, P:P + 1]
            slab = jnp.where(lane_o == j, o / den, slab)
        o_ref[0, h] = slab
        return carry

    lax.fori_loop(0, A_HEADS, head, 0)


def _attn_dec(qkv_t, cache_kt, cache_vt, bias, nb):
    N, P = cache_kt.shape[0], cache_kt.shape[-1]
    cache_spec = pl.BlockSpec((nb, A_HEADS, A_HEAD_DIM, P), lambda i: (i, 0, 0, 0))
    return pl.pallas_call(
        functools.partial(_attn_dec_kernel, nb=nb),
        grid=(N // nb,),
        in_specs=[pl.BlockSpec((3, A_HEADS, A_HEAD_DIM, N), lambda i: (0, 0, 0, 0)), cache_spec, cache_spec,
                  pl.BlockSpec((3, A_HEADS, 1, P + LANES), lambda i: (0, 0, 0, 0))],
        out_specs=pl.BlockSpec((1, A_HEADS, A_HEAD_DIM, nb), lambda i: (i, 0, 0, 0)),
        out_shape=jax.ShapeDtypeStruct((N // nb, A_HEADS, A_HEAD_DIM, nb), F32),
        compiler_params=_params(("parallel",)),
        name="attn_dec",
    )(qkv_t, cache_kt, cache_vt, bias)


def _delta_kernel(alog_ref, dtb_ref, uq_ref, uk_ref, uv_ref, z_ref, ab_ref, wq_ref, wk_ref, wv_ref, og_ref,
                  o_ref, st_ref, pq_ref, pk_ref, pv_ref, *, S):
    h = pl.program_id(1)
    hdr = 8
    for src, dst in ((uq_ref, pq_ref), (uk_ref, pk_ref), (uv_ref, pv_ref)):
        dst[0:hdr, :] = jnp.zeros((hdr, LANES), F32)
        dst[hdr:hdr + S, :] = src[0]

    neg_a = -jnp.exp(jnp.full((1, LANES), alog_ref[h], F32))
    dtb = jnp.full((1, LANES), dtb_ref[h], F32)
    ri = lax.broadcasted_iota(jnp.int32, (CHUNK, CHUNK), 0)
    ci = lax.broadcasted_iota(jnp.int32, (CHUNK, CHUNK), 1)
    incl = ri >= ci
    strict = ri > ci
    tril = incl.astype(F32)
    eye = (ri == ci).astype(F32)
    lane = lax.broadcasted_iota(jnp.int32, (CHUNK, LANES), 1)

    def conv(p_ref, w_ref, base):
        win = p_ref[pl.ds(base, CHUNK + hdr), :]
        acc = win[hdr - 3:hdr - 3 + CHUNK] * w_ref[0:1, :]
        for i in range(1, CONV_WIDTH):
            acc = acc + win[hdr - 3 + i:hdr - 3 + i + CHUNK] * w_ref[i:i + 1, :]
        return _silu(acc)

    def chunk(c, state):
        base = pl.multiple_of(c * CHUNK, CHUNK)
        cq, ck, v = conv(pq_ref, wq_ref, base), conv(pk_ref, wk_ref, base), conv(pv_ref, wv_ref, base)
        q = cq * lax.rsqrt(jnp.sum(cq * cq, axis=1, keepdims=True) + 1e-6) * (B_HEAD_DIM ** -0.5)
        k = ck * lax.rsqrt(jnp.sum(ck * ck, axis=1, keepdims=True) + 1e-6)
        ab = ab_ref[0, pl.ds(base, CHUNK), :]
        a_raw = jnp.sum(jnp.where(lane == h, ab, 0.0), axis=1, keepdims=True)
        b_raw = jnp.sum(jnp.where(lane == h + B_HEADS, ab, 0.0), axis=1, keepdims=True)
        g = neg_a * _softplus(a_raw + dtb)
        beta = _sigmoid(b_raw)
        gc = _fdot(tril, g)
        dmat = _fdot(tril, jnp.where(strict, g[:, 0:CHUNK], 0.0))
        decay = jnp.where(incl, jnp.exp(dmat), 0.0)
        a = jnp.where(strict, beta * _bdot_nt(k, k) * decay, 0.0)
        x = eye - a
        p = _fdot(a, a)
        for _ in range(int(math.log2(CHUNK)) - 2):
            x = x + _fdot(x, p)
            p = _fdot(p, p)
        x = x + _fdot(x, p)
        e_gc = jnp.exp(gc)
        u_val = _bdot(x, v * beta)
        w_key = _bdot(x, k * (beta * e_gc))
        qk = _bdot_nt(q, k) * decay
        gc_last = gc[CHUNK - 1:CHUNK, :]
        v_new = u_val - _bdot(w_key, state)
        out = _bdot(q * e_gc, state) + _bdot(qk, v_new)
        state = state * jnp.exp(gc_last) + _bdot_tn(k * jnp.exp(gc_last - gc), v_new)
        out = out * lax.rsqrt(jnp.mean(out * out, axis=1, keepdims=True) + RMS_EPS) * og_ref[...]
        o_ref[0, pl.ds(base, CHUNK), :] = out * _silu(z_ref[0, pl.ds(base, CHUNK), :])
        return state

    st_ref[0, 0] = lax.fori_loop(0, S // CHUNK, chunk, jnp.zeros((B_HEAD_DIM, B_HEAD_DIM), F32))


def _delta(ub, zb, ab, conv_w, a_log, dt_bias, o_norm_g, B, S):
    seq = lambda off: pl.BlockSpec((1, S, LANES), lambda b, h, *_: (b, 0, off + h))
    wsp = lambda off: pl.BlockSpec((CONV_WIDTH, LANES), lambda b, h, *_: (0, off + h))
    grid_spec = pltpu.PrefetchScalarGridSpec(
        num_scalar_prefetch=2,
        grid=(B, B_HEADS),
        in_specs=[seq(0), seq(B_HEADS), seq(2 * B_HEADS), seq(0),
                  pl.BlockSpec((1, S, LANES), lambda b, h, *_: (b, 0, 0)),
                  wsp(0), wsp(B_HEADS), wsp(2 * B_HEADS),
                  pl.BlockSpec((1, LANES), lambda b, h, *_: (0, 0))],
        out_specs=[pl.BlockSpec((1, S, LANES), lambda b, h, *_: (b, 0, h)),
                   pl.BlockSpec((1, 1, B_HEAD_DIM, B_HEAD_DIM), lambda b, h, *_: (b, h, 0, 0))],
        scratch_shapes=[pltpu.VMEM((S + 8, LANES), F32)] * 3,
    )
    ub3 = ub.reshape(B, S, CONV_DIM)
    return pl.pallas_call(
        functools.partial(_delta_kernel, S=S),
        grid_spec=grid_spec,
        out_shape=[jax.ShapeDtypeStruct((B, S, B_WIDTH), F32),
                   jax.ShapeDtypeStruct((B, B_HEADS, B_HEAD_DIM, B_HEAD_DIM), F32)],
        compiler_params=_params(("parallel", "parallel")),
        name="delta",
    )(a_log, dt_bias, ub3, ub3, ub3, zb.reshape(B, S, B_WIDTH), ab.reshape(B, S, LANES),
      conv_w, conv_w, conv_w, o_norm_g.reshape(1, LANES))


def _delta_dec_kernel(alog_ref, dtb_ref, ub_ref, cs_ref, zb_ref, ab_ref, w_ref, og_ref, st_ref,
                      o_ref, so_ref, *, nb):
    i = pl.program_id(0)
    N = ub_ref.shape[0]
    acc = ub_ref[...] * w_ref[CONV_WIDTH - 1:CONV_WIDTH, :]
    for t in range(CONV_WIDTH - 1):
        acc = acc + cs_ref[t] * w_ref[t:t + 1, :]
    c = _silu(acc)
    ab = ab_ref[...]
    lane = lax.broadcasted_iota(jnp.int32, (N, LANES), 1)
    samp = lax.broadcasted_iota(jnp.int32, (B_HEAD_DIM, N), 1)
    row_id = lax.broadcasted_iota(jnp.int32, (N, LANES), 0)
    out_row = lax.broadcasted_iota(jnp.int32, (nb, LANES), 0)
    for h in range(B_HEADS):
        cq = c[:, h * LANES:(h + 1) * LANES]
        ck = c[:, B_WIDTH + h * LANES:B_WIDTH + (h + 1) * LANES]
        v = c[:, 2 * B_WIDTH + h * LANES:2 * B_WIDTH + (h + 1) * LANES]
        q = cq * lax.rsqrt(jnp.sum(cq * cq, axis=1, keepdims=True) + 1e-6) * (B_HEAD_DIM ** -0.5)
        k = ck * lax.rsqrt(jnp.sum(ck * ck, axis=1, keepdims=True) + 1e-6)
        a_raw = jnp.sum(jnp.where(lane == h, ab, 0.0), axis=1, keepdims=True)
        b_raw = jnp.sum(jnp.where(lane == h + B_HEADS, ab, 0.0), axis=1, keepdims=True)
        neg_a = -jnp.exp(jnp.full((1, 1), alog_ref[h], F32))
        dec = jnp.exp(neg_a * _softplus(a_raw + dtb_ref[h]))
        beta = _sigmoid(b_raw)
        q_t, k_t = q.T, k.T
        z = zb_ref[:, h * LANES:(h + 1) * LANES]

        def sample(j, o_acc, h=h, q_t=q_t, k_t=k_t, v=v, dec=dec, beta=beta, z=z):
            n = i * nb + j
            pick = samp == n
            k_col = jnp.sum(jnp.where(pick, k_t, 0.0), axis=1, keepdims=True)
            q_col = jnp.sum(jnp.where(pick, q_t, 0.0), axis=1, keepdims=True)
            pick_r = row_id == n
            row = lambda t: jnp.sum(jnp.where(pick_r, t, 0.0), axis=0, keepdims=True)
            st = st_ref[j, h] * row(jnp.broadcast_to(dec, (N, LANES)))
            mem = jnp.sum(k_col * st, axis=0, keepdims=True)
            st = st + k_col * ((row(v) - mem) * row(jnp.broadcast_to(beta, (N, LANES))))
            so_ref[j, h] = st
            o = jnp.sum(q_col * st, axis=0, keepdims=True)
            o = o * lax.rsqrt(jnp.mean(o * o, axis=1, keepdims=True) + RMS_EPS) * og_ref[...]
            return jnp.where(out_row == j, o * _silu(row(z)), o_acc)

        o_ref[:, h * LANES:(h + 1) * LANES] = lax.fori_loop(0, nb, sample, jnp.zeros((nb, LANES), F32))


def _delta_dec(ub_s, conv_state, zb_s, ab_s, conv_w, a_log, dt_bias, o_norm_g, state, nb):
    N = ub_s.shape[0]
    full2 = lambda shape: pl.BlockSpec(shape, lambda i, *_: (0, 0))
    grid_spec = pltpu.PrefetchScalarGridSpec(
        num_scalar_prefetch=2,
        grid=(N // nb,),
        in_specs=[full2((N, CONV_DIM)),
                  pl.BlockSpec((CONV_WIDTH - 1, N, CONV_DIM), lambda i, *_: (0, 0, 0)),
                  full2((N, B_WIDTH)), full2((N, LANES)), full2((CONV_WIDTH, CONV_DIM)), full2((1, LANES)),
                  pl.BlockSpec((nb, B_HEADS, B_HEAD_DIM, B_HEAD_DIM), lambda i, *_: (i, 0, 0, 0))],
        out_specs=[pl.BlockSpec((nb, B_WIDTH), lambda i, *_: (i, 0)),
                   pl.BlockSpec((nb, B_HEADS, B_HEAD_DIM, B_HEAD_DIM), lambda i, *_: (i, 0, 0, 0))],
    )
    return pl.pallas_call(
        functools.partial(_delta_dec_kernel, nb=nb),
        grid_spec=grid_spec,
        out_shape=[jax.ShapeDtypeStruct((N, B_WIDTH), F32), jax.ShapeDtypeStruct(state.shape, F32)],
        compiler_params=_params(("parallel",)),
        name="delta_dec",
    )(a_log, dt_bias, ub_s, jnp.swapaxes(conv_state, 0, 1), zb_s, ab_s, conv_w, o_norm_g.reshape(1, LANES), state)


def _layer_norm(r, g, b):
    mu = jnp.mean(r, axis=1, keepdims=True)
    d = r - mu
    var = jnp.mean(d * d, axis=1, keepdims=True)
    return d * lax.rsqrt(var + LN_EPS) * g + b


def _mix_ln_kernel(oa_ref, ob_ref, x_ref, wo_ref, g_ref, b_ref, wr_ref, br_ref, h_ref, route_ref, cnt_ref):
    if wo_ref.dtype == BF16:
        dot = lambda a, w: jnp.dot(a.astype(BF16), w, preferred_element_type=F32)
    else:
        dot = _fdot
    y = dot(oa_ref[...], wo_ref[0:A_WIDTH, :]) + dot(ob_ref[...], wo_ref[A_WIDTH:A_WIDTH + B_WIDTH, :])
    hcur = _layer_norm(DN_ALPHA * x_ref[...] + y, g_ref[...], b_ref[...])
    h_ref[...] = hcur
    logits = _fdot(hcur, wr_ref[...]) + br_ref[...]
    lane = lax.broadcasted_iota(jnp.int32, logits.shape, 1)
    lane_f = lane.astype(F32)
    ninf = -jnp.inf
    big = 1e9
    gl = jnp.where(lane < N_GROUPS, logits, ninf)
    gmax = jnp.max(gl, axis=1, keepdims=True)
    g_idx = jnp.min(jnp.where(gl == gmax, lane_f, big), axis=1, keepdims=True)
    p_group = 1.0 / jnp.sum(jnp.exp(gl - gmax), axis=1, keepdims=True)
    grp_of_lane = ((lane - N_GROUPS) >> 3).astype(F32)
    sel = (lane >= N_GROUPS) & (lane < N_GROUPS + N_EXPERTS) & (grp_of_lane == g_idx)
    el = jnp.where(sel, logits, ninf)
    v1 = jnp.max(el, axis=1, keepdims=True)
    i1 = jnp.min(jnp.where(el == v1, lane_f, big), axis=1, keepdims=True)
    el2 = jnp.where(lane_f == i1, ninf, el)
    v2 = jnp.max(el2, axis=1, keepdims=True)
    i2 = jnp.min(jnp.where(el2 == v2, lane_f, big), axis=1, keepdims=True)
    t = jnp.exp(v2 - v1)
    gate1 = p_group / (1.0 + t)
    gate2 = p_group * t / (1.0 + t)
    e1, e2 = i1 - N_GROUPS, i2 - N_GROUPS
    route_ref[...] = jnp.where(lane == 0, gate1, jnp.where(lane == 1, gate2, jnp.where(
        lane == 2, e1, jnp.where(lane == 3, e2, 0.0))))
    chosen = ((lane_f == e1) | (lane_f == e2)).astype(F32)
    tm = chosen.shape[0]
    cnt_ref[...] = jnp.sum(chosen.reshape(tm // ROUTE_TILE, ROUTE_TILE, LANES), axis=1)[:, None, :]


def _mix_ln(oa, ob, x2d, wo_b, g, b, wr, br, tm):
    T, D = x2d.shape
    row = lambda i: (i, 0)
    fix = lambda i: (0, 0)
    sub = tm // ROUTE_TILE
    return pl.pallas_call(
        _mix_ln_kernel,
        grid=(T // tm,),
        in_specs=[pl.BlockSpec((tm, A_WIDTH), row), pl.BlockSpec((tm, B_WIDTH), row), pl.BlockSpec((tm, D), row),
                  pl.BlockSpec((A_WIDTH + B_WIDTH, D), fix), pl.BlockSpec((1, D), fix), pl.BlockSpec((1, D), fix),
                  pl.BlockSpec((D, LANES), fix), pl.BlockSpec((1, LANES), fix)],
        out_specs=[pl.BlockSpec((tm, D), row), pl.BlockSpec((tm, LANES), row),
                   pl.BlockSpec((sub, 1, LANES), lambda i: (i, 0, 0))],
        out_shape=[jax.ShapeDtypeStruct((T, D), F32), jax.ShapeDtypeStruct((T, LANES), F32),
                   jax.ShapeDtypeStruct((T // ROUTE_TILE, 1, LANES), F32)],
        compiler_params=_params(("parallel",)),
        name="mix_ln",
    )(oa, ob, x2d, wo_b, g, b, wr, br)


def _slot_layout(counts):
    tiles = counts.shape[0]
    n_assign = tiles * ROUTE_TILE * TOP_K
    n_blocks = -(-(n_assign + N_EXPERTS * (MOE_BLK - 1)) // MOE_BLK)
    per_tile = counts.reshape(tiles, LANES).astype(jnp.int32)
    before = jnp.cumsum(per_tile, axis=0) - per_tile
    total = jnp.sum(per_tile, axis=0)[:N_EXPERTS]
    padded = (total + MOE_BLK - 1) // MOE_BLK * MOE_BLK
    pad_end = jnp.cumsum(padded)
    pad_start = pad_end - padded
    base = (before + jnp.pad(pad_start, (0, LANES - N_EXPERTS))[None, :]).astype(F32).reshape(tiles, 1, LANES)
    blk_start = jnp.arange(n_blocks, dtype=jnp.int32) * MOE_BLK
    blk_e = jnp.minimum(jnp.sum(pad_end[None, :] <= blk_start[:, None], axis=1), N_EXPERTS - 1).astype(jnp.int32)
    blk_n = jnp.clip(pad_start[blk_e] + total[blk_e] - blk_start, 0, MOE_BLK).astype(jnp.int32)
    last_e = blk_e[jnp.maximum(pad_end[-1] // MOE_BLK - 1, 0)]
    blk_e = jnp.where(blk_start < pad_end[-1], blk_e, last_e)
    return base, blk_e, blk_n, n_blocks


def _slot_kernel(route_ref, base_ref, dest_ref):
    route = route_ref[...]
    lane = lax.broadcasted_iota(jnp.int32, route.shape, 1)
    lane_f = lane.astype(F32)
    e1 = jnp.sum(jnp.where(lane == 2, route, 0.0), axis=1, keepdims=True)
    e2 = jnp.sum(jnp.where(lane == 3, route, 0.0), axis=1, keepdims=True)
    oh1, oh2 = lane_f == e1, lane_f == e2
    ri = lax.broadcasted_iota(jnp.int32, (ROUTE_TILE, ROUTE_TILE), 0)
    ci = lax.broadcasted_iota(jnp.int32, (ROUTE_TILE, ROUTE_TILE), 1)
    earlier = _bdot((ri > ci).astype(F32), (oh1 | oh2).astype(F32))
    slot = base_ref[0] + earlier
    d1 = jnp.sum(jnp.where(oh1, slot, 0.0), axis=1, keepdims=True)
    d2 = jnp.sum(jnp.where(oh2, slot, 0.0), axis=1, keepdims=True)
    dest_ref[...] = jnp.where(lane == 0, d1, jnp.where(lane == 1, d2, 0.0)).astype(jnp.int32)


def _slots(route, base):
    T = route.shape[0]
    return pl.pallas_call(
        _slot_kernel,
        grid=(T // ROUTE_TILE,),
        in_specs=[pl.BlockSpec((ROUTE_TILE, LANES), lambda i: (i, 0)), pl.BlockSpec((1, 1, LANES), lambda i: (i, 0, 0))],
        out_specs=pl.BlockSpec((ROUTE_TILE, LANES), lambda i: (i, 0)),
        out_shape=jax.ShapeDtypeStruct((T, LANES), jnp.int32),
        compiler_params=_params(("parallel",)),
        name="slots",
    )(route, base)


def _dispatch_kernel(dest_ref, h_ref, xs_in, xs_hbm, sem):
    del xs_in
    a0 = pl.program_id(0) * (ROUTE_TILE * TOP_K)

    def start(r, c):
        for k in range(TOP_K):
            pltpu.make_async_copy(h_ref.at[pl.ds(r, 1)], xs_hbm.at[pl.ds(dest_ref[a0 + r * TOP_K + k], 1)], sem).start()
        return c

    lax.fori_loop(0, ROUTE_TILE, start, 0, unroll=8)
    for k in range(TOP_K):
        pltpu.make_async_copy(h_ref, xs_hbm.at[pl.ds(0, ROUTE_TILE)], sem).wait()


def _dispatch(h_all, dest, slots):
    T, D = h_all.shape
    grid_spec = pltpu.PrefetchScalarGridSpec(
        num_scalar_prefetch=1,
        grid=(T // ROUTE_TILE,),
        in_specs=[pl.BlockSpec((ROUTE_TILE, D), lambda i, *_: (i, 0)), pl.BlockSpec(memory_space=pl.ANY)],
        out_specs=pl.BlockSpec(memory_space=pl.ANY),
        scratch_shapes=[pltpu.SemaphoreType.DMA],
    )
    return pl.pallas_call(
        _dispatch_kernel,
        grid_spec=grid_spec,
        out_shape=jax.ShapeDtypeStruct((slots, D), F32),
        input_output_aliases={2: 0},
        compiler_params=_params(("arbitrary",)),
        name="dispatch",
    )(dest, h_all, jnp.zeros((slots, D), F32))


def _moe_kernel(blk_e_ref, blk_n_ref, x_ref, wg_ref, wu_ref, wd_ref, y_ref):
    del blk_e_ref
    n_valid = blk_n_ref[pl.program_id(0)]

    @pl.when(n_valid > 0)
    def _():
        x = x_ref[...].astype(BF16)
        a = jnp.dot(x, wg_ref[0].astype(BF16), preferred_element_type=F32)
        u = jnp.dot(x, wu_ref[0].astype(BF16), preferred_element_type=F32)
        y_ref[...] = jnp.dot((_silu(a) * u).astype(BF16), wd_ref[0].astype(BF16), preferred_element_type=F32)

    @pl.when(n_valid == 0)
    def _():
        y_ref[...] = jnp.zeros(y_ref.shape, F32)


def _moe(xs, blk_e, blk_n, w_gate, w_up, w_down):
    slots, D = xs.shape
    De = w_gate.shape[-1]
    wspec = lambda shape: pl.BlockSpec((1,) + shape, lambda i, be, *_: (be[i], 0, 0))
    grid_spec = pltpu.PrefetchScalarGridSpec(
        num_scalar_prefetch=2,
        grid=(slots // MOE_BLK,),
        in_specs=[pl.BlockSpec((MOE_BLK, D), lambda i, *_: (i, 0)), wspec((D, De)), wspec((D, De)), wspec((De, D))],
        out_specs=pl.BlockSpec((MOE_BLK, D), lambda i, *_: (i, 0)),
    )
    return pl.pallas_call(
        _moe_kernel,
        grid_spec=grid_spec,
        out_shape=jax.ShapeDtypeStruct((slots, D), F32),
        compiler_params=_params(("arbitrary",)),
        name="moe",
    )(blk_e, blk_n, xs, w_gate, w_up, w_down)


def _final_ln_kernel(dest_ref, h_ref, route_ref, g_ref, b_ref, ys_hbm, o_ref, ybuf, sem, *, tile0):
    a0 = (tile0 + pl.program_id(0)) * (ROUTE_TILE * TOP_K)

    def start(r, c):
        for k in range(TOP_K):
            pltpu.make_async_copy(ys_hbm.at[pl.ds(dest_ref[a0 + r * TOP_K + k], 1)], ybuf.at[k, pl.ds(r, 1)], sem).start()
        return c

    lax.fori_loop(0, ROUTE_TILE, start, 0, unroll=8)
    for k in range(TOP_K):
        pltpu.make_async_copy(ys_hbm.at[pl.ds(0, ROUTE_TILE)], ybuf.at[k], sem).wait()
    route = route_ref[...]
    lane = lax.broadcasted_iota(jnp.int32, route.shape, 1)
    gate1 = jnp.sum(jnp.where(lane == 0, route, 0.0), axis=1, keepdims=True)
    gate2 = jnp.sum(jnp.where(lane == 1, route, 0.0), axis=1, keepdims=True)
    f = ybuf[0] * gate1 + ybuf[1] * gate2
    o_ref[...] = _layer_norm(DN_ALPHA * h_ref[...] + f, g_ref[...], b_ref[...])


def _final_ln(h_all, route, dest, ys, g, b, row0, rows):
    D = h_all.shape[1]
    tile0 = row0 // ROUTE_TILE
    row = lambda i, *_: (tile0 + i, 0)
    fix = lambda i, *_: (0, 0)
    grid_spec = pltpu.PrefetchScalarGridSpec(
        num_scalar_prefetch=1,
        grid=(rows // ROUTE_TILE,),
        in_specs=[pl.BlockSpec((ROUTE_TILE, D), row), pl.BlockSpec((ROUTE_TILE, LANES), row),
                  pl.BlockSpec((1, D), fix), pl.BlockSpec((1, D), fix), pl.BlockSpec(memory_space=pl.ANY)],
        out_specs=pl.BlockSpec((ROUTE_TILE, D), lambda i, *_: (i, 0)),
        scratch_shapes=[pltpu.VMEM((TOP_K, ROUTE_TILE, D), F32), pltpu.SemaphoreType.DMA],
    )
    return pl.pallas_call(
        functools.partial(_final_ln_kernel, tile0=tile0),
        grid_spec=grid_spec,
        out_shape=jax.ShapeDtypeStruct((rows, D), F32),
        compiler_params=_params(("arbitrary",)),
        name="final_ln",
    )(dest, h_all, route, g, b, ys)


def kernel(x_prompt, x_sample, cache_a_k, cache_a_v, state_b_ssm, state_b_conv, w_in, rel_bias, conv_w, a_log, dt_bias, o_norm_g, w_out, ln1_g, ln1_b, w_group, b_group, w_router, b_router, w_gate, w_up, w_down, ln2_g, ln2_b):
    B, S, D = x_prompt.shape
    N, T = x_sample.shape[0], x_sample.shape[1]
    depth = w_in.shape[0]
    assert depth == 1 and T == 1 and S % ATT_TILE == 0 and N % ROUTE_TILE == 0 and cache_a_k.shape[2] % LANES == 0
    l = 0
    win_p = min(BRANCHES[-1][0], S)

    w_pad32 = jnp.pad(w_in[l], ((0, 0), (0, IN_COLS_PAD - IN_COLS)))
    w_pad = w_pad32.astype(BF16)
    wo_b = w_out[l].astype(BF16)
    wr = jnp.pad(jnp.concatenate([w_group[l], w_router[l]], axis=1), ((0, 0), (0, LANES - N_GROUPS - N_EXPERTS)))
    br = jnp.pad(jnp.concatenate([b_group[l], b_router[l].reshape(-1)]), (0, LANES - N_GROUPS - N_EXPERTS))[None, :]
    g1, b1 = ln1_g[l][None, :], ln1_b[l][None, :]
    g2, b2 = ln2_g[l][None, :], ln2_b[l][None, :]

    xp = x_prompt.reshape(B * S, D)
    qkv_p, ub_p, zb_p, ab_p = _proj(xp, w_pad, 256)
    oa_p = _attn(qkv_p, _band_bias(rel_bias), B, S)
    ob_p, st_p = _delta(ub_p, zb_p, ab_p, conv_w[l], a_log[l], dt_bias[l], o_norm_g[l], B, S)
    h_p, route_p, cnt_p = _mix_ln(oa_p.reshape(B * S, A_WIDTH), ob_p.reshape(B * S, B_WIDTH), xp, wo_b, g1, b1,
                                  wr, br, 256)

    xs = x_sample.reshape(N, D)
    qkv_s, ub_s, zb_s, ab_s = _proj(xs, w_pad32, N)
    nb = 2
    qkv_t = jnp.transpose(qkv_s.reshape(N, 3, A_HEADS, A_HEAD_DIM), (1, 2, 3, 0))
    oa_s = _attn_dec(qkv_t, jnp.transpose(cache_a_k[l], (0, 2, 3, 1)), jnp.transpose(cache_a_v[l], (0, 2, 3, 1)),
                     _cache_bias(rel_bias, cache_a_k.shape[2]), nb)
    oa_s = jnp.transpose(oa_s, (0, 3, 1, 2)).reshape(N, A_WIDTH)
    ob_s, st_s = _delta_dec(ub_s, state_b_conv[l], zb_s, ab_s, conv_w[l], a_log[l], dt_bias[l], o_norm_g[l],
                            state_b_ssm[l], 8)
    h_s, route_s, cnt_s = _mix_ln(oa_s, ob_s, xs, w_out[l], g1, b1, wr, br, N)

    h_all = jnp.concatenate([h_p, h_s], axis=0)
    route = jnp.concatenate([route_p, route_s], axis=0)
    base, blk_e, blk_n, n_blocks = _slot_layout(jnp.concatenate([cnt_p, cnt_s], axis=0))
    dest = _slots(route, base)[:, 0:TOP_K].reshape(-1)
    ys = _moe(_dispatch(h_all, dest, n_blocks * MOE_BLK), blk_e, blk_n, w_gate[l], w_up[l], w_down[l])
    y_p = _final_ln(h_all, route, dest, ys, g2, b2, 0, B * S)
    y_s = _final_ln(h_all, route, dest, ys, g2, b2, B * S, N)

    w_kv_t = jnp.transpose(w_in[l][:, A_WIDTH:3 * A_WIDTH]).astype(BF16)
    k_win, v_win = _kv_win(x_prompt, w_kv_t, win_p, 512)
    to_rows = lambda t: jnp.transpose(t, (0, 3, 1, 2))[None]
    conv_p = ub_p.reshape(B, S, CONV_DIM)[:, S - (CONV_WIDTH - 1):]
    conv_s = jnp.concatenate([state_b_conv[l], ub_s[:, None, :]], axis=1)[:, T:]
    new_kv = lambda t: jnp.transpose(t, (2, 0, 1))[None, :, None]
    return (y_p.reshape(B, S, D), y_s.reshape(N, T, D), to_rows(k_win), to_rows(v_win),
            new_kv(qkv_t[1]), new_kv(qkv_t[2]), st_p[None], st_s[None], conv_p[None], conv_s[None])
```

```python
import functools
import math

import jax
import jax.numpy as jnp
from jax import lax
from jax.experimental import pallas as pl
from jax.experimental.pallas import tpu as pltpu

F32 = jnp.float32
BF16 = jnp.bfloat16
HIGHEST = lax.Precision.HIGHEST

LANES = 128
A_HEADS = 8
A_HEAD_DIM = 64
A_WIDTH = A_HEADS * A_HEAD_DIM
BRANCHES = ((128, 1), (512, 4), (2048, 16))
BAND = 128
ATT_TILE = BAND * 16
REL_BUCKETS = 32
REL_MAX_DIST = 2048
B_HEADS = 4
B_HEAD_DIM = 128
B_WIDTH = B_HEADS * B_HEAD_DIM
CONV_WIDTH = 4
CONV_DIM = 3 * B_WIDTH
CHUNK = 64
COL_UB = 3 * A_WIDTH
COL_ZB = COL_UB + CONV_DIM
COL_AB = COL_ZB + B_WIDTH
IN_COLS = COL_AB + 2 * B_HEADS
IN_COLS_PAD = COL_AB + LANES
N_GROUPS = 4
EXPERTS_PER_GROUP = 8
N_EXPERTS = N_GROUPS * EXPERTS_PER_GROUP
TOP_K = 2
DN_ALPHA = 2.0 ** 0.25
LN_EPS = 1e-5
RMS_EPS = 1e-6
MASKED = -1e30
MOE_BLK = 256
ROUTE_TILE = 128
DELTA_TILE = 1024
VMEM_LIMIT = 56 * 1024 * 1024


def _bdot(a, b):
    return jnp.dot(a.astype(BF16), b.astype(BF16), preferred_element_type=F32)


def _bdot_nt(a, b):
    return lax.dot_general(a.astype(BF16), b.astype(BF16), (((1,), (1,)), ((), ())), preferred_element_type=F32)


def _bdot_tn(a, b):
    return lax.dot_general(a.astype(BF16), b.astype(BF16), (((0,), (0,)), ((), ())), preferred_element_type=F32)


def _fdot(a, b):
    return jnp.dot(a, b, precision=HIGHEST, preferred_element_type=F32)


def _sigmoid(x):
    return 1.0 / (1.0 + jnp.exp(-x))


def _silu(x):
    return x * _sigmoid(x)


def _softplus(x):
    return jnp.maximum(x, 0.0) + jnp.log(1.0 + jnp.exp(-jnp.abs(x)))


def _params(sem):
    return pltpu.CompilerParams(dimension_semantics=sem, vmem_limit_bytes=VMEM_LIMIT)


def _proj_kernel(x_ref, w_ref, qkv_ref, ub_ref, zb_ref, ab_ref):
    if w_ref.dtype == BF16:
        xb = x_ref[...].astype(BF16)
        dot = lambda w: jnp.dot(xb, w, preferred_element_type=F32)
    else:
        dot = lambda w: _fdot(x_ref[...], w)
    qkv_ref[...] = dot(w_ref[:, 0:COL_UB])
    ub_ref[...] = dot(w_ref[:, COL_UB:COL_ZB])
    zb_ref[...] = dot(w_ref[:, COL_ZB:COL_AB])
    ab_ref[...] = dot(w_ref[:, COL_AB:IN_COLS_PAD])


def _proj(x2d, w_pad, tm):
    T, D = x2d.shape
    row = lambda i: (i, 0)
    return pl.pallas_call(
        _proj_kernel,
        grid=(T // tm,),
        in_specs=[pl.BlockSpec((tm, D), row), pl.BlockSpec((D, IN_COLS_PAD), lambda i: (0, 0))],
        out_specs=[pl.BlockSpec((tm, COL_UB), row), pl.BlockSpec((tm, CONV_DIM), row),
                   pl.BlockSpec((tm, B_WIDTH), row), pl.BlockSpec((tm, LANES), row)],
        out_shape=[jax.ShapeDtypeStruct((T, COL_UB), F32), jax.ShapeDtypeStruct((T, CONV_DIM), F32),
                   jax.ShapeDtypeStruct((T, B_WIDTH), F32), jax.ShapeDtypeStruct((T, LANES), F32)],
        compiler_params=_params(("parallel",)),
        name="proj",
    )(x2d, w_pad)


def _kv_win_kernel(x_ref, wt_ref, k_ref, v_ref):
    tm = x_ref.shape[1]
    kv = lax.dot_general(wt_ref[...], x_ref[0].astype(BF16), (((1,), (1,)), ((), ())), preferred_element_type=F32)
    k_ref[0] = kv[0:A_WIDTH].reshape(A_HEADS, A_HEAD_DIM, tm)
    v_ref[0] = kv[A_WIDTH:2 * A_WIDTH].reshape(A_HEADS, A_HEAD_DIM, tm)


def _kv_win(x_prompt, w_kv_t, win, tm):
    B, S, D = x_prompt.shape
    t0 = (S - win) // tm
    out = jax.ShapeDtypeStruct((B, A_HEADS, A_HEAD_DIM, win), F32)
    ospec = pl.BlockSpec((1, A_HEADS, A_HEAD_DIM, tm), lambda b, t: (b, 0, 0, t))
    return pl.pallas_call(
        _kv_win_kernel,
        grid=(B, win // tm),
        in_specs=[pl.BlockSpec((1, tm, D), lambda b, t: (b, t0 + t, 0)),
                  pl.BlockSpec((2 * A_WIDTH, D), lambda b, t: (0, 0))],
        out_specs=[ospec, ospec],
        out_shape=[out, out],
        compiler_params=_params(("parallel", "parallel")),
        name="kv_win",
    )(x_prompt, w_kv_t)


def _rel_bucket(dist):
    max_exact = REL_BUCKETS // 2
    n = jnp.maximum(dist, 0)
    ratio = jnp.maximum(n, 1).astype(F32) / max_exact
    large = max_exact + (jnp.log(ratio) / math.log(REL_MAX_DIST / max_exact)
                         * (REL_BUCKETS - max_exact)).astype(jnp.int32)
    return jnp.where(n < max_exact, n, jnp.minimum(large, REL_BUCKETS - 1))


def _bias_of(rel_bias, dist):
    onehot = (_rel_bucket(dist)[None, :] == jnp.arange(REL_BUCKETS)[:, None]).astype(F32)
    return jnp.dot(rel_bias.astype(F32).T, onehot, precision=HIGHEST)


def _band_bias(rel_bias):
    period = 3 * BAND
    tabs = []
    for _, dil in BRANCHES:
        g = jnp.concatenate([_bias_of(rel_bias, (BAND - jnp.arange(BAND + 1)) * dil),
                             jnp.full((A_HEADS, period - BAND - 1), MASKED, F32)], axis=1)
        skew = jnp.tile(g, (1, BAND))[:, :BAND * (period - 1)].reshape(A_HEADS, BAND, period - 1)
        tabs.append(skew[:, :, :2 * BAND])
    return jnp.stack(tabs)


def _cache_bias(rel_bias, P):
    dist = P - jnp.arange(P + LANES)
    tabs = []
    for window, dil in BRANCHES:
        ok = (dist >= 0) & (dist <= window) & (dist % dil == 0)
        tabs.append(jnp.where(ok[None, :], _bias_of(rel_bias, dist), MASKED)[:, None, :])
    return jnp.stack(tabs)


def _attn_kernel(q_ref, k_ref, v_ref, bias_ref, o_ref, acc_ref, m_ref, l_ref):
    t = pl.program_id(2)
    tile0 = t * ATT_TILE
    lane = lax.broadcasted_iota(jnp.int32, (BAND, LANES), 1)
    head0 = lane < A_HEAD_DIM
    col = lax.broadcasted_iota(jnp.int32, (BAND, 2 * BAND), 1)
    prev_cols = col < BAND

    def rows(start, dil):
        return pl.ds(pl.multiple_of(start, BAND), BAND) if dil == 1 else pl.ds(start, BAND, stride=dil)

    for br, (_, dil) in enumerate(BRANCHES):
        span = BAND * dil

        def block(i, carry, br=br, dil=dil, span=span):
            start = (i % dil) + (i // dil) * span
            cur = tile0 + start
            first = cur < span
            prev = jnp.where(first, cur, cur - span)
            q = q_ref[0, rows(start, dil), :] * (A_HEAD_DIM ** -0.5)
            kk = jnp.concatenate([k_ref[0, rows(prev, dil), :], k_ref[0, rows(cur, dil), :]], axis=0).astype(BF16)
            vv = jnp.concatenate([v_ref[0, rows(prev, dil), :], v_ref[0, rows(cur, dil), :]], axis=0).astype(BF16)
            pen = jnp.where(first, MASKED, 0.0)
            outs = []
            for hh in range(2):
                qh = jnp.where(head0 if hh == 0 else ~head0, q, 0.0).astype(BF16)
                s = lax.dot_general(qh, kk, (((1,), (1,)), ((), ())), preferred_element_type=F32)
                s = s + bias_ref[br, hh] + jnp.where(prev_cols, pen, 0.0)
                m = jnp.max(s, axis=1, keepdims=True)
                p = jnp.exp(s - m)
                l = jnp.sum(p, axis=1, keepdims=True)
                pv = jnp.dot(p.astype(BF16), vv, preferred_element_type=F32)
                outs.append((pv, m, l))
            acc_ref[br, rows(start, dil), :] = jnp.where(head0, outs[0][0], outs[1][0])
            m_ref[br, rows(start, dil), :] = jnp.where(head0, outs[0][1], outs[1][1])
            l_ref[br, rows(start, dil), :] = jnp.where(head0, outs[0][2], outs[1][2])
            return carry

        lax.fori_loop(0, ATT_TILE // BAND, block, 0, unroll=4)

    def merge(c, carry):
        r = pl.ds(pl.multiple_of(c * 256, 256), 256)
        m0, m1, m2 = m_ref[0, r, :], m_ref[1, r, :], m_ref[2, r, :]
        mx = jnp.maximum(jnp.maximum(m0, m1), m2)
        w0, w1, w2 = jnp.exp(m0 - mx), jnp.exp(m1 - mx), jnp.exp(m2 - mx)
        num = w0 * acc_ref[0, r, :] + w1 * acc_ref[1, r, :] + w2 * acc_ref[2, r, :]
        den = w0 * l_ref[0, r, :] + w1 * l_ref[1, r, :] + w2 * l_ref[2, r, :]
        o_ref[0, r, :] = num / den
        return carry

    lax.fori_loop(0, ATT_TILE // 256, merge, 0)


def _attn(qkv, bias, B, S):
    n_pairs = A_HEADS // 2
    qkv3 = qkv.reshape(B, S, 3 * A_WIDTH)
    return pl.pallas_call(
        _attn_kernel,
        grid=(B, n_pairs, S // ATT_TILE),
        in_specs=[pl.BlockSpec((1, ATT_TILE, LANES), lambda b, hp, t: (b, t, hp)),
                  pl.BlockSpec((1, S, LANES), lambda b, hp, t: (b, 0, n_pairs + hp)),
                  pl.BlockSpec((1, S, LANES), lambda b, hp, t: (b, 0, 2 * n_pairs + hp)),
                  pl.BlockSpec((3, 2, BAND, 2 * BAND), lambda b, hp, t: (0, hp, 0, 0))],
        out_specs=pl.BlockSpec((1, ATT_TILE, LANES), lambda b, hp, t: (b, t, hp)),
        out_shape=jax.ShapeDtypeStruct((B, S, A_WIDTH), F32),
        scratch_shapes=[pltpu.VMEM((3, ATT_TILE, LANES), F32)] * 3,
        compiler_params=_params(("parallel", "parallel", "arbitrary")),
        name="attn",
    )(qkv3, qkv3, qkv3, bias)


def _attn_dec_kernel(qkv_ref, kt_ref, vt_ref, bias_ref, o_ref, *, nb):
    i = pl.program_id(0)
    N = qkv_ref.shape[-1]
    P = kt_ref.shape[-1]
    lane_n = lax.broadcasted_iota(jnp.int32, (A_HEAD_DIM, N), 1)
    lane_o = lax.broadcasted_iota(jnp.int32, (A_HEAD_DIM, nb), 1)
    lane_t = lax.broadcasted_iota(jnp.int32, (1, LANES), 1)

    def head(h, carry):
        slab = jnp.zeros((A_HEAD_DIM, nb), F32)
        for j in range(nb):
            pick = lane_n == i * nb + j
            col = lambda t: jnp.sum(jnp.where(pick, t, 0.0), axis=1, keepdims=True)
            q = col(qkv_ref[0, h]) * (A_HEAD_DIM ** -0.5)
            k_new, v_new = col(qkv_ref[1, h]), col(qkv_ref[2, h])
            s_new = jnp.sum(q * k_new, axis=0, keepdims=True)
            s = jnp.concatenate([jnp.sum(kt_ref[j, h] * q, axis=0, keepdims=True),
                                 jnp.where(lane_t == 0, s_new, 0.0)], axis=1)
            ps, ms, ls = [], [], []
            for br in range(3):
                sb = s + bias_ref[br, h]
                m = jnp.max(sb, axis=1, keepdims=True)
                p = jnp.exp(sb - m)
                ps.append(p)
                ms.append(m)
                ls.append(jnp.sum(p, axis=1, keepdims=True))
            mx = jnp.maximum(jnp.maximum(ms[0], ms[1]), ms[2])
            w = jnp.zeros((1, P + LANES), F32)
            den = jnp.zeros((1, 1), F32)
            for p, m, l in zip(ps, ms, ls):
                e = jnp.exp(m - mx)
                w = w + e * p
                den = den + e * l
            o = jnp.sum(vt_ref[j, h] * w[:, 0:P], axis=1, keepdims=True) + v_new * w[:, P:P + 1]
            slab = jnp.where(lane_o == j, o / den, slab)
        o_ref[0, h] = slab
        return carry

    lax.fori_loop(0, A_HEADS, head, 0)


def _attn_dec(qkv_t, cache_kt, cache_vt, bias, nb):
    N, P = cache_kt.shape[0], cache_kt.shape[-1]
    cache_spec = pl.BlockSpec((nb, A_HEADS, A_HEAD_DIM, P), lambda i: (i, 0, 0, 0))
    return pl.pallas_call(
        functools.partial(_attn_dec_kernel, nb=nb),
        grid=(N // nb,),
        in_specs=[pl.BlockSpec((3, A_HEADS, A_HEAD_DIM, N), lambda i: (0, 0, 0, 0)), cache_spec, cache_spec,
                  pl.BlockSpec((3, A_HEADS, 1, P + LANES), lambda i: (0, 0, 0, 0))],
        out_specs=pl.BlockSpec((1, A_HEADS, A_HEAD_DIM, nb), lambda i: (i, 0, 0, 0)),
        out_shape=jax.ShapeDtypeStruct((N // nb, A_HEADS, A_HEAD_DIM, nb), F32),
        compiler_params=_params(("parallel",)),
        name="attn_dec",
    )(qkv_t, cache_kt, cache_vt, bias)


def _split3(x):
    hi = x.astype(BF16)
    r = x - hi.astype(F32)
    mid = r.astype(BF16)
    return hi, mid, (r - mid.astype(F32)).astype(BF16)


def _tril_dot(tril_b, g):
    return sum(jnp.dot(tril_b, piece, preferred_element_type=F32) for piece in _split3(g))


def _dot3(a, b):
    ah, bh = a.astype(BF16), b.astype(BF16)
    al, bl = (a - ah.astype(F32)).astype(BF16), (b - bh.astype(F32)).astype(BF16)
    d = lambda x, y: jnp.dot(x, y, preferred_element_type=F32)
    return d(ah, bh) + (d(ah, bl) + d(al, bh))


def _delta_kernel(alog_ref, dtb_ref, ub_ref, z_ref, ab_ref, cw_ref, og_ref, o_ref, st_ref,
                  pad_ref, u_s, wq_s, kt_s, qk_s, gl_s):
    t = pl.program_id(1)
    TS = ub_ref.shape[1]
    nch = TS // CHUNK
    hdr = 8

    @pl.when(t == 0)
    def _():
        pad_ref[0:hdr, :] = jnp.zeros((hdr, CONV_DIM), F32)
        st_ref[...] = jnp.zeros(st_ref.shape, F32)

    @pl.when(t > 0)
    def _():
        pad_ref[0:hdr, :] = pad_ref[TS:TS + hdr, :]

    pad_ref[hdr:hdr + TS, :] = ub_ref[0]

    ri = lax.broadcasted_iota(jnp.int32, (CHUNK, CHUNK), 0)
    ci = lax.broadcasted_iota(jnp.int32, (CHUNK, CHUNK), 1)
    incl = ri >= ci
    strict = ri > ci
    tril_b = incl.astype(BF16)
    eye = (ri == ci).astype(F32)
    lane = lax.broadcasted_iota(jnp.int32, (CHUNK, LANES), 1)

    def local(c, carry):
        base = pl.multiple_of(c * CHUNK, CHUNK)
        ab = ab_ref[0, pl.ds(base, CHUNK), :]
        heads = []
        for h in range(B_HEADS):
            def conv(col):
                win = pad_ref[pl.ds(base, CHUNK + hdr), col:col + LANES]
                acc = win[hdr - 3:hdr - 3 + CHUNK] * cw_ref[0:1, col:col + LANES]
                for i in range(1, CONV_WIDTH):
                    acc = acc + win[hdr - 3 + i:hdr - 3 + i + CHUNK] * cw_ref[i:i + 1, col:col + LANES]
                return _silu(acc)

            cq, ck, v = conv(h * LANES), conv(B_WIDTH + h * LANES), conv(2 * B_WIDTH + h * LANES)
            q = cq * lax.rsqrt(jnp.sum(cq * cq, axis=1, keepdims=True) + 1e-6) * (B_HEAD_DIM ** -0.5)
            k = ck * lax.rsqrt(jnp.sum(ck * ck, axis=1, keepdims=True) + 1e-6)
            a_raw = jnp.sum(jnp.where(lane == h, ab, 0.0), axis=1, keepdims=True)
            b_raw = jnp.sum(jnp.where(lane == h + B_HEADS, ab, 0.0), axis=1, keepdims=True)
            neg_a = -jnp.exp(jnp.full((1, LANES), alog_ref[h], F32))
            g = neg_a * _softplus(a_raw + dtb_ref[h])
            beta = _sigmoid(b_raw)
            heads.append((q, k, v, g, beta))
        hs = range(B_HEADS)
        q, k, v, g, beta = zip(*heads)
        gc = [_tril_dot(tril_b, g[h]) for h in hs]
        dmat = [_tril_dot(tril_b, jnp.where(strict, g[h][:, 0:CHUNK], 0.0)) for h in hs]
        kq = [_bdot_nt(jnp.concatenate([k[h], q[h]], axis=0), k[h]) for h in hs]
        decay = [jnp.where(incl, jnp.exp(dmat[h]), 0.0) for h in hs]
        a = [jnp.where(strict, beta[h] * kq[h][0:CHUNK] * decay[h], 0.0) for h in hs]
        x = [eye - a[h] for h in hs]
        p = [_dot3(a[h], a[h]) for h in hs]
        for _ in range(int(math.log2(CHUNK)) - 2):
            r = [_dot3(jnp.concatenate([x[h], p[h]], axis=0), p[h]) for h in hs]
            x = [x[h] + r[h][0:CHUNK] for h in hs]
            p = [r[h][CHUNK:2 * CHUNK] for h in hs]
        x = [x[h] + _dot3(x[h], p[h]) for h in hs]
        e_gc = [jnp.exp(gc[h]) for h in hs]
        sol = [_bdot(x[h], jnp.concatenate([v[h] * beta[h], k[h] * (beta[h] * e_gc[h])], axis=1)) for h in hs]
        for h in hs:
            gc_last = gc[h][CHUNK - 1:CHUNK, :]
            u_s[h, c] = sol[h][:, 0:LANES]
            wq_s[h, c, 0:CHUNK] = sol[h][:, LANES:2 * LANES]
            wq_s[h, c, CHUNK:2 * CHUNK] = q[h] * e_gc[h]
            kt_s[h, c] = k[h] * jnp.exp(gc_last - gc[h])
            qk_s[h, c] = kq[h][CHUNK:2 * CHUNK] * decay[h]
            gl_s[h, c] = jnp.broadcast_to(jnp.exp(gc_last), (8, LANES))
        return carry

    lax.fori_loop(0, nch, local, 0)

    def scan(c, carry):
        base = pl.multiple_of(c * CHUNK, CHUNK)
        hs = range(B_HEADS)
        state = [st_ref[0, h] for h in hs]
        r = [_bdot(wq_s[h, c], state[h]) for h in hs]
        v_new = [u_s[h, c] - r[h][0:CHUNK] for h in hs]
        upd = [_bdot_tn(kt_s[h, c], v_new[h]) for h in hs]
        out = [r[h][CHUNK:2 * CHUNK] + _bdot(qk_s[h, c], v_new[h]) for h in hs]
        for h in hs:
            st_ref[0, h] = state[h] * gl_s[h, c][0:1, :] + upd[h]
            o = out[h] * lax.rsqrt(jnp.mean(out[h] * out[h], axis=1, keepdims=True) + RMS_EPS) * og_ref[...]
            cols = slice(h * LANES, (h + 1) * LANES)
            o_ref[0, pl.ds(base, CHUNK), cols] = o * _silu(z_ref[0, pl.ds(base, CHUNK), cols])
        return carry

    lax.fori_loop(0, nch, scan, 0)


def _delta(ub, zb, ab, conv_w, a_log, dt_bias, o_norm_g, B, S):
    ts = min(S, DELTA_TILE)
    nch = ts // CHUNK
    seq = lambda width: pl.BlockSpec((1, ts, width), lambda b, t, *_: (b, t, 0))
    fix = lambda shape: pl.BlockSpec(shape, lambda b, t, *_: (0, 0))
    per_chunk = lambda rows, width: pltpu.VMEM((B_HEADS, nch, rows, width), F32)
    grid_spec = pltpu.PrefetchScalarGridSpec(
        num_scalar_prefetch=2,
        grid=(B, S // ts),
        in_specs=[seq(CONV_DIM), seq(B_WIDTH), seq(LANES), fix((CONV_WIDTH, CONV_DIM)), fix((1, LANES))],
        out_specs=[seq(B_WIDTH),
                   pl.BlockSpec((1, B_HEADS, B_HEAD_DIM, B_HEAD_DIM), lambda b, t, *_: (b, 0, 0, 0))],
        scratch_shapes=[pltpu.VMEM((ts + 8, CONV_DIM), F32), per_chunk(CHUNK, LANES), per_chunk(2 * CHUNK, LANES),
                        per_chunk(CHUNK, LANES), per_chunk(CHUNK, CHUNK), per_chunk(8, LANES)],
    )
    return pl.pallas_call(
        _delta_kernel,
        grid_spec=grid_spec,
        out_shape=[jax.ShapeDtypeStruct((B, S, B_WIDTH), F32),
                   jax.ShapeDtypeStruct((B, B_HEADS, B_HEAD_DIM, B_HEAD_DIM), F32)],
        compiler_params=_params(("parallel", "arbitrary")),
        name="delta",
    )(a_log, dt_bias, ub.reshape(B, S, CONV_DIM), zb.reshape(B, S, B_WIDTH), ab.reshape(B, S, LANES),
      conv_w, o_norm_g.reshape(1, LANES))


def _delta_dec_kernel(alog_ref, dtb_ref, ub_ref, cs_ref, zb_ref, ab_ref, w_ref, og_ref, st_ref,
                      o_ref, so_ref, *, nb):
    i = pl.program_id(0)
    N = ub_ref.shape[0]
    acc = ub_ref[...] * w_ref[CONV_WIDTH - 1:CONV_WIDTH, :]
    for t in range(CONV_WIDTH - 1):
        acc = acc + cs_ref[t] * w_ref[t:t + 1, :]
    c = _silu(acc)
    ab = ab_ref[...]
    lane = lax.broadcasted_iota(jnp.int32, (N, LANES), 1)
    samp = lax.broadcasted_iota(jnp.int32, (B_HEAD_DIM, N), 1)
    row_id = lax.broadcasted_iota(jnp.int32, (N, LANES), 0)
    out_row = lax.broadcasted_iota(jnp.int32, (nb, LANES), 0)
    for h in range(B_HEADS):
        cq = c[:, h * LANES:(h + 1) * LANES]
        ck = c[:, B_WIDTH + h * LANES:B_WIDTH + (h + 1) * LANES]
        v = c[:, 2 * B_WIDTH + h * LANES:2 * B_WIDTH + (h + 1) * LANES]
        q = cq * lax.rsqrt(jnp.sum(cq * cq, axis=1, keepdims=True) + 1e-6) * (B_HEAD_DIM ** -0.5)
        k = ck * lax.rsqrt(jnp.sum(ck * ck, axis=1, keepdims=True) + 1e-6)
        a_raw = jnp.sum(jnp.where(lane == h, ab, 0.0), axis=1, keepdims=True)
        b_raw = jnp.sum(jnp.where(lane == h + B_HEADS, ab, 0.0), axis=1, keepdims=True)
        neg_a = -jnp.exp(jnp.full((1, 1), alog_ref[h], F32))
        dec = jnp.exp(neg_a * _softplus(a_raw + dtb_ref[h]))
        beta = _sigmoid(b_raw)
        q_t, k_t = q.T, k.T
        z = zb_ref[:, h * LANES:(h + 1) * LANES]

        def sample(j, o_acc, h=h, q_t=q_t, k_t=k_t, v=v, dec=dec, beta=beta, z=z):
            n = i * nb + j
            pick = samp == n
            k_col = jnp.sum(jnp.where(pick, k_t, 0.0), axis=1, keepdims=True)
            q_col = jnp.sum(jnp.where(pick, q_t, 0.0), axis=1, keepdims=True)
            pick_r = row_id == n
            row = lambda t: jnp.sum(jnp.where(pick_r, t, 0.0), axis=0, keepdims=True)
            st = st_ref[j, h] * row(jnp.broadcast_to(dec, (N, LANES)))
            mem = jnp.sum(k_col * st, axis=0, keepdims=True)
            st = st + k_col * ((row(v) - mem) * row(jnp.broadcast_to(beta, (N, LANES))))
            so_ref[j, h] = st
            o = jnp.sum(q_col * st, axis=0, keepdims=True)
            o = o * lax.rsqrt(jnp.mean(o * o, axis=1, keepdims=True) + RMS_EPS) * og_ref[...]
            return jnp.where(out_row == j, o * _silu(row(z)), o_acc)

        o_ref[:, h * LANES:(h + 1) * LANES] = lax.fori_loop(0, nb, sample, jnp.zeros((nb, LANES), F32))


def _delta_dec(ub_s, conv_state, zb_s, ab_s, conv_w, a_log, dt_bias, o_norm_g, state, nb):
    N = ub_s.shape[0]
    full2 = lambda shape: pl.BlockSpec(shape, lambda i, *_: (0, 0))
    grid_spec = pltpu.PrefetchScalarGridSpec(
        num_scalar_prefetch=2,
        grid=(N // nb,),
        in_specs=[full2((N, CONV_DIM)),
                  pl.BlockSpec((CONV_WIDTH - 1, N, CONV_DIM), lambda i, *_: (0, 0, 0)),
                  full2((N, B_WIDTH)), full2((N, LANES)), full2((CONV_WIDTH, CONV_DIM)), full2((1, LANES)),
                  pl.BlockSpec((nb, B_HEADS, B_HEAD_DIM, B_HEAD_DIM), lambda i, *_: (i, 0, 0, 0))],
        out_specs=[pl.BlockSpec((nb, B_WIDTH), lambda i, *_: (i, 0)),
                   pl.BlockSpec((nb, B_HEADS, B_HEAD_DIM, B_HEAD_DIM), lambda i, *_: (i, 0, 0, 0))],
    )
    return pl.pallas_call(
        functools.partial(_delta_dec_kernel, nb=nb),
        grid_spec=grid_spec,
        out_shape=[jax.ShapeDtypeStruct((N, B_WIDTH), F32), jax.ShapeDtypeStruct(state.shape, F32)],
        compiler_params=_params(("parallel",)),
        name="delta_dec",
    )(a_log, dt_bias, ub_s, jnp.swapaxes(conv_state, 0, 1), zb_s, ab_s, conv_w, o_norm_g.reshape(1, LANES), state)


def _layer_norm(r, g, b):
    mu = jnp.mean(r, axis=1, keepdims=True)
    d = r - mu
    var = jnp.mean(d * d, axis=1, keepdims=True)
    return d * lax.rsqrt(var + LN_EPS) * g + b


def _mix_ln_kernel(oa_ref, ob_ref, x_ref, wo_ref, g_ref, b_ref, wr_ref, br_ref, h_ref, route_ref, cnt_ref):
    if wo_ref.dtype == BF16:
        dot = lambda a, w: jnp.dot(a.astype(BF16), w, preferred_element_type=F32)
    else:
        dot = _fdot
    y = dot(oa_ref[...], wo_ref[0:A_WIDTH, :]) + dot(ob_ref[...], wo_ref[A_WIDTH:A_WIDTH + B_WIDTH, :])
    hcur = _layer_norm(DN_ALPHA * x_ref[...] + y, g_ref[...], b_ref[...])
    h_ref[...] = hcur
    logits = _fdot(hcur, wr_ref[...]) + br_ref[...]
    lane = lax.broadcasted_iota(jnp.int32, logits.shape, 1)
    lane_f = lane.astype(F32)
    ninf = -jnp.inf
    big = 1e9
    gl = jnp.where(lane < N_GROUPS, logits, ninf)
    gmax = jnp.max(gl, axis=1, keepdims=True)
    g_idx = jnp.min(jnp.where(gl == gmax, lane_f, big), axis=1, keepdims=True)
    p_group = 1.0 / jnp.sum(jnp.exp(gl - gmax), axis=1, keepdims=True)
    grp_of_lane = ((lane - N_GROUPS) >> 3).astype(F32)
    sel = (lane >= N_GROUPS) & (lane < N_GROUPS + N_EXPERTS) & (grp_of_lane == g_idx)
    el = jnp.where(sel, logits, ninf)
    v1 = jnp.max(el, axis=1, keepdims=True)
    i1 = jnp.min(jnp.where(el == v1, lane_f, big), axis=1, keepdims=True)
    el2 = jnp.where(lane_f == i1, ninf, el)
    v2 = jnp.max(el2, axis=1, keepdims=True)
    i2 = jnp.min(jnp.where(el2 == v2, lane_f, big), axis=1, keepdims=True)
    t = jnp.exp(v2 - v1)
    gate1 = p_group / (1.0 + t)
    gate2 = p_group * t / (1.0 + t)
    e1, e2 = i1 - N_GROUPS, i2 - N_GROUPS
    route_ref[...] = jnp.where(lane == 0, gate1, jnp.where(lane == 1, gate2, jnp.where(
        lane == 2, e1, jnp.where(lane == 3, e2, 0.0))))
    chosen = ((lane_f == e1) | (lane_f == e2)).astype(F32)
    tm = chosen.shape[0]
    cnt_ref[...] = jnp.sum(chosen.reshape(tm // ROUTE_TILE, ROUTE_TILE, LANES), axis=1)[:, None, :]


def _mix_ln(oa, ob, x2d, wo_b, g, b, wr, br, tm):
    T, D = x2d.shape
    row = lambda i: (i, 0)
    fix = lambda i: (0, 0)
    sub = tm // ROUTE_TILE
    return pl.pallas_call(
        _mix_ln_kernel,
        grid=(T // tm,),
        in_specs=[pl.BlockSpec((tm, A_WIDTH), row), pl.BlockSpec((tm, B_WIDTH), row), pl.BlockSpec((tm, D), row),
                  pl.BlockSpec((A_WIDTH + B_WIDTH, D), fix), pl.BlockSpec((1, D), fix), pl.BlockSpec((1, D), fix),
                  pl.BlockSpec((D, LANES), fix), pl.BlockSpec((1, LANES), fix)],
        out_specs=[pl.BlockSpec((tm, D), row), pl.BlockSpec((tm, LANES), row),
                   pl.BlockSpec((sub, 1, LANES), lambda i: (i, 0, 0))],
        out_shape=[jax.ShapeDtypeStruct((T, D), F32), jax.ShapeDtypeStruct((T, LANES), F32),
                   jax.ShapeDtypeStruct((T // ROUTE_TILE, 1, LANES), F32)],
        compiler_params=_params(("parallel",)),
        name="mix_ln",
    )(oa, ob, x2d, wo_b, g, b, wr, br)


def _slot_layout(counts):
    tiles = counts.shape[0]
    n_assign = tiles * ROUTE_TILE * TOP_K
    n_blocks = -(-(n_assign + N_EXPERTS * (MOE_BLK - 1)) // MOE_BLK)
    per_tile = counts.reshape(tiles, LANES).astype(jnp.int32)
    before = jnp.cumsum(per_tile, axis=0) - per_tile
    total = jnp.sum(per_tile, axis=0)[:N_EXPERTS]
    padded = (total + MOE_BLK - 1) // MOE_BLK * MOE_BLK
    pad_end = jnp.cumsum(padded)
    pad_start = pad_end - padded
    base = (before + jnp.pad(pad_start, (0, LANES - N_EXPERTS))[None, :]).astype(F32).reshape(tiles, 1, LANES)
    blk_start = jnp.arange(n_blocks, dtype=jnp.int32) * MOE_BLK
    blk_e = jnp.minimum(jnp.sum(pad_end[None, :] <= blk_start[:, None], axis=1), N_EXPERTS - 1).astype(jnp.int32)
    blk_n = jnp.clip(pad_start[blk_e] + total[blk_e] - blk_start, 0, MOE_BLK).astype(jnp.int32)
    last_e = blk_e[jnp.maximum(pad_end[-1] // MOE_BLK - 1, 0)]
    blk_e = jnp.where(blk_start < pad_end[-1], blk_e, last_e)
    return base, blk_e, blk_n, n_blocks


def _slot_kernel(route_ref, base_ref, dest_ref):
    route = route_ref[...]
    lane = lax.broadcasted_iota(jnp.int32, route.shape, 1)
    lane_f = lane.astype(F32)
    e1 = jnp.sum(jnp.where(lane == 2, route, 0.0), axis=1, keepdims=True)
    e2 = jnp.sum(jnp.where(lane == 3, route, 0.0), axis=1, keepdims=True)
    oh1, oh2 = lane_f == e1, lane_f == e2
    ri = lax.broadcasted_iota(jnp.int32, (ROUTE_TILE, ROUTE_TILE), 0)
    ci = lax.broadcasted_iota(jnp.int32, (ROUTE_TILE, ROUTE_TILE), 1)
    earlier = _bdot((ri > ci).astype(F32), (oh1 | oh2).astype(F32))
    slot = base_ref[0] + earlier
    d1 = jnp.sum(jnp.where(oh1, slot, 0.0), axis=1, keepdims=True)
    d2 = jnp.sum(jnp.where(oh2, slot, 0.0), axis=1, keepdims=True)
    dest_ref[...] = jnp.where(lane == 0, d1, jnp.where(lane == 1, d2, 0.0)).astype(jnp.int32)


def _slots(route, base):
    T = route.shape[0]
    return pl.pallas_call(
        _slot_kernel,
        grid=(T // ROUTE_TILE,),
        in_specs=[pl.BlockSpec((ROUTE_TILE, LANES), lambda i: (i, 0)), pl.BlockSpec((1, 1, LANES), lambda i: (i, 0, 0))],
        out_specs=pl.BlockSpec((ROUTE_TILE, LANES), lambda i: (i, 0)),
        out_shape=jax.ShapeDtypeStruct((T, LANES), jnp.int32),
        compiler_params=_params(("parallel",)),
        name="slots",
    )(route, base)


def _dispatch_kernel(dest_ref, h_ref, xs_in, xs_hbm, sem):
    del xs_in
    a0 = pl.program_id(0) * (ROUTE_TILE * TOP_K)

    def start(r, c):
        for k in range(TOP_K):
            pltpu.make_async_copy(h_ref.at[pl.ds(r, 1)], xs_hbm.at[pl.ds(dest_ref[a0 + r * TOP_K + k], 1)], sem).start()
        return c

    lax.fori_loop(0, ROUTE_TILE, start, 0, unroll=8)
    for k in range(TOP_K):
        pltpu.make_async_copy(h_ref, xs_hbm.at[pl.ds(0, ROUTE_TILE)], sem).wait()


def _dispatch(h_all, dest, slots):
    T, D = h_all.shape
    grid_spec = pltpu.PrefetchScalarGridSpec(
        num_scalar_prefetch=1,
        grid=(T // ROUTE_TILE,),
        in_specs=[pl.BlockSpec((ROUTE_TILE, D), lambda i, *_: (i, 0)), pl.BlockSpec(memory_space=pl.ANY)],
        out_specs=pl.BlockSpec(memory_space=pl.ANY),
        scratch_shapes=[pltpu.SemaphoreType.DMA],
    )
    return pl.pallas_call(
        _dispatch_kernel,
        grid_spec=grid_spec,
        out_shape=jax.ShapeDtypeStruct((slots, D), F32),
        input_output_aliases={2: 0},
        compiler_params=_params(("arbitrary",)),
        name="dispatch",
    )(dest, h_all, jnp.zeros((slots, D), F32))


def _moe_kernel(blk_e_ref, blk_n_ref, x_ref, wg_ref, wu_ref, wd_ref, y_ref):
    del blk_e_ref
    n_valid = blk_n_ref[pl.program_id(0)]

    @pl.when(n_valid > 0)
    def _():
        x = x_ref[...].astype(BF16)
        a = jnp.dot(x, wg_ref[0].astype(BF16), preferred_element_type=F32)
        u = jnp.dot(x, wu_ref[0].astype(BF16), preferred_element_type=F32)
        y_ref[...] = jnp.dot((_silu(a) * u).astype(BF16), wd_ref[0].astype(BF16), preferred_element_type=F32)

    @pl.when(n_valid == 0)
    def _():
        y_ref[...] = jnp.zeros(y_ref.shape, F32)


def _moe(xs, blk_e, blk_n, w_gate, w_up, w_down):
    slots, D = xs.shape
    De = w_gate.shape[-1]
    wspec = lambda shape: pl.BlockSpec((1,) + shape, lambda i, be, *_: (be[i], 0, 0))
    grid_spec = pltpu.PrefetchScalarGridSpec(
        num_scalar_prefetch=2,
        grid=(slots // MOE_BLK,),
        in_specs=[pl.BlockSpec((MOE_BLK, D), lambda i, *_: (i, 0)), wspec((D, De)), wspec((D, De)), wspec((De, D))],
        out_specs=pl.BlockSpec((MOE_BLK, D), lambda i, *_: (i, 0)),
    )
    return pl.pallas_call(
        _moe_kernel,
        grid_spec=grid_spec,
        out_shape=jax.ShapeDtypeStruct((slots, D), F32),
        compiler_params=_params(("arbitrary",)),
        name="moe",
    )(blk_e, blk_n, xs, w_gate, w_up, w_down)


def _final_ln_kernel(dest_ref, h_ref, route_ref, g_ref, b_ref, ys_hbm, o_ref, ybuf, sem, *, tile0):
    a0 = (tile0 + pl.program_id(0)) * (ROUTE_TILE * TOP_K)

    def start(r, c):
        for k in range(TOP_K):
            pltpu.make_async_copy(ys_hbm.at[pl.ds(dest_ref[a0 + r * TOP_K + k], 1)], ybuf.at[k, pl.ds(r, 1)], sem).start()
        return c

    lax.fori_loop(0, ROUTE_TILE, start, 0, unroll=8)
    for k in range(TOP_K):
        pltpu.make_async_copy(ys_hbm.at[pl.ds(0, ROUTE_TILE)], ybuf.at[k], sem).wait()
    route = route_ref[...]
    lane = lax.broadcasted_iota(jnp.int32, route.shape, 1)
    gate1 = jnp.sum(jnp.where(lane == 0, route, 0.0), axis=1, keepdims=True)
    gate2 = jnp.sum(jnp.where(lane == 1, route, 0.0), axis=1, keepdims=True)
    f = ybuf[0] * gate1 + ybuf[1] * gate2
    o_ref[...] = _layer_norm(DN_ALPHA * h_ref[...] + f, g_ref[...], b_ref[...])


def _final_ln(h_all, route, dest, ys, g, b, row0, rows):
    D = h_all.shape[1]
    tile0 = row0 // ROUTE_TILE
    row = lambda i, *_: (tile0 + i, 0)
    fix = lambda i, *_: (0, 0)
    grid_spec = pltpu.PrefetchScalarGridSpec(
        num_scalar_prefetch=1,
        grid=(rows // ROUTE_TILE,),
        in_specs=[pl.BlockSpec((ROUTE_TILE, D), row), pl.BlockSpec((ROUTE_TILE, LANES), row),
                  pl.BlockSpec((1, D), fix), pl.BlockSpec((1, D), fix), pl.BlockSpec(memory_space=pl.ANY)],
        out_specs=pl.BlockSpec((ROUTE_TILE, D), lambda i, *_: (i, 0)),
        scratch_shapes=[pltpu.VMEM((TOP_K, ROUTE_TILE, D), F32), pltpu.SemaphoreType.DMA],
    )
    return pl.pallas_call(
        functools.partial(_final_ln_kernel, tile0=tile0),
        grid_spec=grid_spec,
        out_shape=jax.ShapeDtypeStruct((rows, D), F32),
        compiler_params=_params(("arbitrary",)),
        name="final_ln",
    )(dest, h_all, route, g, b, ys)


def kernel(x_prompt, x_sample, cache_a_k, cache_a_v, state_b_ssm, state_b_conv, w_in, rel_bias, conv_w, a_log, dt_bias, o_norm_g, w_out, ln1_g, ln1_b, w_group, b_group, w_router, b_router, w_gate, w_up, w_down, ln2_g, ln2_b):
    B, S, D = x_prompt.shape
    N, T = x_sample.shape[0], x_sample.shape[1]
    depth = w_in.shape[0]
    assert depth == 1 and T == 1 and S % ATT_TILE == 0 and N % ROUTE_TILE == 0 and cache_a_k.shape[2] % LANES == 0
    l = 0
    win_p = min(BRANCHES[-1][0], S)

    w_pad32 = jnp.pad(w_in[l], ((0, 0), (0, IN_COLS_PAD - IN_COLS)))
    w_pad = w_pad32.astype(BF16)
    wo_b = w_out[l].astype(BF16)
    wr = jnp.pad(jnp.concatenate([w_group[l], w_router[l]], axis=1), ((0, 0), (0, LANES - N_GROUPS - N_EXPERTS)))
    br = jnp.pad(jnp.concatenate([b_group[l], b_router[l].reshape(-1)]), (0, LANES - N_GROUPS - N_EXPERTS))[None, :]
    g1, b1 = ln1_g[l][None, :], ln1_b[l][None, :]
    g2, b2 = ln2_g[l][None, :], ln2_b[l][None, :]

    xp = x_prompt.reshape(B * S, D)
    qkv_p, ub_p, zb_p, ab_p = _proj(xp, w_pad, 256)
    oa_p = _attn(qkv_p, _band_bias(rel_bias), B, S)
    ob_p, st_p = _delta(ub_p, zb_p, ab_p, conv_w[l], a_log[l], dt_bias[l], o_norm_g[l], B, S)
    h_p, route_p, cnt_p = _mix_ln(oa_p.reshape(B * S, A_WIDTH), ob_p.reshape(B * S, B_WIDTH), xp, wo_b, g1, b1,
                                  wr, br, 256)

    xs = x_sample.reshape(N, D)
    qkv_s, ub_s, zb_s, ab_s = _proj(xs, w_pad32, N)
    nb = 2
    qkv_t = jnp.transpose(qkv_s.reshape(N, 3, A_HEADS, A_HEAD_DIM), (1, 2, 3, 0))
    oa_s = _attn_dec(qkv_t, jnp.transpose(cache_a_k[l], (0, 2, 3, 1)), jnp.transpose(cache_a_v[l], (0, 2, 3, 1)),
                     _cache_bias(rel_bias, cache_a_k.shape[2]), nb)
    oa_s = jnp.transpose(oa_s, (0, 3, 1, 2)).reshape(N, A_WIDTH)
    ob_s, st_s = _delta_dec(ub_s, state_b_conv[l], zb_s, ab_s, conv_w[l], a_log[l], dt_bias[l], o_norm_g[l],
                            state_b_ssm[l], 8)
    h_s, route_s, cnt_s = _mix_ln(oa_s, ob_s, xs, w_out[l], g1, b1, wr, br, N)

    h_all = jnp.concatenate([h_p, h_s], axis=0)
    route = jnp.concatenate([route_p, route_s], axis=0)
    base, blk_e, blk_n, n_blocks = _slot_layout(jnp.concatenate([cnt_p, cnt_s], axis=0))
    dest = _slots(route, base)[:, 0:TOP_K].reshape(-1)
    ys = _moe(_dispatch(h_all, dest, n_blocks * MOE_BLK), blk_e, blk_n, w_gate[l], w_up[l], w_down[l])
    y_p = _final_ln(h_all, route, dest, ys, g2, b2, 0, B * S)
    y_s = _final_ln(h_all, route, dest, ys, g2, b2, B * S, N)

    w_kv_t = jnp.transpose(w_in[l][:, A_WIDTH:3 * A_WIDTH]).astype(BF16)
    k_win, v_win = _kv_win(x_prompt, w_kv_t, win_p, 512)
    to_rows = lambda t: jnp.transpose(t, (0, 3, 1, 2))[None]
    conv_p = ub_p.reshape(B, S, CONV_DIM)[:, S - (CONV_WIDTH - 1):]
    conv_s = jnp.concatenate([state_b_conv[l], ub_s[:, None, :]], axis=1)[:, T:]
    new_kv = lambda t: jnp.transpose(t, (2, 0, 1))[None, :, None]
    return (y_p.reshape(B, S, D), y_s.reshape(N, T, D), to_rows(k_win), to_rows(v_win),
            new_kv(qkv_t[1]), new_kv(qkv_t[2]), st_p[None], st_s[None], conv_p[None], conv_s[None])
```

```python
import functools
import math

import jax
import jax.numpy as jnp
from jax import lax
from jax.experimental import pallas as pl
from jax.experimental.pallas import tpu as pltpu

F32 = jnp.float32
BF16 = jnp.bfloat16
HIGHEST = lax.Precision.HIGHEST

LANES = 128
A_HEADS = 8
A_HEAD_DIM = 64
A_WIDTH = A_HEADS * A_HEAD_DIM
BRANCHES = ((128, 1), (512, 4), (2048, 16))
BAND = 128
ATT_TILE = BAND * 16
REL_BUCKETS = 32
REL_MAX_DIST = 2048
B_HEADS = 4
B_HEAD_DIM = 128
B_WIDTH = B_HEADS * B_HEAD_DIM
CONV_WIDTH = 4
CONV_DIM = 3 * B_WIDTH
CHUNK = 64
COL_UB = 3 * A_WIDTH
COL_ZB = COL_UB + CONV_DIM
COL_AB = COL_ZB + B_WIDTH
IN_COLS = COL_AB + 2 * B_HEADS
IN_COLS_PAD = COL_AB + LANES
N_GROUPS = 4
EXPERTS_PER_GROUP = 8
N_EXPERTS = N_GROUPS * EXPERTS_PER_GROUP
TOP_K = 2
DN_ALPHA = 2.0 ** 0.25
LN_EPS = 1e-5
RMS_EPS = 1e-6
MASKED = -1e30
MOE_BLK = 256
ROUTE_TILE = 128
DELTA_TILE = 1024
VMEM_LIMIT = 56 * 1024 * 1024


def _bdot(a, b):
    return jnp.dot(a.astype(BF16), b.astype(BF16), preferred_element_type=F32)


def _bdot_nt(a, b):
    return lax.dot_general(a.astype(BF16), b.astype(BF16), (((1,), (1,)), ((), ())), preferred_element_type=F32)


def _bdot_tn(a, b):
    return lax.dot_general(a.astype(BF16), b.astype(BF16), (((0,), (0,)), ((), ())), preferred_element_type=F32)


def _fdot(a, b):
    return jnp.dot(a, b, precision=HIGHEST, preferred_element_type=F32)


def _sigmoid(x):
    return 1.0 / (1.0 + jnp.exp(-x))


def _silu(x):
    return x * _sigmoid(x)


def _softplus(x):
    return jnp.maximum(x, 0.0) + jnp.log(1.0 + jnp.exp(-jnp.abs(x)))


def _params(sem):
    return pltpu.CompilerParams(dimension_semantics=sem, vmem_limit_bytes=VMEM_LIMIT)


def _proj_kernel(x_ref, w_ref, qkv_ref, ub_ref, zb_ref, ab_ref):
    if w_ref.dtype == BF16:
        xb = x_ref[...].astype(BF16)
        dot = lambda w: jnp.dot(xb, w, preferred_element_type=F32)
    else:
        dot = lambda w: _fdot(x_ref[...], w)
    qkv_ref[...] = dot(w_ref[:, 0:COL_UB])
    ub_ref[...] = dot(w_ref[:, COL_UB:COL_ZB])
    zb_ref[...] = dot(w_ref[:, COL_ZB:COL_AB])
    ab_ref[...] = dot(w_ref[:, COL_AB:IN_COLS_PAD])


def _proj(x2d, w_pad, tm):
    T, D = x2d.shape
    row = lambda i: (i, 0)
    return pl.pallas_call(
        _proj_kernel,
        grid=(T // tm,),
        in_specs=[pl.BlockSpec((tm, D), row), pl.BlockSpec((D, IN_COLS_PAD), lambda i: (0, 0))],
        out_specs=[pl.BlockSpec((tm, COL_UB), row), pl.BlockSpec((tm, CONV_DIM), row),
                   pl.BlockSpec((tm, B_WIDTH), row), pl.BlockSpec((tm, LANES), row)],
        out_shape=[jax.ShapeDtypeStruct((T, COL_UB), F32), jax.ShapeDtypeStruct((T, CONV_DIM), F32),
                   jax.ShapeDtypeStruct((T, B_WIDTH), F32), jax.ShapeDtypeStruct((T, LANES), F32)],
        compiler_params=_params(("parallel",)),
        name="proj",
    )(x2d, w_pad)


def _kv_win_kernel(x_ref, wt_ref, k_ref, v_ref):
    tm = x_ref.shape[1]
    kv = lax.dot_general(wt_ref[...], x_ref[0].astype(BF16), (((1,), (1,)), ((), ())), preferred_element_type=F32)
    k_ref[0] = kv[0:A_WIDTH].reshape(A_HEADS, A_HEAD_DIM, tm)
    v_ref[0] = kv[A_WIDTH:2 * A_WIDTH].reshape(A_HEADS, A_HEAD_DIM, tm)


def _kv_win(x_prompt, w_kv_t, win, tm):
    B, S, D = x_prompt.shape
    t0 = (S - win) // tm
    out = jax.ShapeDtypeStruct((B, A_HEADS, A_HEAD_DIM, win), F32)
    ospec = pl.BlockSpec((1, A_HEADS, A_HEAD_DIM, tm), lambda b, t: (b, 0, 0, t))
    return pl.pallas_call(
        _kv_win_kernel,
        grid=(B, win // tm),
        in_specs=[pl.BlockSpec((1, tm, D), lambda b, t: (b, t0 + t, 0)),
                  pl.BlockSpec((2 * A_WIDTH, D), lambda b, t: (0, 0))],
        out_specs=[ospec, ospec],
        out_shape=[out, out],
        compiler_params=_params(("parallel", "parallel")),
        name="kv_win",
    )(x_prompt, w_kv_t)


def _rel_bucket(dist):
    max_exact = REL_BUCKETS // 2
    n = jnp.maximum(dist, 0)
    ratio = jnp.maximum(n, 1).astype(F32) / max_exact
    large = max_exact + (jnp.log(ratio) / math.log(REL_MAX_DIST / max_exact)
                         * (REL_BUCKETS - max_exact)).astype(jnp.int32)
    return jnp.where(n < max_exact, n, jnp.minimum(large, REL_BUCKETS - 1))


def _bias_of(rel_bias, dist):
    onehot = (_rel_bucket(dist)[None, :] == jnp.arange(REL_BUCKETS)[:, None]).astype(F32)
    return jnp.dot(rel_bias.astype(F32).T, onehot, precision=HIGHEST)


def _band_bias(rel_bias):
    period = 3 * BAND
    tabs = []
    for _, dil in BRANCHES:
        g = jnp.concatenate([_bias_of(rel_bias, (BAND - jnp.arange(BAND + 1)) * dil),
                             jnp.full((A_HEADS, period - BAND - 1), MASKED, F32)], axis=1)
        skew = jnp.tile(g, (1, BAND))[:, :BAND * (period - 1)].reshape(A_HEADS, BAND, period - 1)
        tabs.append(skew[:, :, :2 * BAND])
    return jnp.stack(tabs)


def _cache_bias(rel_bias, P):
    dist = P - jnp.arange(P + LANES)
    tabs = []
    for window, dil in BRANCHES:
        ok = (dist >= 0) & (dist <= window) & (dist % dil == 0)
        tabs.append(jnp.where(ok[None, :], _bias_of(rel_bias, dist), MASKED)[:, None, :])
    return jnp.stack(tabs)


def _attn_kernel(q_ref, k_ref, v_ref, bias_ref, o_ref, acc_ref, m_ref, l_ref):
    t = pl.program_id(2)
    tile0 = t * ATT_TILE
    lane = lax.broadcasted_iota(jnp.int32, (BAND, LANES), 1)
    head0 = lane < A_HEAD_DIM
    col = lax.broadcasted_iota(jnp.int32, (BAND, 2 * BAND), 1)
    prev_cols = col < BAND

    def rows(start, dil):
        return pl.ds(pl.multiple_of(start, BAND), BAND) if dil == 1 else pl.ds(start, BAND, stride=dil)

    for br, (_, dil) in enumerate(BRANCHES):
        span = BAND * dil

        def block(i, carry, br=br, dil=dil, span=span):
            start = (i % dil) + (i // dil) * span
            cur = tile0 + start
            first = cur < span
            prev = jnp.where(first, cur, cur - span)
            q = q_ref[0, rows(start, dil), :] * (A_HEAD_DIM ** -0.5)
            kk = jnp.concatenate([k_ref[0, rows(prev, dil), :], k_ref[0, rows(cur, dil), :]], axis=0).astype(BF16)
            vv = jnp.concatenate([v_ref[0, rows(prev, dil), :], v_ref[0, rows(cur, dil), :]], axis=0).astype(BF16)
            pen = jnp.where(first, MASKED, 0.0)
            outs = []
            for hh in range(2):
                qh = jnp.where(head0 if hh == 0 else ~head0, q, 0.0).astype(BF16)
                s = lax.dot_general(qh, kk, (((1,), (1,)), ((), ())), preferred_element_type=F32)
                s = s + bias_ref[br, hh] + jnp.where(prev_cols, pen, 0.0)
                m = jnp.max(s, axis=1, keepdims=True)
                p = jnp.exp(s - m)
                l = jnp.sum(p, axis=1, keepdims=True)
                pv = jnp.dot(p.astype(BF16), vv, preferred_element_type=F32)
                outs.append((pv, m, l))
            acc_ref[br, rows(start, dil), :] = jnp.where(head0, outs[0][0], outs[1][0])
            m_ref[br, rows(start, dil), :] = jnp.where(head0, outs[0][1], outs[1][1])
            l_ref[br, rows(start, dil), :] = jnp.where(head0, outs[0][2], outs[1][2])
            return carry

        lax.fori_loop(0, ATT_TILE // BAND, block, 0, unroll=8)

    def merge(c, carry):
        r = pl.ds(pl.multiple_of(c * 256, 256), 256)
        m0, m1, m2 = m_ref[0, r, :], m_ref[1, r, :], m_ref[2, r, :]
        mx = jnp.maximum(jnp.maximum(m0, m1), m2)
        w0, w1, w2 = jnp.exp(m0 - mx), jnp.exp(m1 - mx), jnp.exp(m2 - mx)
        num = w0 * acc_ref[0, r, :] + w1 * acc_ref[1, r, :] + w2 * acc_ref[2, r, :]
        den = w0 * l_ref[0, r, :] + w1 * l_ref[1, r, :] + w2 * l_ref[2, r, :]
        o_ref[0, r, :] = num / den
        return carry

    lax.fori_loop(0, ATT_TILE // 256, merge, 0)


def _attn(qkv, bias, B, S):
    n_pairs = A_HEADS // 2
    qkv3 = qkv.reshape(B, S, 3 * A_WIDTH)
    return pl.pallas_call(
        _attn_kernel,
        grid=(B, n_pairs, S // ATT_TILE),
        in_specs=[pl.BlockSpec((1, ATT_TILE, LANES), lambda b, hp, t: (b, t, hp)),
                  pl.BlockSpec((1, S, LANES), lambda b, hp, t: (b, 0, n_pairs + hp)),
                  pl.BlockSpec((1, S, LANES), lambda b, hp, t: (b, 0, 2 * n_pairs + hp)),
                  pl.BlockSpec((3, 2, BAND, 2 * BAND), lambda b, hp, t: (0, hp, 0, 0))],
        out_specs=pl.BlockSpec((1, ATT_TILE, LANES), lambda b, hp, t: (b, t, hp)),
        out_shape=jax.ShapeDtypeStruct((B, S, A_WIDTH), F32),
        scratch_shapes=[pltpu.VMEM((3, ATT_TILE, LANES), F32)] * 3,
        compiler_params=_params(("parallel", "parallel", "arbitrary")),
        name="attn",
    )(qkv3, qkv3, qkv3, bias)


def _attn_dec_kernel(qkv_ref, kt_ref, vt_ref, bias_ref, o_ref, *, nb):
    i = pl.program_id(0)
    N = qkv_ref.shape[-1]
    P = kt_ref.shape[-1]
    lane_n = lax.broadcasted_iota(jnp.int32, (A_HEAD_DIM, N), 1)
    lane_o = lax.broadcasted_iota(jnp.int32, (A_HEAD_DIM, nb), 1)
    lane_t = lax.broadcasted_iota(jnp.int32, (1, LANES), 1)

    def head(h, carry):
        slab = jnp.zeros((A_HEAD_DIM, nb), F32)
        for j in range(nb):
            pick = lane_n == i * nb + j
            col = lambda t: jnp.sum(jnp.where(pick, t, 0.0), axis=1, keepdims=True)
            q = col(qkv_ref[0, h]) * (A_HEAD_DIM ** -0.5)
            k_new, v_new = col(qkv_ref[1, h]), col(qkv_ref[2, h])
            s_new = jnp.sum(q * k_new, axis=0, keepdims=True)
            s = jnp.concatenate([jnp.sum(kt_ref[j, h] * q, axis=0, keepdims=True),
                                 jnp.where(lane_t == 0, s_new, 0.0)], axis=1)
            ps, ms, ls = [], [], []
            for br in range(3):
                sb = s + bias_ref[br, h]
                m = jnp.max(sb, axis=1, keepdims=True)
                p = jnp.exp(sb - m)
                ps.append(p)
                ms.append(m)
                ls.append(jnp.sum(p, axis=1, keepdims=True))
            mx = jnp.maximum(jnp.maximum(ms[0], ms[1]), ms[2])
            w = jnp.zeros((1, P + LANES), F32)
            den = jnp.zeros((1, 1), F32)
            for p, m, l in zip(ps, ms, ls):
                e = jnp.exp(m - mx)
                w = w + e * p
                den = den + e * l
            o = jnp.sum(vt_ref[j, h] * w[:, 0:P], axis=1, keepdims=True) + v_new * w[:, P:P + 1]
            slab = jnp.where(lane_o == j, o / den, slab)
        o_ref[0, h] = slab
        return carry

    lax.fori_loop(0, A_HEADS, head, 0)


def _attn_dec(qkv_t, cache_kt, cache_vt, bias, nb):
    N, P = cache_kt.shape[0], cache_kt.shape[-1]
    cache_spec = pl.BlockSpec((nb, A_HEADS, A_HEAD_DIM, P), lambda i: (i, 0, 0, 0))
    return pl.pallas_call(
        functools.partial(_attn_dec_kernel, nb=nb),
        grid=(N // nb,),
        in_specs=[pl.BlockSpec((3, A_HEADS, A_HEAD_DIM, N), lambda i: (0, 0, 0, 0)), cache_spec, cache_spec,
                  pl.BlockSpec((3, A_HEADS, 1, P + LANES), lambda i: (0, 0, 0, 0))],
        out_specs=pl.BlockSpec((1, A_HEADS, A_HEAD_DIM, nb), lambda i: (i, 0, 0, 0)),
        out_shape=jax.ShapeDtypeStruct((N // nb, A_HEADS, A_HEAD_DIM, nb), F32),
        compiler_params=_params(("parallel",)),
        name="attn_dec",
    )(qkv_t, cache_kt, cache_vt, bias)


def _split3(x):
    hi = x.astype(BF16)
    r = x - hi.astype(F32)
    mid = r.astype(BF16)
    return hi, mid, (r - mid.astype(F32)).astype(BF16)


def _tril_dot(tril_b, g):
    return sum(jnp.dot(tril_b, piece, preferred_element_type=F32) for piece in _split3(g))


def _dot3(a, b):
    ah, bh = a.astype(BF16), b.astype(BF16)
    al, bl = (a - ah.astype(F32)).astype(BF16), (b - bh.astype(F32)).astype(BF16)
    d = lambda x, y: jnp.dot(x, y, preferred_element_type=F32)
    return d(ah, bh) + (d(ah, bl) + d(al, bh))


def _delta_kernel(alog_ref, dtb_ref, ub_ref, z_ref, ab_ref, cw_ref, og_ref, o_ref, st_ref,
                  pad_ref, u_s, wq_s, kt_s, qk_s, gl_s):
    t = pl.program_id(1)
    TS = ub_ref.shape[1]
    nch = TS // CHUNK
    hdr = 8

    @pl.when(t == 0)
    def _():
        pad_ref[0:hdr, :] = jnp.zeros((hdr, CONV_DIM), F32)
        st_ref[...] = jnp.zeros(st_ref.shape, F32)

    @pl.when(t > 0)
    def _():
        pad_ref[0:hdr, :] = pad_ref[TS:TS + hdr, :]

    pad_ref[hdr:hdr + TS, :] = ub_ref[0]

    ri = lax.broadcasted_iota(jnp.int32, (CHUNK, CHUNK), 0)
    ci = lax.broadcasted_iota(jnp.int32, (CHUNK, CHUNK), 1)
    incl = ri >= ci
    strict = ri > ci
    tril_b = incl.astype(BF16)
    eye = (ri == ci).astype(F32)
    lane = lax.broadcasted_iota(jnp.int32, (CHUNK, LANES), 1)

    def local(c, carry):
        base = pl.multiple_of(c * CHUNK, CHUNK)
        ab = ab_ref[0, pl.ds(base, CHUNK), :]
        heads = []
        for h in range(B_HEADS):
            def conv(col):
                win = pad_ref[pl.ds(base, CHUNK + hdr), col:col + LANES]
                acc = win[hdr - 3:hdr - 3 + CHUNK] * cw_ref[0:1, col:col + LANES]
                for i in range(1, CONV_WIDTH):
                    acc = acc + win[hdr - 3 + i:hdr - 3 + i + CHUNK] * cw_ref[i:i + 1, col:col + LANES]
                return _silu(acc)

            cq, ck, v = conv(h * LANES), conv(B_WIDTH + h * LANES), conv(2 * B_WIDTH + h * LANES)
            q = cq * lax.rsqrt(jnp.sum(cq * cq, axis=1, keepdims=True) + 1e-6) * (B_HEAD_DIM ** -0.5)
            k = ck * lax.rsqrt(jnp.sum(ck * ck, axis=1, keepdims=True) + 1e-6)
            a_raw = jnp.sum(jnp.where(lane == h, ab, 0.0), axis=1, keepdims=True)
            b_raw = jnp.sum(jnp.where(lane == h + B_HEADS, ab, 0.0), axis=1, keepdims=True)
            neg_a = -jnp.exp(jnp.full((1, LANES), alog_ref[h], F32))
            g = neg_a * _softplus(a_raw + dtb_ref[h])
            beta = _sigmoid(b_raw)
            heads.append((q, k, v, g, beta))
        hs = range(B_HEADS)
        q, k, v, g, beta = zip(*heads)
        gc = [_tril_dot(tril_b, g[h]) for h in hs]
        dmat = [_tril_dot(tril_b, jnp.where(strict, g[h][:, 0:CHUNK], 0.0)) for h in hs]
        kq = [_bdot_nt(jnp.concatenate([k[h], q[h]], axis=0), k[h]) for h in hs]
        decay = [jnp.where(incl, jnp.exp(dmat[h]), 0.0) for h in hs]
        a = [jnp.where(strict, beta[h] * kq[h][0:CHUNK] * decay[h], 0.0) for h in hs]
        x = [eye - a[h] for h in hs]
        p = [_bdot(a[h], a[h]) for h in hs]
        for _ in range(int(math.log2(CHUNK)) - 2):
            r = [_bdot(jnp.concatenate([x[h], p[h]], axis=0), p[h]) for h in hs]
            x = [x[h] + r[h][0:CHUNK] for h in hs]
            p = [r[h][CHUNK:2 * CHUNK] for h in hs]
        x = [x[h] + _bdot(x[h], p[h]) for h in hs]
        e_gc = [jnp.exp(gc[h]) for h in hs]
        sol = [_bdot(x[h], jnp.concatenate([v[h] * beta[h], k[h] * (beta[h] * e_gc[h])], axis=1)) for h in hs]
        for h in hs:
            gc_last = gc[h][CHUNK - 1:CHUNK, :]
            u_s[h, c] = sol[h][:, 0:LANES]
            wq_s[h, c, 0:CHUNK] = sol[h][:, LANES:2 * LANES]
            wq_s[h, c, CHUNK:2 * CHUNK] = q[h] * e_gc[h]
            kt_s[h, c] = k[h] * jnp.exp(gc_last - gc[h])
            qk_s[h, c] = kq[h][CHUNK:2 * CHUNK] * decay[h]
            gl_s[h, c] = jnp.broadcast_to(jnp.exp(gc_last), (8, LANES))
        return carry

    lax.fori_loop(0, nch, local, 0)

    def scan(c, carry):
        base = pl.multiple_of(c * CHUNK, CHUNK)
        hs = range(B_HEADS)
        state = [st_ref[0, h] for h in hs]
        r = [_bdot(wq_s[h, c], state[h]) for h in hs]
        v_new = [u_s[h, c] - r[h][0:CHUNK] for h in hs]
        upd = [_bdot_tn(kt_s[h, c], v_new[h]) for h in hs]
        out = [r[h][CHUNK:2 * CHUNK] + _bdot(qk_s[h, c], v_new[h]) for h in hs]
        for h in hs:
            st_ref[0, h] = state[h] * gl_s[h, c][0:1, :] + upd[h]
            o = out[h] * lax.rsqrt(jnp.mean(out[h] * out[h], axis=1, keepdims=True) + RMS_EPS) * og_ref[...]
            cols = slice(h * LANES, (h + 1) * LANES)
            o_ref[0, pl.ds(base, CHUNK), cols] = o * _silu(z_ref[0, pl.ds(base, CHUNK), cols])
        return carry

    lax.fori_loop(0, nch, scan, 0)


def _delta(ub, zb, ab, conv_w, a_log, dt_bias, o_norm_g, B, S):
    ts = min(S, DELTA_TILE)
    nch = ts // CHUNK
    seq = lambda width: pl.BlockSpec((1, ts, width), lambda b, t, *_: (b, t, 0))
    fix = lambda shape: pl.BlockSpec(shape, lambda b, t, *_: (0, 0))
    per_chunk = lambda rows, width: pltpu.VMEM((B_HEADS, nch, rows, width), F32)
    grid_spec = pltpu.PrefetchScalarGridSpec(
        num_scalar_prefetch=2,
        grid=(B, S // ts),
        in_specs=[seq(CONV_DIM), seq(B_WIDTH), seq(LANES), fix((CONV_WIDTH, CONV_DIM)), fix((1, LANES))],
        out_specs=[seq(B_WIDTH),
                   pl.BlockSpec((1, B_HEADS, B_HEAD_DIM, B_HEAD_DIM), lambda b, t, *_: (b, 0, 0, 0))],
        scratch_shapes=[pltpu.VMEM((ts + 8, CONV_DIM), F32), per_chunk(CHUNK, LANES), per_chunk(2 * CHUNK, LANES),
                        per_chunk(CHUNK, LANES), per_chunk(CHUNK, CHUNK), per_chunk(8, LANES)],
    )
    return pl.pallas_call(
        _delta_kernel,
        grid_spec=grid_spec,
        out_shape=[jax.ShapeDtypeStruct((B, S, B_WIDTH), F32),
                   jax.ShapeDtypeStruct((B, B_HEADS, B_HEAD_DIM, B_HEAD_DIM), F32)],
        compiler_params=_params(("parallel", "arbitrary")),
        name="delta",
    )(a_log, dt_bias, ub.reshape(B, S, CONV_DIM), zb.reshape(B, S, B_WIDTH), ab.reshape(B, S, LANES),
      conv_w, o_norm_g.reshape(1, LANES))


def _delta_dec_kernel(alog_ref, dtb_ref, ub_ref, cs_ref, zb_ref, ab_ref, w_ref, og_ref, st_ref,
                      o_ref, so_ref, *, nb):
    i = pl.program_id(0)
    N = ub_ref.shape[0]
    acc = ub_ref[...] * w_ref[CONV_WIDTH - 1:CONV_WIDTH, :]
    for t in range(CONV_WIDTH - 1):
        acc = acc + cs_ref[t] * w_ref[t:t + 1, :]
    c = _silu(acc)
    ab = ab_ref[...]
    lane = lax.broadcasted_iota(jnp.int32, (N, LANES), 1)
    samp = lax.broadcasted_iota(jnp.int32, (B_HEAD_DIM, N), 1)
    row_id = lax.broadcasted_iota(jnp.int32, (N, LANES), 0)
    out_row = lax.broadcasted_iota(jnp.int32, (nb, LANES), 0)
    for h in range(B_HEADS):
        cq = c[:, h * LANES:(h + 1) * LANES]
        ck = c[:, B_WIDTH + h * LANES:B_WIDTH + (h + 1) * LANES]
        v = c[:, 2 * B_WIDTH + h * LANES:2 * B_WIDTH + (h + 1) * LANES]
        q = cq * lax.rsqrt(jnp.sum(cq * cq, axis=1, keepdims=True) + 1e-6) * (B_HEAD_DIM ** -0.5)
        k = ck * lax.rsqrt(jnp.sum(ck * ck, axis=1, keepdims=True) + 1e-6)
        a_raw = jnp.sum(jnp.where(lane == h, ab, 0.0), axis=1, keepdims=True)
        b_raw = jnp.sum(jnp.where(lane == h + B_HEADS, ab, 0.0), axis=1, keepdims=True)
        neg_a = -jnp.exp(jnp.full((1, 1), alog_ref[h], F32))
        dec = jnp.exp(neg_a * _softplus(a_raw + dtb_ref[h]))
        beta = _sigmoid(b_raw)
        q_t, k_t = q.T, k.T
        z = zb_ref[:, h * LANES:(h + 1) * LANES]

        def sample(j, o_acc, h=h, q_t=q_t, k_t=k_t, v=v, dec=dec, beta=beta, z=z):
            n = i * nb + j
            pick = samp == n
            k_col = jnp.sum(jnp.where(pick, k_t, 0.0), axis=1, keepdims=True)
            q_col = jnp.sum(jnp.where(pick, q_t, 0.0), axis=1, keepdims=True)
            pick_r = row_id == n
            row = lambda t: jnp.sum(jnp.where(pick_r, t, 0.0), axis=0, keepdims=True)
            st = st_ref[j, h] * row(jnp.broadcast_to(dec, (N, LANES)))
            mem = jnp.sum(k_col * st, axis=0, keepdims=True)
            st = st + k_col * ((row(v) - mem) * row(jnp.broadcast_to(beta, (N, LANES))))
            so_ref[j, h] = st
            o = jnp.sum(q_col * st, axis=0, keepdims=True)
            o = o * lax.rsqrt(jnp.mean(o * o, axis=1, keepdims=True) + RMS_EPS) * og_ref[...]
            return jnp.where(out_row == j, o * _silu(row(z)), o_acc)

        o_ref[:, h * LANES:(h + 1) * LANES] = lax.fori_loop(0, nb, sample, jnp.zeros((nb, LANES), F32))


def _delta_dec(ub_s, conv_state, zb_s, ab_s, conv_w, a_log, dt_bias, o_norm_g, state, nb):
    N = ub_s.shape[0]
    full2 = lambda shape: pl.BlockSpec(shape, lambda i, *_: (0, 0))
    grid_spec = pltpu.PrefetchScalarGridSpec(
        num_scalar_prefetch=2,
        grid=(N // nb,),
        in_specs=[full2((N, CONV_DIM)),
                  pl.BlockSpec((CONV_WIDTH - 1, N, CONV_DIM), lambda i, *_: (0, 0, 0)),
                  full2((N, B_WIDTH)), full2((N, LANES)), full2((CONV_WIDTH, CONV_DIM)), full2((1, LANES)),
                  pl.BlockSpec((nb, B_HEADS, B_HEAD_DIM, B_HEAD_DIM), lambda i, *_: (i, 0, 0, 0))],
        out_specs=[pl.BlockSpec((nb, B_WIDTH), lambda i, *_: (i, 0)),
                   pl.BlockSpec((nb, B_HEADS, B_HEAD_DIM, B_HEAD_DIM), lambda i, *_: (i, 0, 0, 0))],
    )
    return pl.pallas_call(
        functools.partial(_delta_dec_kernel, nb=nb),
        grid_spec=grid_spec,
        out_shape=[jax.ShapeDtypeStruct((N, B_WIDTH), F32), jax.ShapeDtypeStruct(state.shape, F32)],
        compiler_params=_params(("parallel",)),
        name="delta_dec",
    )(a_log, dt_bias, ub_s, jnp.swapaxes(conv_state, 0, 1), zb_s, ab_s, conv_w, o_norm_g.reshape(1, LANES), state)


def _layer_norm(r, g, b):
    mu = jnp.mean(r, axis=1, keepdims=True)
    d = r - mu
    var = jnp.mean(d * d, axis=1, keepdims=True)
    return d * lax.rsqrt(var + LN_EPS) * g + b


def _mix_ln_kernel(oa_ref, ob_ref, x_ref, wo_ref, g_ref, b_ref, wr_ref, br_ref, h_ref, route_ref, cnt_ref):
    if wo_ref.dtype == BF16:
        dot = lambda a, w: jnp.dot(a.astype(BF16), w, preferred_element_type=F32)
    else:
        dot = _fdot
    y = dot(oa_ref[...], wo_ref[0:A_WIDTH, :]) + dot(ob_ref[...], wo_ref[A_WIDTH:A_WIDTH + B_WIDTH, :])
    hcur = _layer_norm(DN_ALPHA * x_ref[...] + y, g_ref[...], b_ref[...])
    h_ref[...] = hcur
    logits = _dot3(hcur, wr_ref[...]) + br_ref[...]
    lane = lax.broadcasted_iota(jnp.int32, logits.shape, 1)
    lane_f = lane.astype(F32)
    ninf = -jnp.inf
    big = 1e9
    gl = jnp.where(lane < N_GROUPS, logits, ninf)
    gmax = jnp.max(gl, axis=1, keepdims=True)
    g_idx = jnp.min(jnp.where(gl == gmax, lane_f, big), axis=1, keepdims=True)
    p_group = 1.0 / jnp.sum(jnp.exp(gl - gmax), axis=1, keepdims=True)
    grp_of_lane = ((lane - N_GROUPS) >> 3).astype(F32)
    sel = (lane >= N_GROUPS) & (lane < N_GROUPS + N_EXPERTS) & (grp_of_lane == g_idx)
    el = jnp.where(sel, logits, ninf)
    v1 = jnp.max(el, axis=1, keepdims=True)
    i1 = jnp.min(jnp.where(el == v1, lane_f, big), axis=1, keepdims=True)
    el2 = jnp.where(lane_f == i1, ninf, el)
    v2 = jnp.max(el2, axis=1, keepdims=True)
    i2 = jnp.min(jnp.where(el2 == v2, lane_f, big), axis=1, keepdims=True)
    t = jnp.exp(v2 - v1)
    gate1 = p_group / (1.0 + t)
    gate2 = p_group * t / (1.0 + t)
    e1, e2 = i1 - N_GROUPS, i2 - N_GROUPS
    route_ref[...] = jnp.where(lane == 0, gate1, jnp.where(lane == 1, gate2, jnp.where(
        lane == 2, e1, jnp.where(lane == 3, e2, 0.0))))
    chosen = ((lane_f == e1) | (lane_f == e2)).astype(F32)
    tm = chosen.shape[0]
    cnt_ref[...] = jnp.sum(chosen.reshape(tm // ROUTE_TILE, ROUTE_TILE, LANES), axis=1)[:, None, :]


def _mix_ln(oa, ob, x2d, wo_b, g, b, wr, br, tm):
    T, D = x2d.shape
    row = lambda i: (i, 0)
    fix = lambda i: (0, 0)
    sub = tm // ROUTE_TILE
    return pl.pallas_call(
        _mix_ln_kernel,
        grid=(T // tm,),
        in_specs=[pl.BlockSpec((tm, A_WIDTH), row), pl.BlockSpec((tm, B_WIDTH), row), pl.BlockSpec((tm, D), row),
                  pl.BlockSpec((A_WIDTH + B_WIDTH, D), fix), pl.BlockSpec((1, D), fix), pl.BlockSpec((1, D), fix),
                  pl.BlockSpec((D, LANES), fix), pl.BlockSpec((1, LANES), fix)],
        out_specs=[pl.BlockSpec((tm, D), row), pl.BlockSpec((tm, LANES), row),
                   pl.BlockSpec((sub, 1, LANES), lambda i: (i, 0, 0))],
        out_shape=[jax.ShapeDtypeStruct((T, D), F32), jax.ShapeDtypeStruct((T, LANES), F32),
                   jax.ShapeDtypeStruct((T // ROUTE_TILE, 1, LANES), F32)],
        compiler_params=_params(("parallel",)),
        name="mix_ln",
    )(oa, ob, x2d, wo_b, g, b, wr, br)


def _slot_layout(counts):
    tiles = counts.shape[0]
    n_assign = tiles * ROUTE_TILE * TOP_K
    n_blocks = -(-(n_assign + N_EXPERTS * (MOE_BLK - 1)) // MOE_BLK)
    per_tile = counts.reshape(tiles, LANES).astype(jnp.int32)
    before = jnp.cumsum(per_tile, axis=0) - per_tile
    total = jnp.sum(per_tile, axis=0)[:N_EXPERTS]
    padded = (total + MOE_BLK - 1) // MOE_BLK * MOE_BLK
    pad_end = jnp.cumsum(padded)
    pad_start = pad_end - padded
    base = (before + jnp.pad(pad_start, (0, LANES - N_EXPERTS))[None, :]).astype(F32).reshape(tiles, 1, LANES)
    blk_start = jnp.arange(n_blocks, dtype=jnp.int32) * MOE_BLK
    blk_e = jnp.minimum(jnp.sum(pad_end[None, :] <= blk_start[:, None], axis=1), N_EXPERTS - 1).astype(jnp.int32)
    blk_n = jnp.clip(pad_start[blk_e] + total[blk_e] - blk_start, 0, MOE_BLK).astype(jnp.int32)
    last_e = blk_e[jnp.maximum(pad_end[-1] // MOE_BLK - 1, 0)]
    blk_e = jnp.where(blk_start < pad_end[-1], blk_e, last_e)
    return base, blk_e, blk_n, n_blocks


def _slot_kernel(route_ref, base_ref, dest_ref):
    route = route_ref[...]
    lane = lax.broadcasted_iota(jnp.int32, route.shape, 1)
    lane_f = lane.astype(F32)
    e1 = jnp.sum(jnp.where(lane == 2, route, 0.0), axis=1, keepdims=True)
    e2 = jnp.sum(jnp.where(lane == 3, route, 0.0), axis=1, keepdims=True)
    oh1, oh2 = lane_f == e1, lane_f == e2
    ri = lax.broadcasted_iota(jnp.int32, (ROUTE_TILE, ROUTE_TILE), 0)
    ci = lax.broadcasted_iota(jnp.int32, (ROUTE_TILE, ROUTE_TILE), 1)
    earlier = _bdot((ri > ci).astype(F32), (oh1 | oh2).astype(F32))
    slot = base_ref[0] + earlier
    d1 = jnp.sum(jnp.where(oh1, slot, 0.0), axis=1, keepdims=True)
    d2 = jnp.sum(jnp.where(oh2, slot, 0.0), axis=1, keepdims=True)
    dest_ref[...] = jnp.where(lane == 0, d1, jnp.where(lane == 1, d2, 0.0)).astype(jnp.int32)


def _slots(route, base):
    T = route.shape[0]
    return pl.pallas_call(
        _slot_kernel,
        grid=(T // ROUTE_TILE,),
        in_specs=[pl.BlockSpec((ROUTE_TILE, LANES), lambda i: (i, 0)), pl.BlockSpec((1, 1, LANES), lambda i: (i, 0, 0))],
        out_specs=pl.BlockSpec((ROUTE_TILE, LANES), lambda i: (i, 0)),
        out_shape=jax.ShapeDtypeStruct((T, LANES), jnp.int32),
        compiler_params=_params(("parallel",)),
        name="slots",
    )(route, base)


def _dispatch_kernel(dest_ref, h_ref, xs_in, xs_hbm, sem):
    del xs_in
    a0 = pl.program_id(0) * (ROUTE_TILE * TOP_K)

    def start(r, c):
        for k in range(TOP_K):
            pltpu.make_async_copy(h_ref.at[pl.ds(r, 1)], xs_hbm.at[pl.ds(dest_ref[a0 + r * TOP_K + k], 1)], sem).start()
        return c

    lax.fori_loop(0, ROUTE_TILE, start, 0, unroll=8)
    for k in range(TOP_K):
        pltpu.make_async_copy(h_ref, xs_hbm.at[pl.ds(0, ROUTE_TILE)], sem).wait()


def _dispatch(h_all, dest, slots):
    T, D = h_all.shape
    grid_spec = pltpu.PrefetchScalarGridSpec(
        num_scalar_prefetch=1,
        grid=(T // ROUTE_TILE,),
        in_specs=[pl.BlockSpec((ROUTE_TILE, D), lambda i, *_: (i, 0)), pl.BlockSpec(memory_space=pl.ANY)],
        out_specs=pl.BlockSpec(memory_space=pl.ANY),
        scratch_shapes=[pltpu.SemaphoreType.DMA],
    )
    return pl.pallas_call(
        _dispatch_kernel,
        grid_spec=grid_spec,
        out_shape=jax.ShapeDtypeStruct((slots, D), F32),
        input_output_aliases={2: 0},
        compiler_params=_params(("arbitrary",)),
        name="dispatch",
    )(dest, h_all, jnp.zeros((slots, D), F32))


def _moe_kernel(blk_e_ref, blk_n_ref, x_ref, wg_ref, wu_ref, wd_ref, y_ref):
    del blk_e_ref
    n_valid = blk_n_ref[pl.program_id(0)]

    @pl.when(n_valid > 0)
    def _():
        x = x_ref[...].astype(BF16)
        a = jnp.dot(x, wg_ref[0].astype(BF16), preferred_element_type=F32)
        u = jnp.dot(x, wu_ref[0].astype(BF16), preferred_element_type=F32)
        y_ref[...] = jnp.dot((_silu(a) * u).astype(BF16), wd_ref[0].astype(BF16), preferred_element_type=F32)

    @pl.when(n_valid == 0)
    def _():
        y_ref[...] = jnp.zeros(y_ref.shape, F32)


def _moe(xs, blk_e, blk_n, w_gate, w_up, w_down):
    slots, D = xs.shape
    De = w_gate.shape[-1]
    wspec = lambda shape: pl.BlockSpec((1,) + shape, lambda i, be, *_: (be[i], 0, 0))
    grid_spec = pltpu.PrefetchScalarGridSpec(
        num_scalar_prefetch=2,
        grid=(slots // MOE_BLK,),
        in_specs=[pl.BlockSpec((MOE_BLK, D), lambda i, *_: (i, 0)), wspec((D, De)), wspec((D, De)), wspec((De, D))],
        out_specs=pl.BlockSpec((MOE_BLK, D), lambda i, *_: (i, 0)),
    )
    return pl.pallas_call(
        _moe_kernel,
        grid_spec=grid_spec,
        out_shape=jax.ShapeDtypeStruct((slots, D), F32),
        compiler_params=_params(("arbitrary",)),
        name="moe",
    )(blk_e, blk_n, xs, w_gate, w_up, w_down)


def _final_ln_kernel(dest_ref, h_ref, route_ref, g_ref, b_ref, ys_hbm, o_ref, ybuf, sem, *, tile0):
    a0 = (tile0 + pl.program_id(0)) * (ROUTE_TILE * TOP_K)

    def start(r, c):
        for k in range(TOP_K):
            pltpu.make_async_copy(ys_hbm.at[pl.ds(dest_ref[a0 + r * TOP_K + k], 1)], ybuf.at[k, pl.ds(r, 1)], sem).start()
        return c

    lax.fori_loop(0, ROUTE_TILE, start, 0, unroll=8)
    for k in range(TOP_K):
        pltpu.make_async_copy(ys_hbm.at[pl.ds(0, ROUTE_TILE)], ybuf.at[k], sem).wait()
    route = route_ref[...]
    lane = lax.broadcasted_iota(jnp.int32, route.shape, 1)
    gate1 = jnp.sum(jnp.where(lane == 0, route, 0.0), axis=1, keepdims=True)
    gate2 = jnp.sum(jnp.where(lane == 1, route, 0.0), axis=1, keepdims=True)
    f = ybuf[0] * gate1 + ybuf[1] * gate2
    o_ref[...] = _layer_norm(DN_ALPHA * h_ref[...] + f, g_ref[...], b_ref[...])


def _final_ln(h_all, route, dest, ys, g, b, row0, rows):
    D = h_all.shape[1]
    tile0 = row0 // ROUTE_TILE
    row = lambda i, *_: (tile0 + i, 0)
    fix = lambda i, *_: (0, 0)
    grid_spec = pltpu.PrefetchScalarGridSpec(
        num_scalar_prefetch=1,
        grid=(rows // ROUTE_TILE,),
        in_specs=[pl.BlockSpec((ROUTE_TILE, D), row), pl.BlockSpec((ROUTE_TILE, LANES), row),
                  pl.BlockSpec((1, D), fix), pl.BlockSpec((1, D), fix), pl.BlockSpec(memory_space=pl.ANY)],
        out_specs=pl.BlockSpec((ROUTE_TILE, D), lambda i, *_: (i, 0)),
        scratch_shapes=[pltpu.VMEM((TOP_K, ROUTE_TILE, D), F32), pltpu.SemaphoreType.DMA],
    )
    return pl.pallas_call(
        functools.partial(_final_ln_kernel, tile0=tile0),
        grid_spec=grid_spec,
        out_shape=jax.ShapeDtypeStruct((rows, D), F32),
        compiler_params=_params(("arbitrary",)),
        name="final_ln",
    )(dest, h_all, route, g, b, ys)


def kernel(x_prompt, x_sample, cache_a_k, cache_a_v, state_b_ssm, state_b_conv, w_in, rel_bias, conv_w, a_log, dt_bias, o_norm_g, w_out, ln1_g, ln1_b, w_group, b_group, w_router, b_router, w_gate, w_up, w_down, ln2_g, ln2_b):
    B, S, D = x_prompt.shape
    N, T = x_sample.shape[0], x_sample.shape[1]
    depth = w_in.shape[0]
    assert depth == 1 and T == 1 and S % ATT_TILE == 0 and N % ROUTE_TILE == 0 and cache_a_k.shape[2] % LANES == 0
    l = 0
    win_p = min(BRANCHES[-1][0], S)

    w_pad32 = jnp.pad(w_in[l], ((0, 0), (0, IN_COLS_PAD - IN_COLS)))
    w_pad = w_pad32.astype(BF16)
    wo_b = w_out[l].astype(BF16)
    wr = jnp.pad(jnp.concatenate([w_group[l], w_router[l]], axis=1), ((0, 0), (0, LANES - N_GROUPS - N_EXPERTS)))
    br = jnp.pad(jnp.concatenate([b_group[l], b_router[l].reshape(-1)]), (0, LANES - N_GROUPS - N_EXPERTS))[None, :]
    g1, b1 = ln1_g[l][None, :], ln1_b[l][None, :]
    g2, b2 = ln2_g[l][None, :], ln2_b[l][None, :]

    xp = x_prompt.reshape(B * S, D)
    qkv_p, ub_p, zb_p, ab_p = _proj(xp, w_pad, 512)
    oa_p = _attn(qkv_p, _band_bias(rel_bias), B, S)
    ob_p, st_p = _delta(ub_p, zb_p, ab_p, conv_w[l], a_log[l], dt_bias[l], o_norm_g[l], B, S)
    h_p, route_p, cnt_p = _mix_ln(oa_p.reshape(B * S, A_WIDTH), ob_p.reshape(B * S, B_WIDTH), xp, wo_b, g1, b1,
                                  wr, br, 512)

    xs = x_sample.reshape(N, D)
    qkv_s, ub_s, zb_s, ab_s = _proj(xs, w_pad32, N)
    nb = 2
    qkv_t = jnp.transpose(qkv_s.reshape(N, 3, A_HEADS, A_HEAD_DIM), (1, 2, 3, 0))
    oa_s = _attn_dec(qkv_t, jnp.transpose(cache_a_k[l], (0, 2, 3, 1)), jnp.transpose(cache_a_v[l], (0, 2, 3, 1)),
                     _cache_bias(rel_bias, cache_a_k.shape[2]), nb)
    oa_s = jnp.transpose(oa_s, (0, 3, 1, 2)).reshape(N, A_WIDTH)
    ob_s, st_s = _delta_dec(ub_s, state_b_conv[l], zb_s, ab_s, conv_w[l], a_log[l], dt_bias[l], o_norm_g[l],
                            state_b_ssm[l], 8)
    h_s, route_s, cnt_s = _mix_ln(oa_s, ob_s, xs, w_out[l], g1, b1, wr, br, N)

    h_all = jnp.concatenate([h_p, h_s], axis=0)
    route = jnp.concatenate([route_p, route_s], axis=0)
    base, blk_e, blk_n, n_blocks = _slot_layout(jnp.concatenate([cnt_p, cnt_s], axis=0))
    dest = _slots(route, base)[:, 0:TOP_K].reshape(-1)
    ys = _moe(_dispatch(h_all, dest, n_blocks * MOE_BLK), blk_e, blk_n, w_gate[l], w_up[l], w_down[l])
    y_p = _final_ln(h_all, route, dest, ys, g2, b2, 0, B * S)
    y_s = _final_ln(h_all, route, dest, ys, g2, b2, B * S, N)

    w_kv_t = jnp.transpose(w_in[l][:, A_WIDTH:3 * A_WIDTH]).astype(BF16)
    k_win, v_win = _kv_win(x_prompt, w_kv_t, win_p, 512)
    to_rows = lambda t: jnp.transpose(t, (0, 3, 1, 2))[None]
    conv_p = ub_p.reshape(B, S, CONV_DIM)[:, S - (CONV_WIDTH - 1):]
    conv_s = jnp.concatenate([state_b_conv[l], ub_s[:, None, :]], axis=1)[:, T:]
    new_kv = lambda t: jnp.transpose(t, (2, 0, 1))[None, :, None]
    return (y_p.reshape(B, S, D), y_s.reshape(N, T, D), to_rows(k_win), to_rows(v_win),
            new_kv(qkv_t[1]), new_kv(qkv_t[2]), st_p[None], st_s[None], conv_p[None], conv_s[None])
```

```python
import functools
import math

import jax
import jax.numpy as jnp
from jax import lax
from jax.experimental import pallas as pl
from jax.experimental.pallas import tpu as pltpu

F32 = jnp.float32
BF16 = jnp.bfloat16
HIGHEST = lax.Precision.HIGHEST

LANES = 128
A_HEADS = 8
A_HEAD_DIM = 64
A_WIDTH = A_HEADS * A_HEAD_DIM
BRANCHES = ((128, 1), (512, 4), (2048, 16))
BAND = 128
ATT_TILE = BAND * 16
REL_BUCKETS = 32
REL_MAX_DIST = 2048
B_HEADS = 4
B_HEAD_DIM = 128
B_WIDTH = B_HEADS * B_HEAD_DIM
CONV_WIDTH = 4
CONV_DIM = 3 * B_WIDTH
CHUNK = 64
COL_UB = 3 * A_WIDTH
COL_ZB = COL_UB + CONV_DIM
COL_AB = COL_ZB + B_WIDTH
IN_COLS = COL_AB + 2 * B_HEADS
IN_COLS_PAD = COL_AB + LANES
N_GROUPS = 4
EXPERTS_PER_GROUP = 8
N_EXPERTS = N_GROUPS * EXPERTS_PER_GROUP
TOP_K = 2
DN_ALPHA = 2.0 ** 0.25
LN_EPS = 1e-5
RMS_EPS = 1e-6
MASKED = -1e30
MOE_BLK = 256
ROUTE_TILE = 128
DELTA_TILE = 1024
VMEM_LIMIT = 56 * 1024 * 1024


def _bdot(a, b):
    return jnp.dot(a.astype(BF16), b.astype(BF16), preferred_element_type=F32)


def _bdot_nt(a, b):
    return lax.dot_general(a.astype(BF16), b.astype(BF16), (((1,), (1,)), ((), ())), preferred_element_type=F32)


def _bdot_tn(a, b):
    return lax.dot_general(a.astype(BF16), b.astype(BF16), (((0,), (0,)), ((), ())), preferred_element_type=F32)


def _fdot(a, b):
    return jnp.dot(a, b, precision=HIGHEST, preferred_element_type=F32)


def _sigmoid(x):
    return 1.0 / (1.0 + jnp.exp(-x))


def _silu(x):
    return x * _sigmoid(x)


def _softplus(x):
    return jnp.maximum(x, 0.0) + jnp.log(1.0 + jnp.exp(-jnp.abs(x)))


def _params(sem):
    return pltpu.CompilerParams(dimension_semantics=sem, vmem_limit_bytes=VMEM_LIMIT)


def _proj_kernel(x_ref, w_ref, qkv_ref, ub_ref, zb_ref, ab_ref):
    if w_ref.dtype == BF16:
        xb = x_ref[...].astype(BF16)
        dot = lambda w: jnp.dot(xb, w, preferred_element_type=F32)
    else:
        dot = lambda w: _fdot(x_ref[...], w)
    qkv_ref[...] = dot(w_ref[:, 0:COL_UB])
    ub_ref[...] = dot(w_ref[:, COL_UB:COL_ZB])
    zb_ref[...] = dot(w_ref[:, COL_ZB:COL_AB])
    ab_ref[...] = dot(w_ref[:, COL_AB:IN_COLS_PAD])


def _proj(x2d, w_pad, tm):
    T, D = x2d.shape
    row = lambda i: (i, 0)
    return pl.pallas_call(
        _proj_kernel,
        grid=(T // tm,),
        in_specs=[pl.BlockSpec((tm, D), row), pl.BlockSpec((D, IN_COLS_PAD), lambda i: (0, 0))],
        out_specs=[pl.BlockSpec((tm, COL_UB), row), pl.BlockSpec((tm, CONV_DIM), row),
                   pl.BlockSpec((tm, B_WIDTH), row), pl.BlockSpec((tm, LANES), row)],
        out_shape=[jax.ShapeDtypeStruct((T, COL_UB), F32), jax.ShapeDtypeStruct((T, CONV_DIM), F32),
                   jax.ShapeDtypeStruct((T, B_WIDTH), F32), jax.ShapeDtypeStruct((T, LANES), F32)],
        compiler_params=_params(("parallel",)),
        name="proj",
    )(x2d, w_pad)


def _kv_win_kernel(x_ref, wt_ref, k_ref, v_ref):
    tm = x_ref.shape[1]
    kv = lax.dot_general(wt_ref[...], x_ref[0].astype(BF16), (((1,), (1,)), ((), ())), preferred_element_type=F32)
    k_ref[0] = kv[0:A_WIDTH].reshape(A_HEADS, A_HEAD_DIM, tm)
    v_ref[0] = kv[A_WIDTH:2 * A_WIDTH].reshape(A_HEADS, A_HEAD_DIM, tm)


def _kv_win(x_prompt, w_kv_t, win, tm):
    B, S, D = x_prompt.shape
    t0 = (S - win) // tm
    out = jax.ShapeDtypeStruct((B, A_HEADS, A_HEAD_DIM, win), F32)
    ospec = pl.BlockSpec((1, A_HEADS, A_HEAD_DIM, tm), lambda b, t: (b, 0, 0, t))
    return pl.pallas_call(
        _kv_win_kernel,
        grid=(B, win // tm),
        in_specs=[pl.BlockSpec((1, tm, D), lambda b, t: (b, t0 + t, 0)),
                  pl.BlockSpec((2 * A_WIDTH, D), lambda b, t: (0, 0))],
        out_specs=[ospec, ospec],
        out_shape=[out, out],
        compiler_params=_params(("parallel", "parallel")),
        name="kv_win",
    )(x_prompt, w_kv_t)


def _rel_bucket(dist):
    max_exact = REL_BUCKETS // 2
    n = jnp.maximum(dist, 0)
    ratio = jnp.maximum(n, 1).astype(F32) / max_exact
    large = max_exact + (jnp.log(ratio) / math.log(REL_MAX_DIST / max_exact)
                         * (REL_BUCKETS - max_exact)).astype(jnp.int32)
    return jnp.where(n < max_exact, n, jnp.minimum(large, REL_BUCKETS - 1))


def _bias_of(rel_bias, dist):
    onehot = (_rel_bucket(dist)[None, :] == jnp.arange(REL_BUCKETS)[:, None]).astype(F32)
    return jnp.dot(rel_bias.astype(F32).T, onehot, precision=HIGHEST)


def _band_bias(rel_bias):
    period = 3 * BAND
    tabs = []
    for _, dil in BRANCHES:
        g = jnp.concatenate([_bias_of(rel_bias, (BAND - jnp.arange(BAND + 1)) * dil),
                             jnp.full((A_HEADS, period - BAND - 1), MASKED, F32)], axis=1)
        skew = jnp.tile(g, (1, BAND))[:, :BAND * (period - 1)].reshape(A_HEADS, BAND, period - 1)
        tabs.append(skew[:, :, :2 * BAND])
    return jnp.stack(tabs)


def _cache_bias(rel_bias, P):
    dist = P - jnp.arange(P + LANES)
    tabs = []
    for window, dil in BRANCHES:
        ok = (dist >= 0) & (dist <= window) & (dist % dil == 0)
        tabs.append(jnp.where(ok[None, :], _bias_of(rel_bias, dist), MASKED)[:, None, :])
    return jnp.stack(tabs)


def _attn_kernel(q_ref, k_ref, v_ref, bias_ref, o_ref, acc_ref, m_ref, l_ref):
    t = pl.program_id(2)
    tile0 = t * ATT_TILE
    lane = lax.broadcasted_iota(jnp.int32, (BAND, LANES), 1)
    head0 = lane < A_HEAD_DIM
    col = lax.broadcasted_iota(jnp.int32, (BAND, 2 * BAND), 1)
    prev_cols = col < BAND

    def rows(start, dil):
        return pl.ds(pl.multiple_of(start, BAND), BAND) if dil == 1 else pl.ds(start, BAND, stride=dil)

    for br, (_, dil) in enumerate(BRANCHES):
        span = BAND * dil

        def block(i, carry, br=br, dil=dil, span=span):
            start = (i % dil) + (i // dil) * span
            cur = tile0 + start
            first = cur < span
            prev = jnp.where(first, cur, cur - span)
            q = q_ref[0, rows(start, dil), :] * (A_HEAD_DIM ** -0.5)
            kk = jnp.concatenate([k_ref[0, rows(prev, dil), :], k_ref[0, rows(cur, dil), :]], axis=0).astype(BF16)
            vv = jnp.concatenate([v_ref[0, rows(prev, dil), :], v_ref[0, rows(cur, dil), :]], axis=0).astype(BF16)
            pen = jnp.where(first, MASKED, 0.0)
            outs = []
            for hh in range(2):
                qh = jnp.where(head0 if hh == 0 else ~head0, q, 0.0).astype(BF16)
                s = lax.dot_general(qh, kk, (((1,), (1,)), ((), ())), preferred_element_type=F32)
                s = s + bias_ref[br, hh] + jnp.where(prev_cols, pen, 0.0)
                m = jnp.max(s, axis=1, keepdims=True)
                p = jnp.exp(s - m)
                l = jnp.sum(p, axis=1, keepdims=True)
                pv = jnp.dot(p.astype(BF16), vv, preferred_element_type=F32)
                outs.append((pv, m, l))
            acc_ref[br, rows(start, dil), :] = jnp.where(head0, outs[0][0], outs[1][0])
            m_ref[br, rows(start, dil), :] = jnp.where(head0, outs[0][1], outs[1][1])
            l_ref[br, rows(start, dil), :] = jnp.where(head0, outs[0][2], outs[1][2])
            return carry

        lax.fori_loop(0, ATT_TILE // BAND, block, 0, unroll=8)

    def merge(c, carry):
        r = pl.ds(pl.multiple_of(c * 256, 256), 256)
        m0, m1, m2 = m_ref[0, r, :], m_ref[1, r, :], m_ref[2, r, :]
        mx = jnp.maximum(jnp.maximum(m0, m1), m2)
        w0, w1, w2 = jnp.exp(m0 - mx), jnp.exp(m1 - mx), jnp.exp(m2 - mx)
        num = w0 * acc_ref[0, r, :] + w1 * acc_ref[1, r, :] + w2 * acc_ref[2, r, :]
        den = w0 * l_ref[0, r, :] + w1 * l_ref[1, r, :] + w2 * l_ref[2, r, :]
        o_ref[0, r, :] = num / den
        return carry

    lax.fori_loop(0, ATT_TILE // 256, merge, 0)


def _attn(qkv, bias, B, S):
    n_pairs = A_HEADS // 2
    qkv3 = qkv.reshape(B, S, 3 * A_WIDTH)
    return pl.pallas_call(
        _attn_kernel,
        grid=(B, n_pairs, S // ATT_TILE),
        in_specs=[pl.BlockSpec((1, ATT_TILE, LANES), lambda b, hp, t: (b, t, hp)),
                  pl.BlockSpec((1, S, LANES), lambda b, hp, t: (b, 0, n_pairs + hp)),
                  pl.BlockSpec((1, S, LANES), lambda b, hp, t: (b, 0, 2 * n_pairs + hp)),
                  pl.BlockSpec((3, 2, BAND, 2 * BAND), lambda b, hp, t: (0, hp, 0, 0))],
        out_specs=pl.BlockSpec((1, ATT_TILE, LANES), lambda b, hp, t: (b, t, hp)),
        out_shape=jax.ShapeDtypeStruct((B, S, A_WIDTH), F32),
        scratch_shapes=[pltpu.VMEM((3, ATT_TILE, LANES), F32)] * 3,
        compiler_params=_params(("parallel", "parallel", "arbitrary")),
        name="attn",
    )(qkv3, qkv3, qkv3, bias)


def _attn_dec_kernel(qkv_ref, kt_ref, vt_ref, bias_ref, o_ref, *, nb):
    i = pl.program_id(0)
    N = qkv_ref.shape[-1]
    P = kt_ref.shape[-1]
    lane_n = lax.broadcasted_iota(jnp.int32, (A_HEAD_DIM, N), 1)
    lane_o = lax.broadcasted_iota(jnp.int32, (A_HEAD_DIM, nb), 1)
    lane_t = lax.broadcasted_iota(jnp.int32, (1, LANES), 1)

    def head(h, carry):
        slab = jnp.zeros((A_HEAD_DIM, nb), F32)
        for j in range(nb):
            pick = lane_n == i * nb + j
            col = lambda t: jnp.sum(jnp.where(pick, t, 0.0), axis=1, keepdims=True)
            q = col(qkv_ref[0, h]) * (A_HEAD_DIM ** -0.5)
            k_new, v_new = col(qkv_ref[1, h]), col(qkv_ref[2, h])
            s_new = jnp.sum(q * k_new, axis=0, keepdims=True)
            s = jnp.concatenate([jnp.sum(kt_ref[j, h] * q, axis=0, keepdims=True),
                                 jnp.where(lane_t == 0, s_new, 0.0)], axis=1)
            ps, ms, ls = [], [], []
            for br in range(3):
                sb = s + bias_ref[br, h]
                m = jnp.max(sb, axis=1, keepdims=True)
                p = jnp.exp(sb - m)
                ps.append(p)
                ms.append(m)
                ls.append(jnp.sum(p, axis=1, keepdims=True))
            mx = jnp.maximum(jnp.maximum(ms[0], ms[1]), ms[2])
            w = jnp.zeros((1, P + LANES), F32)
            den = jnp.zeros((1, 1), F32)
            for p, m, l in zip(ps, ms, ls):
                e = jnp.exp(m - mx)
                w = w + e * p
                den = den + e * l
            o = jnp.sum(vt_ref[j, h] * w[:, 0:P], axis=1, keepdims=True) + v_new * w[:, P:P + 1]
            slab = jnp.where(lane_o == j, o / den, slab)
        o_ref[0, h] = slab
        return carry

    lax.fori_loop(0, A_HEADS, head, 0)


def _attn_dec(qkv_t, cache_kt, cache_vt, bias, nb):
    N, P = cache_kt.shape[0], cache_kt.shape[-1]
    cache_spec = pl.BlockSpec((nb, A_HEADS, A_HEAD_DIM, P), lambda i: (i, 0, 0, 0))
    return pl.pallas_call(
        functools.partial(_attn_dec_kernel, nb=nb),
        grid=(N // nb,),
        in_specs=[pl.BlockSpec((3, A_HEADS, A_HEAD_DIM, N), lambda i: (0, 0, 0, 0)), cache_spec, cache_spec,
                  pl.BlockSpec((3, A_HEADS, 1, P + LANES), lambda i: (0, 0, 0, 0))],
        out_specs=pl.BlockSpec((1, A_HEADS, A_HEAD_DIM, nb), lambda i: (i, 0, 0, 0)),
        out_shape=jax.ShapeDtypeStruct((N // nb, A_HEADS, A_HEAD_DIM, nb), F32),
        compiler_params=_params(("parallel",)),
        name="attn_dec",
    )(qkv_t, cache_kt, cache_vt, bias)


def _split3(x):
    hi = x.astype(BF16)
    r = x - hi.astype(F32)
    mid = r.astype(BF16)
    return hi, mid, (r - mid.astype(F32)).astype(BF16)


def _tril_dot(tril_b, g):
    return sum(jnp.dot(tril_b, piece, preferred_element_type=F32) for piece in _split3(g))


def _dot3(a, b):
    ah, bh = a.astype(BF16), b.astype(BF16)
    al, bl = (a - ah.astype(F32)).astype(BF16), (b - bh.astype(F32)).astype(BF16)
    d = lambda x, y: jnp.dot(x, y, preferred_element_type=F32)
    return d(ah, bh) + (d(ah, bl) + d(al, bh))


def _delta_kernel(alog_ref, dtb_ref, ub_ref, z_ref, ab_ref, cw_ref, og_ref, o_ref, st_ref,
                  pad_ref, u_s, wq_s, kt_s, qk_s, gl_s):
    t = pl.program_id(1)
    TS = ub_ref.shape[1]
    nch = TS // CHUNK
    hdr = 8

    @pl.when(t == 0)
    def _():
        pad_ref[0:hdr, :] = jnp.zeros((hdr, CONV_DIM), F32)
        st_ref[...] = jnp.zeros(st_ref.shape, F32)

    @pl.when(t > 0)
    def _():
        pad_ref[0:hdr, :] = pad_ref[TS:TS + hdr, :]

    pad_ref[hdr:hdr + TS, :] = ub_ref[0]

    ri = lax.broadcasted_iota(jnp.int32, (CHUNK, CHUNK), 0)
    ci = lax.broadcasted_iota(jnp.int32, (CHUNK, CHUNK), 1)
    incl = ri >= ci
    strict = ri > ci
    tril_b = incl.astype(BF16)
    eye = (ri == ci).astype(F32)
    lane = lax.broadcasted_iota(jnp.int32, (CHUNK, LANES), 1)

    def local(c, carry):
        base = pl.multiple_of(c * CHUNK, CHUNK)
        ab = ab_ref[0, pl.ds(base, CHUNK), :]
        heads = []
        for h in range(B_HEADS):
            def conv(col):
                win = pad_ref[pl.ds(base, CHUNK + hdr), col:col + LANES]
                acc = win[hdr - 3:hdr - 3 + CHUNK] * cw_ref[0:1, col:col + LANES]
                for i in range(1, CONV_WIDTH):
                    acc = acc + win[hdr - 3 + i:hdr - 3 + i + CHUNK] * cw_ref[i:i + 1, col:col + LANES]
                return _silu(acc)

            cq, ck, v = conv(h * LANES), conv(B_WIDTH + h * LANES), conv(2 * B_WIDTH + h * LANES)
            q = cq * lax.rsqrt(jnp.sum(cq * cq, axis=1, keepdims=True) + 1e-6) * (B_HEAD_DIM ** -0.5)
            k = ck * lax.rsqrt(jnp.sum(ck * ck, axis=1, keepdims=True) + 1e-6)
            a_raw = jnp.sum(jnp.where(lane == h, ab, 0.0), axis=1, keepdims=True)
            b_raw = jnp.sum(jnp.where(lane == h + B_HEADS, ab, 0.0), axis=1, keepdims=True)
            neg_a = -jnp.exp(jnp.full((1, LANES), alog_ref[h], F32))
            g = neg_a * _softplus(a_raw + dtb_ref[h])
            beta = _sigmoid(b_raw)
            heads.append((q, k, v, g, beta))
        hs = range(B_HEADS)
        q, k, v, g, beta = zip(*heads)
        gc = [_tril_dot(tril_b, g[h]) for h in hs]
        dmat = [_tril_dot(tril_b, jnp.where(strict, g[h][:, 0:CHUNK], 0.0)) for h in hs]
        kq = [_bdot_nt(jnp.concatenate([k[h], q[h]], axis=0), k[h]) for h in hs]
        decay = [jnp.where(incl, jnp.exp(dmat[h]), 0.0) for h in hs]
        a = [jnp.where(strict, beta[h] * kq[h][0:CHUNK] * decay[h], 0.0) for h in hs]
        x = [eye - a[h] for h in hs]
        p = [_bdot(a[h], a[h]) for h in hs]
        for _ in range(int(math.log2(CHUNK)) - 2):
            r = [_bdot(jnp.concatenate([x[h], p[h]], axis=0), p[h]) for h in hs]
            x = [x[h] + r[h][0:CHUNK] for h in hs]
            p = [r[h][CHUNK:2 * CHUNK] for h in hs]
        x = [x[h] + _bdot(x[h], p[h]) for h in hs]
        e_gc = [jnp.exp(gc[h]) for h in hs]
        sol = [_bdot(x[h], jnp.concatenate([v[h] * beta[h], k[h] * (beta[h] * e_gc[h])], axis=1)) for h in hs]
        for h in hs:
            gc_last = gc[h][CHUNK - 1:CHUNK, :]
            u_s[h, c] = sol[h][:, 0:LANES]
            wq_s[h, c, 0:CHUNK] = sol[h][:, LANES:2 * LANES]
            wq_s[h, c, CHUNK:2 * CHUNK] = q[h] * e_gc[h]
            kt_s[h, c] = k[h] * jnp.exp(gc_last - gc[h])
            qk_s[h, c] = kq[h][CHUNK:2 * CHUNK] * decay[h]
            gl_s[h, c] = jnp.broadcast_to(jnp.exp(gc_last), (8, LANES))
        return carry

    lax.fori_loop(0, nch, local, 0)

    def scan(c, carry):
        base = pl.multiple_of(c * CHUNK, CHUNK)
        hs = range(B_HEADS)
        state = [st_ref[0, h] for h in hs]
        r = [_bdot(wq_s[h, c], state[h]) for h in hs]
        v_new = [u_s[h, c] - r[h][0:CHUNK] for h in hs]
        upd = [_bdot_tn(kt_s[h, c], v_new[h]) for h in hs]
        out = [r[h][CHUNK:2 * CHUNK] + _bdot(qk_s[h, c], v_new[h]) for h in hs]
        for h in hs:
            st_ref[0, h] = state[h] * gl_s[h, c][0:1, :] + upd[h]
            o = out[h] * lax.rsqrt(jnp.mean(out[h] * out[h], axis=1, keepdims=True) + RMS_EPS) * og_ref[...]
            cols = slice(h * LANES, (h + 1) * LANES)
            o_ref[0, pl.ds(base, CHUNK), cols] = o * _silu(z_ref[0, pl.ds(base, CHUNK), cols])
        return carry

    lax.fori_loop(0, nch, scan, 0)


def _delta(ub, zb, ab, conv_w, a_log, dt_bias, o_norm_g, B, S):
    ts = min(S, DELTA_TILE)
    nch = ts // CHUNK
    seq = lambda width: pl.BlockSpec((1, ts, width), lambda b, t, *_: (b, t, 0))
    fix = lambda shape: pl.BlockSpec(shape, lambda b, t, *_: (0, 0))
    per_chunk = lambda rows, width: pltpu.VMEM((B_HEADS, nch, rows, width), F32)
    grid_spec = pltpu.PrefetchScalarGridSpec(
        num_scalar_prefetch=2,
        grid=(B, S // ts),
        in_specs=[seq(CONV_DIM), seq(B_WIDTH), seq(LANES), fix((CONV_WIDTH, CONV_DIM)), fix((1, LANES))],
        out_specs=[seq(B_WIDTH),
                   pl.BlockSpec((1, B_HEADS, B_HEAD_DIM, B_HEAD_DIM), lambda b, t, *_: (b, 0, 0, 0))],
        scratch_shapes=[pltpu.VMEM((ts + 8, CONV_DIM), F32), per_chunk(CHUNK, LANES), per_chunk(2 * CHUNK, LANES),
                        per_chunk(CHUNK, LANES), per_chunk(CHUNK, CHUNK), per_chunk(8, LANES)],
    )
    return pl.pallas_call(
        _delta_kernel,
        grid_spec=grid_spec,
        out_shape=[jax.ShapeDtypeStruct((B, S, B_WIDTH), F32),
                   jax.ShapeDtypeStruct((B, B_HEADS, B_HEAD_DIM, B_HEAD_DIM), F32)],
        compiler_params=_params(("parallel", "arbitrary")),
        name="delta",
    )(a_log, dt_bias, ub.reshape(B, S, CONV_DIM), zb.reshape(B, S, B_WIDTH), ab.reshape(B, S, LANES),
      conv_w, o_norm_g.reshape(1, LANES))


def _delta_dec_kernel(alog_ref, dtb_ref, ub_ref, cs_ref, zb_ref, ab_ref, w_ref, og_ref, st_ref,
                      o_ref, so_ref, *, nb):
    i = pl.program_id(0)
    N = ub_ref.shape[0]
    acc = ub_ref[...] * w_ref[CONV_WIDTH - 1:CONV_WIDTH, :]
    for t in range(CONV_WIDTH - 1):
        acc = acc + cs_ref[t] * w_ref[t:t + 1, :]
    c = _silu(acc)
    ab = ab_ref[...]
    lane = lax.broadcasted_iota(jnp.int32, (N, LANES), 1)
    samp = lax.broadcasted_iota(jnp.int32, (B_HEAD_DIM, N), 1)
    row_id = lax.broadcasted_iota(jnp.int32, (N, LANES), 0)
    out_row = lax.broadcasted_iota(jnp.int32, (nb, LANES), 0)
    for h in range(B_HEADS):
        cq = c[:, h * LANES:(h + 1) * LANES]
        ck = c[:, B_WIDTH + h * LANES:B_WIDTH + (h + 1) * LANES]
        v = c[:, 2 * B_WIDTH + h * LANES:2 * B_WIDTH + (h + 1) * LANES]
        q = cq * lax.rsqrt(jnp.sum(cq * cq, axis=1, keepdims=True) + 1e-6) * (B_HEAD_DIM ** -0.5)
        k = ck * lax.rsqrt(jnp.sum(ck * ck, axis=1, keepdims=True) + 1e-6)
        a_raw = jnp.sum(jnp.where(lane == h, ab, 0.0), axis=1, keepdims=True)
        b_raw = jnp.sum(jnp.where(lane == h + B_HEADS, ab, 0.0), axis=1, keepdims=True)
        neg_a = -jnp.exp(jnp.full((1, 1), alog_ref[h], F32))
        dec = jnp.exp(neg_a * _softplus(a_raw + dtb_ref[h]))
        beta = _sigmoid(b_raw)
        q_t, k_t = q.T, k.T
        z = zb_ref[:, h * LANES:(h + 1) * LANES]

        def sample(j, o_acc, h=h, q_t=q_t, k_t=k_t, v=v, dec=dec, beta=beta, z=z):
            n = i * nb + j
            pick = samp == n
            k_col = jnp.sum(jnp.where(pick, k_t, 0.0), axis=1, keepdims=True)
            q_col = jnp.sum(jnp.where(pick, q_t, 0.0), axis=1, keepdims=True)
            pick_r = row_id == n
            row = lambda t: jnp.sum(jnp.where(pick_r, t, 0.0), axis=0, keepdims=True)
            st = st_ref[j, h] * row(jnp.broadcast_to(dec, (N, LANES)))
            mem = jnp.sum(k_col * st, axis=0, keepdims=True)
            st = st + k_col * ((row(v) - mem) * row(jnp.broadcast_to(beta, (N, LANES))))
            so_ref[j, h] = st
            o = jnp.sum(q_col * st, axis=0, keepdims=True)
            o = o * lax.rsqrt(jnp.mean(o * o, axis=1, keepdims=True) + RMS_EPS) * og_ref[...]
            return jnp.where(out_row == j, o * _silu(row(z)), o_acc)

        o_ref[:, h * LANES:(h + 1) * LANES] = lax.fori_loop(0, nb, sample, jnp.zeros((nb, LANES), F32))


def _delta_dec(ub_s, conv_state, zb_s, ab_s, conv_w, a_log, dt_bias, o_norm_g, state, nb):
    N = ub_s.shape[0]
    full2 = lambda shape: pl.BlockSpec(shape, lambda i, *_: (0, 0))
    grid_spec = pltpu.PrefetchScalarGridSpec(
        num_scalar_prefetch=2,
        grid=(N // nb,),
        in_specs=[full2((N, CONV_DIM)),
                  pl.BlockSpec((CONV_WIDTH - 1, N, CONV_DIM), lambda i, *_: (0, 0, 0)),
                  full2((N, B_WIDTH)), full2((N, LANES)), full2((CONV_WIDTH, CONV_DIM)), full2((1, LANES)),
                  pl.BlockSpec((nb, B_HEADS, B_HEAD_DIM, B_HEAD_DIM), lambda i, *_: (i, 0, 0, 0))],
        out_specs=[pl.BlockSpec((nb, B_WIDTH), lambda i, *_: (i, 0)),
                   pl.BlockSpec((nb, B_HEADS, B_HEAD_DIM, B_HEAD_DIM), lambda i, *_: (i, 0, 0, 0))],
    )
    return pl.pallas_call(
        functools.partial(_delta_dec_kernel, nb=nb),
        grid_spec=grid_spec,
        out_shape=[jax.ShapeDtypeStruct((N, B_WIDTH), F32), jax.ShapeDtypeStruct(state.shape, F32)],
        compiler_params=_params(("parallel",)),
        name="delta_dec",
    )(a_log, dt_bias, ub_s, jnp.swapaxes(conv_state, 0, 1), zb_s, ab_s, conv_w, o_norm_g.reshape(1, LANES), state)


def _layer_norm(r, g, b):
    mu = jnp.mean(r, axis=1, keepdims=True)
    d = r - mu
    var = jnp.mean(d * d, axis=1, keepdims=True)
    return d * lax.rsqrt(var + LN_EPS) * g + b


def _mix_ln_kernel(*refs, steps):
    outs = refs[-3:]

    @pl.when(pl.program_id(0) < steps)
    def _():
        _mix_ln_rows(*refs[:8], *outs)

    @pl.when(pl.program_id(0) >= steps)
    def _():
        for o in outs:
            o[...] = jnp.zeros(o.shape, F32)


def _mix_ln_rows(oa_ref, ob_ref, x_ref, wo_ref, g_ref, b_ref, wr_ref, br_ref, h_ref, route_ref, cnt_ref):
    if wo_ref.dtype == BF16:
        dot = lambda a, w: jnp.dot(a.astype(BF16), w, preferred_element_type=F32)
    else:
        dot = _fdot
    y = dot(oa_ref[...], wo_ref[0:A_WIDTH, :]) + dot(ob_ref[...], wo_ref[A_WIDTH:A_WIDTH + B_WIDTH, :])
    hcur = _layer_norm(DN_ALPHA * x_ref[...] + y, g_ref[...], b_ref[...])
    h_ref[...] = hcur
    logits = _dot3(hcur, wr_ref[...]) + br_ref[...]
    lane = lax.broadcasted_iota(jnp.int32, logits.shape, 1)
    lane_f = lane.astype(F32)
    ninf = -jnp.inf
    big = 1e9
    gl = jnp.where(lane < N_GROUPS, logits, ninf)
    gmax = jnp.max(gl, axis=1, keepdims=True)
    g_idx = jnp.min(jnp.where(gl == gmax, lane_f, big), axis=1, keepdims=True)
    p_group = 1.0 / jnp.sum(jnp.exp(gl - gmax), axis=1, keepdims=True)
    grp_of_lane = ((lane - N_GROUPS) >> 3).astype(F32)
    sel = (lane >= N_GROUPS) & (lane < N_GROUPS + N_EXPERTS) & (grp_of_lane == g_idx)
    el = jnp.where(sel, logits, ninf)
    v1 = jnp.max(el, axis=1, keepdims=True)
    i1 = jnp.min(jnp.where(el == v1, lane_f, big), axis=1, keepdims=True)
    el2 = jnp.where(lane_f == i1, ninf, el)
    v2 = jnp.max(el2, axis=1, keepdims=True)
    i2 = jnp.min(jnp.where(el2 == v2, lane_f, big), axis=1, keepdims=True)
    t = jnp.exp(v2 - v1)
    gate1 = p_group / (1.0 + t)
    gate2 = p_group * t / (1.0 + t)
    e1, e2 = i1 - N_GROUPS, i2 - N_GROUPS
    route_ref[...] = jnp.where(lane == 0, gate1, jnp.where(lane == 1, gate2, jnp.where(
        lane == 2, e1, jnp.where(lane == 3, e2, 0.0))))
    chosen = ((lane_f == e1) | (lane_f == e2)).astype(F32)
    tm = chosen.shape[0]
    cnt_ref[...] = jnp.sum(chosen.reshape(tm // ROUTE_TILE, ROUTE_TILE, LANES), axis=1)[:, None, :]


def _mix_ln(oa, ob, x2d, wo_b, g, b, wr, br, tm, total, row0, prev=()):
    T, D = x2d.shape
    off = row0 // tm
    steps = T // tm
    tail = 1 if row0 + T < total else 0
    assert total - (row0 + T) <= tm
    row = lambda i: (jnp.minimum(i, steps - 1), 0)
    out_row = lambda i: (off + i, 0)
    fix = lambda i: (0, 0)
    sub = tm // ROUTE_TILE
    return pl.pallas_call(
        functools.partial(_mix_ln_kernel, steps=steps),
        grid=(steps + tail,),
        in_specs=[pl.BlockSpec((tm, A_WIDTH), row), pl.BlockSpec((tm, B_WIDTH), row), pl.BlockSpec((tm, D), row),
                  pl.BlockSpec((A_WIDTH + B_WIDTH, D), fix), pl.BlockSpec((1, D), fix), pl.BlockSpec((1, D), fix),
                  pl.BlockSpec((D, LANES), fix), pl.BlockSpec((1, LANES), fix)]
                 + [pl.BlockSpec(memory_space=pl.ANY)] * len(prev),
        out_specs=[pl.BlockSpec((tm, D), out_row), pl.BlockSpec((tm, LANES), out_row),
                   pl.BlockSpec((sub, 1, LANES), lambda i: (off + i, 0, 0))],
        out_shape=[jax.ShapeDtypeStruct((total, D), F32), jax.ShapeDtypeStruct((total, LANES), F32),
                   jax.ShapeDtypeStruct((total // ROUTE_TILE, 1, LANES), F32)],
        input_output_aliases={8 + j: j for j in range(len(prev))},
        compiler_params=_params(("parallel",)),
        name="mix_ln",
    )(oa, ob, x2d, wo_b, g, b, wr, br, *prev)


def _slot_layout(counts):
    tiles = counts.shape[0]
    n_assign = tiles * ROUTE_TILE * TOP_K
    n_blocks = -(-(n_assign + N_EXPERTS * (MOE_BLK - 1)) // MOE_BLK)
    per_tile = counts.reshape(tiles, LANES).astype(jnp.int32)
    before = jnp.cumsum(per_tile, axis=0) - per_tile
    total = jnp.sum(per_tile, axis=0)[:N_EXPERTS]
    padded = (total + MOE_BLK - 1) // MOE_BLK * MOE_BLK
    pad_end = jnp.cumsum(padded)
    pad_start = pad_end - padded
    base = (before + jnp.pad(pad_start, (0, LANES - N_EXPERTS))[None, :]).astype(F32).reshape(tiles, 1, LANES)
    blk_start = jnp.arange(n_blocks, dtype=jnp.int32) * MOE_BLK
    blk_e = jnp.minimum(jnp.sum(pad_end[None, :] <= blk_start[:, None], axis=1), N_EXPERTS - 1).astype(jnp.int32)
    blk_n = jnp.clip(pad_start[blk_e] + total[blk_e] - blk_start, 0, MOE_BLK).astype(jnp.int32)
    last_e = blk_e[jnp.maximum(pad_end[-1] // MOE_BLK - 1, 0)]
    blk_e = jnp.where(blk_start < pad_end[-1], blk_e, last_e)
    return base, blk_e, blk_n, n_blocks


def _slot_kernel(route_ref, base_ref, dest_ref):
    lane = lax.broadcasted_iota(jnp.int32, (ROUTE_TILE, LANES), 1)
    lane_f = lane.astype(F32)
    ri = lax.broadcasted_iota(jnp.int32, (ROUTE_TILE, ROUTE_TILE), 0)
    ci = lax.broadcasted_iota(jnp.int32, (ROUTE_TILE, ROUTE_TILE), 1)
    before = (ri > ci).astype(BF16)
    for t in range(base_ref.shape[0]):
        rows = slice(t * ROUTE_TILE, (t + 1) * ROUTE_TILE)
        route = route_ref[rows, :]
        e1 = jnp.sum(jnp.where(lane == 2, route, 0.0), axis=1, keepdims=True)
        e2 = jnp.sum(jnp.where(lane == 3, route, 0.0), axis=1, keepdims=True)
        oh1, oh2 = lane_f == e1, lane_f == e2
        earlier = _bdot(before, (oh1 | oh2).astype(F32))
        slot = base_ref[t] + earlier
        d1 = jnp.sum(jnp.where(oh1, slot, 0.0), axis=1, keepdims=True)
        d2 = jnp.sum(jnp.where(oh2, slot, 0.0), axis=1, keepdims=True)
        dest_ref[rows, :] = jnp.where(lane == 0, d1, jnp.where(lane == 1, d2, 0.0)).astype(jnp.int32)


def _slots(route, base):
    tiles = route.shape[0] // ROUTE_TILE
    sub = max(d for d in range(1, 9) if tiles % d == 0)
    return pl.pallas_call(
        _slot_kernel,
        grid=(tiles // sub,),
        in_specs=[pl.BlockSpec((sub * ROUTE_TILE, LANES), lambda i: (i, 0)),
                  pl.BlockSpec((sub, 1, LANES), lambda i: (i, 0, 0))],
        out_specs=pl.BlockSpec((sub * ROUTE_TILE, LANES), lambda i: (i, 0)),
        out_shape=jax.ShapeDtypeStruct(route.shape, jnp.int32),
        compiler_params=_params(("parallel",)),
        name="slots",
    )(route, base)


def _dispatch_kernel(dest_ref, h_ref, xs_in, xs_hbm, hbuf, sem):
    del xs_in
    i = pl.program_id(0)
    slot = i % 2
    a0 = i * (ROUTE_TILE * TOP_K)
    hbuf[slot] = h_ref[...]

    def start(r, c):
        for k in range(TOP_K):
            pltpu.make_async_copy(hbuf.at[slot, pl.ds(r, 1)], xs_hbm.at[pl.ds(dest_ref[a0 + r * TOP_K + k], 1)],
                                  sem.at[slot]).start()
        return c

    lax.fori_loop(0, ROUTE_TILE, start, 0, unroll=8)

    def drain(s):
        for k in range(TOP_K):
            pltpu.make_async_copy(hbuf.at[s], xs_hbm.at[pl.ds(0, ROUTE_TILE)], sem.at[s]).wait()

    @pl.when(i > 0)
    def _():
        drain(1 - slot)

    @pl.when(i == pl.num_programs(0) - 1)
    def _():
        drain(slot)


def _dispatch(h_all, dest, slots):
    T, D = h_all.shape
    grid_spec = pltpu.PrefetchScalarGridSpec(
        num_scalar_prefetch=1,
        grid=(T // ROUTE_TILE,),
        in_specs=[pl.BlockSpec((ROUTE_TILE, D), lambda i, *_: (i, 0)), pl.BlockSpec(memory_space=pl.ANY)],
        out_specs=pl.BlockSpec(memory_space=pl.ANY),
        scratch_shapes=[pltpu.VMEM((2, ROUTE_TILE, D), F32), pltpu.SemaphoreType.DMA((2,))],
    )
    return pl.pallas_call(
        _dispatch_kernel,
        grid_spec=grid_spec,
        out_shape=jax.ShapeDtypeStruct((slots, D), F32),
        input_output_aliases={2: 0},
        compiler_params=_params(("arbitrary",)),
        name="dispatch",
    )(dest, h_all, jnp.zeros((slots, D), F32))


def _moe_kernel(blk_e_ref, blk_n_ref, x_ref, wg_ref, wu_ref, wd_ref, y_ref, wg_b, wu_b, wd_b):
    i = pl.program_id(0)
    n_valid = blk_n_ref[i]

    @pl.when((i == 0) | (blk_e_ref[i] != blk_e_ref[jnp.maximum(i - 1, 0)]))
    def _():
        wg_b[...] = wg_ref[0].astype(BF16)
        wu_b[...] = wu_ref[0].astype(BF16)
        wd_b[...] = wd_ref[0].astype(BF16)

    @pl.when(n_valid > 0)
    def _():
        x = x_ref[...].astype(BF16)
        a = jnp.dot(x, wg_b[...], preferred_element_type=F32)
        u = jnp.dot(x, wu_b[...], preferred_element_type=F32)
        y_ref[...] = jnp.dot((_silu(a) * u).astype(BF16), wd_b[...], preferred_element_type=F32)

    @pl.when(n_valid == 0)
    def _():
        y_ref[...] = jnp.zeros(y_ref.shape, F32)


def _moe(xs, blk_e, blk_n, w_gate, w_up, w_down):
    slots, D = xs.shape
    De = w_gate.shape[-1]
    wspec = lambda shape: pl.BlockSpec((1,) + shape, lambda i, be, *_: (be[i], 0, 0))
    grid_spec = pltpu.PrefetchScalarGridSpec(
        num_scalar_prefetch=2,
        grid=(slots // MOE_BLK,),
        in_specs=[pl.BlockSpec((MOE_BLK, D), lambda i, *_: (i, 0)), wspec((D, De)), wspec((D, De)), wspec((De, D))],
        out_specs=pl.BlockSpec((MOE_BLK, D), lambda i, *_: (i, 0)),
        scratch_shapes=[pltpu.VMEM((D, De), BF16), pltpu.VMEM((D, De), BF16), pltpu.VMEM((De, D), BF16)],
    )
    return pl.pallas_call(
        _moe_kernel,
        grid_spec=grid_spec,
        out_shape=jax.ShapeDtypeStruct((slots, D), F32),
        compiler_params=_params(("arbitrary",)),
        name="moe",
    )(blk_e, blk_n, xs, w_gate, w_up, w_down)


def _final_ln_kernel(dest_ref, h_ref, route_ref, g_ref, b_ref, ys_hbm, o_ref, ybuf, sem, *, tile0):
    i = pl.program_id(0)
    slot = i % 2

    def gather(tile, s):
        a0 = (tile0 + tile) * (ROUTE_TILE * TOP_K)

        def start(r, c):
            for k in range(TOP_K):
                pltpu.make_async_copy(ys_hbm.at[pl.ds(dest_ref[a0 + r * TOP_K + k], 1)],
                                      ybuf.at[s, k, pl.ds(r, 1)], sem.at[s]).start()
            return c

        lax.fori_loop(0, ROUTE_TILE, start, 0, unroll=8)

    @pl.when(i == 0)
    def _():
        gather(0, 0)

    @pl.when(i + 1 < pl.num_programs(0))
    def _():
        gather(i + 1, 1 - slot)

    for k in range(TOP_K):
        pltpu.make_async_copy(ys_hbm.at[pl.ds(0, ROUTE_TILE)], ybuf.at[slot, k], sem.at[slot]).wait()
    route = route_ref[...]
    lane = lax.broadcasted_iota(jnp.int32, route.shape, 1)
    gate1 = jnp.sum(jnp.where(lane == 0, route, 0.0), axis=1, keepdims=True)
    gate2 = jnp.sum(jnp.where(lane == 1, route, 0.0), axis=1, keepdims=True)
    f = ybuf[slot, 0] * gate1 + ybuf[slot, 1] * gate2
    o_ref[...] = _layer_norm(DN_ALPHA * h_ref[...] + f, g_ref[...], b_ref[...])


def _final_ln(h_all, route, dest, ys, g, b, row0, rows):
    D = h_all.shape[1]
    tile0 = row0 // ROUTE_TILE
    row = lambda i, *_: (tile0 + i, 0)
    fix = lambda i, *_: (0, 0)
    grid_spec = pltpu.PrefetchScalarGridSpec(
        num_scalar_prefetch=1,
        grid=(rows // ROUTE_TILE,),
        in_specs=[pl.BlockSpec((ROUTE_TILE, D), row), pl.BlockSpec((ROUTE_TILE, LANES), row),
                  pl.BlockSpec((1, D), fix), pl.BlockSpec((1, D), fix), pl.BlockSpec(memory_space=pl.ANY)],
        out_specs=pl.BlockSpec((ROUTE_TILE, D), lambda i, *_: (i, 0)),
        scratch_shapes=[pltpu.VMEM((2, TOP_K, ROUTE_TILE, D), F32), pltpu.SemaphoreType.DMA((2,))],
    )
    return pl.pallas_call(
        functools.partial(_final_ln_kernel, tile0=tile0),
        grid_spec=grid_spec,
        out_shape=jax.ShapeDtypeStruct((rows, D), F32),
        compiler_params=_params(("arbitrary",)),
        name="final_ln",
    )(dest, h_all, route, g, b, ys)


def kernel(x_prompt, x_sample, cache_a_k, cache_a_v, state_b_ssm, state_b_conv, w_in, rel_bias, conv_w, a_log, dt_bias, o_norm_g, w_out, ln1_g, ln1_b, w_group, b_group, w_router, b_router, w_gate, w_up, w_down, ln2_g, ln2_b):
    B, S, D = x_prompt.shape
    N, T = x_sample.shape[0], x_sample.shape[1]
    depth = w_in.shape[0]
    assert depth == 1 and T == 1 and S % ATT_TILE == 0 and N % ROUTE_TILE == 0 and cache_a_k.shape[2] % LANES == 0
    l = 0
    win_p = min(BRANCHES[-1][0], S)

    w_pad32 = jnp.pad(w_in[l], ((0, 0), (0, IN_COLS_PAD - IN_COLS)))
    w_pad = w_pad32.astype(BF16)
    wo_b = w_out[l].astype(BF16)
    wr = jnp.pad(jnp.concatenate([w_group[l], w_router[l]], axis=1), ((0, 0), (0, LANES - N_GROUPS - N_EXPERTS)))
    br = jnp.pad(jnp.concatenate([b_group[l], b_router[l].reshape(-1)]), (0, LANES - N_GROUPS - N_EXPERTS))[None, :]
    g1, b1 = ln1_g[l][None, :], ln1_b[l][None, :]
    g2, b2 = ln2_g[l][None, :], ln2_b[l][None, :]

    xp = x_prompt.reshape(B * S, D)
    qkv_p, ub_p, zb_p, ab_p = _proj(xp, w_pad, 512)
    oa_p = _attn(qkv_p, _band_bias(rel_bias), B, S)
    ob_p, st_p = _delta(ub_p, zb_p, ab_p, conv_w[l], a_log[l], dt_bias[l], o_norm_g[l], B, S)
    rows_all = B * S + N
    routed_p = _mix_ln(oa_p.reshape(B * S, A_WIDTH), ob_p.reshape(B * S, B_WIDTH), xp, wo_b, g1, b1, wr, br,
                       512, rows_all, 0)

    xs = x_sample.reshape(N, D)
    qkv_s, ub_s, zb_s, ab_s = _proj(xs, w_pad32, N)
    nb = 2
    qkv_t = jnp.transpose(qkv_s.reshape(N, 3, A_HEADS, A_HEAD_DIM), (1, 2, 3, 0))
    oa_s = _attn_dec(qkv_t, jnp.transpose(cache_a_k[l], (0, 2, 3, 1)), jnp.transpose(cache_a_v[l], (0, 2, 3, 1)),
                     _cache_bias(rel_bias, cache_a_k.shape[2]), nb)
    oa_s = jnp.transpose(oa_s, (0, 3, 1, 2)).reshape(N, A_WIDTH)
    ob_s, st_s = _delta_dec(ub_s, state_b_conv[l], zb_s, ab_s, conv_w[l], a_log[l], dt_bias[l], o_norm_g[l],
                            state_b_ssm[l], 8)
    h_all, route, cnt = _mix_ln(oa_s, ob_s, xs, w_out[l], g1, b1, wr, br, N, rows_all, B * S, prev=routed_p)

    base, blk_e, blk_n, n_blocks = _slot_layout(cnt)
    dest = _slots(route, base)[:, 0:TOP_K].reshape(-1)
    ys = _moe(_dispatch(h_all, dest, n_blocks * MOE_BLK), blk_e, blk_n, w_gate[l], w_up[l], w_down[l])
    y_p = _final_ln(h_all, route, dest, ys, g2, b2, 0, B * S)
    y_s = _final_ln(h_all, route, dest, ys, g2, b2, B * S, N)

    w_kv_t = jnp.transpose(w_in[l][:, A_WIDTH:3 * A_WIDTH]).astype(BF16)
    k_win, v_win = _kv_win(x_prompt, w_kv_t, win_p, 512)
    to_rows = lambda t: jnp.transpose(t, (0, 3, 1, 2))[None]
    conv_p = ub_p.reshape(B, S, CONV_DIM)[:, S - (CONV_WIDTH - 1):]
    conv_s = jnp.concatenate([state_b_conv[l], ub_s[:, None, :]], axis=1)[:, T:]
    new_kv = lambda t: jnp.transpose(t, (2, 0, 1))[None, :, None]
    return (y_p.reshape(B, S, D), y_s.reshape(N, T, D), to_rows(k_win), to_rows(v_win),
            new_kv(qkv_t[1]), new_kv(qkv_t[2]), st_p[None], st_s[None], conv_p[None], conv_s[None])
```

```python
import functools
import math

import jax
import jax.numpy as jnp
from jax import lax
from jax.experimental import pallas as pl
from jax.experimental.pallas import tpu as pltpu

F32 = jnp.float32
BF16 = jnp.bfloat16
HIGHEST = lax.Precision.HIGHEST

LANES = 128
A_HEADS = 8
A_HEAD_DIM = 64
A_WIDTH = A_HEADS * A_HEAD_DIM
BRANCHES = ((128, 1), (512, 4), (2048, 16))
BAND = 128
ATT_TILE = BAND * 16
REL_BUCKETS = 32
REL_MAX_DIST = 2048
B_HEADS = 4
B_HEAD_DIM = 128
B_WIDTH = B_HEADS * B_HEAD_DIM
CONV_WIDTH = 4
CONV_DIM = 3 * B_WIDTH
CHUNK = 64
COL_UB = 3 * A_WIDTH
COL_ZB = COL_UB + CONV_DIM
COL_AB = COL_ZB + B_WIDTH
IN_COLS = COL_AB + 2 * B_HEADS
IN_COLS_PAD = COL_AB + LANES
N_GROUPS = 4
EXPERTS_PER_GROUP = 8
N_EXPERTS = N_GROUPS * EXPERTS_PER_GROUP
TOP_K = 2
DN_ALPHA = 2.0 ** 0.25
LN_EPS = 1e-5
RMS_EPS = 1e-6
MASKED = -1e30
MOE_BLK = 256
ROUTE_TILE = 128
DELTA_TILE = 1024
VMEM_LIMIT = 56 * 1024 * 1024


def _bdot(a, b):
    return jnp.dot(a.astype(BF16), b.astype(BF16), preferred_element_type=F32)


def _bdot_nt(a, b):
    return lax.dot_general(a.astype(BF16), b.astype(BF16), (((1,), (1,)), ((), ())), preferred_element_type=F32)


def _bdot_tn(a, b):
    return lax.dot_general(a.astype(BF16), b.astype(BF16), (((0,), (0,)), ((), ())), preferred_element_type=F32)


def _fdot(a, b):
    return jnp.dot(a, b, precision=HIGHEST, preferred_element_type=F32)


def _sigmoid(x):
    return 1.0 / (1.0 + jnp.exp(-x))


def _silu(x):
    return x * _sigmoid(x)


def _softplus(x):
    return jnp.maximum(x, 0.0) + jnp.log(1.0 + jnp.exp(-jnp.abs(x)))


def _params(sem):
    return pltpu.CompilerParams(dimension_semantics=sem, vmem_limit_bytes=VMEM_LIMIT)


def _proj_kernel(x_ref, w_ref, qkv_ref, ub_ref, zb_ref, ab_ref):
    if w_ref.dtype == BF16:
        xb = x_ref[...].astype(BF16)
        dot = lambda w: jnp.dot(xb, w, preferred_element_type=F32)
    else:
        dot = lambda w: _fdot(x_ref[...], w)
    qkv_ref[...] = dot(w_ref[:, 0:COL_UB])
    ub_ref[...] = dot(w_ref[:, COL_UB:COL_ZB])
    zb_ref[...] = dot(w_ref[:, COL_ZB:COL_AB])
    ab_ref[...] = dot(w_ref[:, COL_AB:IN_COLS_PAD])


def _proj(x2d, w_pad, tm):
    T, D = x2d.shape
    row = lambda i: (i, 0)
    return pl.pallas_call(
        _proj_kernel,
        grid=(T // tm,),
        in_specs=[pl.BlockSpec((tm, D), row), pl.BlockSpec((D, IN_COLS_PAD), lambda i: (0, 0))],
        out_specs=[pl.BlockSpec((tm, COL_UB), row), pl.BlockSpec((tm, CONV_DIM), row),
                   pl.BlockSpec((tm, B_WIDTH), row), pl.BlockSpec((tm, LANES), row)],
        out_shape=[jax.ShapeDtypeStruct((T, COL_UB), F32), jax.ShapeDtypeStruct((T, CONV_DIM), F32),
                   jax.ShapeDtypeStruct((T, B_WIDTH), F32), jax.ShapeDtypeStruct((T, LANES), F32)],
        compiler_params=_params(("parallel",)),
        name="proj",
    )(x2d, w_pad)


def _kv_win_kernel(x_ref, wt_ref, k_ref, v_ref):
    tm = x_ref.shape[1]
    kv = lax.dot_general(wt_ref[...], x_ref[0].astype(BF16), (((1,), (1,)), ((), ())), preferred_element_type=F32)
    k_ref[0] = kv[0:A_WIDTH].reshape(A_HEADS, A_HEAD_DIM, tm)
    v_ref[0] = kv[A_WIDTH:2 * A_WIDTH].reshape(A_HEADS, A_HEAD_DIM, tm)


def _kv_win(x_prompt, w_kv_t, win, tm):
    B, S, D = x_prompt.shape
    t0 = (S - win) // tm
    out = jax.ShapeDtypeStruct((B, A_HEADS, A_HEAD_DIM, win), F32)
    ospec = pl.BlockSpec((1, A_HEADS, A_HEAD_DIM, tm), lambda b, t: (b, 0, 0, t))
    return pl.pallas_call(
        _kv_win_kernel,
        grid=(B, win // tm),
        in_specs=[pl.BlockSpec((1, tm, D), lambda b, t: (b, t0 + t, 0)),
                  pl.BlockSpec((2 * A_WIDTH, D), lambda b, t: (0, 0))],
        out_specs=[ospec, ospec],
        out_shape=[out, out],
        compiler_params=_params(("parallel", "parallel")),
        name="kv_win",
    )(x_prompt, w_kv_t)


def _rel_bucket(dist):
    max_exact = REL_BUCKETS // 2
    n = jnp.maximum(dist, 0)
    ratio = jnp.maximum(n, 1).astype(F32) / max_exact
    large = max_exact + (jnp.log(ratio) / math.log(REL_MAX_DIST / max_exact)
                         * (REL_BUCKETS - max_exact)).astype(jnp.int32)
    return jnp.where(n < max_exact, n, jnp.minimum(large, REL_BUCKETS - 1))


def _bias_of(rel_bias, dist):
    onehot = (_rel_bucket(dist)[None, :] == jnp.arange(REL_BUCKETS)[:, None]).astype(F32)
    return jnp.dot(rel_bias.astype(F32).T, onehot, precision=HIGHEST)


def _band_bias(rel_bias):
    period = 3 * BAND
    tabs = []
    for _, dil in BRANCHES:
        g = jnp.concatenate([_bias_of(rel_bias, (BAND - jnp.arange(BAND + 1)) * dil),
                             jnp.full((A_HEADS, period - BAND - 1), MASKED, F32)], axis=1)
        skew = jnp.tile(g, (1, BAND))[:, :BAND * (period - 1)].reshape(A_HEADS, BAND, period - 1)
        tabs.append(skew[:, :, :2 * BAND])
    return jnp.stack(tabs)


def _cache_bias(rel_bias, P):
    dist = P - jnp.arange(P + LANES)
    tabs = []
    for window, dil in BRANCHES:
        ok = (dist >= 0) & (dist <= window) & (dist % dil == 0)
        tabs.append(jnp.where(ok[None, :], _bias_of(rel_bias, dist), MASKED)[:, None, :])
    return jnp.stack(tabs)


def _attn_kernel(q_ref, k_ref, v_ref, bias_ref, o_ref, acc_ref, m_ref, l_ref):
    t = pl.program_id(2)
    tile0 = t * ATT_TILE
    lane = lax.broadcasted_iota(jnp.int32, (BAND, LANES), 1)
    head0 = lane < A_HEAD_DIM
    col = lax.broadcasted_iota(jnp.int32, (BAND, 2 * BAND), 1)
    prev_cols = col < BAND

    def rows(start, dil):
        return pl.ds(pl.multiple_of(start, BAND), BAND) if dil == 1 else pl.ds(start, BAND, stride=dil)

    for br, (_, dil) in enumerate(BRANCHES):
        span = BAND * dil

        def block(i, carry, br=br, dil=dil, span=span):
            start = (i % dil) + (i // dil) * span
            cur = tile0 + start
            first = cur < span
            prev = jnp.where(first, cur, cur - span)
            q = q_ref[0, rows(start, dil), :] * (A_HEAD_DIM ** -0.5)
            kk = jnp.concatenate([k_ref[0, rows(prev, dil), :], k_ref[0, rows(cur, dil), :]], axis=0).astype(BF16)
            vv = jnp.concatenate([v_ref[0, rows(prev, dil), :], v_ref[0, rows(cur, dil), :]], axis=0).astype(BF16)
            pen = jnp.where(first, MASKED, 0.0)
            outs = []
            for hh in range(2):
                qh = jnp.where(head0 if hh == 0 else ~head0, q, 0.0).astype(BF16)
                s = lax.dot_general(qh, kk, (((1,), (1,)), ((), ())), preferred_element_type=F32)
                s = s + bias_ref[br, hh] + jnp.where(prev_cols, pen, 0.0)
                m = jnp.max(s, axis=1, keepdims=True)
                p = jnp.exp(s - m)
                l = jnp.sum(p, axis=1, keepdims=True)
                pv = jnp.dot(p.astype(BF16), vv, preferred_element_type=F32)
                outs.append((pv, m, l))
            acc_ref[br, rows(start, dil), :] = jnp.where(head0, outs[0][0], outs[1][0])
            m_ref[br, rows(start, dil), :] = jnp.where(head0, outs[0][1], outs[1][1])
            l_ref[br, rows(start, dil), :] = jnp.where(head0, outs[0][2], outs[1][2])
            return carry

        lax.fori_loop(0, ATT_TILE // BAND, block, 0, unroll=8)

    def merge(c, carry):
        r = pl.ds(pl.multiple_of(c * 256, 256), 256)
        m0, m1, m2 = m_ref[0, r, :], m_ref[1, r, :], m_ref[2, r, :]
        mx = jnp.maximum(jnp.maximum(m0, m1), m2)
        w0, w1, w2 = jnp.exp(m0 - mx), jnp.exp(m1 - mx), jnp.exp(m2 - mx)
        num = w0 * acc_ref[0, r, :] + w1 * acc_ref[1, r, :] + w2 * acc_ref[2, r, :]
        den = w0 * l_ref[0, r, :] + w1 * l_ref[1, r, :] + w2 * l_ref[2, r, :]
        o_ref[0, r, :] = num / den
        return carry

    lax.fori_loop(0, ATT_TILE // 256, merge, 0)


def _attn(qkv, bias, B, S):
    n_pairs = A_HEADS // 2
    qkv3 = qkv.reshape(B, S, 3 * A_WIDTH)
    return pl.pallas_call(
        _attn_kernel,
        grid=(B, n_pairs, S // ATT_TILE),
        in_specs=[pl.BlockSpec((1, ATT_TILE, LANES), lambda b, hp, t: (b, t, hp)),
                  pl.BlockSpec((1, S, LANES), lambda b, hp, t: (b, 0, n_pairs + hp)),
                  pl.BlockSpec((1, S, LANES), lambda b, hp, t: (b, 0, 2 * n_pairs + hp)),
                  pl.BlockSpec((3, 2, BAND, 2 * BAND), lambda b, hp, t: (0, hp, 0, 0))],
        out_specs=pl.BlockSpec((1, ATT_TILE, LANES), lambda b, hp, t: (b, t, hp)),
        out_shape=jax.ShapeDtypeStruct((B, S, A_WIDTH), F32),
        scratch_shapes=[pltpu.VMEM((3, ATT_TILE, LANES), F32)] * 3,
        compiler_params=_params(("parallel", "parallel", "arbitrary")),
        name="attn",
    )(qkv3, qkv3, qkv3, bias)


def _attn_dec_kernel(qkv_ref, kt_ref, vt_ref, bias_ref, o_ref, *, nb):
    i = pl.program_id(0)
    N = qkv_ref.shape[-1]
    P = kt_ref.shape[-1]
    lane_n = lax.broadcasted_iota(jnp.int32, (A_HEAD_DIM, N), 1)
    lane_o = lax.broadcasted_iota(jnp.int32, (A_HEAD_DIM, nb), 1)
    lane_t = lax.broadcasted_iota(jnp.int32, (1, LANES), 1)

    def head(h, carry):
        slab = jnp.zeros((A_HEAD_DIM, nb), F32)
        for j in range(nb):
            pick = lane_n == i * nb + j
            col = lambda t: jnp.sum(jnp.where(pick, t, 0.0), axis=1, keepdims=True)
            q = col(qkv_ref[0, h]) * (A_HEAD_DIM ** -0.5)
            k_new, v_new = col(qkv_ref[1, h]), col(qkv_ref[2, h])
            s_new = jnp.sum(q * k_new, axis=0, keepdims=True)
            s = jnp.concatenate([jnp.sum(kt_ref[j, h] * q, axis=0, keepdims=True),
                                 jnp.where(lane_t == 0, s_new, 0.0)], axis=1)
            ps, ms, ls = [], [], []
            for br in range(3):
                sb = s + bias_ref[br, h]
                m = jnp.max(sb, axis=1, keepdims=True)
                p = jnp.exp(sb - m)
                ps.append(p)
                ms.append(m)
                ls.append(jnp.sum(p, axis=1, keepdims=True))
            mx = jnp.maximum(jnp.maximum(ms[0], ms[1]), ms[2])
            w = jnp.zeros((1, P + LANES), F32)
            den = jnp.zeros((1, 1), F32)
            for p, m, l in zip(ps, ms, ls):
                e = jnp.exp(m - mx)
                w = w + e * p
                den = den + e * l
            o = jnp.sum(vt_ref[j, h] * w[:, 0:P], axis=1, keepdims=True) + v_new * w[:, P:P + 1]
            slab = jnp.where(lane_o == j, o / den, slab)
        o_ref[0, h] = slab
        return carry

    lax.fori_loop(0, A_HEADS, head, 0, unroll=2)


def _attn_dec(qkv_t, cache_kt, cache_vt, bias, nb):
    N, P = cache_kt.shape[0], cache_kt.shape[-1]
    cache_spec = pl.BlockSpec((nb, A_HEADS, A_HEAD_DIM, P), lambda i: (i, 0, 0, 0))
    return pl.pallas_call(
        functools.partial(_attn_dec_kernel, nb=nb),
        grid=(N // nb,),
        in_specs=[pl.BlockSpec((3, A_HEADS, A_HEAD_DIM, N), lambda i: (0, 0, 0, 0)), cache_spec, cache_spec,
                  pl.BlockSpec((3, A_HEADS, 1, P + LANES), lambda i: (0, 0, 0, 0))],
        out_specs=pl.BlockSpec((1, A_HEADS, A_HEAD_DIM, nb), lambda i: (i, 0, 0, 0)),
        out_shape=jax.ShapeDtypeStruct((N // nb, A_HEADS, A_HEAD_DIM, nb), F32),
        compiler_params=_params(("parallel",)),
        name="attn_dec",
    )(qkv_t, cache_kt, cache_vt, bias)


def _split3(x):
    hi = x.astype(BF16)
    r = x - hi.astype(F32)
    mid = r.astype(BF16)
    return hi, mid, (r - mid.astype(F32)).astype(BF16)


def _tril_dot(tril_b, g):
    return sum(jnp.dot(tril_b, piece, preferred_element_type=F32) for piece in _split3(g))


def _dot3(a, b):
    ah, bh = a.astype(BF16), b.astype(BF16)
    al, bl = (a - ah.astype(F32)).astype(BF16), (b - bh.astype(F32)).astype(BF16)
    d = lambda x, y: jnp.dot(x, y, preferred_element_type=F32)
    return d(ah, bh) + (d(ah, bl) + d(al, bh))


def _delta_kernel(alog_ref, dtb_ref, ub_ref, z_ref, ab_ref, cw_ref, og_ref, o_ref, st_ref,
                  pad_ref, u_s, wq_s, kt_s, qk_s, gl_s):
    t = pl.program_id(1)
    TS = ub_ref.shape[1]
    nch = TS // CHUNK
    hdr = 8

    @pl.when(t == 0)
    def _():
        pad_ref[0:hdr, :] = jnp.zeros((hdr, CONV_DIM), F32)
        st_ref[...] = jnp.zeros(st_ref.shape, F32)

    @pl.when(t > 0)
    def _():
        pad_ref[0:hdr, :] = pad_ref[TS:TS + hdr, :]

    pad_ref[hdr:hdr + TS, :] = ub_ref[0]

    ri = lax.broadcasted_iota(jnp.int32, (CHUNK, CHUNK), 0)
    ci = lax.broadcasted_iota(jnp.int32, (CHUNK, CHUNK), 1)
    incl = ri >= ci
    strict = ri > ci
    tril_b = incl.astype(BF16)
    eye = (ri == ci).astype(F32)
    lane = lax.broadcasted_iota(jnp.int32, (CHUNK, LANES), 1)

    def local(c, carry):
        base = pl.multiple_of(c * CHUNK, CHUNK)
        ab = ab_ref[0, pl.ds(base, CHUNK), :]
        heads = []
        for h in range(B_HEADS):
            def conv(col):
                win = pad_ref[pl.ds(base, CHUNK + hdr), col:col + LANES]
                acc = win[hdr - 3:hdr - 3 + CHUNK] * cw_ref[0:1, col:col + LANES]
                for i in range(1, CONV_WIDTH):
                    acc = acc + win[hdr - 3 + i:hdr - 3 + i + CHUNK] * cw_ref[i:i + 1, col:col + LANES]
                return _silu(acc)

            cq, ck, v = conv(h * LANES), conv(B_WIDTH + h * LANES), conv(2 * B_WIDTH + h * LANES)
            q = cq * lax.rsqrt(jnp.sum(cq * cq, axis=1, keepdims=True) + 1e-6) * (B_HEAD_DIM ** -0.5)
            k = ck * lax.rsqrt(jnp.sum(ck * ck, axis=1, keepdims=True) + 1e-6)
            a_raw = jnp.sum(jnp.where(lane == h, ab, 0.0), axis=1, keepdims=True)
            b_raw = jnp.sum(jnp.where(lane == h + B_HEADS, ab, 0.0), axis=1, keepdims=True)
            neg_a = -jnp.exp(jnp.full((1, LANES), alog_ref[h], F32))
            g = neg_a * _softplus(a_raw + dtb_ref[h])
            beta = _sigmoid(b_raw)
            heads.append((q, k, v, g, beta))
        hs = range(B_HEADS)
        q, k, v, g, beta = zip(*heads)
        gc = [_tril_dot(tril_b, g[h]) for h in hs]
        dmat = [_tril_dot(tril_b, jnp.where(strict, g[h][:, 0:CHUNK], 0.0)) for h in hs]
        kq = [_bdot_nt(jnp.concatenate([k[h], q[h]], axis=0), k[h]) for h in hs]
        decay = [jnp.where(incl, jnp.exp(dmat[h]), 0.0) for h in hs]
        a = [jnp.where(strict, beta[h] * kq[h][0:CHUNK] * decay[h], 0.0) for h in hs]
        x = [eye - a[h] for h in hs]
        p = [_bdot(a[h], a[h]) for h in hs]
        for _ in range(int(math.log2(CHUNK)) - 2):
            r = [_bdot(jnp.concatenate([x[h], p[h]], axis=0), p[h]) for h in hs]
            x = [x[h] + r[h][0:CHUNK] for h in hs]
            p = [r[h][CHUNK:2 * CHUNK] for h in hs]
        x = [x[h] + _bdot(x[h], p[h]) for h in hs]
        e_gc = [jnp.exp(gc[h]) for h in hs]
        sol = [_bdot(x[h], jnp.concatenate([v[h] * beta[h], k[h] * (beta[h] * e_gc[h])], axis=1)) for h in hs]
        for h in hs:
            gc_last = gc[h][CHUNK - 1:CHUNK, :]
            u_s[h, c] = sol[h][:, 0:LANES]
            wq_s[h, c, 0:CHUNK] = sol[h][:, LANES:2 * LANES]
            wq_s[h, c, CHUNK:2 * CHUNK] = q[h] * e_gc[h]
            kt_s[h, c] = k[h] * jnp.exp(gc_last - gc[h])
            qk_s[h, c] = kq[h][CHUNK:2 * CHUNK] * decay[h]
            gl_s[h, c] = jnp.broadcast_to(jnp.exp(gc_last), (8, LANES))
        return carry

    lax.fori_loop(0, nch, local, 0)

    def scan(c, carry):
        base = pl.multiple_of(c * CHUNK, CHUNK)
        hs = range(B_HEADS)
        state = [st_ref[0, h] for h in hs]
        r = [_bdot(wq_s[h, c], state[h]) for h in hs]
        v_new = [u_s[h, c] - r[h][0:CHUNK] for h in hs]
        upd = [_bdot_tn(kt_s[h, c], v_new[h]) for h in hs]
        out = [r[h][CHUNK:2 * CHUNK] + _bdot(qk_s[h, c], v_new[h]) for h in hs]
        for h in hs:
            st_ref[0, h] = state[h] * gl_s[h, c][0:1, :] + upd[h]
            o = out[h] * lax.rsqrt(jnp.mean(out[h] * out[h], axis=1, keepdims=True) + RMS_EPS) * og_ref[...]
            cols = slice(h * LANES, (h + 1) * LANES)
            o_ref[0, pl.ds(base, CHUNK), cols] = o * _silu(z_ref[0, pl.ds(base, CHUNK), cols])
        return carry

    lax.fori_loop(0, nch, scan, 0)


def _delta(ub, zb, ab, conv_w, a_log, dt_bias, o_norm_g, B, S):
    ts = min(S, DELTA_TILE)
    nch = ts // CHUNK
    seq = lambda width: pl.BlockSpec((1, ts, width), lambda b, t, *_: (b, t, 0))
    fix = lambda shape: pl.BlockSpec(shape, lambda b, t, *_: (0, 0))
    per_chunk = lambda rows, width: pltpu.VMEM((B_HEADS, nch, rows, width), F32)
    grid_spec = pltpu.PrefetchScalarGridSpec(
        num_scalar_prefetch=2,
        grid=(B, S // ts),
        in_specs=[seq(CONV_DIM), seq(B_WIDTH), seq(LANES), fix((CONV_WIDTH, CONV_DIM)), fix((1, LANES))],
        out_specs=[seq(B_WIDTH),
                   pl.BlockSpec((1, B_HEADS, B_HEAD_DIM, B_HEAD_DIM), lambda b, t, *_: (b, 0, 0, 0))],
        scratch_shapes=[pltpu.VMEM((ts + 8, CONV_DIM), F32), per_chunk(CHUNK, LANES), per_chunk(2 * CHUNK, LANES),
                        per_chunk(CHUNK, LANES), per_chunk(CHUNK, CHUNK), per_chunk(8, LANES)],
    )
    return pl.pallas_call(
        _delta_kernel,
        grid_spec=grid_spec,
        out_shape=[jax.ShapeDtypeStruct((B, S, B_WIDTH), F32),
                   jax.ShapeDtypeStruct((B, B_HEADS, B_HEAD_DIM, B_HEAD_DIM), F32)],
        compiler_params=_params(("parallel", "arbitrary")),
        name="delta",
    )(a_log, dt_bias, ub.reshape(B, S, CONV_DIM), zb.reshape(B, S, B_WIDTH), ab.reshape(B, S, LANES),
      conv_w, o_norm_g.reshape(1, LANES))


def _delta_dec_kernel(alog_ref, dtb_ref, ub_ref, cs_ref, zb_ref, ab_ref, w_ref, og_ref, st_ref,
                      o_ref, so_ref, *, nb):
    i = pl.program_id(0)
    N = ub_ref.shape[0]
    acc = ub_ref[...] * w_ref[CONV_WIDTH - 1:CONV_WIDTH, :]
    for t in range(CONV_WIDTH - 1):
        acc = acc + cs_ref[t] * w_ref[t:t + 1, :]
    c = _silu(acc)
    ab = ab_ref[...]
    lane = lax.broadcasted_iota(jnp.int32, (N, LANES), 1)
    samp = lax.broadcasted_iota(jnp.int32, (B_HEAD_DIM, N), 1)
    row_id = lax.broadcasted_iota(jnp.int32, (N, LANES), 0)
    out_row = lax.broadcasted_iota(jnp.int32, (nb, LANES), 0)
    heads = []
    for h in range(B_HEADS):
        cq = c[:, h * LANES:(h + 1) * LANES]
        ck = c[:, B_WIDTH + h * LANES:B_WIDTH + (h + 1) * LANES]
        v = c[:, 2 * B_WIDTH + h * LANES:2 * B_WIDTH + (h + 1) * LANES]
        q = cq * lax.rsqrt(jnp.sum(cq * cq, axis=1, keepdims=True) + 1e-6) * (B_HEAD_DIM ** -0.5)
        k = ck * lax.rsqrt(jnp.sum(ck * ck, axis=1, keepdims=True) + 1e-6)
        a_raw = jnp.sum(jnp.where(lane == h, ab, 0.0), axis=1, keepdims=True)
        b_raw = jnp.sum(jnp.where(lane == h + B_HEADS, ab, 0.0), axis=1, keepdims=True)
        neg_a = -jnp.exp(jnp.full((1, 1), alog_ref[h], F32))
        dec = jnp.exp(neg_a * _softplus(a_raw + dtb_ref[h]))
        beta = _sigmoid(b_raw)
        heads.append((q.T, k.T, v, jnp.broadcast_to(dec, (N, LANES)), jnp.broadcast_to(beta, (N, LANES)),
                      zb_ref[:, h * LANES:(h + 1) * LANES]))

    hs = range(B_HEADS)
    q_t, k_t, v, dec, beta, z = zip(*heads)

    def sample(j, o_acc):
        n = i * nb + j
        pick = samp == n
        k_col = [jnp.sum(jnp.where(pick, k_t[h], 0.0), axis=1, keepdims=True) for h in hs]
        q_col = [jnp.sum(jnp.where(pick, q_t[h], 0.0), axis=1, keepdims=True) for h in hs]
        pick_r = row_id == n
        row = lambda t: jnp.sum(jnp.where(pick_r, t, 0.0), axis=0, keepdims=True)
        st = [st_ref[j, h] * row(dec[h]) for h in hs]
        mem = [jnp.sum(k_col[h] * st[h], axis=0, keepdims=True) for h in hs]
        st = [st[h] + k_col[h] * ((row(v[h]) - mem[h]) * row(beta[h])) for h in hs]
        o = [jnp.sum(q_col[h] * st[h], axis=0, keepdims=True) for h in hs]
        o = [o[h] * lax.rsqrt(jnp.mean(o[h] * o[h], axis=1, keepdims=True) + RMS_EPS) * og_ref[...] for h in hs]
        for h in hs:
            so_ref[j, h] = st[h]
        return tuple(jnp.where(out_row == j, o[h] * _silu(row(z[h])), o_acc[h]) for h in hs)

    outs = lax.fori_loop(0, nb, sample, tuple(jnp.zeros((nb, LANES), F32) for _ in hs), unroll=2)
    for h in hs:
        o_ref[:, h * LANES:(h + 1) * LANES] = outs[h]


def _delta_dec(ub_s, conv_state, zb_s, ab_s, conv_w, a_log, dt_bias, o_norm_g, state, nb):
    N = ub_s.shape[0]
    full2 = lambda shape: pl.BlockSpec(shape, lambda i, *_: (0, 0))
    grid_spec = pltpu.PrefetchScalarGridSpec(
        num_scalar_prefetch=2,
        grid=(N // nb,),
        in_specs=[full2((N, CONV_DIM)),
                  pl.BlockSpec((CONV_WIDTH - 1, N, CONV_DIM), lambda i, *_: (0, 0, 0)),
                  full2((N, B_WIDTH)), full2((N, LANES)), full2((CONV_WIDTH, CONV_DIM)), full2((1, LANES)),
                  pl.BlockSpec((nb, B_HEADS, B_HEAD_DIM, B_HEAD_DIM), lambda i, *_: (i, 0, 0, 0))],
        out_specs=[pl.BlockSpec((nb, B_WIDTH), lambda i, *_: (i, 0)),
                   pl.BlockSpec((nb, B_HEADS, B_HEAD_DIM, B_HEAD_DIM), lambda i, *_: (i, 0, 0, 0))],
    )
    return pl.pallas_call(
        functools.partial(_delta_dec_kernel, nb=nb),
        grid_spec=grid_spec,
        out_shape=[jax.ShapeDtypeStruct((N, B_WIDTH), F32), jax.ShapeDtypeStruct(state.shape, F32)],
        compiler_params=_params(("parallel",)),
        name="delta_dec",
    )(a_log, dt_bias, ub_s, jnp.swapaxes(conv_state, 0, 1), zb_s, ab_s, conv_w, o_norm_g.reshape(1, LANES), state)


def _layer_norm(r, g, b):
    mu = jnp.mean(r, axis=1, keepdims=True)
    d = r - mu
    var = jnp.mean(d * d, axis=1, keepdims=True)
    return d * lax.rsqrt(var + LN_EPS) * g + b


def _mix_ln_kernel(*refs, steps):
    outs = refs[-3:]

    @pl.when(pl.program_id(0) < steps)
    def _():
        _mix_ln_rows(*refs[:8], *outs)

    @pl.when(pl.program_id(0) >= steps)
    def _():
        for o in outs:
            o[...] = jnp.zeros(o.shape, F32)


def _mix_ln_rows(oa_ref, ob_ref, x_ref, wo_ref, g_ref, b_ref, wr_ref, br_ref, h_ref, route_ref, cnt_ref):
    if wo_ref.dtype == BF16:
        dot = lambda a, w: jnp.dot(a.astype(BF16), w, preferred_element_type=F32)
    else:
        dot = _fdot
    y = dot(oa_ref[...], wo_ref[0:A_WIDTH, :]) + dot(ob_ref[...], wo_ref[A_WIDTH:A_WIDTH + B_WIDTH, :])
    hcur = _layer_norm(DN_ALPHA * x_ref[...] + y, g_ref[...], b_ref[...])
    h_ref[...] = hcur
    logits = _dot3(hcur, wr_ref[...]) + br_ref[...]
    lane = lax.broadcasted_iota(jnp.int32, logits.shape, 1)
    lane_f = lane.astype(F32)
    ninf = -jnp.inf
    big = 1e9
    gl = jnp.where(lane < N_GROUPS, logits, ninf)
    gmax = jnp.max(gl, axis=1, keepdims=True)
    g_idx = jnp.min(jnp.where(gl == gmax, lane_f, big), axis=1, keepdims=True)
    p_group = 1.0 / jnp.sum(jnp.exp(gl - gmax), axis=1, keepdims=True)
    grp_of_lane = ((lane - N_GROUPS) >> 3).astype(F32)
    sel = (lane >= N_GROUPS) & (lane < N_GROUPS + N_EXPERTS) & (grp_of_lane == g_idx)
    el = jnp.where(sel, logits, ninf)
    v1 = jnp.max(el, axis=1, keepdims=True)
    i1 = jnp.min(jnp.where(el == v1, lane_f, big), axis=1, keepdims=True)
    el2 = jnp.where(lane_f == i1, ninf, el)
    v2 = jnp.max(el2, axis=1, keepdims=True)
    i2 = jnp.min(jnp.where(el2 == v2, lane_f, big), axis=1, keepdims=True)
    t = jnp.exp(v2 - v1)
    gate1 = p_group / (1.0 + t)
    gate2 = p_group * t / (1.0 + t)
    e1, e2 = i1 - N_GROUPS, i2 - N_GROUPS
    route_ref[...] = jnp.where(lane == 0, gate1, jnp.where(lane == 1, gate2, jnp.where(
        lane == 2, e1, jnp.where(lane == 3, e2, 0.0))))
    chosen = ((lane_f == e1) | (lane_f == e2)).astype(F32)
    tm = chosen.shape[0]
    cnt_ref[...] = jnp.sum(chosen.reshape(tm // ROUTE_TILE, ROUTE_TILE, LANES), axis=1)[:, None, :]


def _mix_ln(oa, ob, x2d, wo_b, g, b, wr, br, tm, total, row0, prev=()):
    T, D = x2d.shape
    off = row0 // tm
    steps = T // tm
    tail = 1 if row0 + T < total else 0
    assert total - (row0 + T) <= tm
    row = lambda i: (jnp.minimum(i, steps - 1), 0)
    out_row = lambda i: (off + i, 0)
    fix = lambda i: (0, 0)
    sub = tm // ROUTE_TILE
    return pl.pallas_call(
        functools.partial(_mix_ln_kernel, steps=steps),
        grid=(steps + tail,),
        in_specs=[pl.BlockSpec((tm, A_WIDTH), row), pl.BlockSpec((tm, B_WIDTH), row), pl.BlockSpec((tm, D), row),
                  pl.BlockSpec((A_WIDTH + B_WIDTH, D), fix), pl.BlockSpec((1, D), fix), pl.BlockSpec((1, D), fix),
                  pl.BlockSpec((D, LANES), fix), pl.BlockSpec((1, LANES), fix)]
                 + [pl.BlockSpec(memory_space=pl.ANY)] * len(prev),
        out_specs=[pl.BlockSpec((tm, D), out_row), pl.BlockSpec((tm, LANES), out_row),
                   pl.BlockSpec((sub, 1, LANES), lambda i: (off + i, 0, 0))],
        out_shape=[jax.ShapeDtypeStruct((total, D), F32), jax.ShapeDtypeStruct((total, LANES), F32),
                   jax.ShapeDtypeStruct((total // ROUTE_TILE, 1, LANES), F32)],
        input_output_aliases={8 + j: j for j in range(len(prev))},
        compiler_params=_params(("parallel",)),
        name="mix_ln",
    )(oa, ob, x2d, wo_b, g, b, wr, br, *prev)


def _slot_layout(counts):
    tiles = counts.shape[0]
    n_assign = tiles * ROUTE_TILE * TOP_K
    n_blocks = -(-(n_assign + N_EXPERTS * (MOE_BLK - 1)) // MOE_BLK)
    per_tile = counts.reshape(tiles, LANES)
    earlier = (jnp.arange(tiles)[:, None] > jnp.arange(tiles)[None, :]).astype(F32)
    before = jnp.dot(earlier, per_tile, precision=HIGHEST)
    total = jnp.sum(per_tile, axis=0)[:N_EXPERTS]
    padded = jnp.ceil(total / MOE_BLK) * MOE_BLK
    upto = (jnp.arange(N_EXPERTS)[:, None] <= jnp.arange(N_EXPERTS)[None, :]).astype(F32)
    pad_end = jnp.dot(padded, upto, precision=HIGHEST)
    pad_start = pad_end - padded
    base = (before + jnp.pad(pad_start, (0, LANES - N_EXPERTS))[None, :]).reshape(tiles, 1, LANES)
    blk_start = (jnp.arange(n_blocks) * MOE_BLK).astype(F32)
    in_e = ((pad_start[None, :] <= blk_start[:, None]) & (blk_start[:, None] < pad_end[None, :])).astype(F32)
    used = blk_start < pad_end[-1]
    n_used = jnp.sum(used.astype(jnp.int32))
    last_e = jnp.max(jnp.where(padded > 0, jnp.arange(N_EXPERTS), 0)).astype(F32)
    blk_e = jnp.where(used, jnp.dot(in_e, jnp.arange(N_EXPERTS, dtype=F32), precision=HIGHEST), last_e)
    blk_n = jnp.clip(jnp.dot(in_e, pad_start + total, precision=HIGHEST) - blk_start, 0, MOE_BLK)
    blk_x = jnp.minimum(jnp.arange(n_blocks), jnp.maximum(n_used - 1, 0))
    i32 = lambda t: t.astype(jnp.int32)
    return base, i32(blk_e), i32(blk_n), i32(blk_x), i32(pad_end), n_blocks


def _slot_kernel(route_ref, base_ref, dest_ref):
    lane = lax.broadcasted_iota(jnp.int32, (ROUTE_TILE, LANES), 1)
    lane_f = lane.astype(F32)
    ri = lax.broadcasted_iota(jnp.int32, (ROUTE_TILE, ROUTE_TILE), 0)
    ci = lax.broadcasted_iota(jnp.int32, (ROUTE_TILE, ROUTE_TILE), 1)
    before = (ri > ci).astype(BF16)
    for t in range(base_ref.shape[0]):
        rows = slice(t * ROUTE_TILE, (t + 1) * ROUTE_TILE)
        route = route_ref[rows, :]
        e1 = jnp.sum(jnp.where(lane == 2, route, 0.0), axis=1, keepdims=True)
        e2 = jnp.sum(jnp.where(lane == 3, route, 0.0), axis=1, keepdims=True)
        oh1, oh2 = lane_f == e1, lane_f == e2
        earlier = _bdot(before, (oh1 | oh2).astype(F32))
        slot = base_ref[t] + earlier
        d1 = jnp.sum(jnp.where(oh1, slot, 0.0), axis=1, keepdims=True)
        d2 = jnp.sum(jnp.where(oh2, slot, 0.0), axis=1, keepdims=True)
        dest_ref[rows, :] = jnp.where(lane == 0, d1, jnp.where(lane == 1, d2, 0.0)).astype(jnp.int32)


def _slots(route, base):
    tiles = route.shape[0] // ROUTE_TILE
    sub = max(d for d in range(1, 9) if tiles % d == 0)
    return pl.pallas_call(
        _slot_kernel,
        grid=(tiles // sub,),
        in_specs=[pl.BlockSpec((sub * ROUTE_TILE, LANES), lambda i: (i, 0)),
                  pl.BlockSpec((sub, 1, LANES), lambda i: (i, 0, 0))],
        out_specs=pl.BlockSpec((sub * ROUTE_TILE, LANES), lambda i: (i, 0)),
        out_shape=jax.ShapeDtypeStruct(route.shape, jnp.int32),
        compiler_params=_params(("parallel",)),
        name="slots",
    )(route, base)


def _dispatch_kernel(dest_ref, pad_end_ref, h_ref, xs_hbm, hbuf, zbuf, sem, zsem):
    i = pl.program_id(0)
    slot = i % 2
    a0 = i * (ROUTE_TILE * TOP_K)

    @pl.when(i == 0)
    def _():
        zbuf[...] = jnp.zeros(zbuf.shape, F32)
        fills = [pltpu.make_async_copy(zbuf, xs_hbm.at[pl.ds(pl.multiple_of(pad_end_ref[e] - MOE_BLK, MOE_BLK), MOE_BLK)],
                                       zsem) for e in range(N_EXPERTS)]
        has_rows = [pad_end_ref[e] > (pad_end_ref[e - 1] if e else 0) for e in range(N_EXPERTS)]
        for e in range(N_EXPERTS):
            @pl.when(has_rows[e])
            def _(e=e):
                fills[e].start()
        for e in range(N_EXPERTS):
            @pl.when(has_rows[e])
            def _(e=e):
                fills[e].wait()

        def tail(b):
            return pltpu.make_async_copy(zbuf, xs_hbm.at[pl.ds(pl.multiple_of(b * MOE_BLK, MOE_BLK), MOE_BLK)], zsem)

        def tail_start(b, c):
            tail(b).start()
            return c

        def tail_wait(b, c):
            tail(b).wait()
            return c

        first_unused = pad_end_ref[N_EXPERTS - 1] // MOE_BLK
        lax.fori_loop(first_unused, xs_hbm.shape[0] // MOE_BLK, tail_start, 0)
        lax.fori_loop(first_unused, xs_hbm.shape[0] // MOE_BLK, tail_wait, 0)

    hbuf[slot] = h_ref[...]

    def start(r, c):
        for k in range(TOP_K):
            pltpu.make_async_copy(hbuf.at[slot, pl.ds(r, 1)], xs_hbm.at[pl.ds(dest_ref[a0 + r * TOP_K + k], 1)],
                                  sem.at[slot]).start()
        return c

    lax.fori_loop(0, ROUTE_TILE, start, 0, unroll=8)

    def drain(s):
        for k in range(TOP_K):
            pltpu.make_async_copy(hbuf.at[s], xs_hbm.at[pl.ds(0, ROUTE_TILE)], sem.at[s]).wait()

    @pl.when(i > 0)
    def _():
        drain(1 - slot)

    @pl.when(i == pl.num_programs(0) - 1)
    def _():
        drain(slot)


def _dispatch(h_all, dest, pad_end, slots):
    T, D = h_all.shape
    grid_spec = pltpu.PrefetchScalarGridSpec(
        num_scalar_prefetch=2,
        grid=(T // ROUTE_TILE,),
        in_specs=[pl.BlockSpec((ROUTE_TILE, D), lambda i, *_: (i, 0))],
        out_specs=pl.BlockSpec(memory_space=pl.ANY),
        scratch_shapes=[pltpu.VMEM((2, ROUTE_TILE, D), F32), pltpu.VMEM((MOE_BLK, D), F32),
                        pltpu.SemaphoreType.DMA((2,)), pltpu.SemaphoreType.DMA],
    )
    return pl.pallas_call(
        _dispatch_kernel,
        grid_spec=grid_spec,
        out_shape=jax.ShapeDtypeStruct((slots, D), F32),
        compiler_params=_params(("arbitrary",)),
        name="dispatch",
    )(dest, pad_end, h_all)


def _moe_kernel(blk_e_ref, blk_n_ref, blk_x_ref, x_ref, wg_ref, wu_ref, wd_ref, y_ref, wg_b, wu_b, wd_b):
    del blk_x_ref
    i = pl.program_id(0)
    n_valid = blk_n_ref[i]

    @pl.when((i == 0) | (blk_e_ref[i] != blk_e_ref[jnp.maximum(i - 1, 0)]))
    def _():
        wg_b[...] = wg_ref[0].astype(BF16)
        wu_b[...] = wu_ref[0].astype(BF16)
        wd_b[...] = wd_ref[0].astype(BF16)

    @pl.when(n_valid > 0)
    def _():
        x = x_ref[...].astype(BF16)
        a = jnp.dot(x, wg_b[...], preferred_element_type=F32)
        u = jnp.dot(x, wu_b[...], preferred_element_type=F32)
        y_ref[...] = jnp.dot((_silu(a) * u).astype(BF16), wd_b[...], preferred_element_type=F32)

    @pl.when(n_valid == 0)
    def _():
        y_ref[...] = jnp.zeros(y_ref.shape, F32)


def _moe(xs, blk_e, blk_n, blk_x, w_gate, w_up, w_down):
    slots, D = xs.shape
    De = w_gate.shape[-1]
    wspec = lambda shape: pl.BlockSpec((1,) + shape, lambda i, be, *_: (be[i], 0, 0))
    grid_spec = pltpu.PrefetchScalarGridSpec(
        num_scalar_prefetch=3,
        grid=(slots // MOE_BLK,),
        in_specs=[pl.BlockSpec((MOE_BLK, D), lambda i, be, bn, bx: (bx[i], 0)),
                  wspec((D, De)), wspec((D, De)), wspec((De, D))],
        out_specs=pl.BlockSpec((MOE_BLK, D), lambda i, *_: (i, 0)),
        scratch_shapes=[pltpu.VMEM((D, De), BF16), pltpu.VMEM((D, De), BF16), pltpu.VMEM((De, D), BF16)],
    )
    return pl.pallas_call(
        _moe_kernel,
        grid_spec=grid_spec,
        out_shape=jax.ShapeDtypeStruct((slots, D), F32),
        compiler_params=_params(("arbitrary",)),
        name="moe",
    )(blk_e, blk_n, blk_x, xs, w_gate, w_up, w_down)


def _final_ln_kernel(dest_ref, h_ref, route_ref, g_ref, b_ref, ys_hbm, o_ref, ybuf, sem, *, tile0):
    i = pl.program_id(0)
    slot = i % 2

    def gather(tile, s):
        a0 = (tile0 + tile) * (ROUTE_TILE * TOP_K)

        def start(r, c):
            for k in range(TOP_K):
                pltpu.make_async_copy(ys_hbm.at[pl.ds(dest_ref[a0 + r * TOP_K + k], 1)],
                                      ybuf.at[s, k, pl.ds(r, 1)], sem.at[s]).start()
            return c

        lax.fori_loop(0, ROUTE_TILE, start, 0, unroll=8)

    @pl.when(i == 0)
    def _():
        gather(0, 0)

    @pl.when(i + 1 < pl.num_programs(0))
    def _():
        gather(i + 1, 1 - slot)

    for k in range(TOP_K):
        pltpu.make_async_copy(ys_hbm.at[pl.ds(0, ROUTE_TILE)], ybuf.at[slot, k], sem.at[slot]).wait()
    route = route_ref[...]
    lane = lax.broadcasted_iota(jnp.int32, route.shape, 1)
    gate1 = jnp.sum(jnp.where(lane == 0, route, 0.0), axis=1, keepdims=True)
    gate2 = jnp.sum(jnp.where(lane == 1, route, 0.0), axis=1, keepdims=True)
    f = ybuf[slot, 0] * gate1 + ybuf[slot, 1] * gate2
    o_ref[...] = _layer_norm(DN_ALPHA * h_ref[...] + f, g_ref[...], b_ref[...])


def _final_ln(h_all, route, dest, ys, g, b, row0, rows):
    D = h_all.shape[1]
    tile0 = row0 // ROUTE_TILE
    row = lambda i, *_: (tile0 + i, 0)
    fix = lambda i, *_: (0, 0)
    grid_spec = pltpu.PrefetchScalarGridSpec(
        num_scalar_prefetch=1,
        grid=(rows // ROUTE_TILE,),
        in_specs=[pl.BlockSpec((ROUTE_TILE, D), row), pl.BlockSpec((ROUTE_TILE, LANES), row),
                  pl.BlockSpec((1, D), fix), pl.BlockSpec((1, D), fix), pl.BlockSpec(memory_space=pl.ANY)],
        out_specs=pl.BlockSpec((ROUTE_TILE, D), lambda i, *_: (i, 0)),
        scratch_shapes=[pltpu.VMEM((2, TOP_K, ROUTE_TILE, D), F32), pltpu.SemaphoreType.DMA((2,))],
    )
    return pl.pallas_call(
        functools.partial(_final_ln_kernel, tile0=tile0),
        grid_spec=grid_spec,
        out_shape=jax.ShapeDtypeStruct((rows, D), F32),
        compiler_params=_params(("arbitrary",)),
        name="final_ln",
    )(dest, h_all, route, g, b, ys)


def kernel(x_prompt, x_sample, cache_a_k, cache_a_v, state_b_ssm, state_b_conv, w_in, rel_bias, conv_w, a_log, dt_bias, o_norm_g, w_out, ln1_g, ln1_b, w_group, b_group, w_router, b_router, w_gate, w_up, w_down, ln2_g, ln2_b):
    B, S, D = x_prompt.shape
    N, T = x_sample.shape[0], x_sample.shape[1]
    depth = w_in.shape[0]
    assert depth == 1 and T == 1 and S % ATT_TILE == 0 and N % ROUTE_TILE == 0 and cache_a_k.shape[2] % LANES == 0
    l = 0
    win_p = min(BRANCHES[-1][0], S)

    w_pad32 = jnp.pad(w_in[l], ((0, 0), (0, IN_COLS_PAD - IN_COLS)))
    w_pad = w_pad32.astype(BF16)
    wo_b = w_out[l].astype(BF16)
    wr = jnp.pad(jnp.concatenate([w_group[l], w_router[l]], axis=1), ((0, 0), (0, LANES - N_GROUPS - N_EXPERTS)))
    br = jnp.pad(jnp.concatenate([b_group[l], b_router[l].reshape(-1)]), (0, LANES - N_GROUPS - N_EXPERTS))[None, :]
    g1, b1 = ln1_g[l][None, :], ln1_b[l][None, :]
    g2, b2 = ln2_g[l][None, :], ln2_b[l][None, :]

    xp = x_prompt.reshape(B * S, D)
    qkv_p, ub_p, zb_p, ab_p = _proj(xp, w_pad, 512)
    oa_p = _attn(qkv_p, _band_bias(rel_bias), B, S)
    ob_p, st_p = _delta(ub_p, zb_p, ab_p, conv_w[l], a_log[l], dt_bias[l], o_norm_g[l], B, S)
    rows_all = B * S + N
    routed_p = _mix_ln(oa_p.reshape(B * S, A_WIDTH), ob_p.reshape(B * S, B_WIDTH), xp, wo_b, g1, b1, wr, br,
                       512, rows_all, 0)

    xs = x_sample.reshape(N, D)
    qkv_s, ub_s, zb_s, ab_s = _proj(xs, w_pad32, N)
    nb = 2
    qkv_t = jnp.transpose(qkv_s.reshape(N, 3, A_HEADS, A_HEAD_DIM), (1, 2, 3, 0))
    oa_s = _attn_dec(qkv_t, jnp.transpose(cache_a_k[l], (0, 2, 3, 1)), jnp.transpose(cache_a_v[l], (0, 2, 3, 1)),
                     _cache_bias(rel_bias, cache_a_k.shape[2]), nb)
    oa_s = jnp.transpose(oa_s, (0, 3, 1, 2)).reshape(N, A_WIDTH)
    ob_s, st_s = _delta_dec(ub_s, state_b_conv[l], zb_s, ab_s, conv_w[l], a_log[l], dt_bias[l], o_norm_g[l],
                            state_b_ssm[l], 16)
    h_all, route, cnt = _mix_ln(oa_s, ob_s, xs, w_out[l], g1, b1, wr, br, N, rows_all, B * S, prev=routed_p)

    base, blk_e, blk_n, blk_x, pad_end, n_blocks = _slot_layout(cnt)
    dest = _slots(route, base)[:, 0:TOP_K].reshape(-1)
    xs = _dispatch(h_all, dest, pad_end, n_blocks * MOE_BLK)
    ys = _moe(xs, blk_e, blk_n, blk_x, w_gate[l], w_up[l], w_down[l])
    y_p = _final_ln(h_all, route, dest, ys, g2, b2, 0, B * S)
    y_s = _final_ln(h_all, route, dest, ys, g2, b2, B * S, N)

    w_kv_t = jnp.transpose(w_in[l][:, A_WIDTH:3 * A_WIDTH]).astype(BF16)
    k_win, v_win = _kv_win(x_prompt, w_kv_t, win_p, 512)
    to_rows = lambda t: jnp.transpose(t, (0, 3, 1, 2))[None]
    conv_p = ub_p.reshape(B, S, CONV_DIM)[:, S - (CONV_WIDTH - 1):]
    conv_s = jnp.concatenate([state_b_conv[l], ub_s[:, None, :]], axis=1)[:, T:]
    new_kv = lambda t: jnp.transpose(t, (2, 0, 1))[None, :, None]
    return (y_p.reshape(B, S, D), y_s.reshape(N, T, D), to_rows(k_win), to_rows(v_win),
            new_kv(qkv_t[1]), new_kv(qkv_t[2]), st_p[None], st_s[None], conv_p[None], conv_s[None])
```

```python
import functools
import math

import jax
import jax.numpy as jnp
from jax import lax
from jax.experimental import pallas as pl
from jax.experimental.pallas import tpu as pltpu

F32 = jnp.float32
BF16 = jnp.bfloat16
HIGHEST = lax.Precision.HIGHEST

LANES = 128
A_HEADS = 8
A_HEAD_DIM = 64
A_WIDTH = A_HEADS * A_HEAD_DIM
BRANCHES = ((128, 1), (512, 4), (2048, 16))
BAND = 128
ATT_TILE = BAND * 16
REL_BUCKETS = 32
REL_MAX_DIST = 2048
B_HEADS = 4
B_HEAD_DIM = 128
B_WIDTH = B_HEADS * B_HEAD_DIM
CONV_WIDTH = 4
CONV_DIM = 3 * B_WIDTH
CHUNK = 64
COL_UB = 3 * A_WIDTH
COL_ZB = COL_UB + CONV_DIM
COL_AB = COL_ZB + B_WIDTH
IN_COLS = COL_AB + 2 * B_HEADS
IN_COLS_PAD = COL_AB + LANES
N_GROUPS = 4
EXPERTS_PER_GROUP = 8
N_EXPERTS = N_GROUPS * EXPERTS_PER_GROUP
TOP_K = 2
DN_ALPHA = 2.0 ** 0.25
LN_EPS = 1e-5
RMS_EPS = 1e-6
MASKED = -1e30
MOE_BLK = 256
ROUTE_TILE = 128
DELTA_TILE = 1024
VMEM_LIMIT = 56 * 1024 * 1024


def _bdot(a, b):
    return jnp.dot(a.astype(BF16), b.astype(BF16), preferred_element_type=F32)


def _bdot_nt(a, b):
    return lax.dot_general(a.astype(BF16), b.astype(BF16), (((1,), (1,)), ((), ())), preferred_element_type=F32)


def _bdot_tn(a, b):
    return lax.dot_general(a.astype(BF16), b.astype(BF16), (((0,), (0,)), ((), ())), preferred_element_type=F32)


def _fdot(a, b):
    return jnp.dot(a, b, precision=HIGHEST, preferred_element_type=F32)


def _sigmoid(x):
    return 1.0 / (1.0 + jnp.exp(-x))


def _silu(x):
    return x * _sigmoid(x)


def _softplus(x):
    return jnp.maximum(x, 0.0) + jnp.log(1.0 + jnp.exp(-jnp.abs(x)))


def _params(sem):
    return pltpu.CompilerParams(dimension_semantics=sem, vmem_limit_bytes=VMEM_LIMIT)


def _proj_kernel(x_ref, w_ref, qkv_ref, ub_ref, zb_ref, ab_ref):
    if w_ref.dtype == BF16:
        xb = x_ref[...].astype(BF16)
        dot = lambda w: jnp.dot(xb, w, preferred_element_type=F32)
    else:
        dot = lambda w: _fdot(x_ref[...], w)
    qkv_ref[...] = dot(w_ref[:, 0:COL_UB])
    ub_ref[...] = dot(w_ref[:, COL_UB:COL_ZB])
    zb_ref[...] = dot(w_ref[:, COL_ZB:COL_AB])
    ab_ref[...] = dot(w_ref[:, COL_AB:IN_COLS_PAD])


def _proj(x2d, w_pad, tm):
    T, D = x2d.shape
    row = lambda i: (i, 0)
    return pl.pallas_call(
        _proj_kernel,
        grid=(T // tm,),
        in_specs=[pl.BlockSpec((tm, D), row), pl.BlockSpec((D, IN_COLS_PAD), lambda i: (0, 0))],
        out_specs=[pl.BlockSpec((tm, COL_UB), row), pl.BlockSpec((tm, CONV_DIM), row),
                   pl.BlockSpec((tm, B_WIDTH), row), pl.BlockSpec((tm, LANES), row)],
        out_shape=[jax.ShapeDtypeStruct((T, COL_UB), F32), jax.ShapeDtypeStruct((T, CONV_DIM), F32),
                   jax.ShapeDtypeStruct((T, B_WIDTH), F32), jax.ShapeDtypeStruct((T, LANES), F32)],
        compiler_params=_params(("parallel",)),
        name="proj",
    )(x2d, w_pad)


def _kv_win_kernel(x_ref, wt_ref, k_ref, v_ref):
    tm = x_ref.shape[1]
    kv = lax.dot_general(wt_ref[...], x_ref[0].astype(BF16), (((1,), (1,)), ((), ())), preferred_element_type=F32)
    k_ref[0] = kv[0:A_WIDTH].reshape(A_HEADS, A_HEAD_DIM, tm)
    v_ref[0] = kv[A_WIDTH:2 * A_WIDTH].reshape(A_HEADS, A_HEAD_DIM, tm)


def _kv_win(x_prompt, w_kv_t, win, tm):
    B, S, D = x_prompt.shape
    t0 = (S - win) // tm
    out = jax.ShapeDtypeStruct((B, A_HEADS, A_HEAD_DIM, win), F32)
    ospec = pl.BlockSpec((1, A_HEADS, A_HEAD_DIM, tm), lambda b, t: (b, 0, 0, t))
    return pl.pallas_call(
        _kv_win_kernel,
        grid=(B, win // tm),
        in_specs=[pl.BlockSpec((1, tm, D), lambda b, t: (b, t0 + t, 0)),
                  pl.BlockSpec((2 * A_WIDTH, D), lambda b, t: (0, 0))],
        out_specs=[ospec, ospec],
        out_shape=[out, out],
        compiler_params=_params(("parallel", "parallel")),
        name="kv_win",
    )(x_prompt, w_kv_t)


def _rel_bucket(dist):
    max_exact = REL_BUCKETS // 2
    n = jnp.maximum(dist, 0)
    ratio = jnp.maximum(n, 1).astype(F32) / max_exact
    large = max_exact + (jnp.log(ratio) / math.log(REL_MAX_DIST / max_exact)
                         * (REL_BUCKETS - max_exact)).astype(jnp.int32)
    return jnp.where(n < max_exact, n, jnp.minimum(large, REL_BUCKETS - 1))


def _bias_of(rel_bias, dist):
    onehot = (_rel_bucket(dist)[None, :] == jnp.arange(REL_BUCKETS)[:, None]).astype(F32)
    return jnp.dot(rel_bias.astype(F32).T, onehot, precision=HIGHEST)


def _band_bias(rel_bias):
    period = 3 * BAND
    tabs = []
    for _, dil in BRANCHES:
        g = jnp.concatenate([_bias_of(rel_bias, (BAND - jnp.arange(BAND + 1)) * dil),
                             jnp.full((A_HEADS, period - BAND - 1), MASKED, F32)], axis=1)
        skew = jnp.tile(g, (1, BAND))[:, :BAND * (period - 1)].reshape(A_HEADS, BAND, period - 1)
        tabs.append(skew[:, :, :2 * BAND])
    return jnp.stack(tabs)


def _cache_bias(rel_bias, P):
    dist = P - jnp.arange(P + LANES)
    tabs = []
    for window, dil in BRANCHES:
        ok = (dist >= 0) & (dist <= window) & (dist % dil == 0)
        tabs.append(jnp.where(ok[None, :], _bias_of(rel_bias, dist), MASKED)[:, None, :])
    return jnp.stack(tabs)


def _attn_kernel(q_ref, k_ref, v_ref, bias_ref, o_ref, acc_ref, m_ref, l_ref):
    t = pl.program_id(2)
    tile0 = t * ATT_TILE
    lane = lax.broadcasted_iota(jnp.int32, (BAND, LANES), 1)
    head0 = lane < A_HEAD_DIM

    def rows(start, dil):
        return pl.ds(pl.multiple_of(start, BAND), BAND) if dil == 1 else pl.ds(start, BAND, stride=dil)

    for br, (_, dil) in enumerate(BRANCHES):
        span = BAND * dil

        def block(i, carry, br=br, dil=dil, span=span):
            start = (i % dil) + (i // dil) * span
            cur = tile0 + start
            first = cur < span
            prev = jnp.where(first, cur, cur - span)
            q = q_ref[0, rows(start, dil), :] * (A_HEAD_DIM ** -0.5)
            kk = jnp.concatenate([k_ref[0, rows(prev, dil), :], k_ref[0, rows(cur, dil), :]], axis=0).astype(BF16)
            vv = jnp.concatenate([v_ref[0, rows(prev, dil), :], v_ref[0, rows(cur, dil), :]], axis=0).astype(BF16)
            pen = jnp.where(first, MASKED, 0.0)
            outs = []
            for hh in range(2):
                qh = jnp.where(head0 if hh == 0 else ~head0, q, 0.0).astype(BF16)
                s = lax.dot_general(qh, kk, (((1,), (1,)), ((), ())), preferred_element_type=F32)
                s = s + bias_ref[br, hh]
                s = jnp.concatenate([s[:, 0:BAND] + pen, s[:, BAND:2 * BAND]], axis=1)
                m = jnp.max(s, axis=1, keepdims=True)
                p = jnp.exp(s - m)
                l = jnp.sum(p, axis=1, keepdims=True)
                pv = jnp.dot(p.astype(BF16), vv, preferred_element_type=F32)
                outs.append((pv, m, l))
            acc_ref[br, rows(start, dil), :] = jnp.where(head0, outs[0][0], outs[1][0])
            m_ref[br, rows(start, dil), :] = jnp.where(head0, outs[0][1], outs[1][1])
            l_ref[br, rows(start, dil), :] = jnp.where(head0, outs[0][2], outs[1][2])
            return carry

        lax.fori_loop(0, ATT_TILE // BAND, block, 0, unroll=8)

    def merge(c, carry):
        r = pl.ds(pl.multiple_of(c * 256, 256), 256)
        m0, m1, m2 = m_ref[0, r, :], m_ref[1, r, :], m_ref[2, r, :]
        mx = jnp.maximum(jnp.maximum(m0, m1), m2)
        w0, w1, w2 = jnp.exp(m0 - mx), jnp.exp(m1 - mx), jnp.exp(m2 - mx)
        num = w0 * acc_ref[0, r, :] + w1 * acc_ref[1, r, :] + w2 * acc_ref[2, r, :]
        den = w0 * l_ref[0, r, :] + w1 * l_ref[1, r, :] + w2 * l_ref[2, r, :]
        o_ref[0, r, :] = num / den
        return carry

    lax.fori_loop(0, ATT_TILE // 256, merge, 0)


def _attn(qkv, bias, B, S):
    n_pairs = A_HEADS // 2
    qkv3 = qkv.reshape(B, S, 3 * A_WIDTH)
    return pl.pallas_call(
        _attn_kernel,
        grid=(B, n_pairs, S // ATT_TILE),
        in_specs=[pl.BlockSpec((1, ATT_TILE, LANES), lambda b, hp, t: (b, t, hp)),
                  pl.BlockSpec((1, S, LANES), lambda b, hp, t: (b, 0, n_pairs + hp)),
                  pl.BlockSpec((1, S, LANES), lambda b, hp, t: (b, 0, 2 * n_pairs + hp)),
                  pl.BlockSpec((3, 2, BAND, 2 * BAND), lambda b, hp, t: (0, hp, 0, 0))],
        out_specs=pl.BlockSpec((1, ATT_TILE, LANES), lambda b, hp, t: (b, t, hp)),
        out_shape=jax.ShapeDtypeStruct((B, S, A_WIDTH), F32),
        scratch_shapes=[pltpu.VMEM((3, ATT_TILE, LANES), F32)] * 3,
        compiler_params=_params(("parallel", "parallel", "arbitrary")),
        name="attn",
    )(qkv3, qkv3, qkv3, bias)


def _attn_dec_kernel(qkv_ref, kt_ref, vt_ref, bias_ref, o_ref, *, nb):
    i = pl.program_id(0)
    N = qkv_ref.shape[-1]
    P = kt_ref.shape[-1]
    lane_n = lax.broadcasted_iota(jnp.int32, (A_HEAD_DIM, N), 1)
    lane_o = lax.broadcasted_iota(jnp.int32, (A_HEAD_DIM, nb), 1)
    lane_t = lax.broadcasted_iota(jnp.int32, (1, LANES), 1)

    def head(h, carry):
        slab = jnp.zeros((A_HEAD_DIM, nb), F32)
        for j in range(nb):
            pick = lane_n == i * nb + j
            col = lambda t: jnp.sum(jnp.where(pick, t, 0.0), axis=1, keepdims=True)
            q = col(qkv_ref[0, h]) * (A_HEAD_DIM ** -0.5)
            k_new, v_new = col(qkv_ref[1, h]), col(qkv_ref[2, h])
            s_new = jnp.sum(q * k_new, axis=0, keepdims=True)
            s = jnp.concatenate([jnp.sum(kt_ref[j, h] * q, axis=0, keepdims=True),
                                 jnp.where(lane_t == 0, s_new, 0.0)], axis=1)
            ps, ms, ls = [], [], []
            for br in range(3):
                sb = s + bias_ref[br, h]
                m = jnp.max(sb, axis=1, keepdims=True)
                p = jnp.exp(sb - m)
                ps.append(p)
                ms.append(m)
                ls.append(jnp.sum(p, axis=1, keepdims=True))
            mx = jnp.maximum(jnp.maximum(ms[0], ms[1]), ms[2])
            w = jnp.zeros((1, P + LANES), F32)
            den = jnp.zeros((1, 1), F32)
            for p, m, l in zip(ps, ms, ls):
                e = jnp.exp(m - mx)
                w = w + e * p
                den = den + e * l
            o = jnp.sum(vt_ref[j, h] * w[:, 0:P], axis=1, keepdims=True) + v_new * w[:, P:P + 1]
            slab = jnp.where(lane_o == j, o / den, slab)
        o_ref[0, h] = slab
        return carry

    lax.fori_loop(0, A_HEADS, head, 0, unroll=2)


def _attn_dec(qkv_t, cache_kt, cache_vt, bias, nb):
    N, P = cache_kt.shape[0], cache_kt.shape[-1]
    cache_spec = pl.BlockSpec((nb, A_HEADS, A_HEAD_DIM, P), lambda i: (i, 0, 0, 0))
    return pl.pallas_call(
        functools.partial(_attn_dec_kernel, nb=nb),
        grid=(N // nb,),
        in_specs=[pl.BlockSpec((3, A_HEADS, A_HEAD_DIM, N), lambda i: (0, 0, 0, 0)), cache_spec, cache_spec,
                  pl.BlockSpec((3, A_HEADS, 1, P + LANES), lambda i: (0, 0, 0, 0))],
        out_specs=pl.BlockSpec((1, A_HEADS, A_HEAD_DIM, nb), lambda i: (i, 0, 0, 0)),
        out_shape=jax.ShapeDtypeStruct((N // nb, A_HEADS, A_HEAD_DIM, nb), F32),
        compiler_params=_params(("parallel",)),
        name="attn_dec",
    )(qkv_t, cache_kt, cache_vt, bias)


def _split3(x):
    hi = x.astype(BF16)
    r = x - hi.astype(F32)
    mid = r.astype(BF16)
    return hi, mid, (r - mid.astype(F32)).astype(BF16)


def _tril_dot(tril_b, g):
    return sum(jnp.dot(tril_b, piece, preferred_element_type=F32) for piece in _split3(g))


def _dot3(a, b):
    ah, bh = a.astype(BF16), b.astype(BF16)
    al, bl = (a - ah.astype(F32)).astype(BF16), (b - bh.astype(F32)).astype(BF16)
    d = lambda x, y: jnp.dot(x, y, preferred_element_type=F32)
    return d(ah, bh) + (d(ah, bl) + d(al, bh))


def _delta_kernel(alog_ref, dtb_ref, ub_ref, z_ref, ab_ref, cw_ref, og_ref, o_ref, st_ref,
                  pad_ref, u_s, wq_s, kt_s, qk_s, gl_s):
    t = pl.program_id(1)
    TS = ub_ref.shape[1]
    nch = TS // CHUNK
    hdr = 8

    @pl.when(t == 0)
    def _():
        pad_ref[0:hdr, :] = jnp.zeros((hdr, CONV_DIM), F32)
        st_ref[...] = jnp.zeros(st_ref.shape, F32)

    @pl.when(t > 0)
    def _():
        pad_ref[0:hdr, :] = pad_ref[TS:TS + hdr, :]

    pad_ref[hdr:hdr + TS, :] = ub_ref[0]

    ri = lax.broadcasted_iota(jnp.int32, (CHUNK, CHUNK), 0)
    ci = lax.broadcasted_iota(jnp.int32, (CHUNK, CHUNK), 1)
    incl = ri >= ci
    strict = ri > ci
    tril_b = incl.astype(BF16)
    eye = (ri == ci).astype(F32)
    lane = lax.broadcasted_iota(jnp.int32, (CHUNK, LANES), 1)

    def local(c, carry):
        base = c * CHUNK if isinstance(c, int) else pl.multiple_of(c * CHUNK, CHUNK)
        ab = ab_ref[0, pl.ds(base, CHUNK), :]
        heads = []
        for h in range(B_HEADS):
            def conv(col):
                win = pad_ref[pl.ds(base, CHUNK + hdr), col:col + LANES]
                acc = win[hdr - 3:hdr - 3 + CHUNK] * cw_ref[0:1, col:col + LANES]
                for i in range(1, CONV_WIDTH):
                    acc = acc + win[hdr - 3 + i:hdr - 3 + i + CHUNK] * cw_ref[i:i + 1, col:col + LANES]
                return _silu(acc)

            cq, ck, v = conv(h * LANES), conv(B_WIDTH + h * LANES), conv(2 * B_WIDTH + h * LANES)
            q = cq * lax.rsqrt(jnp.sum(cq * cq, axis=1, keepdims=True) + 1e-6) * (B_HEAD_DIM ** -0.5)
            k = ck * lax.rsqrt(jnp.sum(ck * ck, axis=1, keepdims=True) + 1e-6)
            a_raw = jnp.sum(jnp.where(lane == h, ab, 0.0), axis=1, keepdims=True)
            b_raw = jnp.sum(jnp.where(lane == h + B_HEADS, ab, 0.0), axis=1, keepdims=True)
            neg_a = -jnp.exp(jnp.full((1, LANES), alog_ref[h], F32))
            g = neg_a * _softplus(a_raw + dtb_ref[h])
            beta = _sigmoid(b_raw)
            heads.append((q, k, v, g, beta))
        hs = range(B_HEADS)
        q, k, v, g, beta = zip(*heads)
        gc = [_tril_dot(tril_b, g[h]) for h in hs]
        dmat = [_tril_dot(tril_b, jnp.where(strict, g[h][:, 0:CHUNK], 0.0)) for h in hs]
        kq = [_bdot_nt(jnp.concatenate([k[h], q[h]], axis=0), k[h]) for h in hs]
        decay = [jnp.where(incl, jnp.exp(dmat[h]), 0.0) for h in hs]
        a = [jnp.where(strict, beta[h] * kq[h][0:CHUNK] * decay[h], 0.0) for h in hs]
        x = [eye - a[h] for h in hs]
        p = [_bdot(a[h], a[h]) for h in hs]
        for _ in range(int(math.log2(CHUNK)) - 2):
            r = [_bdot(jnp.concatenate([x[h], p[h]], axis=0), p[h]) for h in hs]
            x = [x[h] + r[h][0:CHUNK] for h in hs]
            p = [r[h][CHUNK:2 * CHUNK] for h in hs]
        x = [x[h] + _bdot(x[h], p[h]) for h in hs]
        e_gc = [jnp.exp(gc[h]) for h in hs]
        sol = [_bdot(x[h], jnp.concatenate([v[h] * beta[h], k[h] * (beta[h] * e_gc[h])], axis=1)) for h in hs]
        for h in hs:
            gc_last = gc[h][CHUNK - 1:CHUNK, :]
            u_s[h, c] = sol[h][:, 0:LANES]
            wq_s[h, c, 0:CHUNK] = sol[h][:, LANES:2 * LANES]
            wq_s[h, c, CHUNK:2 * CHUNK] = q[h] * e_gc[h]
            kt_s[h, c] = k[h] * jnp.exp(gc_last - gc[h])
            qk_s[h, c] = kq[h][CHUNK:2 * CHUNK] * decay[h]
            gl_s[h, c] = jnp.broadcast_to(jnp.exp(gc_last), (8, LANES))
        return carry

    def scan(c, carry):
        base = c * CHUNK if isinstance(c, int) else pl.multiple_of(c * CHUNK, CHUNK)
        hs = range(B_HEADS)
        state = [st_ref[0, h] for h in hs]
        r = [_bdot(wq_s[h, c], state[h]) for h in hs]
        v_new = [u_s[h, c] - r[h][0:CHUNK] for h in hs]
        upd = [_bdot_tn(kt_s[h, c], v_new[h]) for h in hs]
        out = [r[h][CHUNK:2 * CHUNK] + _bdot(qk_s[h, c], v_new[h]) for h in hs]
        for h in hs:
            st_ref[0, h] = state[h] * gl_s[h, c][0:1, :] + upd[h]
            o = out[h] * lax.rsqrt(jnp.mean(out[h] * out[h], axis=1, keepdims=True) + RMS_EPS) * og_ref[...]
            cols = slice(h * LANES, (h + 1) * LANES)
            o_ref[0, pl.ds(base, CHUNK), cols] = o * _silu(z_ref[0, pl.ds(base, CHUNK), cols])
        return carry

    def both(c, carry):
        scan(c - 1, carry)
        return local(c, carry)

    local(0, 0)
    lax.fori_loop(1, nch, both, 0)
    scan(nch - 1, 0)


def _delta(ub, zb, ab, conv_w, a_log, dt_bias, o_norm_g, B, S):
    ts = min(S, DELTA_TILE)
    nch = ts // CHUNK
    seq = lambda width: pl.BlockSpec((1, ts, width), lambda b, t, *_: (b, t, 0))
    fix = lambda shape: pl.BlockSpec(shape, lambda b, t, *_: (0, 0))
    per_chunk = lambda rows, width: pltpu.VMEM((B_HEADS, nch, rows, width), F32)
    grid_spec = pltpu.PrefetchScalarGridSpec(
        num_scalar_prefetch=2,
        grid=(B, S // ts),
        in_specs=[seq(CONV_DIM), seq(B_WIDTH), seq(LANES), fix((CONV_WIDTH, CONV_DIM)), fix((1, LANES))],
        out_specs=[seq(B_WIDTH),
                   pl.BlockSpec((1, B_HEADS, B_HEAD_DIM, B_HEAD_DIM), lambda b, t, *_: (b, 0, 0, 0))],
        scratch_shapes=[pltpu.VMEM((ts + 8, CONV_DIM), F32), per_chunk(CHUNK, LANES), per_chunk(2 * CHUNK, LANES),
                        per_chunk(CHUNK, LANES), per_chunk(CHUNK, CHUNK), per_chunk(8, LANES)],
    )
    return pl.pallas_call(
        _delta_kernel,
        grid_spec=grid_spec,
        out_shape=[jax.ShapeDtypeStruct((B, S, B_WIDTH), F32),
                   jax.ShapeDtypeStruct((B, B_HEADS, B_HEAD_DIM, B_HEAD_DIM), F32)],
        compiler_params=_params(("parallel", "arbitrary")),
        name="delta",
    )(a_log, dt_bias, ub.reshape(B, S, CONV_DIM), zb.reshape(B, S, B_WIDTH), ab.reshape(B, S, LANES),
      conv_w, o_norm_g.reshape(1, LANES))


def _delta_dec_kernel(alog_ref, dtb_ref, ub_ref, cs_ref, zb_ref, ab_ref, w_ref, og_ref, st_ref,
                      o_ref, so_ref, *, nb):
    i = pl.program_id(0)
    N = ub_ref.shape[0]
    acc = ub_ref[...] * w_ref[CONV_WIDTH - 1:CONV_WIDTH, :]
    for t in range(CONV_WIDTH - 1):
        acc = acc + cs_ref[t] * w_ref[t:t + 1, :]
    c = _silu(acc)
    ab = ab_ref[...]
    lane = lax.broadcasted_iota(jnp.int32, (N, LANES), 1)
    samp = lax.broadcasted_iota(jnp.int32, (B_HEAD_DIM, N), 1)
    row_id = lax.broadcasted_iota(jnp.int32, (N, LANES), 0)
    out_row = lax.broadcasted_iota(jnp.int32, (nb, LANES), 0)
    heads = []
    for h in range(B_HEADS):
        cq = c[:, h * LANES:(h + 1) * LANES]
        ck = c[:, B_WIDTH + h * LANES:B_WIDTH + (h + 1) * LANES]
        v = c[:, 2 * B_WIDTH + h * LANES:2 * B_WIDTH + (h + 1) * LANES]
        q = cq * lax.rsqrt(jnp.sum(cq * cq, axis=1, keepdims=True) + 1e-6) * (B_HEAD_DIM ** -0.5)
        k = ck * lax.rsqrt(jnp.sum(ck * ck, axis=1, keepdims=True) + 1e-6)
        a_raw = jnp.sum(jnp.where(lane == h, ab, 0.0), axis=1, keepdims=True)
        b_raw = jnp.sum(jnp.where(lane == h + B_HEADS, ab, 0.0), axis=1, keepdims=True)
        neg_a = -jnp.exp(jnp.full((1, 1), alog_ref[h], F32))
        dec = jnp.exp(neg_a * _softplus(a_raw + dtb_ref[h]))
        beta = _sigmoid(b_raw)
        heads.append((q.T, k.T, v, jnp.broadcast_to(dec, (N, LANES)), jnp.broadcast_to(beta, (N, LANES)),
                      zb_ref[:, h * LANES:(h + 1) * LANES]))

    hs = range(B_HEADS)
    q_t, k_t, v, dec, beta, z = zip(*heads)

    def sample(j, o_acc):
        n = i * nb + j
        pick = samp == n
        k_col = [jnp.sum(jnp.where(pick, k_t[h], 0.0), axis=1, keepdims=True) for h in hs]
        q_col = [jnp.sum(jnp.where(pick, q_t[h], 0.0), axis=1, keepdims=True) for h in hs]
        pick_r = row_id == n
        row = lambda t: jnp.sum(jnp.where(pick_r, t, 0.0), axis=0, keepdims=True)
        st = [st_ref[j, h] * row(dec[h]) for h in hs]
        mem = [jnp.sum(k_col[h] * st[h], axis=0, keepdims=True) for h in hs]
        st = [st[h] + k_col[h] * ((row(v[h]) - mem[h]) * row(beta[h])) for h in hs]
        o = [jnp.sum(q_col[h] * st[h], axis=0, keepdims=True) for h in hs]
        o = [o[h] * lax.rsqrt(jnp.mean(o[h] * o[h], axis=1, keepdims=True) + RMS_EPS) * og_ref[...] for h in hs]
        for h in hs:
            so_ref[j, h] = st[h]
        return tuple(jnp.where(out_row == j, o[h] * _silu(row(z[h])), o_acc[h]) for h in hs)

    outs = lax.fori_loop(0, nb, sample, tuple(jnp.zeros((nb, LANES), F32) for _ in hs), unroll=2)
    for h in hs:
        o_ref[:, h * LANES:(h + 1) * LANES] = outs[h]


def _delta_dec(ub_s, conv_state, zb_s, ab_s, conv_w, a_log, dt_bias, o_norm_g, state, nb):
    N = ub_s.shape[0]
    full2 = lambda shape: pl.BlockSpec(shape, lambda i, *_: (0, 0))
    grid_spec = pltpu.PrefetchScalarGridSpec(
        num_scalar_prefetch=2,
        grid=(N // nb,),
        in_specs=[full2((N, CONV_DIM)),
                  pl.BlockSpec((CONV_WIDTH - 1, N, CONV_DIM), lambda i, *_: (0, 0, 0)),
                  full2((N, B_WIDTH)), full2((N, LANES)), full2((CONV_WIDTH, CONV_DIM)), full2((1, LANES)),
                  pl.BlockSpec((nb, B_HEADS, B_HEAD_DIM, B_HEAD_DIM), lambda i, *_: (i, 0, 0, 0))],
        out_specs=[pl.BlockSpec((nb, B_WIDTH), lambda i, *_: (i, 0)),
                   pl.BlockSpec((nb, B_HEADS, B_HEAD_DIM, B_HEAD_DIM), lambda i, *_: (i, 0, 0, 0))],
    )
    return pl.pallas_call(
        functools.partial(_delta_dec_kernel, nb=nb),
        grid_spec=grid_spec,
        out_shape=[jax.ShapeDtypeStruct((N, B_WIDTH), F32), jax.ShapeDtypeStruct(state.shape, F32)],
        compiler_params=_params(("parallel",)),
        name="delta_dec",
    )(a_log, dt_bias, ub_s, jnp.swapaxes(conv_state, 0, 1), zb_s, ab_s, conv_w, o_norm_g.reshape(1, LANES), state)


def _layer_norm(r, g, b):
    mu = jnp.mean(r, axis=1, keepdims=True)
    d = r - mu
    var = jnp.mean(d * d, axis=1, keepdims=True)
    return d * lax.rsqrt(var + LN_EPS) * g + b


def _mix_ln_kernel(*refs, steps):
    outs = refs[-3:]

    @pl.when(pl.program_id(0) < steps)
    def _():
        _mix_ln_rows(*refs[:8], *outs)

    @pl.when(pl.program_id(0) >= steps)
    def _():
        for o in outs:
            o[...] = jnp.zeros(o.shape, F32)


def _mix_ln_rows(oa_ref, ob_ref, x_ref, wo_ref, g_ref, b_ref, wr_ref, br_ref, h_ref, route_ref, cnt_ref):
    if wo_ref.dtype == BF16:
        dot = lambda a, w: jnp.dot(a.astype(BF16), w, preferred_element_type=F32)
    else:
        dot = _fdot
    y = dot(oa_ref[...], wo_ref[0:A_WIDTH, :]) + dot(ob_ref[...], wo_ref[A_WIDTH:A_WIDTH + B_WIDTH, :])
    hcur = _layer_norm(DN_ALPHA * x_ref[...] + y, g_ref[...], b_ref[...])
    h_ref[...] = hcur
    logits = _dot3(hcur, wr_ref[...]) + br_ref[...]
    lane = lax.broadcasted_iota(jnp.int32, logits.shape, 1)
    lane_f = lane.astype(F32)
    ninf = -jnp.inf
    big = 1e9
    gl = jnp.where(lane < N_GROUPS, logits, ninf)
    gmax = jnp.max(gl, axis=1, keepdims=True)
    g_idx = jnp.min(jnp.where(gl == gmax, lane_f, big), axis=1, keepdims=True)
    p_group = 1.0 / jnp.sum(jnp.exp(gl - gmax), axis=1, keepdims=True)
    grp_of_lane = ((lane - N_GROUPS) >> 3).astype(F32)
    sel = (lane >= N_GROUPS) & (lane < N_GROUPS + N_EXPERTS) & (grp_of_lane == g_idx)
    el = jnp.where(sel, logits, ninf)
    v1 = jnp.max(el, axis=1, keepdims=True)
    i1 = jnp.min(jnp.where(el == v1, lane_f, big), axis=1, keepdims=True)
    el2 = jnp.where(lane_f == i1, ninf, el)
    v2 = jnp.max(el2, axis=1, keepdims=True)
    i2 = jnp.min(jnp.where(el2 == v2, lane_f, big), axis=1, keepdims=True)
    t = jnp.exp(v2 - v1)
    gate1 = p_group / (1.0 + t)
    gate2 = p_group * t / (1.0 + t)
    e1, e2 = i1 - N_GROUPS, i2 - N_GROUPS
    route_ref[...] = jnp.where(lane == 0, gate1, jnp.where(lane == 1, gate2, jnp.where(
        lane == 2, e1, jnp.where(lane == 3, e2, 0.0))))
    chosen = ((lane_f == e1) | (lane_f == e2)).astype(F32)
    tm = chosen.shape[0]
    cnt_ref[...] = jnp.sum(chosen.reshape(tm // ROUTE_TILE, ROUTE_TILE, LANES), axis=1)[:, None, :]


def _mix_ln(oa, ob, x2d, wo_b, g, b, wr, br, tm, total, row0, prev=()):
    T, D = x2d.shape
    off = row0 // tm
    steps = T // tm
    tail = 1 if row0 + T < total else 0
    assert total - (row0 + T) <= tm
    row = lambda i: (jnp.minimum(i, steps - 1), 0)
    out_row = lambda i: (off + i, 0)
    fix = lambda i: (0, 0)
    sub = tm // ROUTE_TILE
    return pl.pallas_call(
        functools.partial(_mix_ln_kernel, steps=steps),
        grid=(steps + tail,),
        in_specs=[pl.BlockSpec((tm, A_WIDTH), row), pl.BlockSpec((tm, B_WIDTH), row), pl.BlockSpec((tm, D), row),
                  pl.BlockSpec((A_WIDTH + B_WIDTH, D), fix), pl.BlockSpec((1, D), fix), pl.BlockSpec((1, D), fix),
                  pl.BlockSpec((D, LANES), fix), pl.BlockSpec((1, LANES), fix)]
                 + [pl.BlockSpec(memory_space=pl.ANY)] * len(prev),
        out_specs=[pl.BlockSpec((tm, D), out_row), pl.BlockSpec((tm, LANES), out_row),
                   pl.BlockSpec((sub, 1, LANES), lambda i: (off + i, 0, 0))],
        out_shape=[jax.ShapeDtypeStruct((total, D), F32), jax.ShapeDtypeStruct((total, LANES), F32),
                   jax.ShapeDtypeStruct((total // ROUTE_TILE, 1, LANES), F32)],
        input_output_aliases={8 + j: j for j in range(len(prev))},
        compiler_params=_params(("parallel",)),
        name="mix_ln",
    )(oa, ob, x2d, wo_b, g, b, wr, br, *prev)


def _slot_layout(counts):
    tiles = counts.shape[0]
    n_assign = tiles * ROUTE_TILE * TOP_K
    n_blocks = -(-(n_assign + N_EXPERTS * (MOE_BLK - 1)) // MOE_BLK)
    per_tile = counts.reshape(tiles, LANES)
    earlier = (jnp.arange(tiles)[:, None] > jnp.arange(tiles)[None, :]).astype(F32)
    before = jnp.dot(earlier, per_tile, precision=HIGHEST)
    total = jnp.sum(per_tile, axis=0)[:N_EXPERTS]
    padded = jnp.ceil(total / MOE_BLK) * MOE_BLK
    upto = (jnp.arange(N_EXPERTS)[:, None] <= jnp.arange(N_EXPERTS)[None, :]).astype(F32)
    pad_end = jnp.dot(padded, upto, precision=HIGHEST)
    pad_start = pad_end - padded
    base = (before + jnp.pad(pad_start, (0, LANES - N_EXPERTS))[None, :]).reshape(tiles, 1, LANES)
    blk_start = (jnp.arange(n_blocks) * MOE_BLK).astype(F32)
    in_e = ((pad_start[None, :] <= blk_start[:, None]) & (blk_start[:, None] < pad_end[None, :])).astype(F32)
    used = blk_start < pad_end[-1]
    n_used = jnp.sum(used.astype(jnp.int32))
    last_e = jnp.max(jnp.where(padded > 0, jnp.arange(N_EXPERTS), 0)).astype(F32)
    blk_e = jnp.where(used, jnp.dot(in_e, jnp.arange(N_EXPERTS, dtype=F32), precision=HIGHEST), last_e)
    blk_n = jnp.clip(jnp.dot(in_e, pad_start + total, precision=HIGHEST) - blk_start, 0, MOE_BLK)
    blk_x = jnp.minimum(jnp.arange(n_blocks), jnp.maximum(n_used - 1, 0))
    i32 = lambda t: t.astype(jnp.int32)
    return base, i32(blk_e), i32(blk_n), i32(blk_x), i32(pad_end), n_blocks


def _slot_kernel(route_ref, base_ref, dest_ref):
    lane = lax.broadcasted_iota(jnp.int32, (ROUTE_TILE, LANES), 1)
    lane_f = lane.astype(F32)
    ri = lax.broadcasted_iota(jnp.int32, (ROUTE_TILE, ROUTE_TILE), 0)
    ci = lax.broadcasted_iota(jnp.int32, (ROUTE_TILE, ROUTE_TILE), 1)
    before = (ri > ci).astype(BF16)
    for t in range(base_ref.shape[0]):
        rows = slice(t * ROUTE_TILE, (t + 1) * ROUTE_TILE)
        route = route_ref[rows, :]
        e1 = jnp.sum(jnp.where(lane == 2, route, 0.0), axis=1, keepdims=True)
        e2 = jnp.sum(jnp.where(lane == 3, route, 0.0), axis=1, keepdims=True)
        oh1, oh2 = lane_f == e1, lane_f == e2
        earlier = _bdot(before, (oh1 | oh2).astype(F32))
        slot = base_ref[t] + earlier
        d1 = jnp.sum(jnp.where(oh1, slot, 0.0), axis=1, keepdims=True)
        d2 = jnp.sum(jnp.where(oh2, slot, 0.0), axis=1, keepdims=True)
        dest_ref[rows, :] = jnp.where(lane == 0, d1, jnp.where(lane == 1, d2, 0.0)).astype(jnp.int32)


def _slots(route, base):
    tiles = route.shape[0] // ROUTE_TILE
    sub = max(d for d in range(1, 9) if tiles % d == 0)
    return pl.pallas_call(
        _slot_kernel,
        grid=(tiles // sub,),
        in_specs=[pl.BlockSpec((sub * ROUTE_TILE, LANES), lambda i: (i, 0)),
                  pl.BlockSpec((sub, 1, LANES), lambda i: (i, 0, 0))],
        out_specs=pl.BlockSpec((sub * ROUTE_TILE, LANES), lambda i: (i, 0)),
        out_shape=jax.ShapeDtypeStruct(route.shape, jnp.int32),
        compiler_params=_params(("parallel",)),
        name="slots",
    )(route, base)


def _dispatch_kernel(dest_ref, pad_end_ref, h_ref, xs_hbm, hbuf, zbuf, sem, zsem):
    i = pl.program_id(0)
    slot = i % 2
    a0 = i * (ROUTE_TILE * TOP_K)

    @pl.when(i == 0)
    def _():
        zbuf[...] = jnp.zeros(zbuf.shape, F32)
        fills = [pltpu.make_async_copy(zbuf, xs_hbm.at[pl.ds(pl.multiple_of(pad_end_ref[e] - MOE_BLK, MOE_BLK), MOE_BLK)],
                                       zsem) for e in range(N_EXPERTS)]
        has_rows = [pad_end_ref[e] > (pad_end_ref[e - 1] if e else 0) for e in range(N_EXPERTS)]
        for e in range(N_EXPERTS):
            @pl.when(has_rows[e])
            def _(e=e):
                fills[e].start()
        for e in range(N_EXPERTS):
            @pl.when(has_rows[e])
            def _(e=e):
                fills[e].wait()

        def tail(b):
            return pltpu.make_async_copy(zbuf, xs_hbm.at[pl.ds(pl.multiple_of(b * MOE_BLK, MOE_BLK), MOE_BLK)], zsem)

        def tail_start(b, c):
            tail(b).start()
            return c

        def tail_wait(b, c):
            tail(b).wait()
            return c

        first_unused = pad_end_ref[N_EXPERTS - 1] // MOE_BLK
        lax.fori_loop(first_unused, xs_hbm.shape[0] // MOE_BLK, tail_start, 0)
        lax.fori_loop(first_unused, xs_hbm.shape[0] // MOE_BLK, tail_wait, 0)

    hbuf[slot] = h_ref[...]

    def start(r, c):
        for k in range(TOP_K):
            pltpu.make_async_copy(hbuf.at[slot, pl.ds(r, 1)], xs_hbm.at[pl.ds(dest_ref[a0 + r * TOP_K + k], 1)],
                                  sem.at[slot]).start()
        return c

    lax.fori_loop(0, ROUTE_TILE, start, 0, unroll=8)

    def drain(s):
        for k in range(TOP_K):
            pltpu.make_async_copy(hbuf.at[s], xs_hbm.at[pl.ds(0, ROUTE_TILE)], sem.at[s]).wait()

    @pl.when(i > 0)
    def _():
        drain(1 - slot)

    @pl.when(i == pl.num_programs(0) - 1)
    def _():
        drain(slot)


def _dispatch(h_all, dest, pad_end, slots):
    T, D = h_all.shape
    grid_spec = pltpu.PrefetchScalarGridSpec(
        num_scalar_prefetch=2,
        grid=(T // ROUTE_TILE,),
        in_specs=[pl.BlockSpec((ROUTE_TILE, D), lambda i, *_: (i, 0))],
        out_specs=pl.BlockSpec(memory_space=pl.ANY),
        scratch_shapes=[pltpu.VMEM((2, ROUTE_TILE, D), F32), pltpu.VMEM((MOE_BLK, D), F32),
                        pltpu.SemaphoreType.DMA((2,)), pltpu.SemaphoreType.DMA],
    )
    return pl.pallas_call(
        _dispatch_kernel,
        grid_spec=grid_spec,
        out_shape=jax.ShapeDtypeStruct((slots, D), F32),
        compiler_params=_params(("arbitrary",)),
        name="dispatch",
    )(dest, pad_end, h_all)


def _moe_kernel(blk_e_ref, blk_n_ref, blk_x_ref, x_ref, wg_ref, wu_ref, wd_ref, y_ref, wg_b, wu_b, wd_b):
    del blk_x_ref
    i = pl.program_id(0)
    n_valid = blk_n_ref[i]

    @pl.when((i == 0) | (blk_e_ref[i] != blk_e_ref[jnp.maximum(i - 1, 0)]))
    def _():
        wg_b[...] = wg_ref[0].astype(BF16)
        wu_b[...] = wu_ref[0].astype(BF16)
        wd_b[...] = wd_ref[0].astype(BF16)

    @pl.when(n_valid > 0)
    def _():
        x = x_ref[...].astype(BF16)
        a = jnp.dot(x, wg_b[...], preferred_element_type=F32)
        u = jnp.dot(x, wu_b[...], preferred_element_type=F32)
        y_ref[...] = jnp.dot((_silu(a) * u).astype(BF16), wd_b[...], preferred_element_type=F32)

    @pl.when(n_valid == 0)
    def _():
        y_ref[...] = jnp.zeros(y_ref.shape, F32)


def _moe(xs, blk_e, blk_n, blk_x, w_gate, w_up, w_down):
    slots, D = xs.shape
    De = w_gate.shape[-1]
    wspec = lambda shape: pl.BlockSpec((1,) + shape, lambda i, be, *_: (be[i], 0, 0))
    grid_spec = pltpu.PrefetchScalarGridSpec(
        num_scalar_prefetch=3,
        grid=(slots // MOE_BLK,),
        in_specs=[pl.BlockSpec((MOE_BLK, D), lambda i, be, bn, bx: (bx[i], 0)),
                  wspec((D, De)), wspec((D, De)), wspec((De, D))],
        out_specs=pl.BlockSpec((MOE_BLK, D), lambda i, *_: (i, 0)),
        scratch_shapes=[pltpu.VMEM((D, De), BF16), pltpu.VMEM((D, De), BF16), pltpu.VMEM((De, D), BF16)],
    )
    return pl.pallas_call(
        _moe_kernel,
        grid_spec=grid_spec,
        out_shape=jax.ShapeDtypeStruct((slots, D), F32),
        compiler_params=_params(("arbitrary",)),
        name="moe",
    )(blk_e, blk_n, blk_x, xs, w_gate, w_up, w_down)


def _final_ln_kernel(dest_ref, h_ref, route_ref, g_ref, b_ref, ys_hbm, o_ref, ybuf, sem, *, tile0):
    i = pl.program_id(0)
    slot = i % 2

    def gather(tile, s):
        a0 = (tile0 + tile) * (ROUTE_TILE * TOP_K)

        def start(r, c):
            for k in range(TOP_K):
                pltpu.make_async_copy(ys_hbm.at[pl.ds(dest_ref[a0 + r * TOP_K + k], 1)],
                                      ybuf.at[s, k, pl.ds(r, 1)], sem.at[s]).start()
            return c

        lax.fori_loop(0, ROUTE_TILE, start, 0, unroll=8)

    @pl.when(i == 0)
    def _():
        gather(0, 0)

    @pl.when(i + 1 < pl.num_programs(0))
    def _():
        gather(i + 1, 1 - slot)

    for k in range(TOP_K):
        pltpu.make_async_copy(ys_hbm.at[pl.ds(0, ROUTE_TILE)], ybuf.at[slot, k], sem.at[slot]).wait()
    route = route_ref[...]
    lane = lax.broadcasted_iota(jnp.int32, route.shape, 1)
    gate1 = jnp.sum(jnp.where(lane == 0, route, 0.0), axis=1, keepdims=True)
    gate2 = jnp.sum(jnp.where(lane == 1, route, 0.0), axis=1, keepdims=True)
    f = ybuf[slot, 0] * gate1 + ybuf[slot, 1] * gate2
    o_ref[...] = _layer_norm(DN_ALPHA * h_ref[...] + f, g_ref[...], b_ref[...])


def _final_ln(h_all, route, dest, ys, g, b, row0, rows):
    D = h_all.shape[1]
    tile0 = row0 // ROUTE_TILE
    row = lambda i, *_: (tile0 + i, 0)
    fix = lambda i, *_: (0, 0)
    grid_spec = pltpu.PrefetchScalarGridSpec(
        num_scalar_prefetch=1,
        grid=(rows // ROUTE_TILE,),
        in_specs=[pl.BlockSpec((ROUTE_TILE, D), row), pl.BlockSpec((ROUTE_TILE, LANES), row),
                  pl.BlockSpec((1, D), fix), pl.BlockSpec((1, D), fix), pl.BlockSpec(memory_space=pl.ANY)],
        out_specs=pl.BlockSpec((ROUTE_TILE, D), lambda i, *_: (i, 0)),
        scratch_shapes=[pltpu.VMEM((2, TOP_K, ROUTE_TILE, D), F32), pltpu.SemaphoreType.DMA((2,))],
    )
    return pl.pallas_call(
        functools.partial(_final_ln_kernel, tile0=tile0),
        grid_spec=grid_spec,
        out_shape=jax.ShapeDtypeStruct((rows, D), F32),
        compiler_params=_params(("arbitrary",)),
        name="final_ln",
    )(dest, h_all, route, g, b, ys)


def kernel(x_prompt, x_sample, cache_a_k, cache_a_v, state_b_ssm, state_b_conv, w_in, rel_bias, conv_w, a_log, dt_bias, o_norm_g, w_out, ln1_g, ln1_b, w_group, b_group, w_router, b_router, w_gate, w_up, w_down, ln2_g, ln2_b):
    B, S, D = x_prompt.shape
    N, T = x_sample.shape[0], x_sample.shape[1]
    depth = w_in.shape[0]
    assert depth == 1 and T == 1 and S % ATT_TILE == 0 and N % ROUTE_TILE == 0 and cache_a_k.shape[2] % LANES == 0
    l = 0
    win_p = min(BRANCHES[-1][0], S)

    w_pad32 = jnp.pad(w_in[l], ((0, 0), (0, IN_COLS_PAD - IN_COLS)))
    w_pad = w_pad32.astype(BF16)
    wo_b = w_out[l].astype(BF16)
    wr = jnp.pad(jnp.concatenate([w_group[l], w_router[l]], axis=1), ((0, 0), (0, LANES - N_GROUPS - N_EXPERTS)))
    br = jnp.pad(jnp.concatenate([b_group[l], b_router[l].reshape(-1)]), (0, LANES - N_GROUPS - N_EXPERTS))[None, :]
    g1, b1 = ln1_g[l][None, :], ln1_b[l][None, :]
    g2, b2 = ln2_g[l][None, :], ln2_b[l][None, :]

    xp = x_prompt.reshape(B * S, D)
    qkv_p, ub_p, zb_p, ab_p = _proj(xp, w_pad, 512)
    oa_p = _attn(qkv_p, _band_bias(rel_bias), B, S)
    ob_p, st_p = _delta(ub_p, zb_p, ab_p, conv_w[l], a_log[l], dt_bias[l], o_norm_g[l], B, S)
    rows_all = B * S + N
    routed_p = _mix_ln(oa_p.reshape(B * S, A_WIDTH), ob_p.reshape(B * S, B_WIDTH), xp, wo_b, g1, b1, wr, br,
                       512, rows_all, 0)

    xs = x_sample.reshape(N, D)
    qkv_s, ub_s, zb_s, ab_s = _proj(xs, w_pad32, N)
    nb = 2
    qkv_t = jnp.transpose(qkv_s.reshape(N, 3, A_HEADS, A_HEAD_DIM), (1, 2, 3, 0))
    oa_s = _attn_dec(qkv_t, jnp.transpose(cache_a_k[l], (0, 2, 3, 1)), jnp.transpose(cache_a_v[l], (0, 2, 3, 1)),
                     _cache_bias(rel_bias, cache_a_k.shape[2]), nb)
    oa_s = jnp.transpose(oa_s, (0, 3, 1, 2)).reshape(N, A_WIDTH)
    ob_s, st_s = _delta_dec(ub_s, state_b_conv[l], zb_s, ab_s, conv_w[l], a_log[l], dt_bias[l], o_norm_g[l],
                            state_b_ssm[l], 16)
    h_all, route, cnt = _mix_ln(oa_s, ob_s, xs, w_out[l], g1, b1, wr, br, N, rows_all, B * S, prev=routed_p)

    base, blk_e, blk_n, blk_x, pad_end, n_blocks = _slot_layout(cnt)
    dest = _slots(route, base)[:, 0:TOP_K].reshape(-1)
    xs = _dispatch(h_all, dest, pad_end, n_blocks * MOE_BLK)
    ys = _moe(xs, blk_e, blk_n, blk_x, w_gate[l], w_up[l], w_down[l])
    y_p = _final_ln(h_all, route, dest, ys, g2, b2, 0, B * S)
    y_s = _final_ln(h_all, route, dest, ys, g2, b2, B * S, N)

    w_kv_t = jnp.transpose(w_in[l][:, A_WIDTH:3 * A_WIDTH]).astype(BF16)
    k_win, v_win = _kv_win(x_prompt, w_kv_t, win_p, 512)
    to_rows = lambda t: jnp.transpose(t, (0, 3, 1, 2))[None]
    conv_p = ub_p.reshape(B, S, CONV_DIM)[:, S - (CONV_WIDTH - 1):]
    conv_s = jnp.concatenate([state_b_conv[l], ub_s[:, None, :]], axis=1)[:, T:]
    new_kv = lambda t: jnp.transpose(t, (2, 0, 1))[None, :, None]
    return (y_p.reshape(B, S, D), y_s.reshape(N, T, D), to_rows(k_win), to_rows(v_win),
            new_kv(qkv_t[1]), new_kv(qkv_t[2]), st_p[None], st_s[None], conv_p[None], conv_s[None])
```

```python
import functools
import math

import jax
import jax.numpy as jnp
from jax import lax
from jax.experimental import pallas as pl
from jax.experimental.pallas import tpu as pltpu

F32 = jnp.float32
BF16 = jnp.bfloat16
HIGHEST = lax.Precision.HIGHEST

LANES = 128
A_HEADS = 8
A_HEAD_DIM = 64
A_WIDTH = A_HEADS * A_HEAD_DIM
BRANCHES = ((128, 1), (512, 4), (2048, 16))
BAND = 128
ATT_TILE = BAND * 16
REL_BUCKETS = 32
REL_MAX_DIST = 2048
B_HEADS = 4
B_HEAD_DIM = 128
B_WIDTH = B_HEADS * B_HEAD_DIM
CONV_WIDTH = 4
CONV_DIM = 3 * B_WIDTH
CHUNK = 64
COL_UB = 3 * A_WIDTH
COL_ZB = COL_UB + CONV_DIM
COL_AB = COL_ZB + B_WIDTH
IN_COLS = COL_AB + 2 * B_HEADS
IN_COLS_PAD = COL_AB + LANES
N_GROUPS = 4
EXPERTS_PER_GROUP = 8
N_EXPERTS = N_GROUPS * EXPERTS_PER_GROUP
TOP_K = 2
DN_ALPHA = 2.0 ** 0.25
LN_EPS = 1e-5
RMS_EPS = 1e-6
MASKED = -1e30
MOE_BLK = 256
ROUTE_TILE = 128
DELTA_TILE = 1024
ATT_GROUP = 4
VMEM_LIMIT = 56 * 1024 * 1024


def _bdot(a, b):
    return jnp.dot(a.astype(BF16), b.astype(BF16), preferred_element_type=F32)


def _bdot_nt(a, b):
    return lax.dot_general(a.astype(BF16), b.astype(BF16), (((1,), (1,)), ((), ())), preferred_element_type=F32)


def _bdot_tn(a, b):
    return lax.dot_general(a.astype(BF16), b.astype(BF16), (((0,), (0,)), ((), ())), preferred_element_type=F32)


def _fdot(a, b):
    return jnp.dot(a, b, precision=HIGHEST, preferred_element_type=F32)


def _sigmoid(x):
    return 1.0 / (1.0 + jnp.exp(-x))


def _silu(x):
    return x * _sigmoid(x)


def _softplus(x):
    return jnp.maximum(x, 0.0) + jnp.log(1.0 + jnp.exp(-jnp.abs(x)))


def _params(sem):
    return pltpu.CompilerParams(dimension_semantics=sem, vmem_limit_bytes=VMEM_LIMIT)


def _proj_kernel(x_ref, w_ref, qkv_ref, ub_ref, zb_ref, ab_ref):
    if w_ref.dtype == BF16:
        xb = x_ref[...].astype(BF16)
        dot = lambda w: jnp.dot(xb, w, preferred_element_type=F32)
    else:
        dot = lambda w: _fdot(x_ref[...], w)
    qkv_ref[...] = dot(w_ref[:, 0:COL_UB])
    ub_ref[...] = dot(w_ref[:, COL_UB:COL_ZB])
    zb_ref[...] = dot(w_ref[:, COL_ZB:COL_AB])
    ab_ref[...] = dot(w_ref[:, COL_AB:IN_COLS_PAD])


def _proj(x2d, w_pad, tm):
    T, D = x2d.shape
    row = lambda i: (i, 0)
    return pl.pallas_call(
        _proj_kernel,
        grid=(T // tm,),
        in_specs=[pl.BlockSpec((tm, D), row), pl.BlockSpec((D, IN_COLS_PAD), lambda i: (0, 0))],
        out_specs=[pl.BlockSpec((tm, COL_UB), row), pl.BlockSpec((tm, CONV_DIM), row),
                   pl.BlockSpec((tm, B_WIDTH), row), pl.BlockSpec((tm, LANES), row)],
        out_shape=[jax.ShapeDtypeStruct((T, COL_UB), F32), jax.ShapeDtypeStruct((T, CONV_DIM), F32),
                   jax.ShapeDtypeStruct((T, B_WIDTH), F32), jax.ShapeDtypeStruct((T, LANES), F32)],
        compiler_params=_params(("parallel",)),
        name="proj",
    )(x2d, w_pad)


def _kv_win_kernel(x_ref, wt_ref, k_ref, v_ref):
    tm = x_ref.shape[1]
    kv = lax.dot_general(wt_ref[...], x_ref[0].astype(BF16), (((1,), (1,)), ((), ())), preferred_element_type=F32)
    k_ref[0] = kv[0:A_WIDTH].reshape(A_HEADS, A_HEAD_DIM, tm)
    v_ref[0] = kv[A_WIDTH:2 * A_WIDTH].reshape(A_HEADS, A_HEAD_DIM, tm)


def _kv_win(x_prompt, w_kv_t, win, tm):
    B, S, D = x_prompt.shape
    t0 = (S - win) // tm
    out = jax.ShapeDtypeStruct((B, A_HEADS, A_HEAD_DIM, win), F32)
    ospec = pl.BlockSpec((1, A_HEADS, A_HEAD_DIM, tm), lambda b, t: (b, 0, 0, t))
    return pl.pallas_call(
        _kv_win_kernel,
        grid=(B, win // tm),
        in_specs=[pl.BlockSpec((1, tm, D), lambda b, t: (b, t0 + t, 0)),
                  pl.BlockSpec((2 * A_WIDTH, D), lambda b, t: (0, 0))],
        out_specs=[ospec, ospec],
        out_shape=[out, out],
        compiler_params=_params(("parallel", "parallel")),
        name="kv_win",
    )(x_prompt, w_kv_t)


def _rel_bucket(dist):
    max_exact = REL_BUCKETS // 2
    n = jnp.maximum(dist, 0)
    ratio = jnp.maximum(n, 1).astype(F32) / max_exact
    large = max_exact + (jnp.log(ratio) / math.log(REL_MAX_DIST / max_exact)
                         * (REL_BUCKETS - max_exact)).astype(jnp.int32)
    return jnp.where(n < max_exact, n, jnp.minimum(large, REL_BUCKETS - 1))


def _bias_of(rel_bias, dist):
    onehot = (_rel_bucket(dist)[None, :] == jnp.arange(REL_BUCKETS)[:, None]).astype(F32)
    return jnp.dot(rel_bias.astype(F32).T, onehot, precision=HIGHEST)


def _band_bias(rel_bias):
    period = 3 * BAND
    tabs = []
    for _, dil in BRANCHES:
        g = jnp.concatenate([_bias_of(rel_bias, (BAND - jnp.arange(BAND + 1)) * dil),
                             jnp.full((A_HEADS, period - BAND - 1), MASKED, F32)], axis=1)
        skew = jnp.tile(g, (1, BAND))[:, :BAND * (period - 1)].reshape(A_HEADS, BAND, period - 1)
        tabs.append(skew[:, :, :2 * BAND])
    return jnp.stack(tabs)


def _cache_bias(rel_bias, P):
    dist = P - jnp.arange(P + LANES)
    tabs = []
    for window, dil in BRANCHES:
        ok = (dist >= 0) & (dist <= window) & (dist % dil == 0)
        tabs.append(jnp.where(ok[None, :], _bias_of(rel_bias, dist), MASKED)[:, None, :])
    return jnp.stack(tabs)


def _attn_kernel(q_ref, k_ref, v_ref, bias_ref, o_ref, acc_ref, m_ref, l_ref):
    t = pl.program_id(2)
    tile0 = t * ATT_TILE
    lane = lax.broadcasted_iota(jnp.int32, (BAND, LANES), 1)
    head0 = lane < A_HEAD_DIM

    def rows(start, dil):
        return pl.ds(pl.multiple_of(start, BAND), BAND) if dil == 1 else pl.ds(start, BAND, stride=dil)

    for br, (_, dil) in enumerate(BRANCHES):
        span = BAND * dil

        def blocks(it, carry, br=br, dil=dil, span=span):
            gs = range(ATT_GROUP)
            idx = [it * ATT_GROUP + g for g in gs]
            start = [(i % dil) + (i // dil) * span for i in idx]
            cur = [tile0 + s for s in start]
            first = [c < span for c in cur]
            prev = [jnp.where(first[g], cur[g], cur[g] - span) for g in gs]
            q = [q_ref[0, rows(start[g], dil), :] * (A_HEAD_DIM ** -0.5) for g in gs]
            kk = [jnp.concatenate([k_ref[0, rows(prev[g], dil), :], k_ref[0, rows(cur[g], dil), :]],
                                  axis=0).astype(BF16) for g in gs]
            vv = [jnp.concatenate([v_ref[0, rows(prev[g], dil), :], v_ref[0, rows(cur[g], dil), :]],
                                  axis=0).astype(BF16) for g in gs]
            pen = [jnp.where(first[g], MASKED, 0.0) for g in gs]
            gh = [(g, hh) for g in gs for hh in range(2)]
            qh = [jnp.where(head0 if hh == 0 else ~head0, q[g], 0.0).astype(BF16) for g, hh in gh]
            s = [lax.dot_general(qh[j], kk[g], (((1,), (1,)), ((), ())), preferred_element_type=F32)
                 for j, (g, hh) in enumerate(gh)]
            s = [s[j] + bias_ref[br, hh] for j, (g, hh) in enumerate(gh)]
            s = [jnp.concatenate([s[j][:, 0:BAND] + pen[g], s[j][:, BAND:2 * BAND]], axis=1)
                 for j, (g, hh) in enumerate(gh)]
            m = [jnp.max(t, axis=1, keepdims=True) for t in s]
            p = [jnp.exp(s[j] - m[j]) for j in range(len(gh))]
            l = [jnp.sum(t, axis=1, keepdims=True) for t in p]
            pv = [jnp.dot(p[j].astype(BF16), vv[g], preferred_element_type=F32) for j, (g, hh) in enumerate(gh)]
            for g in gs:
                acc_ref[br, rows(start[g], dil), :] = jnp.where(head0, pv[2 * g], pv[2 * g + 1])
                m_ref[br, rows(start[g], dil), :] = jnp.where(head0, m[2 * g], m[2 * g + 1])
                l_ref[br, rows(start[g], dil), :] = jnp.where(head0, l[2 * g], l[2 * g + 1])
            return carry

        lax.fori_loop(0, ATT_TILE // BAND // ATT_GROUP, blocks, 0)

    def merge(c, carry):
        r = pl.ds(pl.multiple_of(c * 256, 256), 256)
        m0, m1, m2 = m_ref[0, r, :], m_ref[1, r, :], m_ref[2, r, :]
        mx = jnp.maximum(jnp.maximum(m0, m1), m2)
        w0, w1, w2 = jnp.exp(m0 - mx), jnp.exp(m1 - mx), jnp.exp(m2 - mx)
        num = w0 * acc_ref[0, r, :] + w1 * acc_ref[1, r, :] + w2 * acc_ref[2, r, :]
        den = w0 * l_ref[0, r, :] + w1 * l_ref[1, r, :] + w2 * l_ref[2, r, :]
        o_ref[0, r, :] = num / den
        return carry

    lax.fori_loop(0, ATT_TILE // 256, merge, 0)


def _attn(qkv, bias, B, S):
    n_pairs = A_HEADS // 2
    qkv3 = qkv.reshape(B, S, 3 * A_WIDTH)
    return pl.pallas_call(
        _attn_kernel,
        grid=(B, n_pairs, S // ATT_TILE),
        in_specs=[pl.BlockSpec((1, ATT_TILE, LANES), lambda b, hp, t: (b, t, hp)),
                  pl.BlockSpec((1, S, LANES), lambda b, hp, t: (b, 0, n_pairs + hp)),
                  pl.BlockSpec((1, S, LANES), lambda b, hp, t: (b, 0, 2 * n_pairs + hp)),
                  pl.BlockSpec((3, 2, BAND, 2 * BAND), lambda b, hp, t: (0, hp, 0, 0))],
        out_specs=pl.BlockSpec((1, ATT_TILE, LANES), lambda b, hp, t: (b, t, hp)),
        out_shape=jax.ShapeDtypeStruct((B, S, A_WIDTH), F32),
        scratch_shapes=[pltpu.VMEM((3, ATT_TILE, LANES), F32)] * 3,
        compiler_params=_params(("parallel", "parallel", "arbitrary")),
        name="attn",
    )(qkv3, qkv3, qkv3, bias)


def _attn_dec_kernel(qkv_ref, kt_ref, vt_ref, bias_ref, o_ref, *, nb):
    i = pl.program_id(0)
    N = qkv_ref.shape[-1]
    P = kt_ref.shape[-1]
    lane_n = lax.broadcasted_iota(jnp.int32, (A_HEAD_DIM, N), 1)
    lane_o = lax.broadcasted_iota(jnp.int32, (A_HEAD_DIM, nb), 1)
    lane_t = lax.broadcasted_iota(jnp.int32, (1, LANES), 1)

    def head(h, carry):
        slab = jnp.zeros((A_HEAD_DIM, nb), F32)
        for j in range(nb):
            pick = lane_n == i * nb + j
            col = lambda t: jnp.sum(jnp.where(pick, t, 0.0), axis=1, keepdims=True)
            q = col(qkv_ref[0, h]) * (A_HEAD_DIM ** -0.5)
            k_new, v_new = col(qkv_ref[1, h]), col(qkv_ref[2, h])
            s_new = jnp.sum(q * k_new, axis=0, keepdims=True)
            s = jnp.concatenate([jnp.sum(kt_ref[j, h] * q, axis=0, keepdims=True),
                                 jnp.where(lane_t == 0, s_new, 0.0)], axis=1)
            ps, ms, ls = [], [], []
            for br in range(3):
                sb = s + bias_ref[br, h]
                m = jnp.max(sb, axis=1, keepdims=True)
                p = jnp.exp(sb - m)
                ps.append(p)
                ms.append(m)
                ls.append(jnp.sum(p, axis=1, keepdims=True))
            mx = jnp.maximum(jnp.maximum(ms[0], ms[1]), ms[2])
            w = jnp.zeros((1, P + LANES), F32)
            den = jnp.zeros((1, 1), F32)
            for p, m, l in zip(ps, ms, ls):
                e = jnp.exp(m - mx)
                w = w + e * p
                den = den + e * l
            o = jnp.sum(vt_ref[j, h] * w[:, 0:P], axis=1, keepdims=True) + v_new * w[:, P:P + 1]
            slab = jnp.where(lane_o == j, o / den, slab)
        o_ref[0, h] = slab
        return carry

    lax.fori_loop(0, A_HEADS, head, 0, unroll=2)


def _attn_dec(qkv_t, cache_kt, cache_vt, bias, nb):
    N, P = cache_kt.shape[0], cache_kt.shape[-1]
    cache_spec = pl.BlockSpec((nb, A_HEADS, A_HEAD_DIM, P), lambda i: (i, 0, 0, 0))
    return pl.pallas_call(
        functools.partial(_attn_dec_kernel, nb=nb),
        grid=(N // nb,),
        in_specs=[pl.BlockSpec((3, A_HEADS, A_HEAD_DIM, N), lambda i: (0, 0, 0, 0)), cache_spec, cache_spec,
                  pl.BlockSpec((3, A_HEADS, 1, P + LANES), lambda i: (0, 0, 0, 0))],
        out_specs=pl.BlockSpec((1, A_HEADS, A_HEAD_DIM, nb), lambda i: (i, 0, 0, 0)),
        out_shape=jax.ShapeDtypeStruct((N // nb, A_HEADS, A_HEAD_DIM, nb), F32),
        compiler_params=_params(("parallel",)),
        name="attn_dec",
    )(qkv_t, cache_kt, cache_vt, bias)


def _split3(x):
    hi = x.astype(BF16)
    r = x - hi.astype(F32)
    mid = r.astype(BF16)
    return hi, mid, (r - mid.astype(F32)).astype(BF16)


def _tril_dot(tril_b, g):
    return sum(jnp.dot(tril_b, piece, preferred_element_type=F32) for piece in _split3(g))


def _dot3(a, b):
    ah, bh = a.astype(BF16), b.astype(BF16)
    al, bl = (a - ah.astype(F32)).astype(BF16), (b - bh.astype(F32)).astype(BF16)
    d = lambda x, y: jnp.dot(x, y, preferred_element_type=F32)
    return d(ah, bh) + (d(ah, bl) + d(al, bh))


def _delta_kernel(alog_ref, dtb_ref, ub_ref, z_ref, ab_ref, cw_ref, og_ref, o_ref, st_ref,
                  pad_ref, u_s, wq_s, kt_s, qk_s, gl_s):
    t = pl.program_id(1)
    TS = ub_ref.shape[1]
    nch = TS // CHUNK
    hdr = 8

    @pl.when(t == 0)
    def _():
        pad_ref[0:hdr, :] = jnp.zeros((hdr, CONV_DIM), F32)
        st_ref[...] = jnp.zeros(st_ref.shape, F32)

    @pl.when(t > 0)
    def _():
        pad_ref[0:hdr, :] = pad_ref[TS:TS + hdr, :]

    pad_ref[hdr:hdr + TS, :] = ub_ref[0]

    ri = lax.broadcasted_iota(jnp.int32, (CHUNK, CHUNK), 0)
    ci = lax.broadcasted_iota(jnp.int32, (CHUNK, CHUNK), 1)
    incl = ri >= ci
    strict = ri > ci
    tril_b = incl.astype(BF16)
    eye = (ri == ci).astype(F32)
    lane = lax.broadcasted_iota(jnp.int32, (CHUNK, LANES), 1)

    def local(c, carry):
        base = c * CHUNK if isinstance(c, int) else pl.multiple_of(c * CHUNK, CHUNK)
        ab = ab_ref[0, pl.ds(base, CHUNK), :]
        heads = []
        for h in range(B_HEADS):
            def conv(col):
                win = pad_ref[pl.ds(base, CHUNK + hdr), col:col + LANES]
                acc = win[hdr - 3:hdr - 3 + CHUNK] * cw_ref[0:1, col:col + LANES]
                for i in range(1, CONV_WIDTH):
                    acc = acc + win[hdr - 3 + i:hdr - 3 + i + CHUNK] * cw_ref[i:i + 1, col:col + LANES]
                return _silu(acc)

            cq, ck, v = conv(h * LANES), conv(B_WIDTH + h * LANES), conv(2 * B_WIDTH + h * LANES)
            q = cq * lax.rsqrt(jnp.sum(cq * cq, axis=1, keepdims=True) + 1e-6) * (B_HEAD_DIM ** -0.5)
            k = ck * lax.rsqrt(jnp.sum(ck * ck, axis=1, keepdims=True) + 1e-6)
            a_raw = jnp.sum(jnp.where(lane == h, ab, 0.0), axis=1, keepdims=True)
            b_raw = jnp.sum(jnp.where(lane == h + B_HEADS, ab, 0.0), axis=1, keepdims=True)
            neg_a = -jnp.exp(jnp.full((1, LANES), alog_ref[h], F32))
            g = neg_a * _softplus(a_raw + dtb_ref[h])
            beta = _sigmoid(b_raw)
            heads.append((q, k, v, g, beta))
        hs = range(B_HEADS)
        q, k, v, g, beta = zip(*heads)
        gc = [_tril_dot(tril_b, g[h]) for h in hs]
        dmat = [_tril_dot(tril_b, jnp.where(strict, g[h][:, 0:CHUNK], 0.0)) for h in hs]
        kq = [_bdot_nt(jnp.concatenate([k[h], q[h]], axis=0), k[h]) for h in hs]
        decay = [jnp.where(incl, jnp.exp(dmat[h]), 0.0) for h in hs]
        a = [jnp.where(strict, beta[h] * kq[h][0:CHUNK] * decay[h], 0.0) for h in hs]
        x = [eye - a[h] for h in hs]
        p = [_bdot(a[h], a[h]) for h in hs]
        for _ in range(int(math.log2(CHUNK)) - 2):
            r = [_bdot(jnp.concatenate([x[h], p[h]], axis=0), p[h]) for h in hs]
            x = [x[h] + r[h][0:CHUNK] for h in hs]
            p = [r[h][CHUNK:2 * CHUNK] for h in hs]
        x = [x[h] + _bdot(x[h], p[h]) for h in hs]
        e_gc = [jnp.exp(gc[h]) for h in hs]
        sol = [_bdot(x[h], jnp.concatenate([v[h] * beta[h], k[h] * (beta[h] * e_gc[h])], axis=1)) for h in hs]
        for h in hs:
            gc_last = gc[h][CHUNK - 1:CHUNK, :]
            u_s[h, c] = sol[h][:, 0:LANES]
            wq_s[h, c, 0:CHUNK] = sol[h][:, LANES:2 * LANES]
            wq_s[h, c, CHUNK:2 * CHUNK] = q[h] * e_gc[h]
            kt_s[h, c] = k[h] * jnp.exp(gc_last - gc[h])
            qk_s[h, c] = kq[h][CHUNK:2 * CHUNK] * decay[h]
            gl_s[h, c] = jnp.broadcast_to(jnp.exp(gc_last), (8, LANES))
        return carry

    def scan(c, carry):
        base = c * CHUNK if isinstance(c, int) else pl.multiple_of(c * CHUNK, CHUNK)
        hs = range(B_HEADS)
        state = [st_ref[0, h] for h in hs]
        r = [_bdot(wq_s[h, c], state[h]) for h in hs]
        v_new = [u_s[h, c] - r[h][0:CHUNK] for h in hs]
        upd = [_bdot_tn(kt_s[h, c], v_new[h]) for h in hs]
        out = [r[h][CHUNK:2 * CHUNK] + _bdot(qk_s[h, c], v_new[h]) for h in hs]
        for h in hs:
            st_ref[0, h] = state[h] * gl_s[h, c][0:1, :] + upd[h]
            o = out[h] * lax.rsqrt(jnp.mean(out[h] * out[h], axis=1, keepdims=True) + RMS_EPS) * og_ref[...]
            cols = slice(h * LANES, (h + 1) * LANES)
            o_ref[0, pl.ds(base, CHUNK), cols] = o * _silu(z_ref[0, pl.ds(base, CHUNK), cols])
        return carry

    def both(c, carry):
        scan(c - 1, carry)
        return local(c, carry)

    local(0, 0)
    lax.fori_loop(1, nch, both, 0)
    scan(nch - 1, 0)


def _delta(ub, zb, ab, conv_w, a_log, dt_bias, o_norm_g, B, S):
    ts = min(S, DELTA_TILE)
    nch = ts // CHUNK
    seq = lambda width: pl.BlockSpec((1, ts, width), lambda b, t, *_: (b, t, 0))
    fix = lambda shape: pl.BlockSpec(shape, lambda b, t, *_: (0, 0))
    per_chunk = lambda rows, width: pltpu.VMEM((B_HEADS, nch, rows, width), F32)
    grid_spec = pltpu.PrefetchScalarGridSpec(
        num_scalar_prefetch=2,
        grid=(B, S // ts),
        in_specs=[seq(CONV_DIM), seq(B_WIDTH), seq(LANES), fix((CONV_WIDTH, CONV_DIM)), fix((1, LANES))],
        out_specs=[seq(B_WIDTH),
                   pl.BlockSpec((1, B_HEADS, B_HEAD_DIM, B_HEAD_DIM), lambda b, t, *_: (b, 0, 0, 0))],
        scratch_shapes=[pltpu.VMEM((ts + 8, CONV_DIM), F32), per_chunk(CHUNK, LANES), per_chunk(2 * CHUNK, LANES),
                        per_chunk(CHUNK, LANES), per_chunk(CHUNK, CHUNK), per_chunk(8, LANES)],
    )
    return pl.pallas_call(
        _delta_kernel,
        grid_spec=grid_spec,
        out_shape=[jax.ShapeDtypeStruct((B, S, B_WIDTH), F32),
                   jax.ShapeDtypeStruct((B, B_HEADS, B_HEAD_DIM, B_HEAD_DIM), F32)],
        compiler_params=_params(("parallel", "arbitrary")),
        name="delta",
    )(a_log, dt_bias, ub.reshape(B, S, CONV_DIM), zb.reshape(B, S, B_WIDTH), ab.reshape(B, S, LANES),
      conv_w, o_norm_g.reshape(1, LANES))


def _delta_dec_kernel(alog_ref, dtb_ref, ub_ref, cs_ref, zb_ref, ab_ref, w_ref, og_ref, st_ref,
                      o_ref, so_ref, *, nb):
    i = pl.program_id(0)
    N = ub_ref.shape[0]
    acc = ub_ref[...] * w_ref[CONV_WIDTH - 1:CONV_WIDTH, :]
    for t in range(CONV_WIDTH - 1):
        acc = acc + cs_ref[t] * w_ref[t:t + 1, :]
    c = _silu(acc)
    ab = ab_ref[...]
    lane = lax.broadcasted_iota(jnp.int32, (N, LANES), 1)
    samp = lax.broadcasted_iota(jnp.int32, (B_HEAD_DIM, N), 1)
    row_id = lax.broadcasted_iota(jnp.int32, (N, LANES), 0)
    out_row = lax.broadcasted_iota(jnp.int32, (nb, LANES), 0)
    heads = []
    for h in range(B_HEADS):
        cq = c[:, h * LANES:(h + 1) * LANES]
        ck = c[:, B_WIDTH + h * LANES:B_WIDTH + (h + 1) * LANES]
        v = c[:, 2 * B_WIDTH + h * LANES:2 * B_WIDTH + (h + 1) * LANES]
        q = cq * lax.rsqrt(jnp.sum(cq * cq, axis=1, keepdims=True) + 1e-6) * (B_HEAD_DIM ** -0.5)
        k = ck * lax.rsqrt(jnp.sum(ck * ck, axis=1, keepdims=True) + 1e-6)
        a_raw = jnp.sum(jnp.where(lane == h, ab, 0.0), axis=1, keepdims=True)
        b_raw = jnp.sum(jnp.where(lane == h + B_HEADS, ab, 0.0), axis=1, keepdims=True)
        neg_a = -jnp.exp(jnp.full((1, 1), alog_ref[h], F32))
        dec = jnp.exp(neg_a * _softplus(a_raw + dtb_ref[h]))
        beta = _sigmoid(b_raw)
        heads.append((q.T, k.T, v, jnp.broadcast_to(dec, (N, LANES)), jnp.broadcast_to(beta, (N, LANES)),
                      zb_ref[:, h * LANES:(h + 1) * LANES]))

    hs = range(B_HEADS)
    q_t, k_t, v, dec, beta, z = zip(*heads)

    def sample(j, o_acc):
        n = i * nb + j
        pick = samp == n
        k_col = [jnp.sum(jnp.where(pick, k_t[h], 0.0), axis=1, keepdims=True) for h in hs]
        q_col = [jnp.sum(jnp.where(pick, q_t[h], 0.0), axis=1, keepdims=True) for h in hs]
        pick_r = row_id == n
        row = lambda t: jnp.sum(jnp.where(pick_r, t, 0.0), axis=0, keepdims=True)
        st = [st_ref[j, h] * row(dec[h]) for h in hs]
        mem = [jnp.sum(k_col[h] * st[h], axis=0, keepdims=True) for h in hs]
        st = [st[h] + k_col[h] * ((row(v[h]) - mem[h]) * row(beta[h])) for h in hs]
        o = [jnp.sum(q_col[h] * st[h], axis=0, keepdims=True) for h in hs]
        o = [o[h] * lax.rsqrt(jnp.mean(o[h] * o[h], axis=1, keepdims=True) + RMS_EPS) * og_ref[...] for h in hs]
        for h in hs:
            so_ref[j, h] = st[h]
        return tuple(jnp.where(out_row == j, o[h] * _silu(row(z[h])), o_acc[h]) for h in hs)

    outs = lax.fori_loop(0, nb, sample, tuple(jnp.zeros((nb, LANES), F32) for _ in hs), unroll=2)
    for h in hs:
        o_ref[:, h * LANES:(h + 1) * LANES] = outs[h]


def _delta_dec(ub_s, conv_state, zb_s, ab_s, conv_w, a_log, dt_bias, o_norm_g, state, nb):
    N = ub_s.shape[0]
    full2 = lambda shape: pl.BlockSpec(shape, lambda i, *_: (0, 0))
    grid_spec = pltpu.PrefetchScalarGridSpec(
        num_scalar_prefetch=2,
        grid=(N // nb,),
        in_specs=[full2((N, CONV_DIM)),
                  pl.BlockSpec((CONV_WIDTH - 1, N, CONV_DIM), lambda i, *_: (0, 0, 0)),
                  full2((N, B_WIDTH)), full2((N, LANES)), full2((CONV_WIDTH, CONV_DIM)), full2((1, LANES)),
                  pl.BlockSpec((nb, B_HEADS, B_HEAD_DIM, B_HEAD_DIM), lambda i, *_: (i, 0, 0, 0))],
        out_specs=[pl.BlockSpec((nb, B_WIDTH), lambda i, *_: (i, 0)),
                   pl.BlockSpec((nb, B_HEADS, B_HEAD_DIM, B_HEAD_DIM), lambda i, *_: (i, 0, 0, 0))],
    )
    return pl.pallas_call(
        functools.partial(_delta_dec_kernel, nb=nb),
        grid_spec=grid_spec,
        out_shape=[jax.ShapeDtypeStruct((N, B_WIDTH), F32), jax.ShapeDtypeStruct(state.shape, F32)],
        compiler_params=_params(("parallel",)),
        name="delta_dec",
    )(a_log, dt_bias, ub_s, jnp.swapaxes(conv_state, 0, 1), zb_s, ab_s, conv_w, o_norm_g.reshape(1, LANES), state)


def _layer_norm(r, g, b):
    mu = jnp.mean(r, axis=1, keepdims=True)
    d = r - mu
    var = jnp.mean(d * d, axis=1, keepdims=True)
    return d * lax.rsqrt(var + LN_EPS) * g + b


def _mix_ln_kernel(*refs, steps):
    outs = refs[-3:]

    @pl.when(pl.program_id(0) < steps)
    def _():
        _mix_ln_rows(*refs[:8], *outs)

    @pl.when(pl.program_id(0) >= steps)
    def _():
        for o in outs:
            o[...] = jnp.zeros(o.shape, F32)


def _mix_ln_rows(oa_ref, ob_ref, x_ref, wo_ref, g_ref, b_ref, wr_ref, br_ref, h_ref, route_ref, cnt_ref):
    if wo_ref.dtype == BF16:
        dot = lambda a, w: jnp.dot(a.astype(BF16), w, preferred_element_type=F32)
    else:
        dot = _fdot
    y = dot(oa_ref[...], wo_ref[0:A_WIDTH, :]) + dot(ob_ref[...], wo_ref[A_WIDTH:A_WIDTH + B_WIDTH, :])
    hcur = _layer_norm(DN_ALPHA * x_ref[...] + y, g_ref[...], b_ref[...])
    h_ref[...] = hcur
    logits = (_bdot if wo_ref.dtype == BF16 else _dot3)(hcur, wr_ref[...]) + br_ref[...]
    lane = lax.broadcasted_iota(jnp.int32, logits.shape, 1)
    lane_f = lane.astype(F32)
    ninf = -jnp.inf
    big = 1e9
    gl = jnp.where(lane < N_GROUPS, logits, ninf)
    gmax = jnp.max(gl, axis=1, keepdims=True)
    g_idx = jnp.min(jnp.where(gl == gmax, lane_f, big), axis=1, keepdims=True)
    p_group = 1.0 / jnp.sum(jnp.exp(gl - gmax), axis=1, keepdims=True)
    grp_of_lane = ((lane - N_GROUPS) >> 3).astype(F32)
    sel = (lane >= N_GROUPS) & (lane < N_GROUPS + N_EXPERTS) & (grp_of_lane == g_idx)
    el = jnp.where(sel, logits, ninf)
    v1 = jnp.max(el, axis=1, keepdims=True)
    i1 = jnp.min(jnp.where(el == v1, lane_f, big), axis=1, keepdims=True)
    el2 = jnp.where(lane_f == i1, ninf, el)
    v2 = jnp.max(el2, axis=1, keepdims=True)
    i2 = jnp.min(jnp.where(el2 == v2, lane_f, big), axis=1, keepdims=True)
    t = jnp.exp(v2 - v1)
    gate1 = p_group / (1.0 + t)
    gate2 = p_group * t / (1.0 + t)
    e1, e2 = i1 - N_GROUPS, i2 - N_GROUPS
    route_ref[...] = jnp.where(lane == 0, gate1, jnp.where(lane == 1, gate2, jnp.where(
        lane == 2, e1, jnp.where(lane == 3, e2, 0.0))))
    chosen = ((lane_f == e1) | (lane_f == e2)).astype(F32)
    tm = chosen.shape[0]
    cnt_ref[...] = jnp.sum(chosen.reshape(tm // ROUTE_TILE, ROUTE_TILE, LANES), axis=1)[:, None, :]


def _mix_ln(oa, ob, x2d, wo_b, g, b, wr, br, tm, total, row0, prev=()):
    T, D = x2d.shape
    off = row0 // tm
    steps = T // tm
    tail = 1 if row0 + T < total else 0
    assert total - (row0 + T) <= tm
    row = lambda i: (jnp.minimum(i, steps - 1), 0)
    out_row = lambda i: (off + i, 0)
    fix = lambda i: (0, 0)
    sub = tm // ROUTE_TILE
    return pl.pallas_call(
        functools.partial(_mix_ln_kernel, steps=steps),
        grid=(steps + tail,),
        in_specs=[pl.BlockSpec((tm, A_WIDTH), row), pl.BlockSpec((tm, B_WIDTH), row), pl.BlockSpec((tm, D), row),
                  pl.BlockSpec((A_WIDTH + B_WIDTH, D), fix), pl.BlockSpec((1, D), fix), pl.BlockSpec((1, D), fix),
                  pl.BlockSpec((D, LANES), fix), pl.BlockSpec((1, LANES), fix)]
                 + [pl.BlockSpec(memory_space=pl.ANY)] * len(prev),
        out_specs=[pl.BlockSpec((tm, D), out_row), pl.BlockSpec((tm, LANES), out_row),
                   pl.BlockSpec((sub, 1, LANES), lambda i: (off + i, 0, 0))],
        out_shape=[jax.ShapeDtypeStruct((total, D), F32), jax.ShapeDtypeStruct((total, LANES), F32),
                   jax.ShapeDtypeStruct((total // ROUTE_TILE, 1, LANES), F32)],
        input_output_aliases={8 + j: j for j in range(len(prev))},
        compiler_params=_params(("parallel",)),
        name="mix_ln",
    )(oa, ob, x2d, wo_b, g, b, wr, br, *prev)


def _slot_layout(counts):
    tiles = counts.shape[0]
    n_assign = tiles * ROUTE_TILE * TOP_K
    n_blocks = -(-(n_assign + N_EXPERTS * (MOE_BLK - 1)) // MOE_BLK)
    per_tile = counts.reshape(tiles, LANES)
    earlier = (jnp.arange(tiles)[:, None] > jnp.arange(tiles)[None, :]).astype(F32)
    before = jnp.dot(earlier, per_tile, precision=HIGHEST)
    total = jnp.sum(per_tile, axis=0)[:N_EXPERTS]
    padded = jnp.ceil(total / MOE_BLK) * MOE_BLK
    upto = (jnp.arange(N_EXPERTS)[:, None] <= jnp.arange(N_EXPERTS)[None, :]).astype(F32)
    pad_end = jnp.dot(padded, upto, precision=HIGHEST)
    pad_start = pad_end - padded
    base = (before + jnp.pad(pad_start, (0, LANES - N_EXPERTS))[None, :]).reshape(tiles, 1, LANES)
    blk_start = (jnp.arange(n_blocks) * MOE_BLK).astype(F32)
    in_e = ((pad_start[None, :] <= blk_start[:, None]) & (blk_start[:, None] < pad_end[None, :])).astype(F32)
    used = blk_start < pad_end[-1]
    n_used = jnp.sum(used.astype(jnp.int32))
    last_e = jnp.max(jnp.where(padded > 0, jnp.arange(N_EXPERTS), 0)).astype(F32)
    blk_e = jnp.where(used, jnp.dot(in_e, jnp.arange(N_EXPERTS, dtype=F32), precision=HIGHEST), last_e)
    blk_n = jnp.clip(jnp.dot(in_e, pad_start + total, precision=HIGHEST) - blk_start, 0, MOE_BLK)
    blk_x = jnp.minimum(jnp.arange(n_blocks), jnp.maximum(n_used - 1, 0))
    i32 = lambda t: t.astype(jnp.int32)
    return base, i32(blk_e), i32(blk_n), i32(blk_x), i32(pad_end), n_blocks


def _slot_kernel(route_ref, base_ref, dest_ref):
    lane = lax.broadcasted_iota(jnp.int32, (ROUTE_TILE, LANES), 1)
    lane_f = lane.astype(F32)
    ri = lax.broadcasted_iota(jnp.int32, (ROUTE_TILE, ROUTE_TILE), 0)
    ci = lax.broadcasted_iota(jnp.int32, (ROUTE_TILE, ROUTE_TILE), 1)
    before = (ri > ci).astype(BF16)
    for t in range(base_ref.shape[0]):
        rows = slice(t * ROUTE_TILE, (t + 1) * ROUTE_TILE)
        route = route_ref[rows, :]
        e1 = jnp.sum(jnp.where(lane == 2, route, 0.0), axis=1, keepdims=True)
        e2 = jnp.sum(jnp.where(lane == 3, route, 0.0), axis=1, keepdims=True)
        oh1, oh2 = lane_f == e1, lane_f == e2
        earlier = _bdot(before, (oh1 | oh2).astype(F32))
        slot = base_ref[t] + earlier
        d1 = jnp.sum(jnp.where(oh1, slot, 0.0), axis=1, keepdims=True)
        d2 = jnp.sum(jnp.where(oh2, slot, 0.0), axis=1, keepdims=True)
        dest_ref[rows, :] = jnp.where(lane == 0, d1, jnp.where(lane == 1, d2, 0.0)).astype(jnp.int32)


def _slots(route, base):
    tiles = route.shape[0] // ROUTE_TILE
    sub = max(d for d in range(1, 9) if tiles % d == 0)
    return pl.pallas_call(
        _slot_kernel,
        grid=(tiles // sub,),
        in_specs=[pl.BlockSpec((sub * ROUTE_TILE, LANES), lambda i: (i, 0)),
                  pl.BlockSpec((sub, 1, LANES), lambda i: (i, 0, 0))],
        out_specs=pl.BlockSpec((sub * ROUTE_TILE, LANES), lambda i: (i, 0)),
        out_shape=jax.ShapeDtypeStruct(route.shape, jnp.int32),
        compiler_params=_params(("parallel",)),
        name="slots",
    )(route, base)


def _pack_bf16_pairs(x):
    half = x.shape[1] // 2
    bits = pltpu.bitcast(x.astype(BF16).astype(F32), jnp.uint32)
    return (bits[:, 0:half] >> 16) | bits[:, half:2 * half]


def _unpack_bf16_pairs(w):
    lo = pltpu.bitcast(w << 16, F32).astype(BF16)
    hi = pltpu.bitcast(w & jnp.uint32(0xFFFF0000), F32).astype(BF16)
    return jnp.concatenate([lo, hi], axis=1)


def _dispatch_kernel(dest_ref, pad_end_ref, h_ref, xs_hbm, hbuf, zbuf, sem, zsem):
    i = pl.program_id(0)
    slot = i % 2
    a0 = i * (ROUTE_TILE * TOP_K)

    @pl.when(i == 0)
    def _():
        zbuf[...] = jnp.zeros(zbuf.shape, zbuf.dtype)
        fills =[pltpu.make_async_copy(zbuf, xs_hbm.at[pl.ds(pl.multiple_of(pad_end_ref[e] - MOE_BLK, MOE_BLK), MOE_BLK)],
                                       zsem) for e in range(N_EXPERTS)]
        has_rows = [pad_end_ref[e] > (pad_end_ref[e - 1] if e else 0) for e in range(N_EXPERTS)]
        for e in range(N_EXPERTS):
            @pl.when(has_rows[e])
            def _(e=e):
                fills[e].start()
        for e in range(N_EXPERTS):
            @pl.when(has_rows[e])
            def _(e=e):
                fills[e].wait()

        def tail(b):
            return pltpu.make_async_copy(zbuf, xs_hbm.at[pl.ds(pl.multiple_of(b * MOE_BLK, MOE_BLK), MOE_BLK)], zsem)

        def tail_start(b, c):
            tail(b).start()
            return c

        def tail_wait(b, c):
            tail(b).wait()
            return c

        first_unused = pad_end_ref[N_EXPERTS - 1] // MOE_BLK
        lax.fori_loop(first_unused, xs_hbm.shape[0] // MOE_BLK, tail_start, 0)
        lax.fori_loop(first_unused, xs_hbm.shape[0] // MOE_BLK, tail_wait, 0)

    hbuf[slot] = _pack_bf16_pairs(h_ref[...])

    def start(r, c):
        for k in range(TOP_K):
            pltpu.make_async_copy(hbuf.at[slot, pl.ds(r, 1)], xs_hbm.at[pl.ds(dest_ref[a0 + r * TOP_K + k], 1)],
                                  sem.at[slot]).start()
        return c

    lax.fori_loop(0, ROUTE_TILE, start, 0, unroll=8)

    def drain(s):
        for k in range(TOP_K):
            pltpu.make_async_copy(hbuf.at[s], xs_hbm.at[pl.ds(0, ROUTE_TILE)], sem.at[s]).wait()

    @pl.when(i > 0)
    def _():
        drain(1 - slot)

    @pl.when(i == pl.num_programs(0) - 1)
    def _():
        drain(slot)


def _dispatch(h_all, dest, pad_end, slots):
    T, D = h_all.shape
    grid_spec = pltpu.PrefetchScalarGridSpec(
        num_scalar_prefetch=2,
        grid=(T // ROUTE_TILE,),
        in_specs=[pl.BlockSpec((ROUTE_TILE, D), lambda i, *_: (i, 0))],
        out_specs=pl.BlockSpec(memory_space=pl.ANY),
        scratch_shapes=[pltpu.VMEM((2, ROUTE_TILE, D // 2), jnp.uint32), pltpu.VMEM((MOE_BLK, D // 2), jnp.uint32),
                        pltpu.SemaphoreType.DMA((2,)), pltpu.SemaphoreType.DMA],
    )
    return pl.pallas_call(
        _dispatch_kernel,
        grid_spec=grid_spec,
        out_shape=jax.ShapeDtypeStruct((slots, D // 2), jnp.uint32),
        compiler_params=_params(("arbitrary",)),
        name="dispatch",
    )(dest, pad_end, h_all)


def _moe_kernel(blk_e_ref, blk_n_ref, blk_x_ref, x_ref, wg_ref, wu_ref, wd_ref, y_ref, wg_b, wu_b, wd_b):
    del blk_x_ref
    i = pl.program_id(0)
    n_valid = blk_n_ref[i]

    @pl.when((i == 0) | (blk_e_ref[i] != blk_e_ref[jnp.maximum(i - 1, 0)]))
    def _():
        wg_b[...] = wg_ref[0].astype(BF16)
        wu_b[...] = wu_ref[0].astype(BF16)
        wd_b[...] = wd_ref[0].astype(BF16)

    @pl.when(n_valid > 0)
    def _():
        x = _unpack_bf16_pairs(x_ref[...])
        a = jnp.dot(x, wg_b[...], preferred_element_type=F32)
        u = jnp.dot(x, wu_b[...], preferred_element_type=F32)
        y_ref[...] = jnp.dot((_silu(a) * u).astype(BF16), wd_b[...], preferred_element_type=F32)

    @pl.when(n_valid == 0)
    def _():
        y_ref[...] = jnp.zeros(y_ref.shape, F32)


def _moe(xs, blk_e, blk_n, blk_x, w_gate, w_up, w_down):
    slots = xs.shape[0]
    D, De = w_gate.shape[-2:]
    wspec = lambda shape: pl.BlockSpec((1,) + shape, lambda i, be, *_: (be[i], 0, 0))
    grid_spec = pltpu.PrefetchScalarGridSpec(
        num_scalar_prefetch=3,
        grid=(slots // MOE_BLK,),
        in_specs=[pl.BlockSpec((MOE_BLK, xs.shape[1]), lambda i, be, bn, bx: (bx[i], 0)),
                  wspec((D, De)), wspec((D, De)), wspec((De, D))],
        out_specs=pl.BlockSpec((MOE_BLK, D), lambda i, *_: (i, 0)),
        scratch_shapes=[pltpu.VMEM((D, De), BF16), pltpu.VMEM((D, De), BF16), pltpu.VMEM((De, D), BF16)],
    )
    return pl.pallas_call(
        _moe_kernel,
        grid_spec=grid_spec,
        out_shape=jax.ShapeDtypeStruct((slots, D), F32),
        compiler_params=_params(("arbitrary",)),
        name="moe",
    )(blk_e, blk_n, blk_x, xs, w_gate, w_up, w_down)


def _final_ln_kernel(dest_ref, h_ref, route_ref, g_ref, b_ref, ys_hbm, o_ref, ybuf, sem, *, tile0):
    i = pl.program_id(0)
    slot = i % 2

    def gather(tile, s):
        a0 = (tile0 + tile) * (ROUTE_TILE * TOP_K)

        def start(r, c):
            for k in range(TOP_K):
                pltpu.make_async_copy(ys_hbm.at[pl.ds(dest_ref[a0 + r * TOP_K + k], 1)],
                                      ybuf.at[s, k, pl.ds(r, 1)], sem.at[s]).start()
            return c

        lax.fori_loop(0, ROUTE_TILE, start, 0, unroll=8)

    @pl.when(i == 0)
    def _():
        gather(0, 0)

    @pl.when(i + 1 < pl.num_programs(0))
    def _():
        gather(i + 1, 1 - slot)

    for k in range(TOP_K):
        pltpu.make_async_copy(ys_hbm.at[pl.ds(0, ROUTE_TILE)], ybuf.at[slot, k], sem.at[slot]).wait()
    route = route_ref[...]
    lane = lax.broadcasted_iota(jnp.int32, route.shape, 1)
    gate1 = jnp.sum(jnp.where(lane == 0, route, 0.0), axis=1, keepdims=True)
    gate2 = jnp.sum(jnp.where(lane == 1, route, 0.0), axis=1, keepdims=True)
    f = ybuf[slot, 0] * gate1 + ybuf[slot, 1] * gate2
    o_ref[...] = _layer_norm(DN_ALPHA * h_ref[...] + f, g_ref[...], b_ref[...])


def _final_ln(h_all, route, dest, ys, g, b, row0, rows):
    D = h_all.shape[1]
    tile0 = row0 // ROUTE_TILE
    row = lambda i, *_: (tile0 + i, 0)
    fix = lambda i, *_: (0, 0)
    grid_spec = pltpu.PrefetchScalarGridSpec(
        num_scalar_prefetch=1,
        grid=(rows // ROUTE_TILE,),
        in_specs=[pl.BlockSpec((ROUTE_TILE, D), row), pl.BlockSpec((ROUTE_TILE, LANES), row),
                  pl.BlockSpec((1, D), fix), pl.BlockSpec((1, D), fix), pl.BlockSpec(memory_space=pl.ANY)],
        out_specs=pl.BlockSpec((ROUTE_TILE, D), lambda i, *_: (i, 0)),
        scratch_shapes=[pltpu.VMEM((2, TOP_K, ROUTE_TILE, D), F32), pltpu.SemaphoreType.DMA((2,))],
    )
    return pl.pallas_call(
        functools.partial(_final_ln_kernel, tile0=tile0),
        grid_spec=grid_spec,
        out_shape=jax.ShapeDtypeStruct((rows, D), F32),
        compiler_params=_params(("arbitrary",)),
        name="final_ln",
    )(dest, h_all, route, g, b, ys)


def kernel(x_prompt, x_sample, cache_a_k, cache_a_v, state_b_ssm, state_b_conv, w_in, rel_bias, conv_w, a_log, dt_bias, o_norm_g, w_out, ln1_g, ln1_b, w_group, b_group, w_router, b_router, w_gate, w_up, w_down, ln2_g, ln2_b):
    B, S, D = x_prompt.shape
    N, T = x_sample.shape[0], x_sample.shape[1]
    depth = w_in.shape[0]
    assert depth == 1 and T == 1 and S % ATT_TILE == 0 and N % ROUTE_TILE == 0 and cache_a_k.shape[2] % LANES == 0
    l = 0
    win_p = min(BRANCHES[-1][0], S)

    w_pad32 = jnp.pad(w_in[l], ((0, 0), (0, IN_COLS_PAD - IN_COLS)))
    w_pad = w_pad32.astype(BF16)
    wo_b = w_out[l].astype(BF16)
    wr = jnp.pad(jnp.concatenate([w_group[l], w_router[l]], axis=1), ((0, 0), (0, LANES - N_GROUPS - N_EXPERTS)))
    br = jnp.pad(jnp.concatenate([b_group[l], b_router[l].reshape(-1)]), (0, LANES - N_GROUPS - N_EXPERTS))[None, :]
    g1, b1 = ln1_g[l][None, :], ln1_b[l][None, :]
    g2, b2 = ln2_g[l][None, :], ln2_b[l][None, :]

    xp = x_prompt.reshape(B * S, D)
    qkv_p, ub_p, zb_p, ab_p = _proj(xp, w_pad, 512)
    oa_p = _attn(qkv_p, _band_bias(rel_bias), B, S)
    ob_p, st_p = _delta(ub_p, zb_p, ab_p, conv_w[l], a_log[l], dt_bias[l], o_norm_g[l], B, S)
    rows_all = B * S + N
    routed_p = _mix_ln(oa_p.reshape(B * S, A_WIDTH), ob_p.reshape(B * S, B_WIDTH), xp, wo_b, g1, b1, wr, br,
                       512, rows_all, 0)

    xs = x_sample.reshape(N, D)
    qkv_s, ub_s, zb_s, ab_s = _proj(xs, w_pad32, N)
    nb = 2
    qkv_t = jnp.transpose(qkv_s.reshape(N, 3, A_HEADS, A_HEAD_DIM), (1, 2, 3, 0))
    oa_s = _attn_dec(qkv_t, jnp.transpose(cache_a_k[l], (0, 2, 3, 1)), jnp.transpose(cache_a_v[l], (0, 2, 3, 1)),
                     _cache_bias(rel_bias, cache_a_k.shape[2]), nb)
    oa_s = jnp.transpose(oa_s, (0, 3, 1, 2)).reshape(N, A_WIDTH)
    ob_s, st_s = _delta_dec(ub_s, state_b_conv[l], zb_s, ab_s, conv_w[l], a_log[l], dt_bias[l], o_norm_g[l],
                            state_b_ssm[l], 16)
    h_all, route, cnt = _mix_ln(oa_s, ob_s, xs, w_out[l], g1, b1, wr, br, N, rows_all, B * S, prev=routed_p)

    base, blk_e, blk_n, blk_x, pad_end, n_blocks = _slot_layout(cnt)
    dest = _slots(route, base)[:, 0:TOP_K].reshape(-1)
    xs = _dispatch(h_all, dest, pad_end, n_blocks * MOE_BLK)
    ys = _moe(xs, blk_e, blk_n, blk_x, w_gate[l], w_up[l], w_down[l])
    y_p = _final_ln(h_all, route, dest, ys, g2, b2, 0, B * S)
    y_s = _final_ln(h_all, route, dest, ys, g2, b2, B * S, N)

    w_kv_t = jnp.transpose(w_in[l][:, A_WIDTH:3 * A_WIDTH]).astype(BF16)
    k_win, v_win = _kv_win(x_prompt, w_kv_t, win_p, 512)
    to_rows = lambda t: jnp.transpose(t, (0, 3, 1, 2))[None]
    conv_p = ub_p.reshape(B, S, CONV_DIM)[:, S - (CONV_WIDTH - 1):]
    conv_s = jnp.concatenate([state_b_conv[l], ub_s[:, None, :]], axis=1)[:, T:]
    new_kv = lambda t: jnp.transpose(t, (2, 0, 1))[None, :, None]
    return (y_p.reshape(B, S, D), y_s.reshape(N, T, D), to_rows(k_win), to_rows(v_win),
            new_kv(qkv_t[1]), new_kv(qkv_t[2]), st_p[None], st_s[None], conv_p[None], conv_s[None])
```

```python
import functools
import math

import jax
import jax.numpy as jnp
from jax import lax
from jax.experimental import pallas as pl
from jax.experimental.pallas import tpu as pltpu

F32 = jnp.float32
BF16 = jnp.bfloat16
HIGHEST = lax.Precision.HIGHEST

LANES = 128
A_HEADS = 8
A_HEAD_DIM = 64
A_WIDTH = A_HEADS * A_HEAD_DIM
BRANCHES = ((128, 1), (512, 4), (2048, 16))
BAND = 128
ATT_TILE = BAND * 16
REL_BUCKETS = 32
REL_MAX_DIST = 2048
B_HEADS = 4
B_HEAD_DIM = 128
B_WIDTH = B_HEADS * B_HEAD_DIM
CONV_WIDTH = 4
CONV_DIM = 3 * B_WIDTH
CHUNK = 64
COL_UB = 3 * A_WIDTH
COL_ZB = COL_UB + CONV_DIM
COL_AB = COL_ZB + B_WIDTH
IN_COLS = COL_AB + 2 * B_HEADS
IN_COLS_PAD = COL_AB + LANES
N_GROUPS = 4
EXPERTS_PER_GROUP = 8
N_EXPERTS = N_GROUPS * EXPERTS_PER_GROUP
TOP_K = 2
DN_ALPHA = 2.0 ** 0.25
LN_EPS = 1e-5
RMS_EPS = 1e-6
MASKED = -1e30
MOE_BLK = 512
ROUTE_TILE = 128
DELTA_TILE = 1024
ATT_GROUP = 8
VMEM_LIMIT = 56 * 1024 * 1024


def _bdot(a, b):
    return jnp.dot(a.astype(BF16), b.astype(BF16), preferred_element_type=F32)


def _bdot_nt(a, b):
    return lax.dot_general(a.astype(BF16), b.astype(BF16), (((1,), (1,)), ((), ())), preferred_element_type=F32)


def _bdot_tn(a, b):
    return lax.dot_general(a.astype(BF16), b.astype(BF16), (((0,), (0,)), ((), ())), preferred_element_type=F32)


def _fdot(a, b):
    return jnp.dot(a, b, precision=HIGHEST, preferred_element_type=F32)


def _sigmoid(x):
    return 1.0 / (1.0 + jnp.exp(-x))


def _silu(x):
    return x * _sigmoid(x)


def _softplus(x):
    return jnp.maximum(x, 0.0) + jnp.log(1.0 + jnp.exp(-jnp.abs(x)))


def _params(sem):
    return pltpu.CompilerParams(dimension_semantics=sem, vmem_limit_bytes=VMEM_LIMIT)


def _proj_kernel(x_ref, w_ref, qkv_ref, ub_ref, zb_ref, ab_ref):
    if w_ref.dtype == BF16:
        xb = x_ref[...].astype(BF16)
        dot = lambda w: jnp.dot(xb, w, preferred_element_type=F32)
    else:
        dot = lambda w: _fdot(x_ref[...], w)
    qkv_ref[...] = dot(w_ref[:, 0:COL_UB])
    ub_ref[...] = dot(w_ref[:, COL_UB:COL_ZB])
    zb_ref[...] = dot(w_ref[:, COL_ZB:COL_AB])
    ab_ref[...] = dot(w_ref[:, COL_AB:IN_COLS_PAD])


def _proj(x2d, w_pad, tm):
    T, D = x2d.shape
    row = lambda i: (i, 0)
    return pl.pallas_call(
        _proj_kernel,
        grid=(T // tm,),
        in_specs=[pl.BlockSpec((tm, D), row), pl.BlockSpec((D, IN_COLS_PAD), lambda i: (0, 0))],
        out_specs=[pl.BlockSpec((tm, COL_UB), row), pl.BlockSpec((tm, CONV_DIM), row),
                   pl.BlockSpec((tm, B_WIDTH), row), pl.BlockSpec((tm, LANES), row)],
        out_shape=[jax.ShapeDtypeStruct((T, COL_UB), F32), jax.ShapeDtypeStruct((T, CONV_DIM), F32),
                   jax.ShapeDtypeStruct((T, B_WIDTH), F32), jax.ShapeDtypeStruct((T, LANES), F32)],
        compiler_params=_params(("parallel",)),
        name="proj",
    )(x2d, w_pad)


def _kv_win_kernel(x_ref, wt_ref, k_ref, v_ref):
    tm = x_ref.shape[1]
    kv = lax.dot_general(wt_ref[...], x_ref[0].astype(BF16), (((1,), (1,)), ((), ())), preferred_element_type=F32)
    k_ref[0] = kv[0:A_WIDTH].reshape(A_HEADS, A_HEAD_DIM, tm)
    v_ref[0] = kv[A_WIDTH:2 * A_WIDTH].reshape(A_HEADS, A_HEAD_DIM, tm)


def _kv_win(x_prompt, w_kv_t, win, tm):
    B, S, D = x_prompt.shape
    t0 = (S - win) // tm
    out = jax.ShapeDtypeStruct((B, A_HEADS, A_HEAD_DIM, win), F32)
    ospec = pl.BlockSpec((1, A_HEADS, A_HEAD_DIM, tm), lambda b, t: (b, 0, 0, t))
    return pl.pallas_call(
        _kv_win_kernel,
        grid=(B, win // tm),
        in_specs=[pl.BlockSpec((1, tm, D), lambda b, t: (b, t0 + t, 0)),
                  pl.BlockSpec((2 * A_WIDTH, D), lambda b, t: (0, 0))],
        out_specs=[ospec, ospec],
        out_shape=[out, out],
        compiler_params=_params(("parallel", "parallel")),
        name="kv_win",
    )(x_prompt, w_kv_t)


def _rel_bucket(dist):
    max_exact = REL_BUCKETS // 2
    n = jnp.maximum(dist, 0)
    ratio = jnp.maximum(n, 1).astype(F32) / max_exact
    large = max_exact + (jnp.log(ratio) / math.log(REL_MAX_DIST / max_exact)
                         * (REL_BUCKETS - max_exact)).astype(jnp.int32)
    return jnp.where(n < max_exact, n, jnp.minimum(large, REL_BUCKETS - 1))


def _bias_of(rel_bias, dist):
    onehot = (_rel_bucket(dist)[None, :] == jnp.arange(REL_BUCKETS)[:, None]).astype(F32)
    return jnp.dot(rel_bias.astype(F32).T, onehot, precision=HIGHEST)


def _band_bias(rel_bias):
    period = 3 * BAND
    tabs = []
    for _, dil in BRANCHES:
        g = jnp.concatenate([_bias_of(rel_bias, (BAND - jnp.arange(BAND + 1)) * dil),
                             jnp.full((A_HEADS, period - BAND - 1), MASKED, F32)], axis=1)
        skew = jnp.tile(g, (1, BAND))[:, :BAND * (period - 1)].reshape(A_HEADS, BAND, period - 1)
        tabs.append(skew[:, :, :2 * BAND])
    return jnp.stack(tabs)


def _cache_bias(rel_bias, P):
    dist = P - jnp.arange(P + LANES)
    tabs = []
    for window, dil in BRANCHES:
        ok = (dist >= 0) & (dist <= window) & (dist % dil == 0)
        tabs.append(jnp.where(ok[None, :], _bias_of(rel_bias, dist), MASKED)[:, None, :])
    return jnp.stack(tabs)


def _attn_kernel(q_ref, k_ref, v_ref, bias_ref, o_ref, acc_ref, m_ref, l_ref):
    t = pl.program_id(2)
    tile0 = t * ATT_TILE
    lane = lax.broadcasted_iota(jnp.int32, (BAND, LANES), 1)
    head0 = lane < A_HEAD_DIM

    def rows(start, dil):
        return pl.ds(pl.multiple_of(start, BAND), BAND) if dil == 1 else pl.ds(start, BAND, stride=dil)

    for br, (_, dil) in enumerate(BRANCHES):
        span = BAND * dil

        def blocks(it, carry, br=br, dil=dil, span=span):
            gs = range(ATT_GROUP)
            idx = [it * ATT_GROUP + g for g in gs]
            start = [(i % dil) + (i // dil) * span for i in idx]
            cur = [tile0 + s for s in start]
            first = [c < span for c in cur]
            prev = [jnp.where(first[g], cur[g], cur[g] - span) for g in gs]
            q = [q_ref[0, rows(start[g], dil), :] * (A_HEAD_DIM ** -0.5) for g in gs]
            kk = [jnp.concatenate([k_ref[0, rows(prev[g], dil), :], k_ref[0, rows(cur[g], dil), :]],
                                  axis=0).astype(BF16) for g in gs]
            vv = [jnp.concatenate([v_ref[0, rows(prev[g], dil), :], v_ref[0, rows(cur[g], dil), :]],
                                  axis=0).astype(BF16) for g in gs]
            pen = [jnp.where(first[g], MASKED, 0.0) for g in gs]
            gh = [(g, hh) for g in gs for hh in range(2)]
            qh = [jnp.where(head0 if hh == 0 else ~head0, q[g], 0.0).astype(BF16) for g, hh in gh]
            s = [lax.dot_general(qh[j], kk[g], (((1,), (1,)), ((), ())), preferred_element_type=F32)
                 for j, (g, hh) in enumerate(gh)]
            s = [s[j] + bias_ref[br, hh] for j, (g, hh) in enumerate(gh)]
            s = [jnp.concatenate([s[j][:, 0:BAND] + pen[g], s[j][:, BAND:2 * BAND]], axis=1)
                 for j, (g, hh) in enumerate(gh)]
            m = [jnp.max(t, axis=1, keepdims=True) for t in s]
            p = [jnp.exp(s[j] - m[j]) for j in range(len(gh))]
            l = [jnp.sum(t, axis=1, keepdims=True) for t in p]
            pv = [jnp.dot(p[j].astype(BF16), vv[g], preferred_element_type=F32) for j, (g, hh) in enumerate(gh)]
            for g in gs:
                acc_ref[br, rows(start[g], dil), :] = jnp.where(head0, pv[2 * g], pv[2 * g + 1])
                m_ref[br, rows(start[g], dil), :] = jnp.where(head0, m[2 * g], m[2 * g + 1])
                l_ref[br, rows(start[g], dil), :] = jnp.where(head0, l[2 * g], l[2 * g + 1])
            return carry

        lax.fori_loop(0, ATT_TILE // BAND // ATT_GROUP, blocks, 0)

    def merge(c, carry):
        r = pl.ds(pl.multiple_of(c * 256, 256), 256)
        m0, m1, m2 = m_ref[0, r, :], m_ref[1, r, :], m_ref[2, r, :]
        mx = jnp.maximum(jnp.maximum(m0, m1), m2)
        w0, w1, w2 = jnp.exp(m0 - mx), jnp.exp(m1 - mx), jnp.exp(m2 - mx)
        num = w0 * acc_ref[0, r, :] + w1 * acc_ref[1, r, :] + w2 * acc_ref[2, r, :]
        den = w0 * l_ref[0, r, :] + w1 * l_ref[1, r, :] + w2 * l_ref[2, r, :]
        o_ref[0, r, :] = num / den
        return carry

    lax.fori_loop(0, ATT_TILE // 256, merge, 0)


def _attn(qkv, bias, B, S):
    n_pairs = A_HEADS // 2
    qkv3 = qkv.reshape(B, S, 3 * A_WIDTH)
    return pl.pallas_call(
        _attn_kernel,
        grid=(B, n_pairs, S // ATT_TILE),
        in_specs=[pl.BlockSpec((1, ATT_TILE, LANES), lambda b, hp, t: (b, t, hp)),
                  pl.BlockSpec((1, S, LANES), lambda b, hp, t: (b, 0, n_pairs + hp)),
                  pl.BlockSpec((1, S, LANES), lambda b, hp, t: (b, 0, 2 * n_pairs + hp)),
                  pl.BlockSpec((3, 2, BAND, 2 * BAND), lambda b, hp, t: (0, hp, 0, 0))],
        out_specs=pl.BlockSpec((1, ATT_TILE, LANES), lambda b, hp, t: (b, t, hp)),
        out_shape=jax.ShapeDtypeStruct((B, S, A_WIDTH), F32),
        scratch_shapes=[pltpu.VMEM((3, ATT_TILE, LANES), F32)] * 3,
        compiler_params=_params(("parallel", "parallel", "arbitrary")),
        name="attn",
    )(qkv3, qkv3, qkv3, bias)


def _attn_dec_kernel(qkv_ref, kt_ref, vt_ref, bias_ref, o_ref, *, nb):
    i = pl.program_id(0)
    N = qkv_ref.shape[-1]
    P = kt_ref.shape[-1]
    lane_n = lax.broadcasted_iota(jnp.int32, (A_HEAD_DIM, N), 1)
    lane_o = lax.broadcasted_iota(jnp.int32, (A_HEAD_DIM, nb), 1)
    lane_t = lax.broadcasted_iota(jnp.int32, (1, LANES), 1)

    def head(h, carry):
        slab = jnp.zeros((A_HEAD_DIM, nb), F32)
        for j in range(nb):
            pick = lane_n == i * nb + j
            col = lambda t: jnp.sum(jnp.where(pick, t, 0.0), axis=1, keepdims=True)
            q = col(qkv_ref[0, h]) * (A_HEAD_DIM ** -0.5)
            k_new, v_new = col(qkv_ref[1, h]), col(qkv_ref[2, h])
            s_new = jnp.sum(q * k_new, axis=0, keepdims=True)
            s = jnp.concatenate([jnp.sum(kt_ref[j, h] * q, axis=0, keepdims=True),
                                 jnp.where(lane_t == 0, s_new, 0.0)], axis=1)
            ps, ms, ls = [], [], []
            for br in range(3):
                sb = s + bias_ref[br, h]
                m = jnp.max(sb, axis=1, keepdims=True)
                p = jnp.exp(sb - m)
                ps.append(p)
                ms.append(m)
                ls.append(jnp.sum(p, axis=1, keepdims=True))
            mx = jnp.maximum(jnp.maximum(ms[0], ms[1]), ms[2])
            w = jnp.zeros((1, P + LANES), F32)
            den = jnp.zeros((1, 1), F32)
            for p, m, l in zip(ps, ms, ls):
                e = jnp.exp(m - mx)
                w = w + e * p
                den = den + e * l
            o = jnp.sum(vt_ref[j, h] * w[:, 0:P], axis=1, keepdims=True) + v_new * w[:, P:P + 1]
            slab = jnp.where(lane_o == j, o / den, slab)
        o_ref[0, h] = slab
        return carry

    lax.fori_loop(0, A_HEADS, head, 0, unroll=2)


def _attn_dec(qkv_t, cache_kt, cache_vt, bias, nb):
    N, P = cache_kt.shape[0], cache_kt.shape[-1]
    cache_spec = pl.BlockSpec((nb, A_HEADS, A_HEAD_DIM, P), lambda i: (i, 0, 0, 0))
    return pl.pallas_call(
        functools.partial(_attn_dec_kernel, nb=nb),
        grid=(N // nb,),
        in_specs=[pl.BlockSpec((3, A_HEADS, A_HEAD_DIM, N), lambda i: (0, 0, 0, 0)), cache_spec, cache_spec,
                  pl.BlockSpec((3, A_HEADS, 1, P + LANES), lambda i: (0, 0, 0, 0))],
        out_specs=pl.BlockSpec((1, A_HEADS, A_HEAD_DIM, nb), lambda i: (i, 0, 0, 0)),
        out_shape=jax.ShapeDtypeStruct((N // nb, A_HEADS, A_HEAD_DIM, nb), F32),
        compiler_params=_params(("parallel",)),
        name="attn_dec",
    )(qkv_t, cache_kt, cache_vt, bias)


def _split3(x):
    hi = x.astype(BF16)
    r = x - hi.astype(F32)
    mid = r.astype(BF16)
    return hi, mid, (r - mid.astype(F32)).astype(BF16)


def _tril_dot(tril_b, g):
    return sum(jnp.dot(tril_b, piece, preferred_element_type=F32) for piece in _split3(g))


def _dot3(a, b):
    ah, bh = a.astype(BF16), b.astype(BF16)
    al, bl = (a - ah.astype(F32)).astype(BF16), (b - bh.astype(F32)).astype(BF16)
    d = lambda x, y: jnp.dot(x, y, preferred_element_type=F32)
    return d(ah, bh) + (d(ah, bl) + d(al, bh))


def _delta_kernel(alog_ref, dtb_ref, ub_ref, z_ref, ab_ref, cw_ref, og_ref, o_ref, st_ref,
                  pad_ref, u_s, wq_s, kt_s, qk_s, gl_s):
    t = pl.program_id(1)
    TS = ub_ref.shape[1]
    nch = TS // CHUNK
    hdr = 8

    @pl.when(t == 0)
    def _():
        pad_ref[0:hdr, :] = jnp.zeros((hdr, CONV_DIM), F32)
        st_ref[...] = jnp.zeros(st_ref.shape, F32)

    @pl.when(t > 0)
    def _():
        pad_ref[0:hdr, :] = pad_ref[TS:TS + hdr, :]

    pad_ref[hdr:hdr + TS, :] = ub_ref[0]

    ri = lax.broadcasted_iota(jnp.int32, (CHUNK, CHUNK), 0)
    ci = lax.broadcasted_iota(jnp.int32, (CHUNK, CHUNK), 1)
    incl = ri >= ci
    strict = ri > ci
    tril_b = incl.astype(BF16)
    eye = (ri == ci).astype(F32)
    lane = lax.broadcasted_iota(jnp.int32, (CHUNK, LANES), 1)

    def local(c, carry):
        base = c * CHUNK if isinstance(c, int) else pl.multiple_of(c * CHUNK, CHUNK)
        ab = ab_ref[0, pl.ds(base, CHUNK), :]
        heads = []
        for h in range(B_HEADS):
            def conv(col):
                win = pad_ref[pl.ds(base, CHUNK + hdr), col:col + LANES]
                acc = win[hdr - 3:hdr - 3 + CHUNK] * cw_ref[0:1, col:col + LANES]
                for i in range(1, CONV_WIDTH):
                    acc = acc + win[hdr - 3 + i:hdr - 3 + i + CHUNK] * cw_ref[i:i + 1, col:col + LANES]
                return _silu(acc)

            cq, ck, v = conv(h * LANES), conv(B_WIDTH + h * LANES), conv(2 * B_WIDTH + h * LANES)
            q = cq * lax.rsqrt(jnp.sum(cq * cq, axis=1, keepdims=True) + 1e-6) * (B_HEAD_DIM ** -0.5)
            k = ck * lax.rsqrt(jnp.sum(ck * ck, axis=1, keepdims=True) + 1e-6)
            a_raw = jnp.sum(jnp.where(lane == h, ab, 0.0), axis=1, keepdims=True)
            b_raw = jnp.sum(jnp.where(lane == h + B_HEADS, ab, 0.0), axis=1, keepdims=True)
            neg_a = -jnp.exp(jnp.full((1, LANES), alog_ref[h], F32))
            g = neg_a * _softplus(a_raw + dtb_ref[h])
            beta = _sigmoid(b_raw)
            heads.append((q, k, v, g, beta))
        hs = range(B_HEADS)
        q, k, v, g, beta = zip(*heads)
        gc = [_tril_dot(tril_b, g[h]) for h in hs]
        dmat = [_tril_dot(tril_b, jnp.where(strict, g[h][:, 0:CHUNK], 0.0)) for h in hs]
        kq = [_bdot_nt(jnp.concatenate([k[h], q[h]], axis=0), k[h]) for h in hs]
        decay = [jnp.where(incl, jnp.exp(dmat[h]), 0.0) for h in hs]
        a = [jnp.where(strict, beta[h] * kq[h][0:CHUNK] * decay[h], 0.0) for h in hs]
        x = [eye - a[h] for h in hs]
        p = [_bdot(a[h], a[h]) for h in hs]
        for _ in range(int(math.log2(CHUNK)) - 2):
            r = [_bdot(jnp.concatenate([x[h], p[h]], axis=0), p[h]) for h in hs]
            x = [x[h] + r[h][0:CHUNK] for h in hs]
            p = [r[h][CHUNK:2 * CHUNK] for h in hs]
        x = [x[h] + _bdot(x[h], p[h]) for h in hs]
        e_gc = [jnp.exp(gc[h]) for h in hs]
        sol = [_bdot(x[h], jnp.concatenate([v[h] * beta[h], k[h] * (beta[h] * e_gc[h])], axis=1)) for h in hs]
        for h in hs:
            gc_last = gc[h][CHUNK - 1:CHUNK, :]
            u_s[h, c] = sol[h][:, 0:LANES]
            wq_s[h, c, 0:CHUNK] = sol[h][:, LANES:2 * LANES]
            wq_s[h, c, CHUNK:2 * CHUNK] = q[h] * e_gc[h]
            kt_s[h, c] = k[h] * jnp.exp(gc_last - gc[h])
            qk_s[h, c] = kq[h][CHUNK:2 * CHUNK] * decay[h]
            gl_s[h, c] = jnp.broadcast_to(jnp.exp(gc_last), (8, LANES))
        return carry

    def scan(c, carry):
        base = c * CHUNK if isinstance(c, int) else pl.multiple_of(c * CHUNK, CHUNK)
        hs = range(B_HEADS)
        state = [st_ref[0, h] for h in hs]
        r = [_bdot(wq_s[h, c], state[h]) for h in hs]
        v_new = [u_s[h, c] - r[h][0:CHUNK] for h in hs]
        upd = [_bdot_tn(kt_s[h, c], v_new[h]) for h in hs]
        out = [r[h][CHUNK:2 * CHUNK] + _bdot(qk_s[h, c], v_new[h]) for h in hs]
        for h in hs:
            st_ref[0, h] = state[h] * gl_s[h, c][0:1, :] + upd[h]
            o = out[h] * lax.rsqrt(jnp.mean(out[h] * out[h], axis=1, keepdims=True) + RMS_EPS) * og_ref[...]
            cols = slice(h * LANES, (h + 1) * LANES)
            o_ref[0, pl.ds(base, CHUNK), cols] = o * _silu(z_ref[0, pl.ds(base, CHUNK), cols])
        return carry

    def both(c, carry):
        scan(c - 1, carry)
        return local(c, carry)

    local(0, 0)
    lax.fori_loop(1, nch, both, 0)
    scan(nch - 1, 0)


def _delta(ub, zb, ab, conv_w, a_log, dt_bias, o_norm_g, B, S):
    ts = min(S, DELTA_TILE)
    nch = ts // CHUNK
    seq = lambda width: pl.BlockSpec((1, ts, width), lambda b, t, *_: (b, t, 0))
    fix = lambda shape: pl.BlockSpec(shape, lambda b, t, *_: (0, 0))
    per_chunk = lambda rows, width: pltpu.VMEM((B_HEADS, nch, rows, width), F32)
    grid_spec = pltpu.PrefetchScalarGridSpec(
        num_scalar_prefetch=2,
        grid=(B, S // ts),
        in_specs=[seq(CONV_DIM), seq(B_WIDTH), seq(LANES), fix((CONV_WIDTH, CONV_DIM)), fix((1, LANES))],
        out_specs=[seq(B_WIDTH),
                   pl.BlockSpec((1, B_HEADS, B_HEAD_DIM, B_HEAD_DIM), lambda b, t, *_: (b, 0, 0, 0))],
        scratch_shapes=[pltpu.VMEM((ts + 8, CONV_DIM), F32), per_chunk(CHUNK, LANES), per_chunk(2 * CHUNK, LANES),
                        per_chunk(CHUNK, LANES), per_chunk(CHUNK, CHUNK), per_chunk(8, LANES)],
    )
    return pl.pallas_call(
        _delta_kernel,
        grid_spec=grid_spec,
        out_shape=[jax.ShapeDtypeStruct((B, S, B_WIDTH), F32),
                   jax.ShapeDtypeStruct((B, B_HEADS, B_HEAD_DIM, B_HEAD_DIM), F32)],
        compiler_params=_params(("parallel", "arbitrary")),
        name="delta",
    )(a_log, dt_bias, ub.reshape(B, S, CONV_DIM), zb.reshape(B, S, B_WIDTH), ab.reshape(B, S, LANES),
      conv_w, o_norm_g.reshape(1, LANES))


def _delta_dec_kernel(alog_ref, dtb_ref, ub_ref, cs_ref, zb_ref, ab_ref, w_ref, og_ref, st_ref,
                      o_ref, so_ref, *, nb):
    i = pl.program_id(0)
    N = ub_ref.shape[0]
    acc = ub_ref[...] * w_ref[CONV_WIDTH - 1:CONV_WIDTH, :]
    for t in range(CONV_WIDTH - 1):
        acc = acc + cs_ref[t] * w_ref[t:t + 1, :]
    c = _silu(acc)
    ab = ab_ref[...]
    lane = lax.broadcasted_iota(jnp.int32, (N, LANES), 1)
    samp = lax.broadcasted_iota(jnp.int32, (B_HEAD_DIM, N), 1)
    row_id = lax.broadcasted_iota(jnp.int32, (N, LANES), 0)
    out_row = lax.broadcasted_iota(jnp.int32, (nb, LANES), 0)
    heads = []
    for h in range(B_HEADS):
        cq = c[:, h * LANES:(h + 1) * LANES]
        ck = c[:, B_WIDTH + h * LANES:B_WIDTH + (h + 1) * LANES]
        v = c[:, 2 * B_WIDTH + h * LANES:2 * B_WIDTH + (h + 1) * LANES]
        q = cq * lax.rsqrt(jnp.sum(cq * cq, axis=1, keepdims=True) + 1e-6) * (B_HEAD_DIM ** -0.5)
        k = ck * lax.rsqrt(jnp.sum(ck * ck, axis=1, keepdims=True) + 1e-6)
        a_raw = jnp.sum(jnp.where(lane == h, ab, 0.0), axis=1, keepdims=True)
        b_raw = jnp.sum(jnp.where(lane == h + B_HEADS, ab, 0.0), axis=1, keepdims=True)
        neg_a = -jnp.exp(jnp.full((1, 1), alog_ref[h], F32))
        dec = jnp.exp(neg_a * _softplus(a_raw + dtb_ref[h]))
        beta = _sigmoid(b_raw)
        heads.append((q.T, k.T, v, jnp.broadcast_to(dec, (N, LANES)), jnp.broadcast_to(beta, (N, LANES)),
                      zb_ref[:, h * LANES:(h + 1) * LANES]))

    hs = range(B_HEADS)
    q_t, k_t, v, dec, beta, z = zip(*heads)

    def sample(j, o_acc):
        n = i * nb + j
        pick = samp == n
        k_col = [jnp.sum(jnp.where(pick, k_t[h], 0.0), axis=1, keepdims=True) for h in hs]
        q_col = [jnp.sum(jnp.where(pick, q_t[h], 0.0), axis=1, keepdims=True) for h in hs]
        pick_r = row_id == n
        row = lambda t: jnp.sum(jnp.where(pick_r, t, 0.0), axis=0, keepdims=True)
        st = [st_ref[j, h] * row(dec[h]) for h in hs]
        mem = [jnp.sum(k_col[h] * st[h], axis=0, keepdims=True) for h in hs]
        st = [st[h] + k_col[h] * ((row(v[h]) - mem[h]) * row(beta[h])) for h in hs]
        o = [jnp.sum(q_col[h] * st[h], axis=0, keepdims=True) for h in hs]
        o = [o[h] * lax.rsqrt(jnp.mean(o[h] * o[h], axis=1, keepdims=True) + RMS_EPS) * og_ref[...] for h in hs]
        for h in hs:
            so_ref[j, h] = st[h]
        return tuple(jnp.where(out_row == j, o[h] * _silu(row(z[h])), o_acc[h]) for h in hs)

    outs = lax.fori_loop(0, nb, sample, tuple(jnp.zeros((nb, LANES), F32) for _ in hs), unroll=2)
    for h in hs:
        o_ref[:, h * LANES:(h + 1) * LANES] = outs[h]


def _delta_dec(ub_s, conv_state, zb_s, ab_s, conv_w, a_log, dt_bias, o_norm_g, state, nb):
    N = ub_s.shape[0]
    full2 = lambda shape: pl.BlockSpec(shape, lambda i, *_: (0, 0))
    grid_spec = pltpu.PrefetchScalarGridSpec(
        num_scalar_prefetch=2,
        grid=(N // nb,),
        in_specs=[full2((N, CONV_DIM)),
                  pl.BlockSpec((CONV_WIDTH - 1, N, CONV_DIM), lambda i, *_: (0, 0, 0)),
                  full2((N, B_WIDTH)), full2((N, LANES)), full2((CONV_WIDTH, CONV_DIM)), full2((1, LANES)),
                  pl.BlockSpec((nb, B_HEADS, B_HEAD_DIM, B_HEAD_DIM), lambda i, *_: (i, 0, 0, 0))],
        out_specs=[pl.BlockSpec((nb, B_WIDTH), lambda i, *_: (i, 0)),
                   pl.BlockSpec((nb, B_HEADS, B_HEAD_DIM, B_HEAD_DIM), lambda i, *_: (i, 0, 0, 0))],
    )
    return pl.pallas_call(
        functools.partial(_delta_dec_kernel, nb=nb),
        grid_spec=grid_spec,
        out_shape=[jax.ShapeDtypeStruct((N, B_WIDTH), F32), jax.ShapeDtypeStruct(state.shape, F32)],
        compiler_params=_params(("parallel",)),
        name="delta_dec",
    )(a_log, dt_bias, ub_s, jnp.swapaxes(conv_state, 0, 1), zb_s, ab_s, conv_w, o_norm_g.reshape(1, LANES), state)


def _layer_norm(r, g, b):
    mu = jnp.mean(r, axis=1, keepdims=True)
    d = r - mu
    var = jnp.mean(d * d, axis=1, keepdims=True)
    return d * lax.rsqrt(var + LN_EPS) * g + b


def _mix_ln_kernel(*refs, steps):
    outs = refs[-3:]

    @pl.when(pl.program_id(0) < steps)
    def _():
        _mix_ln_rows(*refs[:8], *outs)

    @pl.when(pl.program_id(0) >= steps)
    def _():
        for o in outs:
            o[...] = jnp.zeros(o.shape, F32)


def _mix_ln_rows(oa_ref, ob_ref, x_ref, wo_ref, g_ref, b_ref, wr_ref, br_ref, h_ref, route_ref, cnt_ref):
    if wo_ref.dtype == BF16:
        dot = lambda a, w: jnp.dot(a.astype(BF16), w, preferred_element_type=F32)
    else:
        dot = _fdot
    y = dot(oa_ref[...], wo_ref[0:A_WIDTH, :]) + dot(ob_ref[...], wo_ref[A_WIDTH:A_WIDTH + B_WIDTH, :])
    hcur = _layer_norm(DN_ALPHA * x_ref[...] + y, g_ref[...], b_ref[...])
    h_ref[...] = hcur
    logits = (_bdot if wo_ref.dtype == BF16 else _dot3)(hcur, wr_ref[...]) + br_ref[...]
    lane = lax.broadcasted_iota(jnp.int32, logits.shape, 1)
    lane_f = lane.astype(F32)
    ninf = -jnp.inf
    big = 1e9
    gl = jnp.where(lane < N_GROUPS, logits, ninf)
    gmax = jnp.max(gl, axis=1, keepdims=True)
    g_idx = jnp.min(jnp.where(gl == gmax, lane_f, big), axis=1, keepdims=True)
    p_group = 1.0 / jnp.sum(jnp.exp(gl - gmax), axis=1, keepdims=True)
    grp_of_lane = ((lane - N_GROUPS) >> 3).astype(F32)
    sel = (lane >= N_GROUPS) & (lane < N_GROUPS + N_EXPERTS) & (grp_of_lane == g_idx)
    el = jnp.where(sel, logits, ninf)
    v1 = jnp.max(el, axis=1, keepdims=True)
    i1 = jnp.min(jnp.where(el == v1, lane_f, big), axis=1, keepdims=True)
    el2 = jnp.where(lane_f == i1, ninf, el)
    v2 = jnp.max(el2, axis=1, keepdims=True)
    i2 = jnp.min(jnp.where(el2 == v2, lane_f, big), axis=1, keepdims=True)
    t = jnp.exp(v2 - v1)
    gate1 = p_group / (1.0 + t)
    gate2 = p_group * t / (1.0 + t)
    e1, e2 = i1 - N_GROUPS, i2 - N_GROUPS
    route_ref[...] = jnp.where(lane == 0, gate1, jnp.where(lane == 1, gate2, jnp.where(
        lane == 2, e1, jnp.where(lane == 3, e2, 0.0))))
    chosen = ((lane_f == e1) | (lane_f == e2)).astype(F32)
    tm = chosen.shape[0]
    cnt_ref[...] = jnp.sum(chosen.reshape(tm // ROUTE_TILE, ROUTE_TILE, LANES), axis=1)[:, None, :]


def _mix_ln(oa, ob, x2d, wo_b, g, b, wr, br, tm, total, row0, prev=()):
    T, D = x2d.shape
    off = row0 // tm
    steps = T // tm
    tail = 1 if row0 + T < total else 0
    assert total - (row0 + T) <= tm
    row = lambda i: (jnp.minimum(i, steps - 1), 0)
    out_row = lambda i: (off + i, 0)
    fix = lambda i: (0, 0)
    sub = tm // ROUTE_TILE
    return pl.pallas_call(
        functools.partial(_mix_ln_kernel, steps=steps),
        grid=(steps + tail,),
        in_specs=[pl.BlockSpec((tm, A_WIDTH), row), pl.BlockSpec((tm, B_WIDTH), row), pl.BlockSpec((tm, D), row),
                  pl.BlockSpec((A_WIDTH + B_WIDTH, D), fix), pl.BlockSpec((1, D), fix), pl.BlockSpec((1, D), fix),
                  pl.BlockSpec((D, LANES), fix), pl.BlockSpec((1, LANES), fix)]
                 + [pl.BlockSpec(memory_space=pl.ANY)] * len(prev),
        out_specs=[pl.BlockSpec((tm, D), out_row), pl.BlockSpec((tm, LANES), out_row),
                   pl.BlockSpec((sub, 1, LANES), lambda i: (off + i, 0, 0))],
        out_shape=[jax.ShapeDtypeStruct((total, D), F32), jax.ShapeDtypeStruct((total, LANES), F32),
                   jax.ShapeDtypeStruct((total // ROUTE_TILE, 1, LANES), F32)],
        input_output_aliases={8 + j: j for j in range(len(prev))},
        compiler_params=_params(("parallel",)),
        name="mix_ln",
    )(oa, ob, x2d, wo_b, g, b, wr, br, *prev)


def _slot_layout(counts):
    tiles = counts.shape[0]
    n_assign = tiles * ROUTE_TILE * TOP_K
    n_blocks = -(-(n_assign + N_EXPERTS * (MOE_BLK - 1)) // MOE_BLK)
    per_tile = counts.reshape(tiles, LANES)
    earlier = (jnp.arange(tiles)[:, None] > jnp.arange(tiles)[None, :]).astype(F32)
    before = jnp.dot(earlier, per_tile, precision=HIGHEST)
    total = jnp.sum(per_tile, axis=0)[:N_EXPERTS]
    padded = jnp.ceil(total / MOE_BLK) * MOE_BLK
    upto = (jnp.arange(N_EXPERTS)[:, None] <= jnp.arange(N_EXPERTS)[None, :]).astype(F32)
    pad_end = jnp.dot(padded, upto, precision=HIGHEST)
    pad_start = pad_end - padded
    base = (before + jnp.pad(pad_start, (0, LANES - N_EXPERTS))[None, :]).reshape(tiles, 1, LANES)
    blk_start = (jnp.arange(n_blocks) * MOE_BLK).astype(F32)
    in_e = ((pad_start[None, :] <= blk_start[:, None]) & (blk_start[:, None] < pad_end[None, :])).astype(F32)
    used = blk_start < pad_end[-1]
    n_used = jnp.sum(used.astype(jnp.int32))
    last_e = jnp.max(jnp.where(padded > 0, jnp.arange(N_EXPERTS), 0)).astype(F32)
    blk_e = jnp.where(used, jnp.dot(in_e, jnp.arange(N_EXPERTS, dtype=F32), precision=HIGHEST), last_e)
    blk_n = jnp.clip(jnp.dot(in_e, pad_start + total, precision=HIGHEST) - blk_start, 0, MOE_BLK)
    blk_x = jnp.minimum(jnp.arange(n_blocks), jnp.maximum(n_used - 1, 0))
    i32 = lambda t: t.astype(jnp.int32)
    return base, i32(blk_e), i32(blk_n), i32(blk_x), i32(pad_end), n_blocks


def _slot_kernel(route_ref, base_ref, dest_ref):
    lane = lax.broadcasted_iota(jnp.int32, (ROUTE_TILE, LANES), 1)
    lane_f = lane.astype(F32)
    ri = lax.broadcasted_iota(jnp.int32, (ROUTE_TILE, ROUTE_TILE), 0)
    ci = lax.broadcasted_iota(jnp.int32, (ROUTE_TILE, ROUTE_TILE), 1)
    before = (ri > ci).astype(BF16)
    for t in range(base_ref.shape[0]):
        rows = slice(t * ROUTE_TILE, (t + 1) * ROUTE_TILE)
        route = route_ref[rows, :]
        e1 = jnp.sum(jnp.where(lane == 2, route, 0.0), axis=1, keepdims=True)
        e2 = jnp.sum(jnp.where(lane == 3, route, 0.0), axis=1, keepdims=True)
        oh1, oh2 = lane_f == e1, lane_f == e2
        earlier = _bdot(before, (oh1 | oh2).astype(F32))
        slot = base_ref[t] + earlier
        d1 = jnp.sum(jnp.where(oh1, slot, 0.0), axis=1, keepdims=True)
        d2 = jnp.sum(jnp.where(oh2, slot, 0.0), axis=1, keepdims=True)
        dest_ref[rows, :] = jnp.where(lane == 0, d1, jnp.where(lane == 1, d2, 0.0)).astype(jnp.int32)


def _slots(route, base):
    tiles = route.shape[0] // ROUTE_TILE
    sub = max(d for d in range(1, 9) if tiles % d == 0)
    return pl.pallas_call(
        _slot_kernel,
        grid=(tiles // sub,),
        in_specs=[pl.BlockSpec((sub * ROUTE_TILE, LANES), lambda i: (i, 0)),
                  pl.BlockSpec((sub, 1, LANES), lambda i: (i, 0, 0))],
        out_specs=pl.BlockSpec((sub * ROUTE_TILE, LANES), lambda i: (i, 0)),
        out_shape=jax.ShapeDtypeStruct(route.shape, jnp.int32),
        compiler_params=_params(("parallel",)),
        name="slots",
    )(route, base)


def _pack_bf16_pairs(x):
    half = x.shape[1] // 2
    bits = pltpu.bitcast(x.astype(BF16).astype(F32), jnp.uint32)
    return (bits[:, 0:half] >> 16) | bits[:, half:2 * half]


def _unpack_bf16_pairs(w):
    lo = pltpu.bitcast(w << 16, F32).astype(BF16)
    hi = pltpu.bitcast(w & jnp.uint32(0xFFFF0000), F32).astype(BF16)
    return jnp.concatenate([lo, hi], axis=1)


def _dispatch_kernel(dest_ref, pad_end_ref, h_ref, xs_hbm, hbuf, zbuf, sem, zsem):
    i = pl.program_id(0)
    slot = i % 2
    a0 = i * (ROUTE_TILE * TOP_K)

    @pl.when(i == 0)
    def _():
        zbuf[...] = jnp.zeros(zbuf.shape, zbuf.dtype)
        fills =[pltpu.make_async_copy(zbuf, xs_hbm.at[pl.ds(pl.multiple_of(pad_end_ref[e] - MOE_BLK, MOE_BLK), MOE_BLK)],
                                       zsem) for e in range(N_EXPERTS)]
        has_rows = [pad_end_ref[e] > (pad_end_ref[e - 1] if e else 0) for e in range(N_EXPERTS)]
        for e in range(N_EXPERTS):
            @pl.when(has_rows[e])
            def _(e=e):
                fills[e].start()
        for e in range(N_EXPERTS):
            @pl.when(has_rows[e])
            def _(e=e):
                fills[e].wait()

        def tail(b):
            return pltpu.make_async_copy(zbuf, xs_hbm.at[pl.ds(pl.multiple_of(b * MOE_BLK, MOE_BLK), MOE_BLK)], zsem)

        def tail_start(b, c):
            tail(b).start()
            return c

        def tail_wait(b, c):
            tail(b).wait()
            return c

        first_unused = pad_end_ref[N_EXPERTS - 1] // MOE_BLK
        lax.fori_loop(first_unused, xs_hbm.shape[0] // MOE_BLK, tail_start, 0)
        lax.fori_loop(first_unused, xs_hbm.shape[0] // MOE_BLK, tail_wait, 0)

    hbuf[slot] = _pack_bf16_pairs(h_ref[...])

    def start(r, c):
        for k in range(TOP_K):
            pltpu.make_async_copy(hbuf.at[slot, pl.ds(r, 1)], xs_hbm.at[pl.ds(dest_ref[a0 + r * TOP_K + k], 1)],
                                  sem.at[slot]).start()
        return c

    lax.fori_loop(0, ROUTE_TILE, start, 0, unroll=8)

    def drain(s):
        for k in range(TOP_K):
            pltpu.make_async_copy(hbuf.at[s], xs_hbm.at[pl.ds(0, ROUTE_TILE)], sem.at[s]).wait()

    @pl.when(i > 0)
    def _():
        drain(1 - slot)

    @pl.when(i == pl.num_programs(0) - 1)
    def _():
        drain(slot)


def _dispatch(h_all, dest, pad_end, slots):
    T, D = h_all.shape
    grid_spec = pltpu.PrefetchScalarGridSpec(
        num_scalar_prefetch=2,
        grid=(T // ROUTE_TILE,),
        in_specs=[pl.BlockSpec((ROUTE_TILE, D), lambda i, *_: (i, 0))],
        out_specs=pl.BlockSpec(memory_space=pl.ANY),
        scratch_shapes=[pltpu.VMEM((2, ROUTE_TILE, D // 2), jnp.uint32), pltpu.VMEM((MOE_BLK, D // 2), jnp.uint32),
                        pltpu.SemaphoreType.DMA((2,)), pltpu.SemaphoreType.DMA],
    )
    return pl.pallas_call(
        _dispatch_kernel,
        grid_spec=grid_spec,
        out_shape=jax.ShapeDtypeStruct((slots, D // 2), jnp.uint32),
        compiler_params=_params(("arbitrary",)),
        name="dispatch",
    )(dest, pad_end, h_all)


def _moe_kernel(blk_e_ref, blk_n_ref, blk_x_ref, x_ref, wg_ref, wu_ref, wd_ref, y_ref, wg_b, wu_b, wd_b):
    del blk_x_ref
    i = pl.program_id(0)
    n_valid = blk_n_ref[i]

    @pl.when((i == 0) | (blk_e_ref[i] != blk_e_ref[jnp.maximum(i - 1, 0)]))
    def _():
        wg_b[...] = wg_ref[0].astype(BF16)
        wu_b[...] = wu_ref[0].astype(BF16)
        wd_b[...] = wd_ref[0].astype(BF16)

    @pl.when(n_valid > 0)
    def _():
        x = _unpack_bf16_pairs(x_ref[...])
        a = jnp.dot(x, wg_b[...], preferred_element_type=F32)
        u = jnp.dot(x, wu_b[...], preferred_element_type=F32)
        y_ref[...] = jnp.dot((_silu(a) * u).astype(BF16), wd_b[...], preferred_element_type=F32)

    @pl.when(n_valid == 0)
    def _():
        y_ref[...] = jnp.zeros(y_ref.shape, F32)


def _moe(xs, blk_e, blk_n, blk_x, w_gate, w_up, w_down):
    slots = xs.shape[0]
    D, De = w_gate.shape[-2:]
    wspec = lambda shape: pl.BlockSpec((1,) + shape, lambda i, be, *_: (be[i], 0, 0))
    grid_spec = pltpu.PrefetchScalarGridSpec(
        num_scalar_prefetch=3,
        grid=(slots // MOE_BLK,),
        in_specs=[pl.BlockSpec((MOE_BLK, xs.shape[1]), lambda i, be, bn, bx: (bx[i], 0)),
                  wspec((D, De)), wspec((D, De)), wspec((De, D))],
        out_specs=pl.BlockSpec((MOE_BLK, D), lambda i, *_: (i, 0)),
        scratch_shapes=[pltpu.VMEM((D, De), BF16), pltpu.VMEM((D, De), BF16), pltpu.VMEM((De, D), BF16)],
    )
    return pl.pallas_call(
        _moe_kernel,
        grid_spec=grid_spec,
        out_shape=jax.ShapeDtypeStruct((slots, D), F32),
        compiler_params=_params(("arbitrary",)),
        name="moe",
    )(blk_e, blk_n, blk_x, xs, w_gate, w_up, w_down)


def _final_ln_kernel(dest_ref, h_ref, route_ref, g_ref, b_ref, ys_hbm, o_ref, ybuf, sem, *, tile0):
    i = pl.program_id(0)
    slot = i % 2

    def gather(tile, s):
        a0 = (tile0 + tile) * (ROUTE_TILE * TOP_K)

        def start(r, c):
            for k in range(TOP_K):
                pltpu.make_async_copy(ys_hbm.at[pl.ds(dest_ref[a0 + r * TOP_K + k], 1)],
                                      ybuf.at[s, k, pl.ds(r, 1)], sem.at[s]).start()
            return c

        lax.fori_loop(0, ROUTE_TILE, start, 0, unroll=8)

    @pl.when(i == 0)
    def _():
        gather(0, 0)

    @pl.when(i + 1 < pl.num_programs(0))
    def _():
        gather(i + 1, 1 - slot)

    for k in range(TOP_K):
        pltpu.make_async_copy(ys_hbm.at[pl.ds(0, ROUTE_TILE)], ybuf.at[slot, k], sem.at[slot]).wait()
    route = route_ref[...]
    lane = lax.broadcasted_iota(jnp.int32, route.shape, 1)
    gate1 = jnp.sum(jnp.where(lane == 0, route, 0.0), axis=1, keepdims=True)
    gate2 = jnp.sum(jnp.where(lane == 1, route, 0.0), axis=1, keepdims=True)
    f = ybuf[slot, 0] * gate1 + ybuf[slot, 1] * gate2
    o_ref[...] = _layer_norm(DN_ALPHA * h_ref[...] + f, g_ref[...], b_ref[...])


def _final_ln(h_all, route, dest, ys, g, b, row0, rows):
    D = h_all.shape[1]
    tile0 = row0 // ROUTE_TILE
    row = lambda i, *_: (tile0 + i, 0)
    fix = lambda i, *_: (0, 0)
    grid_spec = pltpu.PrefetchScalarGridSpec(
        num_scalar_prefetch=1,
        grid=(rows // ROUTE_TILE,),
        in_specs=[pl.BlockSpec((ROUTE_TILE, D), row), pl.BlockSpec((ROUTE_TILE, LANES), row),
                  pl.BlockSpec((1, D), fix), pl.BlockSpec((1, D), fix), pl.BlockSpec(memory_space=pl.ANY)],
        out_specs=pl.BlockSpec((ROUTE_TILE, D), lambda i, *_: (i, 0)),
        scratch_shapes=[pltpu.VMEM((2, TOP_K, ROUTE_TILE, D), F32), pltpu.SemaphoreType.DMA((2,))],
    )
    return pl.pallas_call(
        functools.partial(_final_ln_kernel, tile0=tile0),
        grid_spec=grid_spec,
        out_shape=jax.ShapeDtypeStruct((rows, D), F32),
        compiler_params=_params(("arbitrary",)),
        name="final_ln",
    )(dest, h_all, route, g, b, ys)


def kernel(x_prompt, x_sample, cache_a_k, cache_a_v, state_b_ssm, state_b_conv, w_in, rel_bias, conv_w, a_log, dt_bias, o_norm_g, w_out, ln1_g, ln1_b, w_group, b_group, w_router, b_router, w_gate, w_up, w_down, ln2_g, ln2_b):
    B, S, D = x_prompt.shape
    N, T = x_sample.shape[0], x_sample.shape[1]
    depth = w_in.shape[0]
    assert depth == 1 and T == 1 and S % ATT_TILE == 0 and N % ROUTE_TILE == 0 and cache_a_k.shape[2] % LANES == 0
    l = 0
    win_p = min(BRANCHES[-1][0], S)

    w_pad32 = jnp.pad(w_in[l], ((0, 0), (0, IN_COLS_PAD - IN_COLS)))
    w_pad = w_pad32.astype(BF16)
    wo_b = w_out[l].astype(BF16)
    wr = jnp.pad(jnp.concatenate([w_group[l], w_router[l]], axis=1), ((0, 0), (0, LANES - N_GROUPS - N_EXPERTS)))
    br = jnp.pad(jnp.concatenate([b_group[l], b_router[l].reshape(-1)]), (0, LANES - N_GROUPS - N_EXPERTS))[None, :]
    g1, b1 = ln1_g[l][None, :], ln1_b[l][None, :]
    g2, b2 = ln2_g[l][None, :], ln2_b[l][None, :]

    xp = x_prompt.reshape(B * S, D)
    qkv_p, ub_p, zb_p, ab_p = _proj(xp, w_pad, 512)
    oa_p = _attn(qkv_p, _band_bias(rel_bias), B, S)
    ob_p, st_p = _delta(ub_p, zb_p, ab_p, conv_w[l], a_log[l], dt_bias[l], o_norm_g[l], B, S)
    rows_all = B * S + N
    routed_p = _mix_ln(oa_p.reshape(B * S, A_WIDTH), ob_p.reshape(B * S, B_WIDTH), xp, wo_b, g1, b1, wr, br,
                       512, rows_all, 0)

    xs = x_sample.reshape(N, D)
    qkv_s, ub_s, zb_s, ab_s = _proj(xs, w_pad32, N)
    nb = 2
    qkv_t = jnp.transpose(qkv_s.reshape(N, 3, A_HEADS, A_HEAD_DIM), (1, 2, 3, 0))
    oa_s = _attn_dec(qkv_t, jnp.transpose(cache_a_k[l], (0, 2, 3, 1)), jnp.transpose(cache_a_v[l], (0, 2, 3, 1)),
                     _cache_bias(rel_bias, cache_a_k.shape[2]), nb)
    oa_s = jnp.transpose(oa_s, (0, 3, 1, 2)).reshape(N, A_WIDTH)
    ob_s, st_s = _delta_dec(ub_s, state_b_conv[l], zb_s, ab_s, conv_w[l], a_log[l], dt_bias[l], o_norm_g[l],
                            state_b_ssm[l], 16)
    h_all, route, cnt = _mix_ln(oa_s, ob_s, xs, w_out[l], g1, b1, wr, br, N, rows_all, B * S, prev=routed_p)

    base, blk_e, blk_n, blk_x, pad_end, n_blocks = _slot_layout(cnt)
    dest = _slots(route, base)[:, 0:TOP_K].reshape(-1)
    xs = _dispatch(h_all, dest, pad_end, n_blocks * MOE_BLK)
    ys = _moe(xs, blk_e, blk_n, blk_x, w_gate[l], w_up[l], w_down[l])
    y_p = _final_ln(h_all, route, dest, ys, g2, b2, 0, B * S)
    y_s = _final_ln(h_all, route, dest, ys, g2, b2, B * S, N)

    w_kv_t = jnp.transpose(w_in[l][:, A_WIDTH:3 * A_WIDTH]).astype(BF16)
    k_win, v_win = _kv_win(x_prompt, w_kv_t, win_p, 512)
    to_rows = lambda t: jnp.transpose(t, (0, 3, 1, 2))[None]
    conv_p = ub_p.reshape(B, S, CONV_DIM)[:, S - (CONV_WIDTH - 1):]
    conv_s = jnp.concatenate([state_b_conv[l], ub_s[:, None, :]], axis=1)[:, T:]
    new_kv = lambda t: jnp.transpose(t, (2, 0, 1))[None, :, None]
    return (y_p.reshape(B, S, D), y_s.reshape(N, T, D), to_rows(k_win), to_rows(v_win),
            new_kv(qkv_t[1]), new_kv(qkv_t[2]), st_p[None], st_s[None], conv_p[None], conv_s[None])
```

```python
import functools
import math

import jax
import jax.numpy as jnp
from jax import lax
from jax.experimental import pallas as pl
from jax.experimental.pallas import tpu as pltpu

F32 = jnp.float32
BF16 = jnp.bfloat16
HIGHEST = lax.Precision.HIGHEST

LANES = 128
A_HEADS = 8
A_HEAD_DIM = 64
A_WIDTH = A_HEADS * A_HEAD_DIM
BRANCHES = ((128, 1), (512, 4), (2048, 16))
BAND = 128
ATT_TILE = BAND * 16
REL_BUCKETS = 32
REL_MAX_DIST = 2048
B_HEADS = 4
B_HEAD_DIM = 128
B_WIDTH = B_HEADS * B_HEAD_DIM
CONV_WIDTH = 4
CONV_DIM = 3 * B_WIDTH
CHUNK = 64
COL_UB = 3 * A_WIDTH
COL_ZB = COL_UB + CONV_DIM
COL_AB = COL_ZB + B_WIDTH
IN_COLS = COL_AB + 2 * B_HEADS
IN_COLS_PAD = COL_AB + LANES
N_GROUPS = 4
EXPERTS_PER_GROUP = 8
N_EXPERTS = N_GROUPS * EXPERTS_PER_GROUP
TOP_K = 2
DN_ALPHA = 2.0 ** 0.25
LN_EPS = 1e-5
RMS_EPS = 1e-6
MASKED = -1e30
MOE_BLK = 512
ROUTE_TILE = 128
DELTA_TILE = 1024
ATT_GROUP = 8
VMEM_LIMIT = 56 * 1024 * 1024


def _bdot(a, b):
    return jnp.dot(a.astype(BF16), b.astype(BF16), preferred_element_type=F32)


def _bdot_nt(a, b):
    return lax.dot_general(a.astype(BF16), b.astype(BF16), (((1,), (1,)), ((), ())), preferred_element_type=F32)


def _bdot_tn(a, b):
    return lax.dot_general(a.astype(BF16), b.astype(BF16), (((0,), (0,)), ((), ())), preferred_element_type=F32)


def _fdot(a, b):
    return jnp.dot(a, b, precision=HIGHEST, preferred_element_type=F32)


def _sigmoid(x):
    return 1.0 / (1.0 + jnp.exp(-x))


def _silu(x):
    return x * _sigmoid(x)


def _softplus(x):
    return jnp.maximum(x, 0.0) + jnp.log(1.0 + jnp.exp(-jnp.abs(x)))


def _params(sem):
    return pltpu.CompilerParams(dimension_semantics=sem, vmem_limit_bytes=VMEM_LIMIT)


def _proj_kernel(x_ref, w_ref, qkv_ref, ub_ref, zb_ref, ab_ref):
    if w_ref.dtype == BF16:
        xb = x_ref[...].astype(BF16)
        dot = lambda w: jnp.dot(xb, w, preferred_element_type=F32)
    else:
        dot = lambda w: _fdot(x_ref[...], w)
    qkv_ref[...] = dot(w_ref[:, 0:COL_UB])
    ub_ref[...] = dot(w_ref[:, COL_UB:COL_ZB])
    zb_ref[...] = dot(w_ref[:, COL_ZB:COL_AB])
    ab_ref[...] = dot(w_ref[:, COL_AB:IN_COLS_PAD])


def _proj(x2d, w_pad, tm):
    T, D = x2d.shape
    row = lambda i: (i, 0)
    return pl.pallas_call(
        _proj_kernel,
        grid=(T // tm,),
        in_specs=[pl.BlockSpec((tm, D), row), pl.BlockSpec((D, IN_COLS_PAD), lambda i: (0, 0))],
        out_specs=[pl.BlockSpec((tm, COL_UB), row), pl.BlockSpec((tm, CONV_DIM), row),
                   pl.BlockSpec((tm, B_WIDTH), row), pl.BlockSpec((tm, LANES), row)],
        out_shape=[jax.ShapeDtypeStruct((T, COL_UB), F32), jax.ShapeDtypeStruct((T, CONV_DIM), F32),
                   jax.ShapeDtypeStruct((T, B_WIDTH), F32), jax.ShapeDtypeStruct((T, LANES), F32)],
        compiler_params=_params(("parallel",)),
        name="proj",
    )(x2d, w_pad)


def _kv_win_kernel(x_ref, wt_ref, k_ref, v_ref):
    tm = x_ref.shape[1]
    kv = lax.dot_general(wt_ref[...], x_ref[0].astype(BF16), (((1,), (1,)), ((), ())), preferred_element_type=F32)
    k_ref[0] = kv[0:A_WIDTH].reshape(A_HEADS, A_HEAD_DIM, tm)
    v_ref[0] = kv[A_WIDTH:2 * A_WIDTH].reshape(A_HEADS, A_HEAD_DIM, tm)


def _kv_win(x_prompt, w_kv_t, win, tm):
    B, S, D = x_prompt.shape
    t0 = (S - win) // tm
    out = jax.ShapeDtypeStruct((B, A_HEADS, A_HEAD_DIM, win), F32)
    ospec = pl.BlockSpec((1, A_HEADS, A_HEAD_DIM, tm), lambda b, t: (b, 0, 0, t))
    return pl.pallas_call(
        _kv_win_kernel,
        grid=(B, win // tm),
        in_specs=[pl.BlockSpec((1, tm, D), lambda b, t: (b, t0 + t, 0)),
                  pl.BlockSpec((2 * A_WIDTH, D), lambda b, t: (0, 0))],
        out_specs=[ospec, ospec],
        out_shape=[out, out],
        compiler_params=_params(("parallel", "parallel")),
        name="kv_win",
    )(x_prompt, w_kv_t)


def _rel_bucket(dist):
    max_exact = REL_BUCKETS // 2
    n = jnp.maximum(dist, 0)
    ratio = jnp.maximum(n, 1).astype(F32) / max_exact
    large = max_exact + (jnp.log(ratio) / math.log(REL_MAX_DIST / max_exact)
                         * (REL_BUCKETS - max_exact)).astype(jnp.int32)
    return jnp.where(n < max_exact, n, jnp.minimum(large, REL_BUCKETS - 1))


def _bias_of(rel_bias, dist):
    onehot = (_rel_bucket(dist)[None, :] == jnp.arange(REL_BUCKETS)[:, None]).astype(F32)
    return jnp.dot(rel_bias.astype(F32).T, onehot, precision=HIGHEST)


def _band_bias(rel_bias):
    period = 3 * BAND
    tabs = []
    for _, dil in BRANCHES:
        g = jnp.concatenate([_bias_of(rel_bias, (BAND - jnp.arange(BAND + 1)) * dil),
                             jnp.full((A_HEADS, period - BAND - 1), MASKED, F32)], axis=1)
        skew = jnp.tile(g, (1, BAND))[:, :BAND * (period - 1)].reshape(A_HEADS, BAND, period - 1)
        tabs.append(skew[:, :, :2 * BAND])
    return jnp.stack(tabs)


def _cache_bias(rel_bias, P):
    dist = P - jnp.arange(P + LANES)
    tabs = []
    for window, dil in BRANCHES:
        ok = (dist >= 0) & (dist <= window) & (dist % dil == 0)
        tabs.append(jnp.where(ok[None, :], _bias_of(rel_bias, dist), MASKED)[:, None, :])
    return jnp.stack(tabs)


def _attn_kernel(q_ref, k_ref, v_ref, bias_ref, o_ref, acc_ref, m_ref, l_ref):
    t = pl.program_id(2)
    tile0 = t * ATT_TILE
    lane = lax.broadcasted_iota(jnp.int32, (BAND, LANES), 1)
    head0 = lane < A_HEAD_DIM

    def rows(start, dil):
        return pl.ds(pl.multiple_of(start, BAND), BAND) if dil == 1 else pl.ds(start, BAND, stride=dil)

    for br, (_, dil) in enumerate(BRANCHES):
        span = BAND * dil

        def blocks(it, carry, br=br, dil=dil, span=span):
            gs = range(ATT_GROUP)
            idx = [it * ATT_GROUP + g for g in gs]
            start = [(i % dil) + (i // dil) * span for i in idx]
            cur = [tile0 + s for s in start]
            first = [c < span for c in cur]
            prev = [jnp.where(first[g], cur[g], cur[g] - span) for g in gs]
            q = [q_ref[0, rows(start[g], dil), :] * (A_HEAD_DIM ** -0.5) for g in gs]
            kk = [jnp.concatenate([k_ref[0, rows(prev[g], dil), :], k_ref[0, rows(cur[g], dil), :]],
                                  axis=0).astype(BF16) for g in gs]
            vv = [jnp.concatenate([v_ref[0, rows(prev[g], dil), :], v_ref[0, rows(cur[g], dil), :]],
                                  axis=0).astype(BF16) for g in gs]
            pen = [jnp.where(first[g], MASKED, 0.0) for g in gs]
            gh = [(g, hh) for g in gs for hh in range(2)]
            qh = [jnp.where(head0 if hh == 0 else ~head0, q[g], 0.0).astype(BF16) for g, hh in gh]
            s = [lax.dot_general(qh[j], kk[g], (((1,), (1,)), ((), ())), preferred_element_type=F32)
                 for j, (g, hh) in enumerate(gh)]
            s = [s[j] + bias_ref[br, hh] for j, (g, hh) in enumerate(gh)]
            s = [jnp.concatenate([s[j][:, 0:BAND] + pen[g], s[j][:, BAND:2 * BAND]], axis=1)
                 for j, (g, hh) in enumerate(gh)]
            m = [jnp.max(t, axis=1, keepdims=True) for t in s]
            p = [jnp.exp(s[j] - m[j]) for j in range(len(gh))]
            l = [jnp.sum(t, axis=1, keepdims=True) for t in p]
            pv = [jnp.dot(p[j].astype(BF16), vv[g], preferred_element_type=F32) for j, (g, hh) in enumerate(gh)]
            for g in gs:
                acc_ref[br, rows(start[g], dil), :] = jnp.where(head0, pv[2 * g], pv[2 * g + 1])
                m_ref[br, rows(start[g], dil), :] = jnp.where(head0, m[2 * g], m[2 * g + 1])
                l_ref[br, rows(start[g], dil), :] = jnp.where(head0, l[2 * g], l[2 * g + 1])
            return carry

        lax.fori_loop(0, ATT_TILE // BAND // ATT_GROUP, blocks, 0)

    def merge(c, carry):
        r = pl.ds(pl.multiple_of(c * 256, 256), 256)
        m0, m1, m2 = m_ref[0, r, :], m_ref[1, r, :], m_ref[2, r, :]
        mx = jnp.maximum(jnp.maximum(m0, m1), m2)
        w0, w1, w2 = jnp.exp(m0 - mx), jnp.exp(m1 - mx), jnp.exp(m2 - mx)
        num = w0 * acc_ref[0, r, :] + w1 * acc_ref[1, r, :] + w2 * acc_ref[2, r, :]
        den = w0 * l_ref[0, r, :] + w1 * l_ref[1, r, :] + w2 * l_ref[2, r, :]
        o_ref[0, r, :] = num / den
        return carry

    lax.fori_loop(0, ATT_TILE // 256, merge, 0)


def _attn(qkv, bias, B, S):
    n_pairs = A_HEADS // 2
    qkv3 = qkv.reshape(B, S, 3 * A_WIDTH)
    return pl.pallas_call(
        _attn_kernel,
        grid=(B, n_pairs, S // ATT_TILE),
        in_specs=[pl.BlockSpec((1, ATT_TILE, LANES), lambda b, hp, t: (b, t, hp)),
                  pl.BlockSpec((1, S, LANES), lambda b, hp, t: (b, 0, n_pairs + hp)),
                  pl.BlockSpec((1, S, LANES), lambda b, hp, t: (b, 0, 2 * n_pairs + hp)),
                  pl.BlockSpec((3, 2, BAND, 2 * BAND), lambda b, hp, t: (0, hp, 0, 0))],
        out_specs=pl.BlockSpec((1, ATT_TILE, LANES), lambda b, hp, t: (b, t, hp)),
        out_shape=jax.ShapeDtypeStruct((B, S, A_WIDTH), F32),
        scratch_shapes=[pltpu.VMEM((3, ATT_TILE, LANES), F32)] * 3,
        compiler_params=_params(("parallel", "parallel", "arbitrary")),
        name="attn",
    )(qkv3, qkv3, qkv3, bias)


def _attn_dec_kernel(qkv_ref, kt_ref, vt_ref, bias_ref, o_ref, *, nb):
    i = pl.program_id(0)
    N = qkv_ref.shape[-1]
    P = kt_ref.shape[-1]
    lane_n = lax.broadcasted_iota(jnp.int32, (A_HEAD_DIM, N), 1)
    lane_o = lax.broadcasted_iota(jnp.int32, (A_HEAD_DIM, nb), 1)
    lane_t = lax.broadcasted_iota(jnp.int32, (1, LANES), 1)

    def head(h, carry):
        slab = jnp.zeros((A_HEAD_DIM, nb), F32)
        for j in range(nb):
            pick = lane_n == i * nb + j
            col = lambda t: jnp.sum(jnp.where(pick, t, 0.0), axis=1, keepdims=True)
            q = col(qkv_ref[0, h]) * (A_HEAD_DIM ** -0.5)
            k_new, v_new = col(qkv_ref[1, h]), col(qkv_ref[2, h])
            s_new = jnp.sum(q * k_new, axis=0, keepdims=True)
            s = jnp.concatenate([jnp.sum(kt_ref[j, h] * q, axis=0, keepdims=True),
                                 jnp.where(lane_t == 0, s_new, 0.0)], axis=1)
            ps, ms, ls = [], [], []
            for br in range(3):
                sb = s + bias_ref[br, h]
                m = jnp.max(sb, axis=1, keepdims=True)
                p = jnp.exp(sb - m)
                ps.append(p)
                ms.append(m)
                ls.append(jnp.sum(p, axis=1, keepdims=True))
            mx = jnp.maximum(jnp.maximum(ms[0], ms[1]), ms[2])
            w = jnp.zeros((1, P + LANES), F32)
            den = jnp.zeros((1, 1), F32)
            for p, m, l in zip(ps, ms, ls):
                e = jnp.exp(m - mx)
                w = w + e * p
                den = den + e * l
            o = jnp.sum(vt_ref[j, h] * w[:, 0:P], axis=1, keepdims=True) + v_new * w[:, P:P + 1]
            slab = jnp.where(lane_o == j, o / den, slab)
        o_ref[0, h] = slab
        return carry

    lax.fori_loop(0, A_HEADS, head, 0, unroll=2)


def _attn_dec(qkv_t, cache_kt, cache_vt, bias, nb):
    N, P = cache_kt.shape[0], cache_kt.shape[-1]
    cache_spec = pl.BlockSpec((nb, A_HEADS, A_HEAD_DIM, P), lambda i: (i, 0, 0, 0))
    return pl.pallas_call(
        functools.partial(_attn_dec_kernel, nb=nb),
        grid=(N // nb,),
        in_specs=[pl.BlockSpec((3, A_HEADS, A_HEAD_DIM, N), lambda i: (0, 0, 0, 0)), cache_spec, cache_spec,
                  pl.BlockSpec((3, A_HEADS, 1, P + LANES), lambda i: (0, 0, 0, 0))],
        out_specs=pl.BlockSpec((1, A_HEADS, A_HEAD_DIM, nb), lambda i: (i, 0, 0, 0)),
        out_shape=jax.ShapeDtypeStruct((N // nb, A_HEADS, A_HEAD_DIM, nb), F32),
        compiler_params=_params(("parallel",)),
        name="attn_dec",
    )(qkv_t, cache_kt, cache_vt, bias)


def _split3(x):
    hi = x.astype(BF16)
    r = x - hi.astype(F32)
    mid = r.astype(BF16)
    return hi, mid, (r - mid.astype(F32)).astype(BF16)


def _tril_dot(tril_b, g):
    return sum(jnp.dot(tril_b, piece, preferred_element_type=F32) for piece in _split3(g))


def _dot3(a, b):
    ah, bh = a.astype(BF16), b.astype(BF16)
    al, bl = (a - ah.astype(F32)).astype(BF16), (b - bh.astype(F32)).astype(BF16)
    d = lambda x, y: jnp.dot(x, y, preferred_element_type=F32)
    return d(ah, bh) + (d(ah, bl) + d(al, bh))


def _delta_kernel(alog_ref, dtb_ref, ub_ref, z_ref, ab_ref, cw_ref, og_ref, o_ref, st_ref,
                  pad_ref, u_s, wq_s, kt_s, qk_s, gl_s):
    t = pl.program_id(1)
    TS = ub_ref.shape[1]
    nch = TS // CHUNK
    hdr = 8

    @pl.when(t == 0)
    def _():
        pad_ref[0:hdr, :] = jnp.zeros((hdr, CONV_DIM), F32)
        st_ref[...] = jnp.zeros(st_ref.shape, F32)

    @pl.when(t > 0)
    def _():
        pad_ref[0:hdr, :] = pad_ref[TS:TS + hdr, :]

    pad_ref[hdr:hdr + TS, :] = ub_ref[0]

    ri = lax.broadcasted_iota(jnp.int32, (CHUNK, CHUNK), 0)
    ci = lax.broadcasted_iota(jnp.int32, (CHUNK, CHUNK), 1)
    incl = ri >= ci
    strict = ri > ci
    tril_b = incl.astype(BF16)
    eye = (ri == ci).astype(F32)
    lane = lax.broadcasted_iota(jnp.int32, (CHUNK, LANES), 1)

    def local(c, carry):
        base = c * CHUNK if isinstance(c, int) else pl.multiple_of(c * CHUNK, CHUNK)
        ab = ab_ref[0, pl.ds(base, CHUNK), :]
        heads = []
        for h in range(B_HEADS):
            def conv(col):
                win = pad_ref[pl.ds(base, CHUNK + hdr), col:col + LANES]
                acc = win[hdr - 3:hdr - 3 + CHUNK] * cw_ref[0:1, col:col + LANES]
                for i in range(1, CONV_WIDTH):
                    acc = acc + win[hdr - 3 + i:hdr - 3 + i + CHUNK] * cw_ref[i:i + 1, col:col + LANES]
                return _silu(acc)

            cq, ck, v = conv(h * LANES), conv(B_WIDTH + h * LANES), conv(2 * B_WIDTH + h * LANES)
            q = cq * lax.rsqrt(jnp.sum(cq * cq, axis=1, keepdims=True) + 1e-6) * (B_HEAD_DIM ** -0.5)
            k = ck * lax.rsqrt(jnp.sum(ck * ck, axis=1, keepdims=True) + 1e-6)
            a_raw = jnp.sum(jnp.where(lane == h, ab, 0.0), axis=1, keepdims=True)
            b_raw = jnp.sum(jnp.where(lane == h + B_HEADS, ab, 0.0), axis=1, keepdims=True)
            neg_a = -jnp.exp(jnp.full((1, LANES), alog_ref[h], F32))
            g = neg_a * _softplus(a_raw + dtb_ref[h])
            beta = _sigmoid(b_raw)
            heads.append((q, k, v, g, beta))
        hs = range(B_HEADS)
        q, k, v, g, beta = zip(*heads)
        gc = [_tril_dot(tril_b, g[h]) for h in hs]
        dmat = [_tril_dot(tril_b, jnp.where(strict, g[h][:, 0:CHUNK], 0.0)) for h in hs]
        kq = [_bdot_nt(jnp.concatenate([k[h], q[h]], axis=0), k[h]) for h in hs]
        decay = [jnp.where(incl, jnp.exp(dmat[h]), 0.0) for h in hs]
        a = [jnp.where(strict, beta[h] * kq[h][0:CHUNK] * decay[h], 0.0) for h in hs]
        x = [eye - a[h] for h in hs]
        p = [_bdot(a[h], a[h]) for h in hs]
        for _ in range(int(math.log2(CHUNK)) - 2):
            r = [_bdot(jnp.concatenate([x[h], p[h]], axis=0), p[h]) for h in hs]
            x = [x[h] + r[h][0:CHUNK] for h in hs]
            p = [r[h][CHUNK:2 * CHUNK] for h in hs]
        x = [x[h] + _bdot(x[h], p[h]) for h in hs]
        e_gc = [jnp.exp(gc[h]) for h in hs]
        sol = [_bdot(x[h], jnp.concatenate([v[h] * beta[h], k[h] * (beta[h] * e_gc[h])], axis=1)) for h in hs]
        for h in hs:
            gc_last = gc[h][CHUNK - 1:CHUNK, :]
            u_s[h, c] = sol[h][:, 0:LANES]
            wq_s[h, c, 0:CHUNK] = sol[h][:, LANES:2 * LANES]
            wq_s[h, c, CHUNK:2 * CHUNK] = q[h] * e_gc[h]
            kt_s[h, c] = k[h] * jnp.exp(gc_last - gc[h])
            qk_s[h, c] = kq[h][CHUNK:2 * CHUNK] * decay[h]
            gl_s[h, c] = jnp.broadcast_to(jnp.exp(gc_last), (8, LANES))
        return carry

    def scan(c, carry):
        base = c * CHUNK if isinstance(c, int) else pl.multiple_of(c * CHUNK, CHUNK)
        hs = range(B_HEADS)
        state = [st_ref[0, h] for h in hs]
        r = [_bdot(wq_s[h, c], state[h]) for h in hs]
        v_new = [u_s[h, c] - r[h][0:CHUNK] for h in hs]
        upd = [_bdot_tn(kt_s[h, c], v_new[h]) for h in hs]
        out = [r[h][CHUNK:2 * CHUNK] + _bdot(qk_s[h, c], v_new[h]) for h in hs]
        for h in hs:
            st_ref[0, h] = state[h] * gl_s[h, c][0:1, :] + upd[h]
            o = out[h] * lax.rsqrt(jnp.mean(out[h] * out[h], axis=1, keepdims=True) + RMS_EPS) * og_ref[...]
            cols = slice(h * LANES, (h + 1) * LANES)
            o_ref[0, pl.ds(base, CHUNK), cols] = o * _silu(z_ref[0, pl.ds(base, CHUNK), cols])
        return carry

    def both(c, carry):
        scan(c - 1, carry)
        return local(c, carry)

    local(0, 0)
    lax.fori_loop(1, nch, both, 0)
    scan(nch - 1, 0)


def _delta(ub, zb, ab, conv_w, a_log, dt_bias, o_norm_g, B, S):
    ts = min(S, DELTA_TILE)
    nch = ts // CHUNK
    seq = lambda width: pl.BlockSpec((1, ts, width), lambda b, t, *_: (b, t, 0))
    fix = lambda shape: pl.BlockSpec(shape, lambda b, t, *_: (0, 0))
    per_chunk = lambda rows, width: pltpu.VMEM((B_HEADS, nch, rows, width), F32)
    grid_spec = pltpu.PrefetchScalarGridSpec(
        num_scalar_prefetch=2,
        grid=(B, S // ts),
        in_specs=[seq(CONV_DIM), seq(B_WIDTH), seq(LANES), fix((CONV_WIDTH, CONV_DIM)), fix((1, LANES))],
        out_specs=[seq(B_WIDTH),
                   pl.BlockSpec((1, B_HEADS, B_HEAD_DIM, B_HEAD_DIM), lambda b, t, *_: (b, 0, 0, 0))],
        scratch_shapes=[pltpu.VMEM((ts + 8, CONV_DIM), F32), per_chunk(CHUNK, LANES), per_chunk(2 * CHUNK, LANES),
                        per_chunk(CHUNK, LANES), per_chunk(CHUNK, CHUNK), per_chunk(8, LANES)],
    )
    return pl.pallas_call(
        _delta_kernel,
        grid_spec=grid_spec,
        out_shape=[jax.ShapeDtypeStruct((B, S, B_WIDTH), F32),
                   jax.ShapeDtypeStruct((B, B_HEADS, B_HEAD_DIM, B_HEAD_DIM), F32)],
        compiler_params=_params(("parallel", "arbitrary")),
        name="delta",
    )(a_log, dt_bias, ub.reshape(B, S, CONV_DIM), zb.reshape(B, S, B_WIDTH), ab.reshape(B, S, LANES),
      conv_w, o_norm_g.reshape(1, LANES))


def _delta_dec_kernel(alog_ref, dtb_ref, ub_ref, cs_ref, zb_ref, ab_ref, w_ref, og_ref, st_ref,
                      o_ref, so_ref, *, nb):
    i = pl.program_id(0)
    N = ub_ref.shape[0]
    acc = ub_ref[...] * w_ref[CONV_WIDTH - 1:CONV_WIDTH, :]
    for t in range(CONV_WIDTH - 1):
        acc = acc + cs_ref[t] * w_ref[t:t + 1, :]
    c = _silu(acc)
    ab = ab_ref[...]
    lane = lax.broadcasted_iota(jnp.int32, (N, LANES), 1)
    samp = lax.broadcasted_iota(jnp.int32, (B_HEAD_DIM, N), 1)
    row_id = lax.broadcasted_iota(jnp.int32, (N, LANES), 0)
    out_row = lax.broadcasted_iota(jnp.int32, (nb, LANES), 0)
    heads = []
    for h in range(B_HEADS):
        cq = c[:, h * LANES:(h + 1) * LANES]
        ck = c[:, B_WIDTH + h * LANES:B_WIDTH + (h + 1) * LANES]
        v = c[:, 2 * B_WIDTH + h * LANES:2 * B_WIDTH + (h + 1) * LANES]
        q = cq * lax.rsqrt(jnp.sum(cq * cq, axis=1, keepdims=True) + 1e-6) * (B_HEAD_DIM ** -0.5)
        k = ck * lax.rsqrt(jnp.sum(ck * ck, axis=1, keepdims=True) + 1e-6)
        a_raw = jnp.sum(jnp.where(lane == h, ab, 0.0), axis=1, keepdims=True)
        b_raw = jnp.sum(jnp.where(lane == h + B_HEADS, ab, 0.0), axis=1, keepdims=True)
        neg_a = -jnp.exp(jnp.full((1, 1), alog_ref[h], F32))
        dec = jnp.exp(neg_a * _softplus(a_raw + dtb_ref[h]))
        beta = _sigmoid(b_raw)
        heads.append((q.T, k.T, v, jnp.broadcast_to(dec, (N, LANES)), jnp.broadcast_to(beta, (N, LANES)),
                      zb_ref[:, h * LANES:(h + 1) * LANES]))

    hs = range(B_HEADS)
    q_t, k_t, v, dec, beta, z = zip(*heads)

    def sample(j, o_acc):
        n = i * nb + j
        pick = samp == n
        k_col = [jnp.sum(jnp.where(pick, k_t[h], 0.0), axis=1, keepdims=True) for h in hs]
        q_col = [jnp.sum(jnp.where(pick, q_t[h], 0.0), axis=1, keepdims=True) for h in hs]
        pick_r = row_id == n
        row = lambda t: jnp.sum(jnp.where(pick_r, t, 0.0), axis=0, keepdims=True)
        st = [st_ref[j, h] * row(dec[h]) for h in hs]
        mem = [jnp.sum(k_col[h] * st[h], axis=0, keepdims=True) for h in hs]
        st = [st[h] + k_col[h] * ((row(v[h]) - mem[h]) * row(beta[h])) for h in hs]
        o = [jnp.sum(q_col[h] * st[h], axis=0, keepdims=True) for h in hs]
        o = [o[h] * lax.rsqrt(jnp.mean(o[h] * o[h], axis=1, keepdims=True) + RMS_EPS) * og_ref[...] for h in hs]
        for h in hs:
            so_ref[j, h] = st[h]
        return tuple(jnp.where(out_row == j, o[h] * _silu(row(z[h])), o_acc[h]) for h in hs)

    outs = lax.fori_loop(0, nb, sample, tuple(jnp.zeros((nb, LANES), F32) for _ in hs), unroll=2)
    for h in hs:
        o_ref[:, h * LANES:(h + 1) * LANES] = outs[h]


def _delta_dec(ub_s, conv_state, zb_s, ab_s, conv_w, a_log, dt_bias, o_norm_g, state, nb):
    N = ub_s.shape[0]
    full2 = lambda shape: pl.BlockSpec(shape, lambda i, *_: (0, 0))
    grid_spec = pltpu.PrefetchScalarGridSpec(
        num_scalar_prefetch=2,
        grid=(N // nb,),
        in_specs=[full2((N, CONV_DIM)),
                  pl.BlockSpec((CONV_WIDTH - 1, N, CONV_DIM), lambda i, *_: (0, 0, 0)),
                  full2((N, B_WIDTH)), full2((N, LANES)), full2((CONV_WIDTH, CONV_DIM)), full2((1, LANES)),
                  pl.BlockSpec((nb, B_HEADS, B_HEAD_DIM, B_HEAD_DIM), lambda i, *_: (i, 0, 0, 0))],
        out_specs=[pl.BlockSpec((nb, B_WIDTH), lambda i, *_: (i, 0)),
                   pl.BlockSpec((nb, B_HEADS, B_HEAD_DIM, B_HEAD_DIM), lambda i, *_: (i, 0, 0, 0))],
    )
    return pl.pallas_call(
        functools.partial(_delta_dec_kernel, nb=nb),
        grid_spec=grid_spec,
        out_shape=[jax.ShapeDtypeStruct((N, B_WIDTH), F32), jax.ShapeDtypeStruct(state.shape, F32)],
        compiler_params=_params(("parallel",)),
        name="delta_dec",
    )(a_log, dt_bias, ub_s, jnp.swapaxes(conv_state, 0, 1), zb_s, ab_s, conv_w, o_norm_g.reshape(1, LANES), state)


def _layer_norm(r, g, b):
    mu = jnp.mean(r, axis=1, keepdims=True)
    d = r - mu
    var = jnp.mean(d * d, axis=1, keepdims=True)
    return d * lax.rsqrt(var + LN_EPS) * g + b


def _mix_ln_kernel(*refs, steps):
    outs = refs[-3:]

    @pl.when(pl.program_id(0) < steps)
    def _():
        _mix_ln_rows(*refs[:8], *outs)

    @pl.when(pl.program_id(0) >= steps)
    def _():
        for o in outs:
            o[...] = jnp.zeros(o.shape, F32)


def _mix_ln_rows(oa_ref, ob_ref, x_ref, wo_ref, g_ref, b_ref, wr_ref, br_ref, h_ref, route_ref, cnt_ref):
    if wo_ref.dtype == BF16:
        dot = lambda a, w: jnp.dot(a.astype(BF16), w, preferred_element_type=F32)
    else:
        dot = _fdot
    y = dot(oa_ref[...], wo_ref[0:A_WIDTH, :]) + dot(ob_ref[...], wo_ref[A_WIDTH:A_WIDTH + B_WIDTH, :])
    hcur = _layer_norm(DN_ALPHA * x_ref[...] + y, g_ref[...], b_ref[...])
    h_ref[...] = hcur
    logits = (_bdot if wo_ref.dtype == BF16 else _dot3)(hcur, wr_ref[...]) + br_ref[...]
    lane = lax.broadcasted_iota(jnp.int32, logits.shape, 1)
    lane_f = lane.astype(F32)
    ninf = -jnp.inf
    big = 1e9
    gl = jnp.where(lane < N_GROUPS, logits, ninf)
    gmax = jnp.max(gl, axis=1, keepdims=True)
    g_idx = jnp.min(jnp.where(gl == gmax, lane_f, big), axis=1, keepdims=True)
    p_group = 1.0 / jnp.sum(jnp.exp(gl - gmax), axis=1, keepdims=True)
    grp_of_lane = ((lane - N_GROUPS) >> 3).astype(F32)
    sel = (lane >= N_GROUPS) & (lane < N_GROUPS + N_EXPERTS) & (grp_of_lane == g_idx)
    el = jnp.where(sel, logits, ninf)
    v1 = jnp.max(el, axis=1, keepdims=True)
    i1 = jnp.min(jnp.where(el == v1, lane_f, big), axis=1, keepdims=True)
    el2 = jnp.where(lane_f == i1, ninf, el)
    v2 = jnp.max(el2, axis=1, keepdims=True)
    i2 = jnp.min(jnp.where(el2 == v2, lane_f, big), axis=1, keepdims=True)
    t = jnp.exp(v2 - v1)
    gate1 = p_group / (1.0 + t)
    gate2 = p_group * t / (1.0 + t)
    e1, e2 = i1 - N_GROUPS, i2 - N_GROUPS
    route_ref[...] = jnp.where(lane == 0, gate1, jnp.where(lane == 1, gate2, jnp.where(
        lane == 2, e1, jnp.where(lane == 3, e2, 0.0))))
    chosen = ((lane_f == e1) | (lane_f == e2)).astype(F32)
    tm = chosen.shape[0]
    cnt_ref[...] = jnp.sum(chosen.reshape(tm // ROUTE_TILE, ROUTE_TILE, LANES), axis=1)[:, None, :]


def _mix_ln(oa, ob, x2d, wo_b, g, b, wr, br, tm, total, row0, prev=()):
    T, D = x2d.shape
    off = row0 // tm
    steps = T // tm
    tail = 1 if row0 + T < total else 0
    assert total - (row0 + T) <= tm
    row = lambda i: (jnp.minimum(i, steps - 1), 0)
    out_row = lambda i: (off + i, 0)
    fix = lambda i: (0, 0)
    sub = tm // ROUTE_TILE
    return pl.pallas_call(
        functools.partial(_mix_ln_kernel, steps=steps),
        grid=(steps + tail,),
        in_specs=[pl.BlockSpec((tm, A_WIDTH), row), pl.BlockSpec((tm, B_WIDTH), row), pl.BlockSpec((tm, D), row),
                  pl.BlockSpec((A_WIDTH + B_WIDTH, D), fix), pl.BlockSpec((1, D), fix), pl.BlockSpec((1, D), fix),
                  pl.BlockSpec((D, LANES), fix), pl.BlockSpec((1, LANES), fix)]
                 + [pl.BlockSpec(memory_space=pl.ANY)] * len(prev),
        out_specs=[pl.BlockSpec((tm, D), out_row), pl.BlockSpec((tm, LANES), out_row),
                   pl.BlockSpec((sub, 1, LANES), lambda i: (off + i, 0, 0))],
        out_shape=[jax.ShapeDtypeStruct((total, D), F32), jax.ShapeDtypeStruct((total, LANES), F32),
                   jax.ShapeDtypeStruct((total // ROUTE_TILE, 1, LANES), F32)],
        input_output_aliases={8 + j: j for j in range(len(prev))},
        compiler_params=_params(("parallel",)),
        name="mix_ln",
    )(oa, ob, x2d, wo_b, g, b, wr, br, *prev)


def _slot_layout(counts):
    tiles = counts.shape[0]
    n_assign = tiles * ROUTE_TILE * TOP_K
    n_blocks = -(-(n_assign + N_EXPERTS * (MOE_BLK - 1)) // MOE_BLK)
    per_tile = counts.reshape(tiles, LANES)
    earlier = (jnp.arange(tiles)[:, None] > jnp.arange(tiles)[None, :]).astype(F32)
    before = jnp.dot(earlier, per_tile, precision=HIGHEST)
    total = jnp.sum(per_tile, axis=0)[:N_EXPERTS]
    padded = jnp.ceil(total / MOE_BLK) * MOE_BLK
    upto = (jnp.arange(N_EXPERTS)[:, None] <= jnp.arange(N_EXPERTS)[None, :]).astype(F32)
    pad_end = jnp.dot(padded, upto, precision=HIGHEST)
    pad_start = pad_end - padded
    base = (before + jnp.pad(pad_start, (0, LANES - N_EXPERTS))[None, :]).reshape(tiles, 1, LANES)
    blk_start = (jnp.arange(n_blocks) * MOE_BLK).astype(F32)
    in_e = ((pad_start[None, :] <= blk_start[:, None]) & (blk_start[:, None] < pad_end[None, :])).astype(F32)
    used = blk_start < pad_end[-1]
    n_used = jnp.sum(used.astype(jnp.int32))
    last_e = jnp.max(jnp.where(padded > 0, jnp.arange(N_EXPERTS), 0)).astype(F32)
    blk_e = jnp.where(used, jnp.dot(in_e, jnp.arange(N_EXPERTS, dtype=F32), precision=HIGHEST), last_e)
    blk_n = jnp.clip(jnp.dot(in_e, pad_start + total, precision=HIGHEST) - blk_start, 0, MOE_BLK)
    blk_x = jnp.minimum(jnp.arange(n_blocks), jnp.maximum(n_used - 1, 0))
    i32 = lambda t: t.astype(jnp.int32)
    return base, i32(blk_e), i32(blk_n), i32(blk_x), i32(pad_end), n_blocks


def _slot_kernel(route_ref, base_ref, dest_ref):
    lane = lax.broadcasted_iota(jnp.int32, (ROUTE_TILE, LANES), 1)
    lane_f = lane.astype(F32)
    ri = lax.broadcasted_iota(jnp.int32, (ROUTE_TILE, ROUTE_TILE), 0)
    ci = lax.broadcasted_iota(jnp.int32, (ROUTE_TILE, ROUTE_TILE), 1)
    before = (ri > ci).astype(BF16)
    for t in range(base_ref.shape[0]):
        rows = slice(t * ROUTE_TILE, (t + 1) * ROUTE_TILE)
        route = route_ref[rows, :]
        e1 = jnp.sum(jnp.where(lane == 2, route, 0.0), axis=1, keepdims=True)
        e2 = jnp.sum(jnp.where(lane == 3, route, 0.0), axis=1, keepdims=True)
        oh1, oh2 = lane_f == e1, lane_f == e2
        earlier = _bdot(before, (oh1 | oh2).astype(F32))
        slot = base_ref[t] + earlier
        d1 = jnp.sum(jnp.where(oh1, slot, 0.0), axis=1, keepdims=True)
        d2 = jnp.sum(jnp.where(oh2, slot, 0.0), axis=1, keepdims=True)
        dest_ref[rows, :] = jnp.where(lane == 0, d1, jnp.where(lane == 1, d2, 0.0)).astype(jnp.int32)


def _slots(route, base):
    tiles = route.shape[0] // ROUTE_TILE
    sub = max(d for d in range(1, 9) if tiles % d == 0)
    return pl.pallas_call(
        _slot_kernel,
        grid=(tiles // sub,),
        in_specs=[pl.BlockSpec((sub * ROUTE_TILE, LANES), lambda i: (i, 0)),
                  pl.BlockSpec((sub, 1, LANES), lambda i: (i, 0, 0))],
        out_specs=pl.BlockSpec((sub * ROUTE_TILE, LANES), lambda i: (i, 0)),
        out_shape=jax.ShapeDtypeStruct(route.shape, jnp.int32),
        compiler_params=_params(("parallel",)),
        name="slots",
    )(route, base)


def _pack_bf16_pairs(x):
    half = x.shape[1] // 2
    bits = pltpu.bitcast(x.astype(BF16).astype(F32), jnp.uint32)
    return (bits[:, 0:half] >> 16) | bits[:, half:2 * half]


def _unpack_bf16_pairs(w):
    lo = pltpu.bitcast(w << 16, F32).astype(BF16)
    hi = pltpu.bitcast(w & jnp.uint32(0xFFFF0000), F32).astype(BF16)
    return jnp.concatenate([lo, hi], axis=1)


def _dispatch_kernel(dest_ref, pad_end_ref, h_ref, xs_hbm, hbuf, zbuf, sem, zsem):
    i = pl.program_id(0)
    slot = i % 2
    tile = h_ref.shape[0]
    a0 = i * (tile * TOP_K)

    @pl.when(i == 0)
    def _():
        zbuf[...] = jnp.zeros(zbuf.shape, zbuf.dtype)
        fills =[pltpu.make_async_copy(zbuf, xs_hbm.at[pl.ds(pl.multiple_of(pad_end_ref[e] - MOE_BLK, MOE_BLK), MOE_BLK)],
                                       zsem) for e in range(N_EXPERTS)]
        has_rows = [pad_end_ref[e] > (pad_end_ref[e - 1] if e else 0) for e in range(N_EXPERTS)]
        for e in range(N_EXPERTS):
            @pl.when(has_rows[e])
            def _(e=e):
                fills[e].start()
        for e in range(N_EXPERTS):
            @pl.when(has_rows[e])
            def _(e=e):
                fills[e].wait()

        def tail(b):
            return pltpu.make_async_copy(zbuf, xs_hbm.at[pl.ds(pl.multiple_of(b * MOE_BLK, MOE_BLK), MOE_BLK)], zsem)

        def tail_start(b, c):
            tail(b).start()
            return c

        def tail_wait(b, c):
            tail(b).wait()
            return c

        first_unused = pad_end_ref[N_EXPERTS - 1] // MOE_BLK
        lax.fori_loop(first_unused, xs_hbm.shape[0] // MOE_BLK, tail_start, 0)
        lax.fori_loop(first_unused, xs_hbm.shape[0] // MOE_BLK, tail_wait, 0)

    hbuf[slot] = _pack_bf16_pairs(h_ref[...])

    def start(r, c):
        for k in range(TOP_K):
            pltpu.make_async_copy(hbuf.at[slot, pl.ds(r, 1)], xs_hbm.at[pl.ds(dest_ref[a0 + r * TOP_K + k], 1)],
                                  sem.at[slot]).start()
        return c

    lax.fori_loop(0, tile, start, 0, unroll=8)

    def drain(s):
        for k in range(TOP_K):
            pltpu.make_async_copy(hbuf.at[s], xs_hbm.at[pl.ds(0, tile)], sem.at[s]).wait()

    @pl.when(i > 0)
    def _():
        drain(1 - slot)

    @pl.when(i == pl.num_programs(0) - 1)
    def _():
        drain(slot)


def _dispatch(h_all, dest, pad_end, slots):
    T, D = h_all.shape
    tile = max(t for t in range(ROUTE_TILE, 4 * ROUTE_TILE + 1, 8) if T % t == 0)
    grid_spec = pltpu.PrefetchScalarGridSpec(
        num_scalar_prefetch=2,
        grid=(T // tile,),
        in_specs=[pl.BlockSpec((tile, D), lambda i, *_: (i, 0))],
        out_specs=pl.BlockSpec(memory_space=pl.ANY),
        scratch_shapes=[pltpu.VMEM((2, tile, D // 2), jnp.uint32), pltpu.VMEM((MOE_BLK, D // 2), jnp.uint32),
                        pltpu.SemaphoreType.DMA((2,)), pltpu.SemaphoreType.DMA],
    )
    return pl.pallas_call(
        _dispatch_kernel,
        grid_spec=grid_spec,
        out_shape=jax.ShapeDtypeStruct((slots, D // 2), jnp.uint32),
        compiler_params=_params(("arbitrary",)),
        name="dispatch",
    )(dest, pad_end, h_all)


def _moe_kernel(blk_e_ref, blk_n_ref, blk_x_ref, x_ref, wg_ref, wu_ref, wd_ref, y_ref, wg_b, wu_b, wd_b):
    del blk_x_ref
    i = pl.program_id(0)
    n_valid = blk_n_ref[i]

    @pl.when((i == 0) | (blk_e_ref[i] != blk_e_ref[jnp.maximum(i - 1, 0)]))
    def _():
        wg_b[...] = wg_ref[0].astype(BF16)
        wu_b[...] = wu_ref[0].astype(BF16)
        wd_b[...] = wd_ref[0].astype(BF16)

    @pl.when(n_valid > 0)
    def _():
        x = _unpack_bf16_pairs(x_ref[...])
        a = jnp.dot(x, wg_b[...], preferred_element_type=F32)
        u = jnp.dot(x, wu_b[...], preferred_element_type=F32)
        y_ref[...] = jnp.dot((_silu(a) * u).astype(BF16), wd_b[...], preferred_element_type=F32)

    @pl.when(n_valid == 0)
    def _():
        y_ref[...] = jnp.zeros(y_ref.shape, F32)


def _moe(xs, blk_e, blk_n, blk_x, w_gate, w_up, w_down):
    slots = xs.shape[0]
    D, De = w_gate.shape[-2:]
    wspec = lambda shape: pl.BlockSpec((1,) + shape, lambda i, be, *_: (be[i], 0, 0))
    grid_spec = pltpu.PrefetchScalarGridSpec(
        num_scalar_prefetch=3,
        grid=(slots // MOE_BLK,),
        in_specs=[pl.BlockSpec((MOE_BLK, xs.shape[1]), lambda i, be, bn, bx: (bx[i], 0)),
                  wspec((D, De)), wspec((D, De)), wspec((De, D))],
        out_specs=pl.BlockSpec((MOE_BLK, D), lambda i, *_: (i, 0)),
        scratch_shapes=[pltpu.VMEM((D, De), BF16), pltpu.VMEM((D, De), BF16), pltpu.VMEM((De, D), BF16)],
    )
    return pl.pallas_call(
        _moe_kernel,
        grid_spec=grid_spec,
        out_shape=jax.ShapeDtypeStruct((slots, D), F32),
        compiler_params=_params(("arbitrary",)),
        name="moe",
    )(blk_e, blk_n, blk_x, xs, w_gate, w_up, w_down)


def _final_ln_kernel(dest_ref, h_ref, route_ref, g_ref, b_ref, ys_hbm, o_ref, ybuf, sem, *, tile0):
    i = pl.program_id(0)
    slot = i % 2
    rows = h_ref.shape[0]

    def gather(tile, s):
        a0 = (tile0 + tile) * (rows * TOP_K)

        def start(r, c):
            for k in range(TOP_K):
                pltpu.make_async_copy(ys_hbm.at[pl.ds(dest_ref[a0 + r * TOP_K + k], 1)],
                                      ybuf.at[s, k, pl.ds(r, 1)], sem.at[s]).start()
            return c

        lax.fori_loop(0, rows, start, 0, unroll=8)

    @pl.when(i == 0)
    def _():
        gather(0, 0)

    @pl.when(i + 1 < pl.num_programs(0))
    def _():
        gather(i + 1, 1 - slot)

    for k in range(TOP_K):
        pltpu.make_async_copy(ys_hbm.at[pl.ds(0, rows)], ybuf.at[slot, k], sem.at[slot]).wait()
    route = route_ref[...]
    lane = lax.broadcasted_iota(jnp.int32, route.shape, 1)
    gate1 = jnp.sum(jnp.where(lane == 0, route, 0.0), axis=1, keepdims=True)
    gate2 = jnp.sum(jnp.where(lane == 1, route, 0.0), axis=1, keepdims=True)
    f = ybuf[slot, 0] * gate1 + ybuf[slot, 1] * gate2
    o_ref[...] = _layer_norm(DN_ALPHA * h_ref[...] + f, g_ref[...], b_ref[...])


def _final_ln(h_all, route, dest, ys, g, b, row0, rows):
    D = h_all.shape[1]
    tile = max(t for t in (ROUTE_TILE, 2 * ROUTE_TILE) if rows % t == 0 and row0 % t == 0)
    tile0 = row0 // tile
    row = lambda i, *_: (tile0 + i, 0)
    fix = lambda i, *_: (0, 0)
    grid_spec = pltpu.PrefetchScalarGridSpec(
        num_scalar_prefetch=1,
        grid=(rows // tile,),
        in_specs=[pl.BlockSpec((tile, D), row), pl.BlockSpec((tile, LANES), row),
                  pl.BlockSpec((1, D), fix), pl.BlockSpec((1, D), fix), pl.BlockSpec(memory_space=pl.ANY)],
        out_specs=pl.BlockSpec((tile, D), lambda i, *_: (i, 0)),
        scratch_shapes=[pltpu.VMEM((2, TOP_K, tile, D), F32), pltpu.SemaphoreType.DMA((2,))],
    )
    return pl.pallas_call(
        functools.partial(_final_ln_kernel, tile0=tile0),
        grid_spec=grid_spec,
        out_shape=jax.ShapeDtypeStruct((rows, D), F32),
        compiler_params=_params(("arbitrary",)),
        name="final_ln",
    )(dest, h_all, route, g, b, ys)


def kernel(x_prompt, x_sample, cache_a_k, cache_a_v, state_b_ssm, state_b_conv, w_in, rel_bias, conv_w, a_log, dt_bias, o_norm_g, w_out, ln1_g, ln1_b, w_group, b_group, w_router, b_router, w_gate, w_up, w_down, ln2_g, ln2_b):
    B, S, D = x_prompt.shape
    N, T = x_sample.shape[0], x_sample.shape[1]
    depth = w_in.shape[0]
    assert depth == 1 and T == 1 and S % ATT_TILE == 0 and N % ROUTE_TILE == 0 and cache_a_k.shape[2] % LANES == 0
    l = 0
    win_p = min(BRANCHES[-1][0], S)

    w_pad32 = jnp.pad(w_in[l], ((0, 0), (0, IN_COLS_PAD - IN_COLS)))
    w_pad = w_pad32.astype(BF16)
    wo_b = w_out[l].astype(BF16)
    wr = jnp.pad(jnp.concatenate([w_group[l], w_router[l]], axis=1), ((0, 0), (0, LANES - N_GROUPS - N_EXPERTS)))
    br = jnp.pad(jnp.concatenate([b_group[l], b_router[l].reshape(-1)]), (0, LANES - N_GROUPS - N_EXPERTS))[None, :]
    g1, b1 = ln1_g[l][None, :], ln1_b[l][None, :]
    g2, b2 = ln2_g[l][None, :], ln2_b[l][None, :]

    xp = x_prompt.reshape(B * S, D)
    qkv_p, ub_p, zb_p, ab_p = _proj(xp, w_pad, 512)
    oa_p = _attn(qkv_p, _band_bias(rel_bias), B, S)
    ob_p, st_p = _delta(ub_p, zb_p, ab_p, conv_w[l], a_log[l], dt_bias[l], o_norm_g[l], B, S)
    rows_all = B * S + N
    routed_p = _mix_ln(oa_p.reshape(B * S, A_WIDTH), ob_p.reshape(B * S, B_WIDTH), xp, wo_b, g1, b1, wr, br,
                       512, rows_all, 0)

    xs = x_sample.reshape(N, D)
    qkv_s, ub_s, zb_s, ab_s = _proj(xs, w_pad32, N)
    nb = 2
    qkv_t = jnp.transpose(qkv_s.reshape(N, 3, A_HEADS, A_HEAD_DIM), (1, 2, 3, 0))
    oa_s = _attn_dec(qkv_t, jnp.transpose(cache_a_k[l], (0, 2, 3, 1)), jnp.transpose(cache_a_v[l], (0, 2, 3, 1)),
                     _cache_bias(rel_bias, cache_a_k.shape[2]), nb)
    oa_s = jnp.transpose(oa_s, (0, 3, 1, 2)).reshape(N, A_WIDTH)
    ob_s, st_s = _delta_dec(ub_s, state_b_conv[l], zb_s, ab_s, conv_w[l], a_log[l], dt_bias[l], o_norm_g[l],
                            state_b_ssm[l], 16)
    h_all, route, cnt = _mix_ln(oa_s, ob_s, xs, w_out[l], g1, b1, wr, br, N, rows_all, B * S, prev=routed_p)

    base, blk_e, blk_n, blk_x, pad_end, n_blocks = _slot_layout(cnt)
    dest = _slots(route, base)[:, 0:TOP_K].reshape(-1)
    xs = _dispatch(h_all, dest, pad_end, n_blocks * MOE_BLK)
    ys = _moe(xs, blk_e, blk_n, blk_x, w_gate[l], w_up[l], w_down[l])
    y_p = _final_ln(h_all, route, dest, ys, g2, b2, 0, B * S)
    y_s = _final_ln(h_all, route, dest, ys, g2, b2, B * S, N)

    w_kv_t = jnp.transpose(w_in[l][:, A_WIDTH:3 * A_WIDTH]).astype(BF16)
    k_win, v_win = _kv_win(x_prompt, w_kv_t, win_p, 512)
    to_rows = lambda t: jnp.transpose(t, (0, 3, 1, 2))[None]
    conv_p = ub_p.reshape(B, S, CONV_DIM)[:, S - (CONV_WIDTH - 1):]
    conv_s = jnp.concatenate([state_b_conv[l], ub_s[:, None, :]], axis=1)[:, T:]
    new_kv = lambda t: jnp.transpose(t, (2, 0, 1))[None, :, None]
    return (y_p.reshape(B, S, D), y_s.reshape(N, T, D), to_rows(k_win), to_rows(v_win),
            new_kv(qkv_t[1]), new_kv(qkv_t[2]), st_p[None], st_s[None], conv_p[None], conv_s[None])
```

```python
import functools
import math

import jax
import jax.numpy as jnp
from jax import lax
from jax.experimental import pallas as pl
from jax.experimental.pallas import tpu as pltpu

F32 = jnp.float32
BF16 = jnp.bfloat16
HIGHEST = lax.Precision.HIGHEST

LANES = 128
A_HEADS = 8
A_HEAD_DIM = 64
A_WIDTH = A_HEADS * A_HEAD_DIM
BRANCHES = ((128, 1), (512, 4), (2048, 16))
BAND = 128
ATT_TILE = BAND * 16
REL_BUCKETS = 32
REL_MAX_DIST = 2048
B_HEADS = 4
B_HEAD_DIM = 128
B_WIDTH = B_HEADS * B_HEAD_DIM
CONV_WIDTH = 4
CONV_DIM = 3 * B_WIDTH
CHUNK = 64
COL_UB = 3 * A_WIDTH
COL_ZB = COL_UB + CONV_DIM
COL_AB = COL_ZB + B_WIDTH
IN_COLS = COL_AB + 2 * B_HEADS
IN_COLS_PAD = COL_AB + LANES
N_GROUPS = 4
EXPERTS_PER_GROUP = 8
N_EXPERTS = N_GROUPS * EXPERTS_PER_GROUP
TOP_K = 2
DN_ALPHA = 2.0 ** 0.25
LN_EPS = 1e-5
RMS_EPS = 1e-6
MASKED = -1e30
MOE_BLK = 512
ROUTE_TILE = 128
DELTA_TILE = 1024
ATT_GROUP = 8
VMEM_LIMIT = 56 * 1024 * 1024


def _bdot(a, b):
    return jnp.dot(a.astype(BF16), b.astype(BF16), preferred_element_type=F32)


def _bdot_nt(a, b):
    return lax.dot_general(a.astype(BF16), b.astype(BF16), (((1,), (1,)), ((), ())), preferred_element_type=F32)


def _bdot_tn(a, b):
    return lax.dot_general(a.astype(BF16), b.astype(BF16), (((0,), (0,)), ((), ())), preferred_element_type=F32)


def _fdot(a, b):
    return jnp.dot(a, b, precision=HIGHEST, preferred_element_type=F32)


def _sigmoid(x):
    return 1.0 / (1.0 + jnp.exp(-x))


def _silu(x):
    return x * _sigmoid(x)


def _softplus(x):
    return jnp.maximum(x, 0.0) + jnp.log(1.0 + jnp.exp(-jnp.abs(x)))


def _params(sem):
    return pltpu.CompilerParams(dimension_semantics=sem, vmem_limit_bytes=VMEM_LIMIT)


def _proj_kernel(x_ref, w_ref, qkv_ref, ub_ref, zb_ref, ab_ref):
    if w_ref.dtype == BF16:
        xb = x_ref[...].astype(BF16)
        dot = lambda w: jnp.dot(xb, w, preferred_element_type=F32)
    else:
        dot = lambda w: _fdot(x_ref[...], w)
    qkv_ref[...] = dot(w_ref[:, 0:COL_UB])
    ub_ref[...] = dot(w_ref[:, COL_UB:COL_ZB])
    zb_ref[...] = dot(w_ref[:, COL_ZB:COL_AB])
    ab_ref[...] = dot(w_ref[:, COL_AB:IN_COLS_PAD])


def _proj(x2d, w_pad, tm):
    T, D = x2d.shape
    row = lambda i: (i, 0)
    return pl.pallas_call(
        _proj_kernel,
        grid=(T // tm,),
        in_specs=[pl.BlockSpec((tm, D), row), pl.BlockSpec((D, IN_COLS_PAD), lambda i: (0, 0))],
        out_specs=[pl.BlockSpec((tm, COL_UB), row), pl.BlockSpec((tm, CONV_DIM), row),
                   pl.BlockSpec((tm, B_WIDTH), row), pl.BlockSpec((tm, LANES), row)],
        out_shape=[jax.ShapeDtypeStruct((T, COL_UB), F32), jax.ShapeDtypeStruct((T, CONV_DIM), F32),
                   jax.ShapeDtypeStruct((T, B_WIDTH), F32), jax.ShapeDtypeStruct((T, LANES), F32)],
        compiler_params=_params(("parallel",)),
        name="proj",
    )(x2d, w_pad)


def _kv_win_kernel(x_ref, wt_ref, k_ref, v_ref):
    tm = x_ref.shape[1]
    kv = lax.dot_general(wt_ref[...], x_ref[0].astype(BF16), (((1,), (1,)), ((), ())), preferred_element_type=F32)
    k_ref[0] = kv[0:A_WIDTH].reshape(A_HEADS, A_HEAD_DIM, tm)
    v_ref[0] = kv[A_WIDTH:2 * A_WIDTH].reshape(A_HEADS, A_HEAD_DIM, tm)


def _kv_win(x_prompt, w_kv_t, win, tm):
    B, S, D = x_prompt.shape
    t0 = (S - win) // tm
    out = jax.ShapeDtypeStruct((B, A_HEADS, A_HEAD_DIM, win), F32)
    ospec = pl.BlockSpec((1, A_HEADS, A_HEAD_DIM, tm), lambda b, t: (b, 0, 0, t))
    return pl.pallas_call(
        _kv_win_kernel,
        grid=(B, win // tm),
        in_specs=[pl.BlockSpec((1, tm, D), lambda b, t: (b, t0 + t, 0)),
                  pl.BlockSpec((2 * A_WIDTH, D), lambda b, t: (0, 0))],
        out_specs=[ospec, ospec],
        out_shape=[out, out],
        compiler_params=_params(("parallel", "parallel")),
        name="kv_win",
    )(x_prompt, w_kv_t)


def _rel_bucket(dist):
    max_exact = REL_BUCKETS // 2
    n = jnp.maximum(dist, 0)
    ratio = jnp.maximum(n, 1).astype(F32) / max_exact
    large = max_exact + (jnp.log(ratio) / math.log(REL_MAX_DIST / max_exact)
                         * (REL_BUCKETS - max_exact)).astype(jnp.int32)
    return jnp.where(n < max_exact, n, jnp.minimum(large, REL_BUCKETS - 1))


def _bias_of(rel_bias, dist):
    onehot = (_rel_bucket(dist)[None, :] == jnp.arange(REL_BUCKETS)[:, None]).astype(F32)
    return jnp.dot(rel_bias.astype(F32).T, onehot, precision=HIGHEST)


def _band_bias(rel_bias):
    period = 3 * BAND
    tabs = []
    for _, dil in BRANCHES:
        g = jnp.concatenate([_bias_of(rel_bias, (BAND - jnp.arange(BAND + 1)) * dil),
                             jnp.full((A_HEADS, period - BAND - 1), MASKED, F32)], axis=1)
        skew = jnp.tile(g, (1, BAND))[:, :BAND * (period - 1)].reshape(A_HEADS, BAND, period - 1)
        tabs.append(skew[:, :, :2 * BAND])
    return jnp.stack(tabs)


def _cache_bias(rel_bias, P):
    dist = P - jnp.arange(P + LANES)
    tabs = []
    for window, dil in BRANCHES:
        ok = (dist >= 0) & (dist <= window) & (dist % dil == 0)
        tabs.append(jnp.where(ok[None, :], _bias_of(rel_bias, dist), MASKED)[:, None, :])
    return jnp.stack(tabs)


def _attn_kernel(q_ref, k_ref, v_ref, bias_ref, o_ref, acc_ref, m_ref, l_ref):
    t = pl.program_id(2)
    tile0 = t * ATT_TILE
    lane = lax.broadcasted_iota(jnp.int32, (BAND, LANES), 1)
    head0 = lane < A_HEAD_DIM

    def rows(start, dil):
        return pl.ds(pl.multiple_of(start, BAND), BAND) if dil == 1 else pl.ds(start, BAND, stride=dil)

    for br, (_, dil) in enumerate(BRANCHES):
        span = BAND * dil

        def blocks(it, carry, br=br, dil=dil, span=span):
            gs = range(ATT_GROUP)
            idx = [it * ATT_GROUP + g for g in gs]
            start = [(i % dil) + (i // dil) * span for i in idx]
            cur = [tile0 + s for s in start]
            first = [c < span for c in cur]
            prev = [jnp.where(first[g], cur[g], cur[g] - span) for g in gs]
            q = [q_ref[0, rows(start[g], dil), :] * (A_HEAD_DIM ** -0.5) for g in gs]
            kk = [jnp.concatenate([k_ref[0, rows(prev[g], dil), :], k_ref[0, rows(cur[g], dil), :]],
                                  axis=0).astype(BF16) for g in gs]
            vv = [jnp.concatenate([v_ref[0, rows(prev[g], dil), :], v_ref[0, rows(cur[g], dil), :]],
                                  axis=0).astype(BF16) for g in gs]
            pen = [jnp.where(first[g], MASKED, 0.0) for g in gs]
            gh = [(g, hh) for g in gs for hh in range(2)]
            qh = [jnp.where(head0 if hh == 0 else ~head0, q[g], 0.0).astype(BF16) for g, hh in gh]
            s = [lax.dot_general(qh[j], kk[g], (((1,), (1,)), ((), ())), preferred_element_type=F32)
                 for j, (g, hh) in enumerate(gh)]
            s = [s[j] + bias_ref[br, hh] for j, (g, hh) in enumerate(gh)]
            s = [jnp.concatenate([s[j][:, 0:BAND] + pen[g], s[j][:, BAND:2 * BAND]], axis=1)
                 for j, (g, hh) in enumerate(gh)]
            m = [jnp.max(t, axis=1, keepdims=True) for t in s]
            p = [jnp.exp(s[j] - m[j]) for j in range(len(gh))]
            l = [jnp.sum(t, axis=1, keepdims=True) for t in p]
            pv = [jnp.dot(p[j].astype(BF16), vv[g], preferred_element_type=F32) for j, (g, hh) in enumerate(gh)]
            for g in gs:
                acc_ref[br, rows(start[g], dil), :] = jnp.where(head0, pv[2 * g], pv[2 * g + 1])
                m_ref[br, rows(start[g], dil), :] = jnp.where(head0, m[2 * g], m[2 * g + 1])
                l_ref[br, rows(start[g], dil), :] = jnp.where(head0, l[2 * g], l[2 * g + 1])
            return carry

        lax.fori_loop(0, ATT_TILE // BAND // ATT_GROUP, blocks, 0)

    def merge(c, carry):
        r = pl.ds(pl.multiple_of(c * 256, 256), 256)
        m0, m1, m2 = m_ref[0, r, :], m_ref[1, r, :], m_ref[2, r, :]
        mx = jnp.maximum(jnp.maximum(m0, m1), m2)
        w0, w1, w2 = jnp.exp(m0 - mx), jnp.exp(m1 - mx), jnp.exp(m2 - mx)
        num = w0 * acc_ref[0, r, :] + w1 * acc_ref[1, r, :] + w2 * acc_ref[2, r, :]
        den = w0 * l_ref[0, r, :] + w1 * l_ref[1, r, :] + w2 * l_ref[2, r, :]
        o_ref[0, r, :] = num / den
        return carry

    lax.fori_loop(0, ATT_TILE // 256, merge, 0)


def _attn(qkv, bias, B, S):
    n_pairs = A_HEADS // 2
    qkv3 = qkv.reshape(B, S, 3 * A_WIDTH)
    return pl.pallas_call(
        _attn_kernel,
        grid=(B, n_pairs, S // ATT_TILE),
        in_specs=[pl.BlockSpec((1, ATT_TILE, LANES), lambda b, hp, t: (b, t, hp)),
                  pl.BlockSpec((1, S, LANES), lambda b, hp, t: (b, 0, n_pairs + hp)),
                  pl.BlockSpec((1, S, LANES), lambda b, hp, t: (b, 0, 2 * n_pairs + hp)),
                  pl.BlockSpec((3, 2, BAND, 2 * BAND), lambda b, hp, t: (0, hp, 0, 0))],
        out_specs=pl.BlockSpec((1, ATT_TILE, LANES), lambda b, hp, t: (b, t, hp)),
        out_shape=jax.ShapeDtypeStruct((B, S, A_WIDTH), F32),
        scratch_shapes=[pltpu.VMEM((3, ATT_TILE, LANES), F32)] * 3,
        compiler_params=_params(("parallel", "parallel", "arbitrary")),
        name="attn",
    )(qkv3, qkv3, qkv3, bias)


def _attn_dec_kernel(qkv_ref, kt_ref, vt_ref, bias_ref, o_ref, *, nb):
    i = pl.program_id(0)
    N = qkv_ref.shape[-1]
    P = kt_ref.shape[-1]
    lane_n = lax.broadcasted_iota(jnp.int32, (A_HEAD_DIM, N), 1)
    lane_o = lax.broadcasted_iota(jnp.int32, (A_HEAD_DIM, nb), 1)
    lane_t = lax.broadcasted_iota(jnp.int32, (1, LANES), 1)

    def head(h, carry):
        slab = jnp.zeros((A_HEAD_DIM, nb), F32)
        for j in range(nb):
            pick = lane_n == i * nb + j
            col = lambda t: jnp.sum(jnp.where(pick, t, 0.0), axis=1, keepdims=True)
            q = col(qkv_ref[0, h]) * (A_HEAD_DIM ** -0.5)
            k_new, v_new = col(qkv_ref[1, h]), col(qkv_ref[2, h])
            s_new = jnp.sum(q * k_new, axis=0, keepdims=True)
            s = jnp.concatenate([jnp.sum(kt_ref[j, h] * q, axis=0, keepdims=True),
                                 jnp.where(lane_t == 0, s_new, 0.0)], axis=1)
            ps, ms, ls = [], [], []
            for br in range(3):
                sb = s + bias_ref[br, h]
                m = jnp.max(sb, axis=1, keepdims=True)
                p = jnp.exp(sb - m)
                ps.append(p)
                ms.append(m)
                ls.append(jnp.sum(p, axis=1, keepdims=True))
            mx = jnp.maximum(jnp.maximum(ms[0], ms[1]), ms[2])
            w = jnp.zeros((1, P + LANES), F32)
            den = jnp.zeros((1, 1), F32)
            for p, m, l in zip(ps, ms, ls):
                e = jnp.exp(m - mx)
                w = w + e * p
                den = den + e * l
            o = jnp.sum(vt_ref[j, h] * w[:, 0:P], axis=1, keepdims=True) + v_new * w[:, P:P + 1]
            slab = jnp.where(lane_o == j, o / den, slab)
        o_ref[0, h] = slab
        return carry

    lax.fori_loop(0, A_HEADS, head, 0, unroll=2)


def _attn_dec(qkv_t, cache_kt, cache_vt, bias, nb):
    N, P = cache_kt.shape[0], cache_kt.shape[-1]
    cache_spec = pl.BlockSpec((nb, A_HEADS, A_HEAD_DIM, P), lambda i: (i, 0, 0, 0))
    return pl.pallas_call(
        functools.partial(_attn_dec_kernel, nb=nb),
        grid=(N // nb,),
        in_specs=[pl.BlockSpec((3, A_HEADS, A_HEAD_DIM, N), lambda i: (0, 0, 0, 0)), cache_spec, cache_spec,
                  pl.BlockSpec((3, A_HEADS, 1, P + LANES), lambda i: (0, 0, 0, 0))],
        out_specs=pl.BlockSpec((1, A_HEADS, A_HEAD_DIM, nb), lambda i: (i, 0, 0, 0)),
        out_shape=jax.ShapeDtypeStruct((N // nb, A_HEADS, A_HEAD_DIM, nb), F32),
        compiler_params=_params(("parallel",)),
        name="attn_dec",
    )(qkv_t, cache_kt, cache_vt, bias)


def _split3(x):
    hi = x.astype(BF16)
    r = x - hi.astype(F32)
    mid = r.astype(BF16)
    return hi, mid, (r - mid.astype(F32)).astype(BF16)


def _tril_dot(tril_b, g):
    return sum(jnp.dot(tril_b, piece, preferred_element_type=F32) for piece in _split3(g))


def _dot3(a, b):
    ah, bh = a.astype(BF16), b.astype(BF16)
    al, bl = (a - ah.astype(F32)).astype(BF16), (b - bh.astype(F32)).astype(BF16)
    d = lambda x, y: jnp.dot(x, y, preferred_element_type=F32)
    return d(ah, bh) + (d(ah, bl) + d(al, bh))


def _delta_kernel(alog_ref, dtb_ref, ub_ref, z_ref, ab_ref, cw_ref, og_ref, o_ref, st_ref,
                  pad_ref, u_s, wq_s, kt_s, qk_s, gl_s, a_s, rhs_s):
    t = pl.program_id(1)
    TS = ub_ref.shape[1]
    nch = TS // CHUNK
    hdr = 8

    slabs = range(CONV_DIM // LANES)
    seq_rows = lambda first, n: pl.ds(2 * first, n, stride=2)

    @pl.when(t == 0)
    def _():
        for s in slabs:
            pad_ref[s, seq_rows(0, hdr), :] = jnp.zeros((hdr, LANES), F32)
        st_ref[...] = jnp.zeros(st_ref.shape, F32)

    @pl.when(t > 0)
    def _():
        for s in slabs:
            pad_ref[s, seq_rows(0, hdr), :] = pad_ref[s, seq_rows(TS, hdr), :]

    for s in slabs:
        pad_ref[s, seq_rows(hdr, TS), :] = ub_ref[0, :, s * LANES:(s + 1) * LANES]

    ri = lax.broadcasted_iota(jnp.int32, (CHUNK, CHUNK), 0)
    ci = lax.broadcasted_iota(jnp.int32, (CHUNK, CHUNK), 1)
    incl = ri >= ci
    strict = ri > ci
    tril_b = incl.astype(BF16)
    eye = (ri == ci).astype(F32)
    lane = lax.broadcasted_iota(jnp.int32, (CHUNK, LANES), 1)

    def local(c, carry):
        base = c * CHUNK if isinstance(c, int) else pl.multiple_of(c * CHUNK, CHUNK)
        ab = ab_ref[0, pl.ds(base, CHUNK), :]
        heads = []
        for h in range(B_HEADS):
            def conv(col):
                first = c * CHUNK + (hdr - (CONV_WIDTH - 1))
                acc = pad_ref[col // LANES, seq_rows(first, CHUNK), :] * cw_ref[0:1, col:col + LANES]
                for i in range(1, CONV_WIDTH):
                    acc = acc + pad_ref[col // LANES, seq_rows(first + i, CHUNK), :] * cw_ref[i:i + 1, col:col + LANES]
                return _silu(acc)

            cq, ck, v = conv(h * LANES), conv(B_WIDTH + h * LANES), conv(2 * B_WIDTH + h * LANES)
            q = cq * lax.rsqrt(jnp.sum(cq * cq, axis=1, keepdims=True) + 1e-6) * (B_HEAD_DIM ** -0.5)
            k = ck * lax.rsqrt(jnp.sum(ck * ck, axis=1, keepdims=True) + 1e-6)
            a_raw = jnp.sum(jnp.where(lane == h, ab, 0.0), axis=1, keepdims=True)
            b_raw = jnp.sum(jnp.where(lane == h + B_HEADS, ab, 0.0), axis=1, keepdims=True)
            neg_a = -jnp.exp(jnp.full((1, LANES), alog_ref[h], F32))
            g = neg_a * _softplus(a_raw + dtb_ref[h])
            beta = _sigmoid(b_raw)
            heads.append((q, k, v, g, beta))
        hs = range(B_HEADS)
        q, k, v, g, beta = zip(*heads)
        gc = [_tril_dot(tril_b, g[h]) for h in hs]
        dmat = [_tril_dot(tril_b, jnp.where(strict, g[h][:, 0:CHUNK], 0.0)) for h in hs]
        kq = [_bdot_nt(jnp.concatenate([k[h], q[h]], axis=0), k[h]) for h in hs]
        decay = [jnp.where(incl, jnp.exp(dmat[h]), 0.0) for h in hs]
        e_gc = [jnp.exp(gc[h]) for h in hs]
        for h in hs:
            gc_last = gc[h][CHUNK - 1:CHUNK, :]
            a_s[h, c] = jnp.where(strict, beta[h] * kq[h][0:CHUNK] * decay[h], 0.0)
            rhs_s[h, c] = jnp.concatenate([v[h] * beta[h], k[h] * (beta[h] * e_gc[h])], axis=1)
            wq_s[h, c, CHUNK:2 * CHUNK] = q[h] * e_gc[h]
            kt_s[h, c] = k[h] * jnp.exp(gc_last - gc[h])
            qk_s[h, c] = kq[h][CHUNK:2 * CHUNK] * decay[h]
            gl_s[h, c] = jnp.broadcast_to(jnp.exp(gc_last), (8, LANES))
        return carry

    def solve(c, carry):
        hs = range(B_HEADS)
        a = [a_s[h, c] for h in hs]
        x = [eye - a[h] for h in hs]
        p = [_bdot(a[h], a[h]) for h in hs]
        for _ in range(int(math.log2(CHUNK)) - 2):
            r = [_bdot(jnp.concatenate([x[h], p[h]], axis=0), p[h]) for h in hs]
            x = [x[h] + r[h][0:CHUNK] for h in hs]
            p = [r[h][CHUNK:2 * CHUNK] for h in hs]
        x = [x[h] + _bdot(x[h], p[h]) for h in hs]
        sol = [_bdot(x[h], rhs_s[h, c]) for h in hs]
        for h in hs:
            u_s[h, c] = sol[h][:, 0:LANES]
            wq_s[h, c, 0:CHUNK] = sol[h][:, LANES:2 * LANES]
        return carry

    def scan(c, carry):
        base = c * CHUNK if isinstance(c, int) else pl.multiple_of(c * CHUNK, CHUNK)
        hs = range(B_HEADS)
        state = [st_ref[0, h] for h in hs]
        r = [_bdot(wq_s[h, c], state[h]) for h in hs]
        v_new = [u_s[h, c] - r[h][0:CHUNK] for h in hs]
        upd = [_bdot_tn(kt_s[h, c], v_new[h]) for h in hs]
        out = [r[h][CHUNK:2 * CHUNK] + _bdot(qk_s[h, c], v_new[h]) for h in hs]
        for h in hs:
            st_ref[0, h] = state[h] * gl_s[h, c][0:1, :] + upd[h]
            o = out[h] * lax.rsqrt(jnp.mean(out[h] * out[h], axis=1, keepdims=True) + RMS_EPS) * og_ref[...]
            cols = slice(h * LANES, (h + 1) * LANES)
            o_ref[0, pl.ds(base, CHUNK), cols] = o * _silu(z_ref[0, pl.ds(base, CHUNK), cols])
        return carry

    def stages(c, carry):
        scan(c - 1, carry)
        solve(c, carry)
        return local(c + 1, carry)

    assert nch >= 3
    local(0, 0)
    solve(0, 0)
    local(1, 0)
    lax.fori_loop(1, nch - 1, stages, 0)
    scan(nch - 2, 0)
    solve(nch - 1, 0)
    scan(nch - 1, 0)


def _delta(ub, zb, ab, conv_w, a_log, dt_bias, o_norm_g, B, S):
    ts = min(S, DELTA_TILE)
    nch = ts // CHUNK
    seq = lambda width: pl.BlockSpec((1, ts, width), lambda b, t, *_: (b, t, 0))
    fix = lambda shape: pl.BlockSpec(shape, lambda b, t, *_: (0, 0))
    per_chunk = lambda rows, width: pltpu.VMEM((B_HEADS, nch, rows, width), F32)
    grid_spec = pltpu.PrefetchScalarGridSpec(
        num_scalar_prefetch=2,
        grid=(B, S // ts),
        in_specs=[seq(CONV_DIM), seq(B_WIDTH), seq(LANES), fix((CONV_WIDTH, CONV_DIM)), fix((1, LANES))],
        out_specs=[seq(B_WIDTH),
                   pl.BlockSpec((1, B_HEADS, B_HEAD_DIM, B_HEAD_DIM), lambda b, t, *_: (b, 0, 0, 0))],
        scratch_shapes=[pltpu.VMEM((CONV_DIM // LANES, 2 * (ts + 8), LANES), F32),
                        per_chunk(CHUNK, LANES), per_chunk(2 * CHUNK, LANES),
                        per_chunk(CHUNK, LANES), per_chunk(CHUNK, CHUNK), per_chunk(8, LANES),
                        per_chunk(CHUNK, CHUNK), per_chunk(CHUNK, 2 * LANES)],
    )
    return pl.pallas_call(
        _delta_kernel,
        grid_spec=grid_spec,
        out_shape=[jax.ShapeDtypeStruct((B, S, B_WIDTH), F32),
                   jax.ShapeDtypeStruct((B, B_HEADS, B_HEAD_DIM, B_HEAD_DIM), F32)],
        compiler_params=_params(("parallel", "arbitrary")),
        name="delta",
    )(a_log, dt_bias, ub.reshape(B, S, CONV_DIM), zb.reshape(B, S, B_WIDTH), ab.reshape(B, S, LANES),
      conv_w, o_norm_g.reshape(1, LANES))


def _delta_dec_kernel(alog_ref, dtb_ref, ub_ref, cs_ref, zb_ref, ab_ref, w_ref, og_ref, st_ref,
                      o_ref, so_ref, *, nb):
    i = pl.program_id(0)
    N = ub_ref.shape[0]
    acc = ub_ref[...] * w_ref[CONV_WIDTH - 1:CONV_WIDTH, :]
    for t in range(CONV_WIDTH - 1):
        acc = acc + cs_ref[t] * w_ref[t:t + 1, :]
    c = _silu(acc)
    ab = ab_ref[...]
    lane = lax.broadcasted_iota(jnp.int32, (N, LANES), 1)
    samp = lax.broadcasted_iota(jnp.int32, (B_HEAD_DIM, N), 1)
    row_id = lax.broadcasted_iota(jnp.int32, (N, LANES), 0)
    out_row = lax.broadcasted_iota(jnp.int32, (nb, LANES), 0)
    heads = []
    for h in range(B_HEADS):
        cq = c[:, h * LANES:(h + 1) * LANES]
        ck = c[:, B_WIDTH + h * LANES:B_WIDTH + (h + 1) * LANES]
        v = c[:, 2 * B_WIDTH + h * LANES:2 * B_WIDTH + (h + 1) * LANES]
        q = cq * lax.rsqrt(jnp.sum(cq * cq, axis=1, keepdims=True) + 1e-6) * (B_HEAD_DIM ** -0.5)
        k = ck * lax.rsqrt(jnp.sum(ck * ck, axis=1, keepdims=True) + 1e-6)
        a_raw = jnp.sum(jnp.where(lane == h, ab, 0.0), axis=1, keepdims=True)
        b_raw = jnp.sum(jnp.where(lane == h + B_HEADS, ab, 0.0), axis=1, keepdims=True)
        neg_a = -jnp.exp(jnp.full((1, 1), alog_ref[h], F32))
        dec = jnp.exp(neg_a * _softplus(a_raw + dtb_ref[h]))
        beta = _sigmoid(b_raw)
        heads.append((q.T, k.T, v, jnp.broadcast_to(dec, (N, LANES)), jnp.broadcast_to(beta, (N, LANES)),
                      zb_ref[:, h * LANES:(h + 1) * LANES]))

    hs = range(B_HEADS)
    q_t, k_t, v, dec, beta, z = zip(*heads)

    def sample(j, o_acc):
        n = i * nb + j
        pick = samp == n
        k_col = [jnp.sum(jnp.where(pick, k_t[h], 0.0), axis=1, keepdims=True) for h in hs]
        q_col = [jnp.sum(jnp.where(pick, q_t[h], 0.0), axis=1, keepdims=True) for h in hs]
        pick_r = row_id == n
        row = lambda t: jnp.sum(jnp.where(pick_r, t, 0.0), axis=0, keepdims=True)
        st = [st_ref[j, h] * row(dec[h]) for h in hs]
        mem = [jnp.sum(k_col[h] * st[h], axis=0, keepdims=True) for h in hs]
        st = [st[h] + k_col[h] * ((row(v[h]) - mem[h]) * row(beta[h])) for h in hs]
        o = [jnp.sum(q_col[h] * st[h], axis=0, keepdims=True) for h in hs]
        o = [o[h] * lax.rsqrt(jnp.mean(o[h] * o[h], axis=1, keepdims=True) + RMS_EPS) * og_ref[...] for h in hs]
        for h in hs:
            so_ref[j, h] = st[h]
        return tuple(jnp.where(out_row == j, o[h] * _silu(row(z[h])), o_acc[h]) for h in hs)

    outs = lax.fori_loop(0, nb, sample, tuple(jnp.zeros((nb, LANES), F32) for _ in hs), unroll=2)
    for h in hs:
        o_ref[:, h * LANES:(h + 1) * LANES] = outs[h]


def _delta_dec(ub_s, conv_state, zb_s, ab_s, conv_w, a_log, dt_bias, o_norm_g, state, nb):
    N = ub_s.shape[0]
    full2 = lambda shape: pl.BlockSpec(shape, lambda i, *_: (0, 0))
    grid_spec = pltpu.PrefetchScalarGridSpec(
        num_scalar_prefetch=2,
        grid=(N // nb,),
        in_specs=[full2((N, CONV_DIM)),
                  pl.BlockSpec((CONV_WIDTH - 1, N, CONV_DIM), lambda i, *_: (0, 0, 0)),
                  full2((N, B_WIDTH)), full2((N, LANES)), full2((CONV_WIDTH, CONV_DIM)), full2((1, LANES)),
                  pl.BlockSpec((nb, B_HEADS, B_HEAD_DIM, B_HEAD_DIM), lambda i, *_: (i, 0, 0, 0))],
        out_specs=[pl.BlockSpec((nb, B_WIDTH), lambda i, *_: (i, 0)),
                   pl.BlockSpec((nb, B_HEADS, B_HEAD_DIM, B_HEAD_DIM), lambda i, *_: (i, 0, 0, 0))],
    )
    return pl.pallas_call(
        functools.partial(_delta_dec_kernel, nb=nb),
        grid_spec=grid_spec,
        out_shape=[jax.ShapeDtypeStruct((N, B_WIDTH), F32), jax.ShapeDtypeStruct(state.shape, F32)],
        compiler_params=_params(("parallel",)),
        name="delta_dec",
    )(a_log, dt_bias, ub_s, jnp.swapaxes(conv_state, 0, 1), zb_s, ab_s, conv_w, o_norm_g.reshape(1, LANES), state)


def _layer_norm(r, g, b):
    mu = jnp.mean(r, axis=1, keepdims=True)
    d = r - mu
    var = jnp.mean(d * d, axis=1, keepdims=True)
    return d * lax.rsqrt(var + LN_EPS) * g + b


def _mix_ln_kernel(*refs, steps):
    outs = refs[-3:]

    @pl.when(pl.program_id(0) < steps)
    def _():
        _mix_ln_rows(*refs[:8], *outs)

    @pl.when(pl.program_id(0) >= steps)
    def _():
        for o in outs:
            o[...] = jnp.zeros(o.shape, F32)


def _mix_ln_rows(oa_ref, ob_ref, x_ref, wo_ref, g_ref, b_ref, wr_ref, br_ref, h_ref, route_ref, cnt_ref):
    if wo_ref.dtype == BF16:
        dot = lambda a, w: jnp.dot(a.astype(BF16), w, preferred_element_type=F32)
    else:
        dot = _fdot
    y = dot(oa_ref[...], wo_ref[0:A_WIDTH, :]) + dot(ob_ref[...], wo_ref[A_WIDTH:A_WIDTH + B_WIDTH, :])
    hcur = _layer_norm(DN_ALPHA * x_ref[...] + y, g_ref[...], b_ref[...])
    h_ref[...] = hcur
    logits = (_bdot if wo_ref.dtype == BF16 else _dot3)(hcur, wr_ref[...]) + br_ref[...]
    lane = lax.broadcasted_iota(jnp.int32, logits.shape, 1)
    lane_f = lane.astype(F32)
    ninf = -jnp.inf
    big = 1e9
    gl = jnp.where(lane < N_GROUPS, logits, ninf)
    gmax = jnp.max(gl, axis=1, keepdims=True)
    g_idx = jnp.min(jnp.where(gl == gmax, lane_f, big), axis=1, keepdims=True)
    p_group = 1.0 / jnp.sum(jnp.exp(gl - gmax), axis=1, keepdims=True)
    grp_of_lane = ((lane - N_GROUPS) >> 3).astype(F32)
    sel = (lane >= N_GROUPS) & (lane < N_GROUPS + N_EXPERTS) & (grp_of_lane == g_idx)
    el = jnp.where(sel, logits, ninf)
    v1 = jnp.max(el, axis=1, keepdims=True)
    i1 = jnp.min(jnp.where(el == v1, lane_f, big), axis=1, keepdims=True)
    el2 = jnp.where(lane_f == i1, ninf, el)
    v2 = jnp.max(el2, axis=1, keepdims=True)
    i2 = jnp.min(jnp.where(el2 == v2, lane_f, big), axis=1, keepdims=True)
    t = jnp.exp(v2 - v1)
    gate1 = p_group / (1.0 + t)
    gate2 = p_group * t / (1.0 + t)
    e1, e2 = i1 - N_GROUPS, i2 - N_GROUPS
    route_ref[...] = jnp.where(lane == 0, gate1, jnp.where(lane == 1, gate2, jnp.where(
        lane == 2, e1, jnp.where(lane == 3, e2, 0.0))))
    chosen = ((lane_f == e1) | (lane_f == e2)).astype(F32)
    tm = chosen.shape[0]
    cnt_ref[...] = jnp.sum(chosen.reshape(tm // ROUTE_TILE, ROUTE_TILE, LANES), axis=1)[:, None, :]


def _mix_ln(oa, ob, x2d, wo_b, g, b, wr, br, tm, total, row0, prev=()):
    T, D = x2d.shape
    off = row0 // tm
    steps = T // tm
    tail = 1 if row0 + T < total else 0
    assert total - (row0 + T) <= tm
    row = lambda i: (jnp.minimum(i, steps - 1), 0)
    out_row = lambda i: (off + i, 0)
    fix = lambda i: (0, 0)
    sub = tm // ROUTE_TILE
    return pl.pallas_call(
        functools.partial(_mix_ln_kernel, steps=steps),
        grid=(steps + tail,),
        in_specs=[pl.BlockSpec((tm, A_WIDTH), row), pl.BlockSpec((tm, B_WIDTH), row), pl.BlockSpec((tm, D), row),
                  pl.BlockSpec((A_WIDTH + B_WIDTH, D), fix), pl.BlockSpec((1, D), fix), pl.BlockSpec((1, D), fix),
                  pl.BlockSpec((D, LANES), fix), pl.BlockSpec((1, LANES), fix)]
                 + [pl.BlockSpec(memory_space=pl.ANY)] * len(prev),
        out_specs=[pl.BlockSpec((tm, D), out_row), pl.BlockSpec((tm, LANES), out_row),
                   pl.BlockSpec((sub, 1, LANES), lambda i: (off + i, 0, 0))],
        out_shape=[jax.ShapeDtypeStruct((total, D), F32), jax.ShapeDtypeStruct((total, LANES), F32),
                   jax.ShapeDtypeStruct((total // ROUTE_TILE, 1, LANES), F32)],
        input_output_aliases={8 + j: j for j in range(len(prev))},
        compiler_params=_params(("parallel",)),
        name="mix_ln",
    )(oa, ob, x2d, wo_b, g, b, wr, br, *prev)


def _slot_layout(counts):
    tiles = counts.shape[0]
    n_assign = tiles * ROUTE_TILE * TOP_K
    n_blocks = -(-(n_assign + N_EXPERTS * (MOE_BLK - 1)) // MOE_BLK)
    per_tile = counts.reshape(tiles, LANES)
    earlier = (jnp.arange(tiles)[:, None] > jnp.arange(tiles)[None, :]).astype(F32)
    before = jnp.dot(earlier, per_tile, precision=HIGHEST)
    total = jnp.sum(per_tile, axis=0)[:N_EXPERTS]
    padded = jnp.ceil(total / MOE_BLK) * MOE_BLK
    upto = (jnp.arange(N_EXPERTS)[:, None] <= jnp.arange(N_EXPERTS)[None, :]).astype(F32)
    pad_end = jnp.dot(padded, upto, precision=HIGHEST)
    pad_start = pad_end - padded
    base = (before + jnp.pad(pad_start, (0, LANES - N_EXPERTS))[None, :]).reshape(tiles, 1, LANES)
    blk_start = (jnp.arange(n_blocks) * MOE_BLK).astype(F32)
    in_e = ((pad_start[None, :] <= blk_start[:, None]) & (blk_start[:, None] < pad_end[None, :])).astype(F32)
    used = blk_start < pad_end[-1]
    n_used = jnp.sum(used.astype(jnp.int32))
    last_e = jnp.max(jnp.where(padded > 0, jnp.arange(N_EXPERTS), 0)).astype(F32)
    blk_e = jnp.where(used, jnp.dot(in_e, jnp.arange(N_EXPERTS, dtype=F32), precision=HIGHEST), last_e)
    blk_n = jnp.clip(jnp.dot(in_e, pad_start + total, precision=HIGHEST) - blk_start, 0, MOE_BLK)
    blk_x = jnp.minimum(jnp.arange(n_blocks), jnp.maximum(n_used - 1, 0))
    i32 = lambda t: t.astype(jnp.int32)
    return base, i32(blk_e), i32(blk_n), i32(blk_x), i32(pad_end), n_blocks


def _slot_kernel(route_ref, base_ref, dest_ref):
    lane = lax.broadcasted_iota(jnp.int32, (ROUTE_TILE, LANES), 1)
    lane_f = lane.astype(F32)
    ri = lax.broadcasted_iota(jnp.int32, (ROUTE_TILE, ROUTE_TILE), 0)
    ci = lax.broadcasted_iota(jnp.int32, (ROUTE_TILE, ROUTE_TILE), 1)
    before = (ri > ci).astype(BF16)
    for t in range(base_ref.shape[0]):
        rows = slice(t * ROUTE_TILE, (t + 1) * ROUTE_TILE)
        route = route_ref[rows, :]
        e1 = jnp.sum(jnp.where(lane == 2, route, 0.0), axis=1, keepdims=True)
        e2 = jnp.sum(jnp.where(lane == 3, route, 0.0), axis=1, keepdims=True)
        oh1, oh2 = lane_f == e1, lane_f == e2
        earlier = _bdot(before, (oh1 | oh2).astype(F32))
        slot = base_ref[t] + earlier
        d1 = jnp.sum(jnp.where(oh1, slot, 0.0), axis=1, keepdims=True)
        d2 = jnp.sum(jnp.where(oh2, slot, 0.0), axis=1, keepdims=True)
        dest_ref[rows, :] = jnp.where(lane == 0, d1, jnp.where(lane == 1, d2, 0.0)).astype(jnp.int32)


def _slots(route, base):
    tiles = route.shape[0] // ROUTE_TILE
    sub = max(d for d in range(1, 9) if tiles % d == 0)
    return pl.pallas_call(
        _slot_kernel,
        grid=(tiles // sub,),
        in_specs=[pl.BlockSpec((sub * ROUTE_TILE, LANES), lambda i: (i, 0)),
                  pl.BlockSpec((sub, 1, LANES), lambda i: (i, 0, 0))],
        out_specs=pl.BlockSpec((sub * ROUTE_TILE, LANES), lambda i: (i, 0)),
        out_shape=jax.ShapeDtypeStruct(route.shape, jnp.int32),
        compiler_params=_params(("parallel",)),
        name="slots",
    )(route, base)


def _pack_bf16_pairs(x):
    half = x.shape[1] // 2
    bits = pltpu.bitcast(x.astype(BF16).astype(F32), jnp.uint32)
    return (bits[:, 0:half] >> 16) | bits[:, half:2 * half]


def _unpack_bf16_pairs(w):
    lo = pltpu.bitcast(w << 16, F32).astype(BF16)
    hi = pltpu.bitcast(w & jnp.uint32(0xFFFF0000), F32).astype(BF16)
    return jnp.concatenate([lo, hi], axis=1)


def _dispatch_kernel(dest_ref, pad_end_ref, h_ref, xs_hbm, hbuf, zbuf, sem, zsem):
    i = pl.program_id(0)
    slot = i % 2
    tile = h_ref.shape[0]
    a0 = i * (tile * TOP_K)

    @pl.when(i == 0)
    def _():
        zbuf[...] = jnp.zeros(zbuf.shape, zbuf.dtype)
        fills =[pltpu.make_async_copy(zbuf, xs_hbm.at[pl.ds(pl.multiple_of(pad_end_ref[e] - MOE_BLK, MOE_BLK), MOE_BLK)],
                                       zsem) for e in range(N_EXPERTS)]
        has_rows = [pad_end_ref[e] > (pad_end_ref[e - 1] if e else 0) for e in range(N_EXPERTS)]
        for e in range(N_EXPERTS):
            @pl.when(has_rows[e])
            def _(e=e):
                fills[e].start()
        for e in range(N_EXPERTS):
            @pl.when(has_rows[e])
            def _(e=e):
                fills[e].wait()

        def tail(b):
            return pltpu.make_async_copy(zbuf, xs_hbm.at[pl.ds(pl.multiple_of(b * MOE_BLK, MOE_BLK), MOE_BLK)], zsem)

        def tail_start(b, c):
            tail(b).start()
            return c

        def tail_wait(b, c):
            tail(b).wait()
            return c

        first_unused = pad_end_ref[N_EXPERTS - 1] // MOE_BLK
        lax.fori_loop(first_unused, xs_hbm.shape[0] // MOE_BLK, tail_start, 0)
        lax.fori_loop(first_unused, xs_hbm.shape[0] // MOE_BLK, tail_wait, 0)

    hbuf[slot] = _pack_bf16_pairs(h_ref[...])

    def start(r, c):
        for k in range(TOP_K):
            pltpu.make_async_copy(hbuf.at[slot, pl.ds(r, 1)], xs_hbm.at[pl.ds(dest_ref[a0 + r * TOP_K + k], 1)],
                                  sem.at[slot]).start()
        return c

    lax.fori_loop(0, tile, start, 0, unroll=8)

    def drain(s):
        for k in range(TOP_K):
            pltpu.make_async_copy(hbuf.at[s], xs_hbm.at[pl.ds(0, tile)], sem.at[s]).wait()

    @pl.when(i > 0)
    def _():
        drain(1 - slot)

    @pl.when(i == pl.num_programs(0) - 1)
    def _():
        drain(slot)


def _dispatch(h_all, dest, pad_end, slots):
    T, D = h_all.shape
    tile = max(t for t in range(ROUTE_TILE, 4 * ROUTE_TILE + 1, 8) if T % t == 0)
    grid_spec = pltpu.PrefetchScalarGridSpec(
        num_scalar_prefetch=2,
        grid=(T // tile,),
        in_specs=[pl.BlockSpec((tile, D), lambda i, *_: (i, 0))],
        out_specs=pl.BlockSpec(memory_space=pl.ANY),
        scratch_shapes=[pltpu.VMEM((2, tile, D // 2), jnp.uint32), pltpu.VMEM((MOE_BLK, D // 2), jnp.uint32),
                        pltpu.SemaphoreType.DMA((2,)), pltpu.SemaphoreType.DMA],
    )
    return pl.pallas_call(
        _dispatch_kernel,
        grid_spec=grid_spec,
        out_shape=jax.ShapeDtypeStruct((slots, D // 2), jnp.uint32),
        compiler_params=_params(("arbitrary",)),
        name="dispatch",
    )(dest, pad_end, h_all)


def _moe_kernel(blk_e_ref, blk_n_ref, blk_x_ref, x_ref, wg_ref, wu_ref, wd_ref, y_ref, wg_b, wu_b, wd_b):
    del blk_x_ref
    i = pl.program_id(0)
    n_valid = blk_n_ref[i]

    @pl.when((i == 0) | (blk_e_ref[i] != blk_e_ref[jnp.maximum(i - 1, 0)]))
    def _():
        wg_b[...] = wg_ref[0].astype(BF16)
        wu_b[...] = wu_ref[0].astype(BF16)
        wd_b[...] = wd_ref[0].astype(BF16)

    @pl.when(n_valid > 0)
    def _():
        x = _unpack_bf16_pairs(x_ref[...])
        a = jnp.dot(x, wg_b[...], preferred_element_type=F32)
        u = jnp.dot(x, wu_b[...], preferred_element_type=F32)
        y_ref[...] = jnp.dot((_silu(a) * u).astype(BF16), wd_b[...], preferred_element_type=F32)

    @pl.when(n_valid == 0)
    def _():
        y_ref[...] = jnp.zeros(y_ref.shape, F32)


def _moe(xs, blk_e, blk_n, blk_x, w_gate, w_up, w_down):
    slots = xs.shape[0]
    D, De = w_gate.shape[-2:]
    wspec = lambda shape: pl.BlockSpec((1,) + shape, lambda i, be, *_: (be[i], 0, 0))
    grid_spec = pltpu.PrefetchScalarGridSpec(
        num_scalar_prefetch=3,
        grid=(slots // MOE_BLK,),
        in_specs=[pl.BlockSpec((MOE_BLK, xs.shape[1]), lambda i, be, bn, bx: (bx[i], 0)),
                  wspec((D, De)), wspec((D, De)), wspec((De, D))],
        out_specs=pl.BlockSpec((MOE_BLK, D), lambda i, *_: (i, 0)),
        scratch_shapes=[pltpu.VMEM((D, De), BF16), pltpu.VMEM((D, De), BF16), pltpu.VMEM((De, D), BF16)],
    )
    return pl.pallas_call(
        _moe_kernel,
        grid_spec=grid_spec,
        out_shape=jax.ShapeDtypeStruct((slots, D), F32),
        compiler_params=_params(("arbitrary",)),
        name="moe",
    )(blk_e, blk_n, blk_x, xs, w_gate, w_up, w_down)


def _final_ln_kernel(dest_ref, h_ref, route_ref, g_ref, b_ref, ys_hbm, o_ref, ybuf, sem, *, tile0):
    i = pl.program_id(0)
    slot = i % 2
    rows = h_ref.shape[0]

    def gather(tile, s):
        a0 = (tile0 + tile) * (rows * TOP_K)

        def start(r, c):
            for k in range(TOP_K):
                pltpu.make_async_copy(ys_hbm.at[pl.ds(dest_ref[a0 + r * TOP_K + k], 1)],
                                      ybuf.at[s, k, pl.ds(r, 1)], sem.at[s]).start()
            return c

        lax.fori_loop(0, rows, start, 0, unroll=8)

    @pl.when(i == 0)
    def _():
        gather(0, 0)

    @pl.when(i + 1 < pl.num_programs(0))
    def _():
        gather(i + 1, 1 - slot)

    for k in range(TOP_K):
        pltpu.make_async_copy(ys_hbm.at[pl.ds(0, rows)], ybuf.at[slot, k], sem.at[slot]).wait()
    route = route_ref[...]
    lane = lax.broadcasted_iota(jnp.int32, route.shape, 1)
    gate1 = jnp.sum(jnp.where(lane == 0, route, 0.0), axis=1, keepdims=True)
    gate2 = jnp.sum(jnp.where(lane == 1, route, 0.0), axis=1, keepdims=True)
    f = ybuf[slot, 0] * gate1 + ybuf[slot, 1] * gate2
    o_ref[...] = _layer_norm(DN_ALPHA * h_ref[...] + f, g_ref[...], b_ref[...])


def _final_ln(h_all, route, dest, ys, g, b, row0, rows):
    D = h_all.shape[1]
    tile = max(t for t in (ROUTE_TILE, 2 * ROUTE_TILE) if rows % t == 0 and row0 % t == 0)
    tile0 = row0 // tile
    row = lambda i, *_: (tile0 + i, 0)
    fix = lambda i, *_: (0, 0)
    grid_spec = pltpu.PrefetchScalarGridSpec(
        num_scalar_prefetch=1,
        grid=(rows // tile,),
        in_specs=[pl.BlockSpec((tile, D), row), pl.BlockSpec((tile, LANES), row),
                  pl.BlockSpec((1, D), fix), pl.BlockSpec((1, D), fix), pl.BlockSpec(memory_space=pl.ANY)],
        out_specs=pl.BlockSpec((tile, D), lambda i, *_: (i, 0)),
        scratch_shapes=[pltpu.VMEM((2, TOP_K, tile, D), F32), pltpu.SemaphoreType.DMA((2,))],
    )
    return pl.pallas_call(
        functools.partial(_final_ln_kernel, tile0=tile0),
        grid_spec=grid_spec,
        out_shape=jax.ShapeDtypeStruct((rows, D), F32),
        compiler_params=_params(("arbitrary",)),
        name="final_ln",
    )(dest, h_all, route, g, b, ys)


def kernel(x_prompt, x_sample, cache_a_k, cache_a_v, state_b_ssm, state_b_conv, w_in, rel_bias, conv_w, a_log, dt_bias, o_norm_g, w_out, ln1_g, ln1_b, w_group, b_group, w_router, b_router, w_gate, w_up, w_down, ln2_g, ln2_b):
    B, S, D = x_prompt.shape
    N, T = x_sample.shape[0], x_sample.shape[1]
    depth = w_in.shape[0]
    assert depth == 1 and T == 1 and S % ATT_TILE == 0 and N % ROUTE_TILE == 0 and cache_a_k.shape[2] % LANES == 0
    l = 0
    win_p = min(BRANCHES[-1][0], S)

    w_pad32 = jnp.pad(w_in[l], ((0, 0), (0, IN_COLS_PAD - IN_COLS)))
    w_pad = w_pad32.astype(BF16)
    wo_b = w_out[l].astype(BF16)
    wr = jnp.pad(jnp.concatenate([w_group[l], w_router[l]], axis=1), ((0, 0), (0, LANES - N_GROUPS - N_EXPERTS)))
    br = jnp.pad(jnp.concatenate([b_group[l], b_router[l].reshape(-1)]), (0, LANES - N_GROUPS - N_EXPERTS))[None, :]
    g1, b1 = ln1_g[l][None, :], ln1_b[l][None, :]
    g2, b2 = ln2_g[l][None, :], ln2_b[l][None, :]

    xp = x_prompt.reshape(B * S, D)
    qkv_p, ub_p, zb_p, ab_p = _proj(xp, w_pad, 512)
    oa_p = _attn(qkv_p, _band_bias(rel_bias), B, S)
    ob_p, st_p = _delta(ub_p, zb_p, ab_p, conv_w[l], a_log[l], dt_bias[l], o_norm_g[l], B, S)
    rows_all = B * S + N
    routed_p = _mix_ln(oa_p.reshape(B * S, A_WIDTH), ob_p.reshape(B * S, B_WIDTH), xp, wo_b, g1, b1, wr, br,
                       512, rows_all, 0)

    xs = x_sample.reshape(N, D)
    qkv_s, ub_s, zb_s, ab_s = _proj(xs, w_pad32, N)
    nb = 2
    qkv_t = jnp.transpose(qkv_s.reshape(N, 3, A_HEADS, A_HEAD_DIM), (1, 2, 3, 0))
    oa_s = _attn_dec(qkv_t, jnp.transpose(cache_a_k[l], (0, 2, 3, 1)), jnp.transpose(cache_a_v[l], (0, 2, 3, 1)),
                     _cache_bias(rel_bias, cache_a_k.shape[2]), nb)
    oa_s = jnp.transpose(oa_s, (0, 3, 1, 2)).reshape(N, A_WIDTH)
    ob_s, st_s = _delta_dec(ub_s, state_b_conv[l], zb_s, ab_s, conv_w[l], a_log[l], dt_bias[l], o_norm_g[l],
                            state_b_ssm[l], 16)
    h_all, route, cnt = _mix_ln(oa_s, ob_s, xs, w_out[l], g1, b1, wr, br, N, rows_all, B * S, prev=routed_p)

    base, blk_e, blk_n, blk_x, pad_end, n_blocks = _slot_layout(cnt)
    dest = _slots(route, base)[:, 0:TOP_K].reshape(-1)
    xs = _dispatch(h_all, dest, pad_end, n_blocks * MOE_BLK)
    ys = _moe(xs, blk_e, blk_n, blk_x, w_gate[l], w_up[l], w_down[l])
    y_p = _final_ln(h_all, route, dest, ys, g2, b2, 0, B * S)
    y_s = _final_ln(h_all, route, dest, ys, g2, b2, B * S, N)

    w_kv_t = jnp.transpose(w_in[l][:, A_WIDTH:3 * A_WIDTH]).astype(BF16)
    k_win, v_win = _kv_win(x_prompt, w_kv_t, win_p, 512)
    to_rows = lambda t: jnp.transpose(t, (0, 3, 1, 2))[None]
    conv_p = ub_p.reshape(B, S, CONV_DIM)[:, S - (CONV_WIDTH - 1):]
    conv_s = jnp.concatenate([state_b_conv[l], ub_s[:, None, :]], axis=1)[:, T:]
    new_kv = lambda t: jnp.transpose(t, (2, 0, 1))[None, :, None]
    return (y_p.reshape(B, S, D), y_s.reshape(N, T, D), to_rows(k_win), to_rows(v_win),
            new_kv(qkv_t[1]), new_kv(qkv_t[2]), st_p[None], st_s[None], conv_p[None], conv_s[None])
```

```python
import functools
import math

import jax
import jax.numpy as jnp
from jax import lax
from jax.experimental import pallas as pl
from jax.experimental.pallas import tpu as pltpu

F32 = jnp.float32
BF16 = jnp.bfloat16
HIGHEST = lax.Precision.HIGHEST

LANES = 128
A_HEADS = 8
A_HEAD_DIM = 64
A_WIDTH = A_HEADS * A_HEAD_DIM
BRANCHES = ((128, 1), (512, 4), (2048, 16))
BAND = 128
ATT_TILE = BAND * 16
REL_BUCKETS = 32
REL_MAX_DIST = 2048
B_HEADS = 4
B_HEAD_DIM = 128
B_WIDTH = B_HEADS * B_HEAD_DIM
CONV_WIDTH = 4
CONV_DIM = 3 * B_WIDTH
CHUNK = 64
COL_UB = 3 * A_WIDTH
COL_ZB = COL_UB + CONV_DIM
COL_AB = COL_ZB + B_WIDTH
IN_COLS = COL_AB + 2 * B_HEADS
IN_COLS_PAD = COL_AB + LANES
N_GROUPS = 4
EXPERTS_PER_GROUP = 8
N_EXPERTS = N_GROUPS * EXPERTS_PER_GROUP
TOP_K = 2
DN_ALPHA = 2.0 ** 0.25
LN_EPS = 1e-5
RMS_EPS = 1e-6
MASKED = -1e30
MOE_BLK = 512
ROUTE_TILE = 128
DELTA_TILE = 1024
ATT_GROUP = 8
VMEM_LIMIT = 56 * 1024 * 1024


def _bdot(a, b):
    return jnp.dot(a.astype(BF16), b.astype(BF16), preferred_element_type=F32)


def _bdot_nt(a, b):
    return lax.dot_general(a.astype(BF16), b.astype(BF16), (((1,), (1,)), ((), ())), preferred_element_type=F32)


def _bdot_tn(a, b):
    return lax.dot_general(a.astype(BF16), b.astype(BF16), (((0,), (0,)), ((), ())), preferred_element_type=F32)


def _fdot(a, b):
    return jnp.dot(a, b, precision=HIGHEST, preferred_element_type=F32)


def _sigmoid(x):
    return 1.0 / (1.0 + jnp.exp(-x))


def _silu(x):
    return x * _sigmoid(x)


def _softplus(x):
    return jnp.maximum(x, 0.0) + jnp.log(1.0 + jnp.exp(-jnp.abs(x)))


def _params(sem):
    return pltpu.CompilerParams(dimension_semantics=sem, vmem_limit_bytes=VMEM_LIMIT)


def _proj_kernel(x_ref, w_ref, qkv_ref, ub_ref, zb_ref, ab_ref, *win_refs, tiles_per_seq=0, first_win_tile=0):
    if w_ref.dtype == BF16:
        xb = x_ref[...].astype(BF16)
        dot = lambda w: jnp.dot(xb, w, preferred_element_type=F32)
    else:
        dot = lambda w: _fdot(x_ref[...], w)
    qkv = dot(w_ref[:, 0:COL_UB])
    qkv_ref[...] = qkv
    ub_ref[...] = dot(w_ref[:, COL_UB:COL_ZB])
    zb_ref[...] = dot(w_ref[:, COL_ZB:COL_AB])
    ab_ref[...] = dot(w_ref[:, COL_AB:IN_COLS_PAD])
    if win_refs:
        @pl.when(pl.program_id(0) % tiles_per_seq >= first_win_tile)
        def _():
            tm = qkv.shape[0]
            for j, ref in enumerate(win_refs):
                cols = qkv[:, (1 + j) * A_WIDTH:(2 + j) * A_WIDTH]
                ref[0] = jnp.transpose(cols).reshape(A_HEADS, A_HEAD_DIM, tm)


def _proj(x2d, w_pad, tm, seq=None, win=None):
    T, D = x2d.shape
    row = lambda i: (i, 0)
    out_specs = [pl.BlockSpec((tm, COL_UB), row), pl.BlockSpec((tm, CONV_DIM), row),
                 pl.BlockSpec((tm, B_WIDTH), row), pl.BlockSpec((tm, LANES), row)]
    out_shape = [jax.ShapeDtypeStruct((T, COL_UB), F32), jax.ShapeDtypeStruct((T, CONV_DIM), F32),
                 jax.ShapeDtypeStruct((T, B_WIDTH), F32), jax.ShapeDtypeStruct((T, LANES), F32)]
    body = _proj_kernel
    if win:
        per_seq, first = seq // tm, (seq - win) // tm
        wspec = pl.BlockSpec((1, A_HEADS, A_HEAD_DIM, tm),
                             lambda i: (i // per_seq, 0, 0, jnp.maximum(i % per_seq - first, 0)))
        out_specs += [wspec, wspec]
        out_shape += [jax.ShapeDtypeStruct((T // seq, A_HEADS, A_HEAD_DIM, win), F32)] * 2
        body = functools.partial(_proj_kernel, tiles_per_seq=per_seq, first_win_tile=first)
    return pl.pallas_call(
        body,
        grid=(T // tm,),
        in_specs=[pl.BlockSpec((tm, D), row), pl.BlockSpec((D, IN_COLS_PAD), lambda i: (0, 0))],
        out_specs=out_specs,
        out_shape=out_shape,
        compiler_params=_params(("arbitrary",)),
        name="proj",
    )(x2d, w_pad)


def _rel_bucket(dist):
    max_exact = REL_BUCKETS // 2
    n = jnp.maximum(dist, 0)
    ratio = jnp.maximum(n, 1).astype(F32) / max_exact
    large = max_exact + (jnp.log(ratio) / math.log(REL_MAX_DIST / max_exact)
                         * (REL_BUCKETS - max_exact)).astype(jnp.int32)
    return jnp.where(n < max_exact, n, jnp.minimum(large, REL_BUCKETS - 1))


def _bias_of(rel_bias, dist):
    onehot = (_rel_bucket(dist)[None, :] == jnp.arange(REL_BUCKETS)[:, None]).astype(F32)
    return jnp.dot(rel_bias.astype(F32).T, onehot, precision=HIGHEST)


def _band_bias(rel_bias):
    period = 3 * BAND
    tabs = []
    for _, dil in BRANCHES:
        g = jnp.concatenate([_bias_of(rel_bias, (BAND - jnp.arange(BAND + 1)) * dil),
                             jnp.full((A_HEADS, period - BAND - 1), MASKED, F32)], axis=1)
        skew = jnp.tile(g, (1, BAND))[:, :BAND * (period - 1)].reshape(A_HEADS, BAND, period - 1)
        tabs.append(skew[:, :, :2 * BAND])
    return jnp.stack(tabs)


def _cache_bias(rel_bias, P):
    dist = P - jnp.arange(P + LANES)
    tabs = []
    for window, dil in BRANCHES:
        ok = (dist >= 0) & (dist <= window) & (dist % dil == 0)
        tabs.append(jnp.where(ok[None, :], _bias_of(rel_bias, dist), MASKED)[:, None, :])
    return jnp.stack(tabs)


def _attn_kernel(q_ref, k_ref, v_ref, bias_ref, o_ref, acc_ref, m_ref, l_ref):
    t = pl.program_id(2)
    tile0 = t * ATT_TILE
    lane = lax.broadcasted_iota(jnp.int32, (BAND, LANES), 1)
    head0 = lane < A_HEAD_DIM

    def rows(start, dil):
        return pl.ds(pl.multiple_of(start, BAND), BAND) if dil == 1 else pl.ds(start, BAND, stride=dil)

    for br, (_, dil) in enumerate(BRANCHES):
        span = BAND * dil

        def blocks(it, carry, br=br, dil=dil, span=span):
            gs = range(ATT_GROUP)
            idx = [it * ATT_GROUP + g for g in gs]
            start = [(i % dil) + (i // dil) * span for i in idx]
            cur = [tile0 + s for s in start]
            first = [c < span for c in cur]
            prev = [jnp.where(first[g], cur[g], cur[g] - span) for g in gs]
            q = [q_ref[0, rows(start[g], dil), :] * (A_HEAD_DIM ** -0.5) for g in gs]
            kk = [jnp.concatenate([k_ref[0, rows(prev[g], dil), :], k_ref[0, rows(cur[g], dil), :]],
                                  axis=0).astype(BF16) for g in gs]
            vv = [jnp.concatenate([v_ref[0, rows(prev[g], dil), :], v_ref[0, rows(cur[g], dil), :]],
                                  axis=0).astype(BF16) for g in gs]
            pen = [jnp.where(first[g], MASKED, 0.0) for g in gs]
            gh = [(g, hh) for g in gs for hh in range(2)]
            qh = [jnp.where(head0 if hh == 0 else ~head0, q[g], 0.0).astype(BF16) for g, hh in gh]
            s = [lax.dot_general(qh[j], kk[g], (((1,), (1,)), ((), ())), preferred_element_type=F32)
                 for j, (g, hh) in enumerate(gh)]
            s = [s[j] + bias_ref[br, hh] for j, (g, hh) in enumerate(gh)]
            s = [jnp.concatenate([s[j][:, 0:BAND] + pen[g], s[j][:, BAND:2 * BAND]], axis=1)
                 for j, (g, hh) in enumerate(gh)]
            m = [jnp.max(t, axis=1, keepdims=True) for t in s]
            p = [jnp.exp(s[j] - m[j]) for j in range(len(gh))]
            l = [jnp.sum(t, axis=1, keepdims=True) for t in p]
            pv = [jnp.dot(p[j].astype(BF16), vv[g], preferred_element_type=F32) for j, (g, hh) in enumerate(gh)]
            for g in gs:
                acc_ref[br, rows(start[g], dil), :] = jnp.where(head0, pv[2 * g], pv[2 * g + 1])
                m_ref[br, rows(start[g], dil), :] = jnp.where(head0, m[2 * g], m[2 * g + 1])
                l_ref[br, rows(start[g], dil), :] = jnp.where(head0, l[2 * g], l[2 * g + 1])
            return carry

        lax.fori_loop(0, ATT_TILE // BAND // ATT_GROUP, blocks, 0)

    def merge(c, carry):
        r = pl.ds(pl.multiple_of(c * 256, 256), 256)
        m0, m1, m2 = m_ref[0, r, :], m_ref[1, r, :], m_ref[2, r, :]
        mx = jnp.maximum(jnp.maximum(m0, m1), m2)
        w0, w1, w2 = jnp.exp(m0 - mx), jnp.exp(m1 - mx), jnp.exp(m2 - mx)
        num = w0 * acc_ref[0, r, :] + w1 * acc_ref[1, r, :] + w2 * acc_ref[2, r, :]
        den = w0 * l_ref[0, r, :] + w1 * l_ref[1, r, :] + w2 * l_ref[2, r, :]
        o_ref[0, r, :] = num / den
        return carry

    lax.fori_loop(0, ATT_TILE // 256, merge, 0)


def _attn(qkv, bias, B, S):
    n_pairs = A_HEADS // 2
    qkv3 = qkv.reshape(B, S, 3 * A_WIDTH)
    return pl.pallas_call(
        _attn_kernel,
        grid=(B, n_pairs, S // ATT_TILE),
        in_specs=[pl.BlockSpec((1, ATT_TILE, LANES), lambda b, hp, t: (b, t, hp)),
                  pl.BlockSpec((1, S, LANES), lambda b, hp, t: (b, 0, n_pairs + hp)),
                  pl.BlockSpec((1, S, LANES), lambda b, hp, t: (b, 0, 2 * n_pairs + hp)),
                  pl.BlockSpec((3, 2, BAND, 2 * BAND), lambda b, hp, t: (0, hp, 0, 0))],
        out_specs=pl.BlockSpec((1, ATT_TILE, LANES), lambda b, hp, t: (b, t, hp)),
        out_shape=jax.ShapeDtypeStruct((B, S, A_WIDTH), F32),
        scratch_shapes=[pltpu.VMEM((3, ATT_TILE, LANES), F32)] * 3,
        compiler_params=_params(("parallel", "parallel", "arbitrary")),
        name="attn",
    )(qkv3, qkv3, qkv3, bias)


def _attn_dec_kernel(qkv_ref, kt_ref, vt_ref, bias_ref, o_ref, *, nb):
    i = pl.program_id(0)
    N = qkv_ref.shape[-1]
    P = kt_ref.shape[-1]
    lane_n = lax.broadcasted_iota(jnp.int32, (A_HEAD_DIM, N), 1)
    lane_o = lax.broadcasted_iota(jnp.int32, (A_HEAD_DIM, nb), 1)
    lane_t = lax.broadcasted_iota(jnp.int32, (1, LANES), 1)

    def head(h, carry):
        slab = jnp.zeros((A_HEAD_DIM, nb), F32)
        for j in range(nb):
            pick = lane_n == i * nb + j
            col = lambda t: jnp.sum(jnp.where(pick, t, 0.0), axis=1, keepdims=True)
            q = col(qkv_ref[0, h]) * (A_HEAD_DIM ** -0.5)
            k_new, v_new = col(qkv_ref[1, h]), col(qkv_ref[2, h])
            s_new = jnp.sum(q * k_new, axis=0, keepdims=True)
            s = jnp.concatenate([jnp.sum(kt_ref[j, h] * q, axis=0, keepdims=True),
                                 jnp.where(lane_t == 0, s_new, 0.0)], axis=1)
            ps, ms, ls = [], [], []
            for br in range(3):
                sb = s + bias_ref[br, h]
                m = jnp.max(sb, axis=1, keepdims=True)
                p = jnp.exp(sb - m)
                ps.append(p)
                ms.append(m)
                ls.append(jnp.sum(p, axis=1, keepdims=True))
            mx = jnp.maximum(jnp.maximum(ms[0], ms[1]), ms[2])
            w = jnp.zeros((1, P + LANES), F32)
            den = jnp.zeros((1, 1), F32)
            for p, m, l in zip(ps, ms, ls):
                e = jnp.exp(m - mx)
                w = w + e * p
                den = den + e * l
            o = jnp.sum(vt_ref[j, h] * w[:, 0:P], axis=1, keepdims=True) + v_new * w[:, P:P + 1]
            slab = jnp.where(lane_o == j, o / den, slab)
        o_ref[0, h] = slab
        return carry

    lax.fori_loop(0, A_HEADS, head, 0, unroll=2)


def _attn_dec(qkv_t, cache_kt, cache_vt, bias, nb):
    N, P = cache_kt.shape[0], cache_kt.shape[-1]
    cache_spec = pl.BlockSpec((nb, A_HEADS, A_HEAD_DIM, P), lambda i: (i, 0, 0, 0))
    return pl.pallas_call(
        functools.partial(_attn_dec_kernel, nb=nb),
        grid=(N // nb,),
        in_specs=[pl.BlockSpec((3, A_HEADS, A_HEAD_DIM, N), lambda i: (0, 0, 0, 0)), cache_spec, cache_spec,
                  pl.BlockSpec((3, A_HEADS, 1, P + LANES), lambda i: (0, 0, 0, 0))],
        out_specs=pl.BlockSpec((1, A_HEADS, A_HEAD_DIM, nb), lambda i: (i, 0, 0, 0)),
        out_shape=jax.ShapeDtypeStruct((N // nb, A_HEADS, A_HEAD_DIM, nb), F32),
        compiler_params=_params(("parallel",)),
        name="attn_dec",
    )(qkv_t, cache_kt, cache_vt, bias)


def _split3(x):
    hi = x.astype(BF16)
    r = x - hi.astype(F32)
    mid = r.astype(BF16)
    return hi, mid, (r - mid.astype(F32)).astype(BF16)


def _tril_dot(tril_b, g):
    return sum(jnp.dot(tril_b, piece, preferred_element_type=F32) for piece in _split3(g))


def _dot3(a, b):
    ah, bh = a.astype(BF16), b.astype(BF16)
    al, bl = (a - ah.astype(F32)).astype(BF16), (b - bh.astype(F32)).astype(BF16)
    d = lambda x, y: jnp.dot(x, y, preferred_element_type=F32)
    return d(ah, bh) + (d(ah, bl) + d(al, bh))


def _delta_kernel(alog_ref, dtb_ref, ub_ref, z_ref, ab_ref, cw_ref, og_ref, o_ref, st_ref,
                  pad_ref, u_s, wq_s, kt_s, qk_s, gl_s, a_s, rhs_s):
    t = pl.program_id(1)
    TS = ub_ref.shape[1]
    nch = TS // CHUNK
    hdr = 8

    slabs = range(CONV_DIM // LANES)
    seq_rows = lambda first, n: pl.ds(2 * first, n, stride=2)

    @pl.when(t == 0)
    def _():
        for s in slabs:
            pad_ref[s, seq_rows(0, hdr), :] = jnp.zeros((hdr, LANES), F32)
        st_ref[...] = jnp.zeros(st_ref.shape, F32)

    @pl.when(t > 0)
    def _():
        for s in slabs:
            pad_ref[s, seq_rows(0, hdr), :] = pad_ref[s, seq_rows(TS, hdr), :]

    for s in slabs:
        pad_ref[s, seq_rows(hdr, TS), :] = ub_ref[0, :, s * LANES:(s + 1) * LANES]

    ri = lax.broadcasted_iota(jnp.int32, (CHUNK, CHUNK), 0)
    ci = lax.broadcasted_iota(jnp.int32, (CHUNK, CHUNK), 1)
    incl = ri >= ci
    strict = ri > ci
    tril_b = incl.astype(BF16)
    eye = (ri == ci).astype(F32)
    lane = lax.broadcasted_iota(jnp.int32, (CHUNK, LANES), 1)

    def local(c, carry):
        base = c * CHUNK if isinstance(c, int) else pl.multiple_of(c * CHUNK, CHUNK)
        ab = ab_ref[0, pl.ds(base, CHUNK), :]
        heads = []
        for h in range(B_HEADS):
            def conv(col):
                first = c * CHUNK + (hdr - (CONV_WIDTH - 1))
                acc = pad_ref[col // LANES, seq_rows(first, CHUNK), :] * cw_ref[0:1, col:col + LANES]
                for i in range(1, CONV_WIDTH):
                    acc = acc + pad_ref[col // LANES, seq_rows(first + i, CHUNK), :] * cw_ref[i:i + 1, col:col + LANES]
                return _silu(acc)

            cq, ck, v = conv(h * LANES), conv(B_WIDTH + h * LANES), conv(2 * B_WIDTH + h * LANES)
            q = cq * lax.rsqrt(jnp.sum(cq * cq, axis=1, keepdims=True) + 1e-6) * (B_HEAD_DIM ** -0.5)
            k = ck * lax.rsqrt(jnp.sum(ck * ck, axis=1, keepdims=True) + 1e-6)
            a_raw = jnp.sum(jnp.where(lane == h, ab, 0.0), axis=1, keepdims=True)
            b_raw = jnp.sum(jnp.where(lane == h + B_HEADS, ab, 0.0), axis=1, keepdims=True)
            neg_a = -jnp.exp(jnp.full((1, LANES), alog_ref[h], F32))
            g = neg_a * _softplus(a_raw + dtb_ref[h])
            beta = _sigmoid(b_raw)
            heads.append((q, k, v, g, beta))
        hs = range(B_HEADS)
        q, k, v, g, beta = zip(*heads)
        gc = [_tril_dot(tril_b, g[h]) for h in hs]
        dmat = [_tril_dot(tril_b, jnp.where(strict, g[h][:, 0:CHUNK], 0.0)) for h in hs]
        kq = [_bdot_nt(jnp.concatenate([k[h], q[h]], axis=0), k[h]) for h in hs]
        decay = [jnp.where(incl, jnp.exp(dmat[h]), 0.0) for h in hs]
        e_gc = [jnp.exp(gc[h]) for h in hs]
        for h in hs:
            gc_last = gc[h][CHUNK - 1:CHUNK, :]
            a_s[h, c] = jnp.where(strict, beta[h] * kq[h][0:CHUNK] * decay[h], 0.0)
            rhs_s[h, c] = jnp.concatenate([v[h] * beta[h], k[h] * (beta[h] * e_gc[h])], axis=1)
            wq_s[h, c, CHUNK:2 * CHUNK] = q[h] * e_gc[h]
            kt_s[h, c] = k[h] * jnp.exp(gc_last - gc[h])
            qk_s[h, c] = kq[h][CHUNK:2 * CHUNK] * decay[h]
            gl_s[h, c] = jnp.broadcast_to(jnp.exp(gc_last), (8, LANES))
        return carry

    def solve(c, carry):
        hs = range(B_HEADS)
        a = [a_s[h, c] for h in hs]
        x = [eye - a[h] for h in hs]
        p = [_bdot(a[h], a[h]) for h in hs]
        for _ in range(int(math.log2(CHUNK)) - 2):
            r = [_bdot(jnp.concatenate([x[h], p[h]], axis=0), p[h]) for h in hs]
            x = [x[h] + r[h][0:CHUNK] for h in hs]
            p = [r[h][CHUNK:2 * CHUNK] for h in hs]
        x = [x[h] + _bdot(x[h], p[h]) for h in hs]
        sol = [_bdot(x[h], rhs_s[h, c]) for h in hs]
        for h in hs:
            u_s[h, c] = sol[h][:, 0:LANES]
            wq_s[h, c, 0:CHUNK] = sol[h][:, LANES:2 * LANES]
        return carry

    def scan(c, carry):
        base = c * CHUNK if isinstance(c, int) else pl.multiple_of(c * CHUNK, CHUNK)
        hs = range(B_HEADS)
        state = [st_ref[0, h] for h in hs]
        r = [_bdot(wq_s[h, c], state[h]) for h in hs]
        v_new = [u_s[h, c] - r[h][0:CHUNK] for h in hs]
        upd = [_bdot_tn(kt_s[h, c], v_new[h]) for h in hs]
        out = [r[h][CHUNK:2 * CHUNK] + _bdot(qk_s[h, c], v_new[h]) for h in hs]
        for h in hs:
            st_ref[0, h] = state[h] * gl_s[h, c][0:1, :] + upd[h]
            o = out[h] * lax.rsqrt(jnp.mean(out[h] * out[h], axis=1, keepdims=True) + RMS_EPS) * og_ref[...]
            cols = slice(h * LANES, (h + 1) * LANES)
            o_ref[0, pl.ds(base, CHUNK), cols] = o * _silu(z_ref[0, pl.ds(base, CHUNK), cols])
        return carry

    def stages(c, carry):
        scan(c - 1, carry)
        solve(c, carry)
        return local(c + 1, carry)

    assert nch >= 3
    local(0, 0)
    solve(0, 0)
    local(1, 0)
    lax.fori_loop(1, nch - 1, stages, 0)
    scan(nch - 2, 0)
    solve(nch - 1, 0)
    scan(nch - 1, 0)


def _delta(ub, zb, ab, conv_w, a_log, dt_bias, o_norm_g, B, S):
    ts = min(S, DELTA_TILE)
    nch = ts // CHUNK
    seq = lambda width: pl.BlockSpec((1, ts, width), lambda b, t, *_: (b, t, 0))
    fix = lambda shape: pl.BlockSpec(shape, lambda b, t, *_: (0, 0))
    per_chunk = lambda rows, width: pltpu.VMEM((B_HEADS, nch, rows, width), F32)
    grid_spec = pltpu.PrefetchScalarGridSpec(
        num_scalar_prefetch=2,
        grid=(B, S // ts),
        in_specs=[seq(CONV_DIM), seq(B_WIDTH), seq(LANES), fix((CONV_WIDTH, CONV_DIM)), fix((1, LANES))],
        out_specs=[seq(B_WIDTH),
                   pl.BlockSpec((1, B_HEADS, B_HEAD_DIM, B_HEAD_DIM), lambda b, t, *_: (b, 0, 0, 0))],
        scratch_shapes=[pltpu.VMEM((CONV_DIM // LANES, 2 * (ts + 8), LANES), F32),
                        per_chunk(CHUNK, LANES), per_chunk(2 * CHUNK, LANES),
                        per_chunk(CHUNK, LANES), per_chunk(CHUNK, CHUNK), per_chunk(8, LANES),
                        per_chunk(CHUNK, CHUNK), per_chunk(CHUNK, 2 * LANES)],
    )
    return pl.pallas_call(
        _delta_kernel,
        grid_spec=grid_spec,
        out_shape=[jax.ShapeDtypeStruct((B, S, B_WIDTH), F32),
                   jax.ShapeDtypeStruct((B, B_HEADS, B_HEAD_DIM, B_HEAD_DIM), F32)],
        compiler_params=_params(("parallel", "arbitrary")),
        name="delta",
    )(a_log, dt_bias, ub.reshape(B, S, CONV_DIM), zb.reshape(B, S, B_WIDTH), ab.reshape(B, S, LANES),
      conv_w, o_norm_g.reshape(1, LANES))


def _delta_dec_kernel(alog_ref, dtb_ref, ub_ref, cs_ref, zb_ref, ab_ref, w_ref, og_ref, st_ref,
                      o_ref, so_ref, *, nb):
    i = pl.program_id(0)
    N = ub_ref.shape[0]
    acc = ub_ref[...] * w_ref[CONV_WIDTH - 1:CONV_WIDTH, :]
    for t in range(CONV_WIDTH - 1):
        acc = acc + cs_ref[t] * w_ref[t:t + 1, :]
    c = _silu(acc)
    ab = ab_ref[...]
    lane = lax.broadcasted_iota(jnp.int32, (N, LANES), 1)
    samp = lax.broadcasted_iota(jnp.int32, (B_HEAD_DIM, N), 1)
    row_id = lax.broadcasted_iota(jnp.int32, (N, LANES), 0)
    out_row = lax.broadcasted_iota(jnp.int32, (nb, LANES), 0)
    heads = []
    for h in range(B_HEADS):
        cq = c[:, h * LANES:(h + 1) * LANES]
        ck = c[:, B_WIDTH + h * LANES:B_WIDTH + (h + 1) * LANES]
        v = c[:, 2 * B_WIDTH + h * LANES:2 * B_WIDTH + (h + 1) * LANES]
        q = cq * lax.rsqrt(jnp.sum(cq * cq, axis=1, keepdims=True) + 1e-6) * (B_HEAD_DIM ** -0.5)
        k = ck * lax.rsqrt(jnp.sum(ck * ck, axis=1, keepdims=True) + 1e-6)
        a_raw = jnp.sum(jnp.where(lane == h, ab, 0.0), axis=1, keepdims=True)
        b_raw = jnp.sum(jnp.where(lane == h + B_HEADS, ab, 0.0), axis=1, keepdims=True)
        neg_a = -jnp.exp(jnp.full((1, 1), alog_ref[h], F32))
        dec = jnp.exp(neg_a * _softplus(a_raw + dtb_ref[h]))
        beta = _sigmoid(b_raw)
        heads.append((q.T, k.T, v, jnp.broadcast_to(dec, (N, LANES)), jnp.broadcast_to(beta, (N, LANES)),
                      zb_ref[:, h * LANES:(h + 1) * LANES]))

    hs = range(B_HEADS)
    q_t, k_t, v, dec, beta, z = zip(*heads)

    def sample(j, o_acc):
        n = i * nb + j
        pick = samp == n
        k_col = [jnp.sum(jnp.where(pick, k_t[h], 0.0), axis=1, keepdims=True) for h in hs]
        q_col = [jnp.sum(jnp.where(pick, q_t[h], 0.0), axis=1, keepdims=True) for h in hs]
        pick_r = row_id == n
        row = lambda t: jnp.sum(jnp.where(pick_r, t, 0.0), axis=0, keepdims=True)
        st = [st_ref[j, h] * row(dec[h]) for h in hs]
        mem = [jnp.sum(k_col[h] * st[h], axis=0, keepdims=True) for h in hs]
        st = [st[h] + k_col[h] * ((row(v[h]) - mem[h]) * row(beta[h])) for h in hs]
        o = [jnp.sum(q_col[h] * st[h], axis=0, keepdims=True) for h in hs]
        o = [o[h] * lax.rsqrt(jnp.mean(o[h] * o[h], axis=1, keepdims=True) + RMS_EPS) * og_ref[...] for h in hs]
        for h in hs:
            so_ref[j, h] = st[h]
        return tuple(jnp.where(out_row == j, o[h] * _silu(row(z[h])), o_acc[h]) for h in hs)

    outs = lax.fori_loop(0, nb, sample, tuple(jnp.zeros((nb, LANES), F32) for _ in hs), unroll=2)
    for h in hs:
        o_ref[:, h * LANES:(h + 1) * LANES] = outs[h]


def _delta_dec(ub_s, conv_state, zb_s, ab_s, conv_w, a_log, dt_bias, o_norm_g, state, nb):
    N = ub_s.shape[0]
    full2 = lambda shape: pl.BlockSpec(shape, lambda i, *_: (0, 0))
    grid_spec = pltpu.PrefetchScalarGridSpec(
        num_scalar_prefetch=2,
        grid=(N // nb,),
        in_specs=[full2((N, CONV_DIM)),
                  pl.BlockSpec((CONV_WIDTH - 1, N, CONV_DIM), lambda i, *_: (0, 0, 0)),
                  full2((N, B_WIDTH)), full2((N, LANES)), full2((CONV_WIDTH, CONV_DIM)), full2((1, LANES)),
                  pl.BlockSpec((nb, B_HEADS, B_HEAD_DIM, B_HEAD_DIM), lambda i, *_: (i, 0, 0, 0))],
        out_specs=[pl.BlockSpec((nb, B_WIDTH), lambda i, *_: (i, 0)),
                   pl.BlockSpec((nb, B_HEADS, B_HEAD_DIM, B_HEAD_DIM), lambda i, *_: (i, 0, 0, 0))],
    )
    return pl.pallas_call(
        functools.partial(_delta_dec_kernel, nb=nb),
        grid_spec=grid_spec,
        out_shape=[jax.ShapeDtypeStruct((N, B_WIDTH), F32), jax.ShapeDtypeStruct(state.shape, F32)],
        compiler_params=_params(("parallel",)),
        name="delta_dec",
    )(a_log, dt_bias, ub_s, jnp.swapaxes(conv_state, 0, 1), zb_s, ab_s, conv_w, o_norm_g.reshape(1, LANES), state)


def _layer_norm(r, g, b):
    mu = jnp.mean(r, axis=1, keepdims=True)
    d = r - mu
    var = jnp.mean(d * d, axis=1, keepdims=True)
    return d * lax.rsqrt(var + LN_EPS) * g + b


def _mix_ln_kernel(*refs, steps):
    outs = refs[-3:]

    @pl.when(pl.program_id(0) < steps)
    def _():
        _mix_ln_rows(*refs[:8], *outs)

    @pl.when(pl.program_id(0) >= steps)
    def _():
        for o in outs:
            o[...] = jnp.zeros(o.shape, F32)


def _mix_ln_rows(oa_ref, ob_ref, x_ref, wo_ref, g_ref, b_ref, wr_ref, br_ref, h_ref, route_ref, cnt_ref):
    if wo_ref.dtype == BF16:
        dot = lambda a, w: jnp.dot(a.astype(BF16), w, preferred_element_type=F32)
    else:
        dot = _fdot
    y = dot(oa_ref[...], wo_ref[0:A_WIDTH, :]) + dot(ob_ref[...], wo_ref[A_WIDTH:A_WIDTH + B_WIDTH, :])
    hcur = _layer_norm(DN_ALPHA * x_ref[...] + y, g_ref[...], b_ref[...])
    h_ref[...] = hcur
    logits = (_bdot if wo_ref.dtype == BF16 else _dot3)(hcur, wr_ref[...]) + br_ref[...]
    lane = lax.broadcasted_iota(jnp.int32, logits.shape, 1)
    lane_f = lane.astype(F32)
    ninf = -jnp.inf
    big = 1e9
    gl = jnp.where(lane < N_GROUPS, logits, ninf)
    gmax = jnp.max(gl, axis=1, keepdims=True)
    g_idx = jnp.min(jnp.where(gl == gmax, lane_f, big), axis=1, keepdims=True)
    p_group = 1.0 / jnp.sum(jnp.exp(gl - gmax), axis=1, keepdims=True)
    grp_of_lane = ((lane - N_GROUPS) >> 3).astype(F32)
    sel = (lane >= N_GROUPS) & (lane < N_GROUPS + N_EXPERTS) & (grp_of_lane == g_idx)
    el = jnp.where(sel, logits, ninf)
    v1 = jnp.max(el, axis=1, keepdims=True)
    i1 = jnp.min(jnp.where(el == v1, lane_f, big), axis=1, keepdims=True)
    el2 = jnp.where(lane_f == i1, ninf, el)
    v2 = jnp.max(el2, axis=1, keepdims=True)
    i2 = jnp.min(jnp.where(el2 == v2, lane_f, big), axis=1, keepdims=True)
    t = jnp.exp(v2 - v1)
    gate1 = p_group / (1.0 + t)
    gate2 = p_group * t / (1.0 + t)
    e1, e2 = i1 - N_GROUPS, i2 - N_GROUPS
    route_ref[...] = jnp.where(lane == 0, gate1, jnp.where(lane == 1, gate2, jnp.where(
        lane == 2, e1, jnp.where(lane == 3, e2, 0.0))))
    chosen = ((lane_f == e1) | (lane_f == e2)).astype(F32)
    tm = chosen.shape[0]
    cnt_ref[...] = jnp.sum(chosen.reshape(tm // ROUTE_TILE, ROUTE_TILE, LANES), axis=1)[:, None, :]


def _mix_ln(oa, ob, x2d, wo_b, g, b, wr, br, tm, total, row0, prev=()):
    T, D = x2d.shape
    off = row0 // tm
    steps = T // tm
    tail = 1 if row0 + T < total else 0
    assert total - (row0 + T) <= tm
    row = lambda i: (jnp.minimum(i, steps - 1), 0)
    out_row = lambda i: (off + i, 0)
    fix = lambda i: (0, 0)
    sub = tm // ROUTE_TILE
    return pl.pallas_call(
        functools.partial(_mix_ln_kernel, steps=steps),
        grid=(steps + tail,),
        in_specs=[pl.BlockSpec((tm, A_WIDTH), row), pl.BlockSpec((tm, B_WIDTH), row), pl.BlockSpec((tm, D), row),
                  pl.BlockSpec((A_WIDTH + B_WIDTH, D), fix), pl.BlockSpec((1, D), fix), pl.BlockSpec((1, D), fix),
                  pl.BlockSpec((D, LANES), fix), pl.BlockSpec((1, LANES), fix)]
                 + [pl.BlockSpec(memory_space=pl.ANY)] * len(prev),
        out_specs=[pl.BlockSpec((tm, D), out_row), pl.BlockSpec((tm, LANES), out_row),
                   pl.BlockSpec((sub, 1, LANES), lambda i: (off + i, 0, 0))],
        out_shape=[jax.ShapeDtypeStruct((total, D), F32), jax.ShapeDtypeStruct((total, LANES), F32),
                   jax.ShapeDtypeStruct((total // ROUTE_TILE, 1, LANES), F32)],
        input_output_aliases={8 + j: j for j in range(len(prev))},
        compiler_params=_params(("parallel",)),
        name="mix_ln",
    )(oa, ob, x2d, wo_b, g, b, wr, br, *prev)


def _slot_layout(counts):
    tiles = counts.shape[0]
    n_assign = tiles * ROUTE_TILE * TOP_K
    n_blocks = -(-(n_assign + N_EXPERTS * (MOE_BLK - 1)) // MOE_BLK)
    per_tile = counts.reshape(tiles, LANES)
    earlier = (jnp.arange(tiles)[:, None] > jnp.arange(tiles)[None, :]).astype(F32)
    before = jnp.dot(earlier, per_tile, precision=HIGHEST)
    total = jnp.sum(per_tile, axis=0)[:N_EXPERTS]
    padded = jnp.ceil(total / MOE_BLK) * MOE_BLK
    upto = (jnp.arange(N_EXPERTS)[:, None] <= jnp.arange(N_EXPERTS)[None, :]).astype(F32)
    pad_end = jnp.dot(padded, upto, precision=HIGHEST)
    pad_start = pad_end - padded
    base = (before + jnp.pad(pad_start, (0, LANES - N_EXPERTS))[None, :]).reshape(tiles, 1, LANES)
    blk_start = (jnp.arange(n_blocks) * MOE_BLK).astype(F32)
    in_e = ((pad_start[None, :] <= blk_start[:, None]) & (blk_start[:, None] < pad_end[None, :])).astype(F32)
    used = blk_start < pad_end[-1]
    n_used = jnp.sum(used.astype(jnp.int32))
    last_e = jnp.max(jnp.where(padded > 0, jnp.arange(N_EXPERTS), 0)).astype(F32)
    blk_e = jnp.where(used, jnp.dot(in_e, jnp.arange(N_EXPERTS, dtype=F32), precision=HIGHEST), last_e)
    blk_n = jnp.clip(jnp.dot(in_e, pad_start + total, precision=HIGHEST) - blk_start, 0, MOE_BLK)
    blk_x = jnp.minimum(jnp.arange(n_blocks), jnp.maximum(n_used - 1, 0))
    i32 = lambda t: t.astype(jnp.int32)
    return base, i32(blk_e), i32(blk_n), i32(blk_x), i32(pad_end), n_blocks


def _slot_kernel(route_ref, base_ref, dest_ref):
    lane = lax.broadcasted_iota(jnp.int32, (ROUTE_TILE, LANES), 1)
    lane_f = lane.astype(F32)
    ri = lax.broadcasted_iota(jnp.int32, (ROUTE_TILE, ROUTE_TILE), 0)
    ci = lax.broadcasted_iota(jnp.int32, (ROUTE_TILE, ROUTE_TILE), 1)
    before = (ri > ci).astype(BF16)
    for t in range(base_ref.shape[0]):
        rows = slice(t * ROUTE_TILE, (t + 1) * ROUTE_TILE)
        route = route_ref[rows, :]
        e1 = jnp.sum(jnp.where(lane == 2, route, 0.0), axis=1, keepdims=True)
        e2 = jnp.sum(jnp.where(lane == 3, route, 0.0), axis=1, keepdims=True)
        oh1, oh2 = lane_f == e1, lane_f == e2
        earlier = _bdot(before, (oh1 | oh2).astype(F32))
        slot = base_ref[t] + earlier
        d1 = jnp.sum(jnp.where(oh1, slot, 0.0), axis=1, keepdims=True)
        d2 = jnp.sum(jnp.where(oh2, slot, 0.0), axis=1, keepdims=True)
        dest_ref[rows, :] = jnp.where(lane == 0, d1, jnp.where(lane == 1, d2, 0.0)).astype(jnp.int32)


def _slots(route, base):
    tiles = route.shape[0] // ROUTE_TILE
    sub = max(d for d in range(1, 9) if tiles % d == 0)
    return pl.pallas_call(
        _slot_kernel,
        grid=(tiles // sub,),
        in_specs=[pl.BlockSpec((sub * ROUTE_TILE, LANES), lambda i: (i, 0)),
                  pl.BlockSpec((sub, 1, LANES), lambda i: (i, 0, 0))],
        out_specs=pl.BlockSpec((sub * ROUTE_TILE, LANES), lambda i: (i, 0)),
        out_shape=jax.ShapeDtypeStruct(route.shape, jnp.int32),
        compiler_params=_params(("parallel",)),
        name="slots",
    )(route, base)


def _pack_bf16_pairs(x):
    half = x.shape[1] // 2
    bits = pltpu.bitcast(x.astype(BF16).astype(F32), jnp.uint32)
    return (bits[:, 0:half] >> 16) | bits[:, half:2 * half]


def _unpack_bf16_pairs(w):
    lo = pltpu.bitcast(w << 16, F32).astype(BF16)
    hi = pltpu.bitcast(w & jnp.uint32(0xFFFF0000), F32).astype(BF16)
    return jnp.concatenate([lo, hi], axis=1)


def _dispatch_kernel(dest_ref, pad_end_ref, h_ref, xs_hbm, hbuf, zbuf, sem, zsem):
    i = pl.program_id(0)
    slot = i % 2
    tile = h_ref.shape[0]
    a0 = i * (tile * TOP_K)

    @pl.when(i == 0)
    def _():
        zbuf[...] = jnp.zeros(zbuf.shape, zbuf.dtype)
        fills =[pltpu.make_async_copy(zbuf, xs_hbm.at[pl.ds(pl.multiple_of(pad_end_ref[e] - MOE_BLK, MOE_BLK), MOE_BLK)],
                                       zsem) for e in range(N_EXPERTS)]
        has_rows = [pad_end_ref[e] > (pad_end_ref[e - 1] if e else 0) for e in range(N_EXPERTS)]
        for e in range(N_EXPERTS):
            @pl.when(has_rows[e])
            def _(e=e):
                fills[e].start()
        for e in range(N_EXPERTS):
            @pl.when(has_rows[e])
            def _(e=e):
                fills[e].wait()

        def tail(b):
            return pltpu.make_async_copy(zbuf, xs_hbm.at[pl.ds(pl.multiple_of(b * MOE_BLK, MOE_BLK), MOE_BLK)], zsem)

        def tail_start(b, c):
            tail(b).start()
            return c

        def tail_wait(b, c):
            tail(b).wait()
            return c

        first_unused = pad_end_ref[N_EXPERTS - 1] // MOE_BLK
        lax.fori_loop(first_unused, xs_hbm.shape[0] // MOE_BLK, tail_start, 0)
        lax.fori_loop(first_unused, xs_hbm.shape[0] // MOE_BLK, tail_wait, 0)

    hbuf[slot] = _pack_bf16_pairs(h_ref[...])

    def start(r, c):
        for k in range(TOP_K):
            pltpu.make_async_copy(hbuf.at[slot, pl.ds(r, 1)], xs_hbm.at[pl.ds(dest_ref[a0 + r * TOP_K + k], 1)],
                                  sem.at[slot]).start()
        return c

    lax.fori_loop(0, tile, start, 0, unroll=8)

    def drain(s):
        for k in range(TOP_K):
            pltpu.make_async_copy(hbuf.at[s], xs_hbm.at[pl.ds(0, tile)], sem.at[s]).wait()

    @pl.when(i > 0)
    def _():
        drain(1 - slot)

    @pl.when(i == pl.num_programs(0) - 1)
    def _():
        drain(slot)


def _dispatch(h_all, dest, pad_end, slots):
    T, D = h_all.shape
    tile = max(t for t in range(ROUTE_TILE, 4 * ROUTE_TILE + 1, 8) if T % t == 0)
    grid_spec = pltpu.PrefetchScalarGridSpec(
        num_scalar_prefetch=2,
        grid=(T // tile,),
        in_specs=[pl.BlockSpec((tile, D), lambda i, *_: (i, 0))],
        out_specs=pl.BlockSpec(memory_space=pl.ANY),
        scratch_shapes=[pltpu.VMEM((2, tile, D // 2), jnp.uint32), pltpu.VMEM((MOE_BLK, D // 2), jnp.uint32),
                        pltpu.SemaphoreType.DMA((2,)), pltpu.SemaphoreType.DMA],
    )
    return pl.pallas_call(
        _dispatch_kernel,
        grid_spec=grid_spec,
        out_shape=jax.ShapeDtypeStruct((slots, D // 2), jnp.uint32),
        compiler_params=_params(("arbitrary",)),
        name="dispatch",
    )(dest, pad_end, h_all)


def _moe_kernel(blk_e_ref, blk_n_ref, blk_x_ref, x_ref, wg_ref, wu_ref, wd_ref, y_ref, wg_b, wu_b, wd_b):
    del blk_x_ref
    i = pl.program_id(0)
    n_valid = blk_n_ref[i]

    @pl.when((i == 0) | (blk_e_ref[i] != blk_e_ref[jnp.maximum(i - 1, 0)]))
    def _():
        wg_b[...] = wg_ref[0].astype(BF16)
        wu_b[...] = wu_ref[0].astype(BF16)
        wd_b[...] = wd_ref[0].astype(BF16)

    @pl.when(n_valid > 0)
    def _():
        x = _unpack_bf16_pairs(x_ref[...])
        a = jnp.dot(x, wg_b[...], preferred_element_type=F32)
        u = jnp.dot(x, wu_b[...], preferred_element_type=F32)
        y_ref[...] = jnp.dot((_silu(a) * u).astype(BF16), wd_b[...], preferred_element_type=F32)

    @pl.when(n_valid == 0)
    def _():
        y_ref[...] = jnp.zeros(y_ref.shape, F32)


def _moe(xs, blk_e, blk_n, blk_x, w_gate, w_up, w_down):
    slots = xs.shape[0]
    D, De = w_gate.shape[-2:]
    wspec = lambda shape: pl.BlockSpec((1,) + shape, lambda i, be, *_: (be[i], 0, 0))
    grid_spec = pltpu.PrefetchScalarGridSpec(
        num_scalar_prefetch=3,
        grid=(slots // MOE_BLK,),
        in_specs=[pl.BlockSpec((MOE_BLK, xs.shape[1]), lambda i, be, bn, bx: (bx[i], 0)),
                  wspec((D, De)), wspec((D, De)), wspec((De, D))],
        out_specs=pl.BlockSpec((MOE_BLK, D), lambda i, *_: (i, 0)),
        scratch_shapes=[pltpu.VMEM((D, De), BF16), pltpu.VMEM((D, De), BF16), pltpu.VMEM((De, D), BF16)],
    )
    return pl.pallas_call(
        _moe_kernel,
        grid_spec=grid_spec,
        out_shape=jax.ShapeDtypeStruct((slots, D), F32),
        compiler_params=_params(("arbitrary",)),
        name="moe",
    )(blk_e, blk_n, blk_x, xs, w_gate, w_up, w_down)


def _final_ln_kernel(dest_ref, h_ref, route_ref, g_ref, b_ref, ys_hbm, o_ref, ybuf, sem, *, tile0):
    i = pl.program_id(0)
    slot = i % 2
    rows = h_ref.shape[0]

    def gather(tile, s):
        a0 = (tile0 + tile) * (rows * TOP_K)

        def start(r, c):
            for k in range(TOP_K):
                pltpu.make_async_copy(ys_hbm.at[pl.ds(dest_ref[a0 + r * TOP_K + k], 1)],
                                      ybuf.at[s, k, pl.ds(r, 1)], sem.at[s]).start()
            return c

        lax.fori_loop(0, rows, start, 0, unroll=8)

    @pl.when(i == 0)
    def _():
        gather(0, 0)

    @pl.when(i + 1 < pl.num_programs(0))
    def _():
        gather(i + 1, 1 - slot)

    for k in range(TOP_K):
        pltpu.make_async_copy(ys_hbm.at[pl.ds(0, rows)], ybuf.at[slot, k], sem.at[slot]).wait()
    route = route_ref[...]
    lane = lax.broadcasted_iota(jnp.int32, route.shape, 1)
    gate1 = jnp.sum(jnp.where(lane == 0, route, 0.0), axis=1, keepdims=True)
    gate2 = jnp.sum(jnp.where(lane == 1, route, 0.0), axis=1, keepdims=True)
    f = ybuf[slot, 0] * gate1 + ybuf[slot, 1] * gate2
    o_ref[...] = _layer_norm(DN_ALPHA * h_ref[...] + f, g_ref[...], b_ref[...])


def _final_ln(h_all, route, dest, ys, g, b, row0, rows):
    D = h_all.shape[1]
    tile = max(t for t in (ROUTE_TILE, 2 * ROUTE_TILE) if rows % t == 0 and row0 % t == 0)
    tile0 = row0 // tile
    row = lambda i, *_: (tile0 + i, 0)
    fix = lambda i, *_: (0, 0)
    grid_spec = pltpu.PrefetchScalarGridSpec(
        num_scalar_prefetch=1,
        grid=(rows // tile,),
        in_specs=[pl.BlockSpec((tile, D), row), pl.BlockSpec((tile, LANES), row),
                  pl.BlockSpec((1, D), fix), pl.BlockSpec((1, D), fix), pl.BlockSpec(memory_space=pl.ANY)],
        out_specs=pl.BlockSpec((tile, D), lambda i, *_: (i, 0)),
        scratch_shapes=[pltpu.VMEM((2, TOP_K, tile, D), F32), pltpu.SemaphoreType.DMA((2,))],
    )
    return pl.pallas_call(
        functools.partial(_final_ln_kernel, tile0=tile0),
        grid_spec=grid_spec,
        out_shape=jax.ShapeDtypeStruct((rows, D), F32),
        compiler_params=_params(("arbitrary",)),
        name="final_ln",
    )(dest, h_all, route, g, b, ys)


def kernel(x_prompt, x_sample, cache_a_k, cache_a_v, state_b_ssm, state_b_conv, w_in, rel_bias, conv_w, a_log, dt_bias, o_norm_g, w_out, ln1_g, ln1_b, w_group, b_group, w_router, b_router, w_gate, w_up, w_down, ln2_g, ln2_b):
    B, S, D = x_prompt.shape
    N, T = x_sample.shape[0], x_sample.shape[1]
    depth = w_in.shape[0]
    assert depth == 1 and T == 1 and S % ATT_TILE == 0 and N % ROUTE_TILE == 0 and cache_a_k.shape[2] % LANES == 0
    l = 0
    win_p = min(BRANCHES[-1][0], S)

    w_pad32 = jnp.pad(w_in[l], ((0, 0), (0, IN_COLS_PAD - IN_COLS)))
    w_pad = w_pad32.astype(BF16)
    wo_b = w_out[l].astype(BF16)
    wr = jnp.pad(jnp.concatenate([w_group[l], w_router[l]], axis=1), ((0, 0), (0, LANES - N_GROUPS - N_EXPERTS)))
    br = jnp.pad(jnp.concatenate([b_group[l], b_router[l].reshape(-1)]), (0, LANES - N_GROUPS - N_EXPERTS))[None, :]
    g1, b1 = ln1_g[l][None, :], ln1_b[l][None, :]
    g2, b2 = ln2_g[l][None, :], ln2_b[l][None, :]

    xp = x_prompt.reshape(B * S, D)
    qkv_p, ub_p, zb_p, ab_p, k_win, v_win = _proj(xp, w_pad, 512, seq=S, win=win_p)
    oa_p = _attn(qkv_p, _band_bias(rel_bias), B, S)
    ob_p, st_p = _delta(ub_p, zb_p, ab_p, conv_w[l], a_log[l], dt_bias[l], o_norm_g[l], B, S)
    rows_all = B * S + N
    routed_p = _mix_ln(oa_p.reshape(B * S, A_WIDTH), ob_p.reshape(B * S, B_WIDTH), xp, wo_b, g1, b1, wr, br,
                       512, rows_all, 0)

    xs = x_sample.reshape(N, D)
    qkv_s, ub_s, zb_s, ab_s = _proj(xs, w_pad32, N)
    nb = 2
    qkv_t = jnp.transpose(qkv_s.reshape(N, 3, A_HEADS, A_HEAD_DIM), (1, 2, 3, 0))
    oa_s = _attn_dec(qkv_t, jnp.transpose(cache_a_k[l], (0, 2, 3, 1)), jnp.transpose(cache_a_v[l], (0, 2, 3, 1)),
                     _cache_bias(rel_bias, cache_a_k.shape[2]), nb)
    oa_s = jnp.transpose(oa_s, (0, 3, 1, 2)).reshape(N, A_WIDTH)
    ob_s, st_s = _delta_dec(ub_s, state_b_conv[l], zb_s, ab_s, conv_w[l], a_log[l], dt_bias[l], o_norm_g[l],
                            state_b_ssm[l], 16)
    h_all, route, cnt = _mix_ln(oa_s, ob_s, xs, w_out[l], g1, b1, wr, br, N, rows_all, B * S, prev=routed_p)

    base, blk_e, blk_n, blk_x, pad_end, n_blocks = _slot_layout(cnt)
    dest = _slots(route, base)[:, 0:TOP_K].reshape(-1)
    xs = _dispatch(h_all, dest, pad_end, n_blocks * MOE_BLK)
    ys = _moe(xs, blk_e, blk_n, blk_x, w_gate[l], w_up[l], w_down[l])
    y_p = _final_ln(h_all, route, dest, ys, g2, b2, 0, B * S)
    y_s = _final_ln(h_all, route, dest, ys, g2, b2, B * S, N)

    to_rows = lambda t: jnp.transpose(t, (0, 3, 1, 2))[None]
    conv_p = ub_p.reshape(B, S, CONV_DIM)[:, S - (CONV_WIDTH - 1):]
    conv_s = jnp.concatenate([state_b_conv[l], ub_s[:, None, :]], axis=1)[:, T:]
    new_kv = lambda t: jnp.transpose(t, (2, 0, 1))[None, :, None]
    return (y_p.reshape(B, S, D), y_s.reshape(N, T, D), to_rows(k_win), to_rows(v_win),
            new_kv(qkv_t[1]), new_kv(qkv_t[2]), st_p[None], st_s[None], conv_p[None], conv_s[None])
```

```python
import functools
import math

import jax
import jax.numpy as jnp
from jax import lax
from jax.experimental import pallas as pl
from jax.experimental.pallas import tpu as pltpu

F32 = jnp.float32
BF16 = jnp.bfloat16
HIGHEST = lax.Precision.HIGHEST

LANES = 128
A_HEADS = 8
A_HEAD_DIM = 64
A_WIDTH = A_HEADS * A_HEAD_DIM
BRANCHES = ((128, 1), (512, 4), (2048, 16))
BAND = 128
ATT_TILE = BAND * 16
REL_BUCKETS = 32
REL_MAX_DIST = 2048
B_HEADS = 4
B_HEAD_DIM = 128
B_WIDTH = B_HEADS * B_HEAD_DIM
CONV_WIDTH = 4
CONV_DIM = 3 * B_WIDTH
CHUNK = 64
COL_UB = 3 * A_WIDTH
COL_ZB = COL_UB + CONV_DIM
COL_AB = COL_ZB + B_WIDTH
IN_COLS = COL_AB + 2 * B_HEADS
IN_COLS_PAD = COL_AB + LANES
N_GROUPS = 4
EXPERTS_PER_GROUP = 8
N_EXPERTS = N_GROUPS * EXPERTS_PER_GROUP
TOP_K = 2
DN_ALPHA = 2.0 ** 0.25
LN_EPS = 1e-5
RMS_EPS = 1e-6
MASKED = -1e30
MOE_BLK = 512
ROUTE_TILE = 128
DELTA_TILE = 1024
ATT_GROUP = 16
VMEM_LIMIT = 56 * 1024 * 1024


def _bdot(a, b):
    return jnp.dot(a.astype(BF16), b.astype(BF16), preferred_element_type=F32)


def _bdot_nt(a, b):
    return lax.dot_general(a.astype(BF16), b.astype(BF16), (((1,), (1,)), ((), ())), preferred_element_type=F32)


def _bdot_tn(a, b):
    return lax.dot_general(a.astype(BF16), b.astype(BF16), (((0,), (0,)), ((), ())), preferred_element_type=F32)


def _fdot(a, b):
    return jnp.dot(a, b, precision=HIGHEST, preferred_element_type=F32)


def _sigmoid(x):
    return 1.0 / (1.0 + jnp.exp(-x))


def _silu(x):
    return x * _sigmoid(x)


def _softplus(x):
    return jnp.maximum(x, 0.0) + jnp.log(1.0 + jnp.exp(-jnp.abs(x)))


def _params(sem):
    return pltpu.CompilerParams(dimension_semantics=sem, vmem_limit_bytes=VMEM_LIMIT)


def _proj_kernel(x_ref, w_ref, qkv_ref, ub_ref, zb_ref, ab_ref, *win_refs, tiles_per_seq=0, first_win_tile=0):
    if w_ref.dtype == BF16:
        xb = x_ref[...].astype(BF16)
        dot = lambda w: jnp.dot(xb, w, preferred_element_type=F32)
    else:
        dot = lambda w: _fdot(x_ref[...], w)
    qkv = dot(w_ref[:, 0:COL_UB])
    qkv_ref[...] = qkv
    ub_ref[...] = dot(w_ref[:, COL_UB:COL_ZB])
    zb_ref[...] = dot(w_ref[:, COL_ZB:COL_AB])
    ab_ref[...] = dot(w_ref[:, COL_AB:IN_COLS_PAD])
    if win_refs:
        @pl.when(pl.program_id(0) % tiles_per_seq >= first_win_tile)
        def _():
            tm = qkv.shape[0]
            for j, ref in enumerate(win_refs):
                cols = qkv[:, (1 + j) * A_WIDTH:(2 + j) * A_WIDTH]
                ref[0] = jnp.transpose(cols).reshape(A_HEADS, A_HEAD_DIM, tm)


def _proj(x2d, w_pad, tm, seq=None, win=None):
    T, D = x2d.shape
    row = lambda i: (i, 0)
    out_specs = [pl.BlockSpec((tm, COL_UB), row), pl.BlockSpec((tm, CONV_DIM), row),
                 pl.BlockSpec((tm, B_WIDTH), row), pl.BlockSpec((tm, LANES), row)]
    out_shape = [jax.ShapeDtypeStruct((T, COL_UB), F32), jax.ShapeDtypeStruct((T, CONV_DIM), F32),
                 jax.ShapeDtypeStruct((T, B_WIDTH), F32), jax.ShapeDtypeStruct((T, LANES), F32)]
    body = _proj_kernel
    if win:
        per_seq, first = seq // tm, (seq - win) // tm
        wspec = pl.BlockSpec((1, A_HEADS, A_HEAD_DIM, tm),
                             lambda i: (i // per_seq, 0, 0, jnp.maximum(i % per_seq - first, 0)))
        out_specs += [wspec, wspec]
        out_shape += [jax.ShapeDtypeStruct((T // seq, A_HEADS, A_HEAD_DIM, win), F32)] * 2
        body = functools.partial(_proj_kernel, tiles_per_seq=per_seq, first_win_tile=first)
    return pl.pallas_call(
        body,
        grid=(T // tm,),
        in_specs=[pl.BlockSpec((tm, D), row), pl.BlockSpec((D, IN_COLS_PAD), lambda i: (0, 0))],
        out_specs=out_specs,
        out_shape=out_shape,
        compiler_params=_params(("arbitrary",)),
        name="proj",
    )(x2d, w_pad)


def _rel_bucket(dist):
    max_exact = REL_BUCKETS // 2
    n = jnp.maximum(dist, 0)
    ratio = jnp.maximum(n, 1).astype(F32) / max_exact
    large = max_exact + (jnp.log(ratio) / math.log(REL_MAX_DIST / max_exact)
                         * (REL_BUCKETS - max_exact)).astype(jnp.int32)
    return jnp.where(n < max_exact, n, jnp.minimum(large, REL_BUCKETS - 1))


def _bias_of(rel_bias, dist):
    onehot = (_rel_bucket(dist)[None, :] == jnp.arange(REL_BUCKETS)[:, None]).astype(F32)
    return jnp.dot(rel_bias.astype(F32).T, onehot, precision=HIGHEST)


def _band_bias(rel_bias):
    period = 3 * BAND
    tabs = []
    for _, dil in BRANCHES:
        g = jnp.concatenate([_bias_of(rel_bias, (BAND - jnp.arange(BAND + 1)) * dil),
                             jnp.full((A_HEADS, period - BAND - 1), MASKED, F32)], axis=1)
        skew = jnp.tile(g, (1, BAND))[:, :BAND * (period - 1)].reshape(A_HEADS, BAND, period - 1)
        tabs.append(skew[:, :, :2 * BAND])
    return jnp.stack(tabs)


def _cache_bias(rel_bias, P):
    dist = P - jnp.arange(P + LANES)
    tabs = []
    for window, dil in BRANCHES:
        ok = (dist >= 0) & (dist <= window) & (dist % dil == 0)
        tabs.append(jnp.where(ok[None, :], _bias_of(rel_bias, dist), MASKED)[:, None, :])
    return jnp.stack(tabs)


def _attn_kernel(q_ref, k_ref, v_ref, bias_ref, o_ref, acc_ref, m_ref, l_ref):
    t = pl.program_id(2)
    tile0 = t * ATT_TILE
    lane = lax.broadcasted_iota(jnp.int32, (BAND, LANES), 1)
    head0 = lane < A_HEAD_DIM

    def rows(start, dil):
        return pl.ds(pl.multiple_of(start, BAND), BAND) if dil == 1 else pl.ds(start, BAND, stride=dil)

    for br, (_, dil) in enumerate(BRANCHES):
        span = BAND * dil

        def blocks(it, carry, br=br, dil=dil, span=span):
            gs = range(ATT_GROUP)
            idx = [it * ATT_GROUP + g for g in gs]
            start = [(i % dil) + (i // dil) * span for i in idx]
            cur = [tile0 + s for s in start]
            first = [c < span for c in cur]
            prev = [jnp.where(first[g], cur[g], cur[g] - span) for g in gs]
            q = [q_ref[0, rows(start[g], dil), :] * (A_HEAD_DIM ** -0.5) for g in gs]
            kk = [jnp.concatenate([k_ref[0, rows(prev[g], dil), :], k_ref[0, rows(cur[g], dil), :]],
                                  axis=0).astype(BF16) for g in gs]
            vv = [jnp.concatenate([v_ref[0, rows(prev[g], dil), :], v_ref[0, rows(cur[g], dil), :]],
                                  axis=0).astype(BF16) for g in gs]
            pen = [jnp.where(first[g], MASKED, 0.0) for g in gs]
            gh = [(g, hh) for g in gs for hh in range(2)]
            qh = [jnp.where(head0 if hh == 0 else ~head0, q[g], 0.0).astype(BF16) for g, hh in gh]
            s = [lax.dot_general(qh[j], kk[g], (((1,), (1,)), ((), ())), preferred_element_type=F32)
                 for j, (g, hh) in enumerate(gh)]
            s = [s[j] + bias_ref[br, hh] for j, (g, hh) in enumerate(gh)]
            s = [jnp.concatenate([s[j][:, 0:BAND] + pen[g], s[j][:, BAND:2 * BAND]], axis=1)
                 for j, (g, hh) in enumerate(gh)]
            m = [jnp.max(t, axis=1, keepdims=True) for t in s]
            p = [jnp.exp(s[j] - m[j]) for j in range(len(gh))]
            l = [jnp.sum(t, axis=1, keepdims=True) for t in p]
            pv = [jnp.dot(p[j].astype(BF16), vv[g], preferred_element_type=F32) for j, (g, hh) in enumerate(gh)]
            for g in gs:
                acc_ref[br, rows(start[g], dil), :] = jnp.where(head0, pv[2 * g], pv[2 * g + 1])
                m_ref[br, rows(start[g], dil), :] = jnp.where(head0, m[2 * g], m[2 * g + 1])
                l_ref[br, rows(start[g], dil), :] = jnp.where(head0, l[2 * g], l[2 * g + 1])
            return carry

        lax.fori_loop(0, ATT_TILE // BAND // ATT_GROUP, blocks, 0)

    def merge(c, carry):
        r = pl.ds(pl.multiple_of(c * 256, 256), 256)
        m0, m1, m2 = m_ref[0, r, :], m_ref[1, r, :], m_ref[2, r, :]
        mx = jnp.maximum(jnp.maximum(m0, m1), m2)
        w0, w1, w2 = jnp.exp(m0 - mx), jnp.exp(m1 - mx), jnp.exp(m2 - mx)
        num = w0 * acc_ref[0, r, :] + w1 * acc_ref[1, r, :] + w2 * acc_ref[2, r, :]
        den = w0 * l_ref[0, r, :] + w1 * l_ref[1, r, :] + w2 * l_ref[2, r, :]
        o_ref[0, r, :] = num / den
        return carry

    lax.fori_loop(0, ATT_TILE // 256, merge, 0)


def _attn(qkv, bias, B, S):
    n_pairs = A_HEADS // 2
    qkv3 = qkv.reshape(B, S, 3 * A_WIDTH)
    return pl.pallas_call(
        _attn_kernel,
        grid=(B, n_pairs, S // ATT_TILE),
        in_specs=[pl.BlockSpec((1, ATT_TILE, LANES), lambda b, hp, t: (b, t, hp)),
                  pl.BlockSpec((1, S, LANES), lambda b, hp, t: (b, 0, n_pairs + hp)),
                  pl.BlockSpec((1, S, LANES), lambda b, hp, t: (b, 0, 2 * n_pairs + hp)),
                  pl.BlockSpec((3, 2, BAND, 2 * BAND), lambda b, hp, t: (0, hp, 0, 0))],
        out_specs=pl.BlockSpec((1, ATT_TILE, LANES), lambda b, hp, t: (b, t, hp)),
        out_shape=jax.ShapeDtypeStruct((B, S, A_WIDTH), F32),
        scratch_shapes=[pltpu.VMEM((3, ATT_TILE, LANES), F32)] * 3,
        compiler_params=_params(("parallel", "parallel", "arbitrary")),
        name="attn",
    )(qkv3, qkv3, qkv3, bias)


def _attn_dec_kernel(qkv_ref, kt_ref, vt_ref, bias_ref, o_ref, *, nb):
    i = pl.program_id(0)
    N = qkv_ref.shape[-1]
    P = kt_ref.shape[-1]
    lane_n = lax.broadcasted_iota(jnp.int32, (A_HEAD_DIM, N), 1)
    lane_o = lax.broadcasted_iota(jnp.int32, (A_HEAD_DIM, nb), 1)
    lane_t = lax.broadcasted_iota(jnp.int32, (1, LANES), 1)

    def head(h, carry):
        slab = jnp.zeros((A_HEAD_DIM, nb), F32)
        for j in range(nb):
            pick = lane_n == i * nb + j
            col = lambda t: jnp.sum(jnp.where(pick, t, 0.0), axis=1, keepdims=True)
            q = col(qkv_ref[0, h]) * (A_HEAD_DIM ** -0.5)
            k_new, v_new = col(qkv_ref[1, h]), col(qkv_ref[2, h])
            s_new = jnp.sum(q * k_new, axis=0, keepdims=True)
            s = jnp.concatenate([jnp.sum(kt_ref[j, h] * q, axis=0, keepdims=True),
                                 jnp.where(lane_t == 0, s_new, 0.0)], axis=1)
            ps, ms, ls = [], [], []
            for br in range(3):
                sb = s + bias_ref[br, h]
                m = jnp.max(sb, axis=1, keepdims=True)
                p = jnp.exp(sb - m)
                ps.append(p)
                ms.append(m)
                ls.append(jnp.sum(p, axis=1, keepdims=True))
            mx = jnp.maximum(jnp.maximum(ms[0], ms[1]), ms[2])
            w = jnp.zeros((1, P + LANES), F32)
            den = jnp.zeros((1, 1), F32)
            for p, m, l in zip(ps, ms, ls):
                e = jnp.exp(m - mx)
                w = w + e * p
                den = den + e * l
            o = jnp.sum(vt_ref[j, h] * w[:, 0:P], axis=1, keepdims=True) + v_new * w[:, P:P + 1]
            slab = jnp.where(lane_o == j, o / den, slab)
        o_ref[0, h] = slab
        return carry

    lax.fori_loop(0, A_HEADS, head, 0, unroll=2)


def _attn_dec(qkv_t, cache_kt, cache_vt, bias, nb):
    N, P = cache_kt.shape[0], cache_kt.shape[-1]
    cache_spec = pl.BlockSpec((nb, A_HEADS, A_HEAD_DIM, P), lambda i: (i, 0, 0, 0))
    return pl.pallas_call(
        functools.partial(_attn_dec_kernel, nb=nb),
        grid=(N // nb,),
        in_specs=[pl.BlockSpec((3, A_HEADS, A_HEAD_DIM, N), lambda i: (0, 0, 0, 0)), cache_spec, cache_spec,
                  pl.BlockSpec((3, A_HEADS, 1, P + LANES), lambda i: (0, 0, 0, 0))],
        out_specs=pl.BlockSpec((1, A_HEADS, A_HEAD_DIM, nb), lambda i: (i, 0, 0, 0)),
        out_shape=jax.ShapeDtypeStruct((N // nb, A_HEADS, A_HEAD_DIM, nb), F32),
        compiler_params=_params(("parallel",)),
        name="attn_dec",
    )(qkv_t, cache_kt, cache_vt, bias)


def _split3(x):
    hi = x.astype(BF16)
    r = x - hi.astype(F32)
    mid = r.astype(BF16)
    return hi, mid, (r - mid.astype(F32)).astype(BF16)


def _tril_dot(tril_b, g):
    return sum(jnp.dot(tril_b, piece, preferred_element_type=F32) for piece in _split3(g))


def _dot3(a, b):
    ah, bh = a.astype(BF16), b.astype(BF16)
    al, bl = (a - ah.astype(F32)).astype(BF16), (b - bh.astype(F32)).astype(BF16)
    d = lambda x, y: jnp.dot(x, y, preferred_element_type=F32)
    return d(ah, bh) + (d(ah, bl) + d(al, bh))


def _delta_kernel(alog_ref, dtb_ref, ub_ref, z_ref, ab_ref, cw_ref, og_ref, o_ref, st_ref,
                  pad_ref, u_s, wq_s, kt_s, qk_s, gl_s, a_s, rhs_s):
    t = pl.program_id(1)
    TS = ub_ref.shape[1]
    nch = TS // CHUNK
    hdr = 8

    slabs = range(CONV_DIM // LANES)
    seq_rows = lambda first, n: pl.ds(2 * first, n, stride=2)

    @pl.when(t == 0)
    def _():
        for s in slabs:
            pad_ref[s, seq_rows(0, hdr), :] = jnp.zeros((hdr, LANES), F32)
        st_ref[...] = jnp.zeros(st_ref.shape, F32)

    @pl.when(t > 0)
    def _():
        for s in slabs:
            pad_ref[s, seq_rows(0, hdr), :] = pad_ref[s, seq_rows(TS, hdr), :]

    for s in slabs:
        pad_ref[s, seq_rows(hdr, TS), :] = ub_ref[0, :, s * LANES:(s + 1) * LANES]

    ri = lax.broadcasted_iota(jnp.int32, (CHUNK, CHUNK), 0)
    ci = lax.broadcasted_iota(jnp.int32, (CHUNK, CHUNK), 1)
    incl = ri >= ci
    strict = ri > ci
    tril_b = incl.astype(BF16)
    eye = (ri == ci).astype(F32)
    lane = lax.broadcasted_iota(jnp.int32, (CHUNK, LANES), 1)

    def local(c, carry):
        base = c * CHUNK if isinstance(c, int) else pl.multiple_of(c * CHUNK, CHUNK)
        ab = ab_ref[0, pl.ds(base, CHUNK), :]
        heads = []
        for h in range(B_HEADS):
            def conv(col):
                first = c * CHUNK + (hdr - (CONV_WIDTH - 1))
                acc = pad_ref[col // LANES, seq_rows(first, CHUNK), :] * cw_ref[0:1, col:col + LANES]
                for i in range(1, CONV_WIDTH):
                    acc = acc + pad_ref[col // LANES, seq_rows(first + i, CHUNK), :] * cw_ref[i:i + 1, col:col + LANES]
                return _silu(acc)

            cq, ck, v = conv(h * LANES), conv(B_WIDTH + h * LANES), conv(2 * B_WIDTH + h * LANES)
            q = cq * lax.rsqrt(jnp.sum(cq * cq, axis=1, keepdims=True) + 1e-6) * (B_HEAD_DIM ** -0.5)
            k = ck * lax.rsqrt(jnp.sum(ck * ck, axis=1, keepdims=True) + 1e-6)
            a_raw = jnp.sum(jnp.where(lane == h, ab, 0.0), axis=1, keepdims=True)
            b_raw = jnp.sum(jnp.where(lane == h + B_HEADS, ab, 0.0), axis=1, keepdims=True)
            neg_a = -jnp.exp(jnp.full((1, LANES), alog_ref[h], F32))
            g = neg_a * _softplus(a_raw + dtb_ref[h])
            beta = _sigmoid(b_raw)
            heads.append((q, k, v, g, beta))
        hs = range(B_HEADS)
        q, k, v, g, beta = zip(*heads)
        gc = [_tril_dot(tril_b, g[h]) for h in hs]
        dmat = [_tril_dot(tril_b, jnp.where(strict, g[h][:, 0:CHUNK], 0.0)) for h in hs]
        kq = [_bdot_nt(jnp.concatenate([k[h], q[h]], axis=0), k[h]) for h in hs]
        decay = [jnp.where(incl, jnp.exp(dmat[h]), 0.0) for h in hs]
        e_gc = [jnp.exp(gc[h]) for h in hs]
        for h in hs:
            gc_last = gc[h][CHUNK - 1:CHUNK, :]
            a_s[h, c] = jnp.where(strict, beta[h] * kq[h][0:CHUNK] * decay[h], 0.0)
            rhs_s[h, c] = jnp.concatenate([v[h] * beta[h], k[h] * (beta[h] * e_gc[h])], axis=1)
            wq_s[h, c, CHUNK:2 * CHUNK] = q[h] * e_gc[h]
            kt_s[h, c] = k[h] * jnp.exp(gc_last - gc[h])
            qk_s[h, c] = kq[h][CHUNK:2 * CHUNK] * decay[h]
            gl_s[h, c] = jnp.broadcast_to(jnp.exp(gc_last), (8, LANES))
        return carry

    def solve(c, carry):
        hs = range(B_HEADS)
        a = [a_s[h, c] for h in hs]
        x = [eye - a[h] for h in hs]
        p = [_bdot(a[h], a[h]) for h in hs]
        for _ in range(int(math.log2(CHUNK)) - 2):
            r = [_bdot(jnp.concatenate([x[h], p[h]], axis=0), p[h]) for h in hs]
            x = [x[h] + r[h][0:CHUNK] for h in hs]
            p = [r[h][CHUNK:2 * CHUNK] for h in hs]
        x = [x[h] + _bdot(x[h], p[h]) for h in hs]
        sol = [_bdot(x[h], rhs_s[h, c]) for h in hs]
        for h in hs:
            u_s[h, c] = sol[h][:, 0:LANES]
            wq_s[h, c, 0:CHUNK] = sol[h][:, LANES:2 * LANES]
        return carry

    def scan(c, carry):
        base = c * CHUNK if isinstance(c, int) else pl.multiple_of(c * CHUNK, CHUNK)
        hs = range(B_HEADS)
        state = [st_ref[0, h] for h in hs]
        r = [_bdot(wq_s[h, c], state[h]) for h in hs]
        v_new = [u_s[h, c] - r[h][0:CHUNK] for h in hs]
        upd = [_bdot_tn(kt_s[h, c], v_new[h]) for h in hs]
        out = [r[h][CHUNK:2 * CHUNK] + _bdot(qk_s[h, c], v_new[h]) for h in hs]
        for h in hs:
            st_ref[0, h] = state[h] * gl_s[h, c][0:1, :] + upd[h]
            o = out[h] * lax.rsqrt(jnp.mean(out[h] * out[h], axis=1, keepdims=True) + RMS_EPS) * og_ref[...]
            cols = slice(h * LANES, (h + 1) * LANES)
            o_ref[0, pl.ds(base, CHUNK), cols] = o * _silu(z_ref[0, pl.ds(base, CHUNK), cols])
        return carry

    def stages(c, carry):
        scan(c - 1, carry)
        solve(c, carry)
        return local(c + 1, carry)

    assert nch >= 3
    local(0, 0)
    solve(0, 0)
    local(1, 0)
    lax.fori_loop(1, nch - 1, stages, 0)
    scan(nch - 2, 0)
    solve(nch - 1, 0)
    scan(nch - 1, 0)


def _delta(ub, zb, ab, conv_w, a_log, dt_bias, o_norm_g, B, S):
    ts = min(S, DELTA_TILE)
    nch = ts // CHUNK
    seq = lambda width: pl.BlockSpec((1, ts, width), lambda b, t, *_: (b, t, 0))
    fix = lambda shape: pl.BlockSpec(shape, lambda b, t, *_: (0, 0))
    per_chunk = lambda rows, width: pltpu.VMEM((B_HEADS, nch, rows, width), F32)
    grid_spec = pltpu.PrefetchScalarGridSpec(
        num_scalar_prefetch=2,
        grid=(B, S // ts),
        in_specs=[seq(CONV_DIM), seq(B_WIDTH), seq(LANES), fix((CONV_WIDTH, CONV_DIM)), fix((1, LANES))],
        out_specs=[seq(B_WIDTH),
                   pl.BlockSpec((1, B_HEADS, B_HEAD_DIM, B_HEAD_DIM), lambda b, t, *_: (b, 0, 0, 0))],
        scratch_shapes=[pltpu.VMEM((CONV_DIM // LANES, 2 * (ts + 8), LANES), F32),
                        per_chunk(CHUNK, LANES), per_chunk(2 * CHUNK, LANES),
                        per_chunk(CHUNK, LANES), per_chunk(CHUNK, CHUNK), per_chunk(8, LANES),
                        per_chunk(CHUNK, CHUNK), per_chunk(CHUNK, 2 * LANES)],
    )
    return pl.pallas_call(
        _delta_kernel,
        grid_spec=grid_spec,
        out_shape=[jax.ShapeDtypeStruct((B, S, B_WIDTH), F32),
                   jax.ShapeDtypeStruct((B, B_HEADS, B_HEAD_DIM, B_HEAD_DIM), F32)],
        compiler_params=_params(("parallel", "arbitrary")),
        name="delta",
    )(a_log, dt_bias, ub.reshape(B, S, CONV_DIM), zb.reshape(B, S, B_WIDTH), ab.reshape(B, S, LANES),
      conv_w, o_norm_g.reshape(1, LANES))


def _delta_dec_kernel(alog_ref, dtb_ref, ub_ref, cs_ref, zb_ref, ab_ref, w_ref, og_ref, st_ref,
                      o_ref, so_ref, *, nb):
    i = pl.program_id(0)
    N = ub_ref.shape[0]
    acc = ub_ref[...] * w_ref[CONV_WIDTH - 1:CONV_WIDTH, :]
    for t in range(CONV_WIDTH - 1):
        acc = acc + cs_ref[t] * w_ref[t:t + 1, :]
    c = _silu(acc)
    ab = ab_ref[...]
    lane = lax.broadcasted_iota(jnp.int32, (N, LANES), 1)
    samp = lax.broadcasted_iota(jnp.int32, (B_HEAD_DIM, N), 1)
    row_id = lax.broadcasted_iota(jnp.int32, (N, LANES), 0)
    out_row = lax.broadcasted_iota(jnp.int32, (nb, LANES), 0)
    heads = []
    for h in range(B_HEADS):
        cq = c[:, h * LANES:(h + 1) * LANES]
        ck = c[:, B_WIDTH + h * LANES:B_WIDTH + (h + 1) * LANES]
        v = c[:, 2 * B_WIDTH + h * LANES:2 * B_WIDTH + (h + 1) * LANES]
        q = cq * lax.rsqrt(jnp.sum(cq * cq, axis=1, keepdims=True) + 1e-6) * (B_HEAD_DIM ** -0.5)
        k = ck * lax.rsqrt(jnp.sum(ck * ck, axis=1, keepdims=True) + 1e-6)
        a_raw = jnp.sum(jnp.where(lane == h, ab, 0.0), axis=1, keepdims=True)
        b_raw = jnp.sum(jnp.where(lane == h + B_HEADS, ab, 0.0), axis=1, keepdims=True)
        neg_a = -jnp.exp(jnp.full((1, 1), alog_ref[h], F32))
        dec = jnp.exp(neg_a * _softplus(a_raw + dtb_ref[h]))
        beta = _sigmoid(b_raw)
        heads.append((q.T, k.T, v, jnp.broadcast_to(dec, (N, LANES)), jnp.broadcast_to(beta, (N, LANES)),
                      zb_ref[:, h * LANES:(h + 1) * LANES]))

    hs = range(B_HEADS)
    q_t, k_t, v, dec, beta, z = zip(*heads)

    def sample(j, o_acc):
        n = i * nb + j
        pick = samp == n
        k_col = [jnp.sum(jnp.where(pick, k_t[h], 0.0), axis=1, keepdims=True) for h in hs]
        q_col = [jnp.sum(jnp.where(pick, q_t[h], 0.0), axis=1, keepdims=True) for h in hs]
        pick_r = row_id == n
        row = lambda t: jnp.sum(jnp.where(pick_r, t, 0.0), axis=0, keepdims=True)
        st = [st_ref[j, h] * row(dec[h]) for h in hs]
        mem = [jnp.sum(k_col[h] * st[h], axis=0, keepdims=True) for h in hs]
        st = [st[h] + k_col[h] * ((row(v[h]) - mem[h]) * row(beta[h])) for h in hs]
        o = [jnp.sum(q_col[h] * st[h], axis=0, keepdims=True) for h in hs]
        o = [o[h] * lax.rsqrt(jnp.mean(o[h] * o[h], axis=1, keepdims=True) + RMS_EPS) * og_ref[...] for h in hs]
        for h in hs:
            so_ref[j, h] = st[h]
        return tuple(jnp.where(out_row == j, o[h] * _silu(row(z[h])), o_acc[h]) for h in hs)

    outs = lax.fori_loop(0, nb, sample, tuple(jnp.zeros((nb, LANES), F32) for _ in hs), unroll=2)
    for h in hs:
        o_ref[:, h * LANES:(h + 1) * LANES] = outs[h]


def _delta_dec(ub_s, conv_state, zb_s, ab_s, conv_w, a_log, dt_bias, o_norm_g, state, nb):
    N = ub_s.shape[0]
    full2 = lambda shape: pl.BlockSpec(shape, lambda i, *_: (0, 0))
    grid_spec = pltpu.PrefetchScalarGridSpec(
        num_scalar_prefetch=2,
        grid=(N // nb,),
        in_specs=[full2((N, CONV_DIM)),
                  pl.BlockSpec((CONV_WIDTH - 1, N, CONV_DIM), lambda i, *_: (0, 0, 0)),
                  full2((N, B_WIDTH)), full2((N, LANES)), full2((CONV_WIDTH, CONV_DIM)), full2((1, LANES)),
                  pl.BlockSpec((nb, B_HEADS, B_HEAD_DIM, B_HEAD_DIM), lambda i, *_: (i, 0, 0, 0))],
        out_specs=[pl.BlockSpec((nb, B_WIDTH), lambda i, *_: (i, 0)),
                   pl.BlockSpec((nb, B_HEADS, B_HEAD_DIM, B_HEAD_DIM), lambda i, *_: (i, 0, 0, 0))],
    )
    return pl.pallas_call(
        functools.partial(_delta_dec_kernel, nb=nb),
        grid_spec=grid_spec,
        out_shape=[jax.ShapeDtypeStruct((N, B_WIDTH), F32), jax.ShapeDtypeStruct(state.shape, F32)],
        compiler_params=_params(("parallel",)),
        name="delta_dec",
    )(a_log, dt_bias, ub_s, jnp.swapaxes(conv_state, 0, 1), zb_s, ab_s, conv_w, o_norm_g.reshape(1, LANES), state)


def _layer_norm(r, g, b):
    mu = jnp.mean(r, axis=1, keepdims=True)
    d = r - mu
    var = jnp.mean(d * d, axis=1, keepdims=True)
    return d * lax.rsqrt(var + LN_EPS) * g + b


def _mix_ln_kernel(*refs, steps):
    outs = refs[-3:]

    @pl.when(pl.program_id(0) < steps)
    def _():
        _mix_ln_rows(*refs[:8], *outs)

    @pl.when(pl.program_id(0) >= steps)
    def _():
        for o in outs:
            o[...] = jnp.zeros(o.shape, F32)


def _mix_ln_rows(oa_ref, ob_ref, x_ref, wo_ref, g_ref, b_ref, wr_ref, br_ref, h_ref, route_ref, cnt_ref):
    if wo_ref.dtype == BF16:
        dot = lambda a, w: jnp.dot(a.astype(BF16), w, preferred_element_type=F32)
    else:
        dot = _fdot
    y = dot(oa_ref[...], wo_ref[0:A_WIDTH, :]) + dot(ob_ref[...], wo_ref[A_WIDTH:A_WIDTH + B_WIDTH, :])
    hcur = _layer_norm(DN_ALPHA * x_ref[...] + y, g_ref[...], b_ref[...])
    h_ref[...] = hcur
    logits = (_bdot if wo_ref.dtype == BF16 else _dot3)(hcur, wr_ref[...]) + br_ref[...]
    lane = lax.broadcasted_iota(jnp.int32, logits.shape, 1)
    lane_f = lane.astype(F32)
    ninf = -jnp.inf
    big = 1e9
    gl = jnp.where(lane < N_GROUPS, logits, ninf)
    gmax = jnp.max(gl, axis=1, keepdims=True)
    g_idx = jnp.min(jnp.where(gl == gmax, lane_f, big), axis=1, keepdims=True)
    p_group = 1.0 / jnp.sum(jnp.exp(gl - gmax), axis=1, keepdims=True)
    grp_of_lane = ((lane - N_GROUPS) >> 3).astype(F32)
    sel = (lane >= N_GROUPS) & (lane < N_GROUPS + N_EXPERTS) & (grp_of_lane == g_idx)
    el = jnp.where(sel, logits, ninf)
    v1 = jnp.max(el, axis=1, keepdims=True)
    i1 = jnp.min(jnp.where(el == v1, lane_f, big), axis=1, keepdims=True)
    el2 = jnp.where(lane_f == i1, ninf, el)
    v2 = jnp.max(el2, axis=1, keepdims=True)
    i2 = jnp.min(jnp.where(el2 == v2, lane_f, big), axis=1, keepdims=True)
    t = jnp.exp(v2 - v1)
    gate1 = p_group / (1.0 + t)
    gate2 = p_group * t / (1.0 + t)
    e1, e2 = i1 - N_GROUPS, i2 - N_GROUPS
    route_ref[...] = jnp.where(lane == 0, gate1, jnp.where(lane == 1, gate2, jnp.where(
        lane == 2, e1, jnp.where(lane == 3, e2, 0.0))))
    chosen = ((lane_f == e1) | (lane_f == e2)).astype(F32)
    tm = chosen.shape[0]
    cnt_ref[...] = jnp.sum(chosen.reshape(tm // ROUTE_TILE, ROUTE_TILE, LANES), axis=1)[:, None, :]


def _mix_ln(oa, ob, x2d, wo_b, g, b, wr, br, tm, total, row0, prev=()):
    T, D = x2d.shape
    off = row0 // tm
    steps = T // tm
    tail = 1 if row0 + T < total else 0
    assert total - (row0 + T) <= tm
    row = lambda i: (jnp.minimum(i, steps - 1), 0)
    out_row = lambda i: (off + i, 0)
    fix = lambda i: (0, 0)
    sub = tm // ROUTE_TILE
    return pl.pallas_call(
        functools.partial(_mix_ln_kernel, steps=steps),
        grid=(steps + tail,),
        in_specs=[pl.BlockSpec((tm, A_WIDTH), row), pl.BlockSpec((tm, B_WIDTH), row), pl.BlockSpec((tm, D), row),
                  pl.BlockSpec((A_WIDTH + B_WIDTH, D), fix), pl.BlockSpec((1, D), fix), pl.BlockSpec((1, D), fix),
                  pl.BlockSpec((D, LANES), fix), pl.BlockSpec((1, LANES), fix)]
                 + [pl.BlockSpec(memory_space=pl.ANY)] * len(prev),
        out_specs=[pl.BlockSpec((tm, D), out_row), pl.BlockSpec((tm, LANES), out_row),
                   pl.BlockSpec((sub, 1, LANES), lambda i: (off + i, 0, 0))],
        out_shape=[jax.ShapeDtypeStruct((total, D), F32), jax.ShapeDtypeStruct((total, LANES), F32),
                   jax.ShapeDtypeStruct((total // ROUTE_TILE, 1, LANES), F32)],
        input_output_aliases={8 + j: j for j in range(len(prev))},
        compiler_params=_params(("parallel",)),
        name="mix_ln",
    )(oa, ob, x2d, wo_b, g, b, wr, br, *prev)


def _slot_layout(counts):
    tiles = counts.shape[0]
    n_assign = tiles * ROUTE_TILE * TOP_K
    n_blocks = -(-(n_assign + N_EXPERTS * (MOE_BLK - 1)) // MOE_BLK)
    per_tile = counts.reshape(tiles, LANES)
    earlier = (jnp.arange(tiles)[:, None] > jnp.arange(tiles)[None, :]).astype(F32)
    before = jnp.dot(earlier, per_tile, precision=HIGHEST)
    total = jnp.sum(per_tile, axis=0)[:N_EXPERTS]
    padded = jnp.ceil(total / MOE_BLK) * MOE_BLK
    upto = (jnp.arange(N_EXPERTS)[:, None] <= jnp.arange(N_EXPERTS)[None, :]).astype(F32)
    pad_end = jnp.dot(padded, upto, precision=HIGHEST)
    pad_start = pad_end - padded
    base = (before + jnp.pad(pad_start, (0, LANES - N_EXPERTS))[None, :]).reshape(tiles, 1, LANES)
    blk_start = (jnp.arange(n_blocks) * MOE_BLK).astype(F32)
    in_e = ((pad_start[None, :] <= blk_start[:, None]) & (blk_start[:, None] < pad_end[None, :])).astype(F32)
    used = blk_start < pad_end[-1]
    n_used = jnp.sum(used.astype(jnp.int32))
    last_e = jnp.max(jnp.where(padded > 0, jnp.arange(N_EXPERTS), 0)).astype(F32)
    blk_e = jnp.where(used, jnp.dot(in_e, jnp.arange(N_EXPERTS, dtype=F32), precision=HIGHEST), last_e)
    blk_n = jnp.clip(jnp.dot(in_e, pad_start + total, precision=HIGHEST) - blk_start, 0, MOE_BLK)
    blk_x = jnp.minimum(jnp.arange(n_blocks), jnp.maximum(n_used - 1, 0))
    i32 = lambda t: t.astype(jnp.int32)
    return base, i32(blk_e), i32(blk_n), i32(blk_x), i32(pad_end), n_blocks


def _slot_kernel(route_ref, base_ref, dest_ref):
    lane = lax.broadcasted_iota(jnp.int32, (ROUTE_TILE, LANES), 1)
    lane_f = lane.astype(F32)
    ri = lax.broadcasted_iota(jnp.int32, (ROUTE_TILE, ROUTE_TILE), 0)
    ci = lax.broadcasted_iota(jnp.int32, (ROUTE_TILE, ROUTE_TILE), 1)
    before = (ri > ci).astype(BF16)
    for t in range(base_ref.shape[0]):
        rows = slice(t * ROUTE_TILE, (t + 1) * ROUTE_TILE)
        route = route_ref[rows, :]
        e1 = jnp.sum(jnp.where(lane == 2, route, 0.0), axis=1, keepdims=True)
        e2 = jnp.sum(jnp.where(lane == 3, route, 0.0), axis=1, keepdims=True)
        oh1, oh2 = lane_f == e1, lane_f == e2
        earlier = _bdot(before, (oh1 | oh2).astype(F32))
        slot = base_ref[t] + earlier
        d1 = jnp.sum(jnp.where(oh1, slot, 0.0), axis=1, keepdims=True)
        d2 = jnp.sum(jnp.where(oh2, slot, 0.0), axis=1, keepdims=True)
        dest_ref[rows, :] = jnp.where(lane == 0, d1, jnp.where(lane == 1, d2, 0.0)).astype(jnp.int32)


def _slots(route, base):
    tiles = route.shape[0] // ROUTE_TILE
    sub = max(d for d in range(1, 9) if tiles % d == 0)
    return pl.pallas_call(
        _slot_kernel,
        grid=(tiles // sub,),
        in_specs=[pl.BlockSpec((sub * ROUTE_TILE, LANES), lambda i: (i, 0)),
                  pl.BlockSpec((sub, 1, LANES), lambda i: (i, 0, 0))],
        out_specs=pl.BlockSpec((sub * ROUTE_TILE, LANES), lambda i: (i, 0)),
        out_shape=jax.ShapeDtypeStruct(route.shape, jnp.int32),
        compiler_params=_params(("parallel",)),
        name="slots",
    )(route, base)


def _pack_bf16_pairs(x):
    half = x.shape[1] // 2
    bits = pltpu.bitcast(x.astype(BF16).astype(F32), jnp.uint32)
    return (bits[:, 0:half] >> 16) | bits[:, half:2 * half]


def _unpack_bf16_pairs(w):
    lo = pltpu.bitcast(w << 16, F32).astype(BF16)
    hi = pltpu.bitcast(w & jnp.uint32(0xFFFF0000), F32).astype(BF16)
    return jnp.concatenate([lo, hi], axis=1)


def _dispatch_kernel(dest_ref, pad_end_ref, h_ref, xs_hbm, hbuf, zbuf, sem, zsem):
    i = pl.program_id(0)
    slot = i % 2
    tile = h_ref.shape[0]
    a0 = i * (tile * TOP_K)

    @pl.when(i == 0)
    def _():
        zbuf[...] = jnp.zeros(zbuf.shape, zbuf.dtype)
        fills =[pltpu.make_async_copy(zbuf, xs_hbm.at[pl.ds(pl.multiple_of(pad_end_ref[e] - MOE_BLK, MOE_BLK), MOE_BLK)],
                                       zsem) for e in range(N_EXPERTS)]
        has_rows = [pad_end_ref[e] > (pad_end_ref[e - 1] if e else 0) for e in range(N_EXPERTS)]
        for e in range(N_EXPERTS):
            @pl.when(has_rows[e])
            def _(e=e):
                fills[e].start()
        for e in range(N_EXPERTS):
            @pl.when(has_rows[e])
            def _(e=e):
                fills[e].wait()

        def tail(b):
            return pltpu.make_async_copy(zbuf, xs_hbm.at[pl.ds(pl.multiple_of(b * MOE_BLK, MOE_BLK), MOE_BLK)], zsem)

        def tail_start(b, c):
            tail(b).start()
            return c

        def tail_wait(b, c):
            tail(b).wait()
            return c

        first_unused = pad_end_ref[N_EXPERTS - 1] // MOE_BLK
        lax.fori_loop(first_unused, xs_hbm.shape[0] // MOE_BLK, tail_start, 0)
        lax.fori_loop(first_unused, xs_hbm.shape[0] // MOE_BLK, tail_wait, 0)

    hbuf[slot] = _pack_bf16_pairs(h_ref[...])

    def start(r, c):
        for k in range(TOP_K):
            pltpu.make_async_copy(hbuf.at[slot, pl.ds(r, 1)], xs_hbm.at[pl.ds(dest_ref[a0 + r * TOP_K + k], 1)],
                                  sem.at[slot]).start()
        return c

    lax.fori_loop(0, tile, start, 0, unroll=8)

    def drain(s):
        for k in range(TOP_K):
            pltpu.make_async_copy(hbuf.at[s], xs_hbm.at[pl.ds(0, tile)], sem.at[s]).wait()

    @pl.when(i > 0)
    def _():
        drain(1 - slot)

    @pl.when(i == pl.num_programs(0) - 1)
    def _():
        drain(slot)


def _dispatch(h_all, dest, pad_end, slots):
    T, D = h_all.shape
    tile = max(t for t in range(ROUTE_TILE, 4 * ROUTE_TILE + 1, 8) if T % t == 0)
    grid_spec = pltpu.PrefetchScalarGridSpec(
        num_scalar_prefetch=2,
        grid=(T // tile,),
        in_specs=[pl.BlockSpec((tile, D), lambda i, *_: (i, 0))],
        out_specs=pl.BlockSpec(memory_space=pl.ANY),
        scratch_shapes=[pltpu.VMEM((2, tile, D // 2), jnp.uint32), pltpu.VMEM((MOE_BLK, D // 2), jnp.uint32),
                        pltpu.SemaphoreType.DMA((2,)), pltpu.SemaphoreType.DMA],
    )
    return pl.pallas_call(
        _dispatch_kernel,
        grid_spec=grid_spec,
        out_shape=jax.ShapeDtypeStruct((slots, D // 2), jnp.uint32),
        compiler_params=_params(("arbitrary",)),
        name="dispatch",
    )(dest, pad_end, h_all)


def _moe_kernel(blk_e_ref, blk_n_ref, blk_x_ref, x_ref, wg_ref, wu_ref, wd_ref, y_ref, wg_b, wu_b, wd_b):
    del blk_x_ref
    i = pl.program_id(0)
    n_valid = blk_n_ref[i]

    @pl.when((i == 0) | (blk_e_ref[i] != blk_e_ref[jnp.maximum(i - 1, 0)]))
    def _():
        wg_b[...] = wg_ref[0].astype(BF16)
        wu_b[...] = wu_ref[0].astype(BF16)
        wd_b[...] = wd_ref[0].astype(BF16)

    @pl.when(n_valid > 0)
    def _():
        x = _unpack_bf16_pairs(x_ref[...])
        a = jnp.dot(x, wg_b[...], preferred_element_type=F32)
        u = jnp.dot(x, wu_b[...], preferred_element_type=F32)
        y_ref[...] = jnp.dot((_silu(a) * u).astype(BF16), wd_b[...], preferred_element_type=F32)

    @pl.when(n_valid == 0)
    def _():
        y_ref[...] = jnp.zeros(y_ref.shape, F32)


def _moe(xs, blk_e, blk_n, blk_x, w_gate, w_up, w_down):
    slots = xs.shape[0]
    D, De = w_gate.shape[-2:]
    wspec = lambda shape: pl.BlockSpec((1,) + shape, lambda i, be, *_: (be[i], 0, 0))
    grid_spec = pltpu.PrefetchScalarGridSpec(
        num_scalar_prefetch=3,
        grid=(slots // MOE_BLK,),
        in_specs=[pl.BlockSpec((MOE_BLK, xs.shape[1]), lambda i, be, bn, bx: (bx[i], 0)),
                  wspec((D, De)), wspec((D, De)), wspec((De, D))],
        out_specs=pl.BlockSpec((MOE_BLK, D), lambda i, *_: (i, 0)),
        scratch_shapes=[pltpu.VMEM((D, De), BF16), pltpu.VMEM((D, De), BF16), pltpu.VMEM((De, D), BF16)],
    )
    return pl.pallas_call(
        _moe_kernel,
        grid_spec=grid_spec,
        out_shape=jax.ShapeDtypeStruct((slots, D), F32),
        compiler_params=_params(("arbitrary",)),
        name="moe",
    )(blk_e, blk_n, blk_x, xs, w_gate, w_up, w_down)


def _final_ln_kernel(dest_ref, h_ref, route_ref, g_ref, b_ref, ys_hbm, o_ref, ybuf, sem, *, tile0):
    i = pl.program_id(0)
    slot = i % 2
    rows = h_ref.shape[0]

    def gather(tile, s):
        a0 = (tile0 + tile) * (rows * TOP_K)

        def start(r, c):
            for k in range(TOP_K):
                pltpu.make_async_copy(ys_hbm.at[pl.ds(dest_ref[a0 + r * TOP_K + k], 1)],
                                      ybuf.at[s, k, pl.ds(r, 1)], sem.at[s]).start()
            return c

        lax.fori_loop(0, rows, start, 0, unroll=8)

    @pl.when(i == 0)
    def _():
        gather(0, 0)

    @pl.when(i + 1 < pl.num_programs(0))
    def _():
        gather(i + 1, 1 - slot)

    for k in range(TOP_K):
        pltpu.make_async_copy(ys_hbm.at[pl.ds(0, rows)], ybuf.at[slot, k], sem.at[slot]).wait()
    route = route_ref[...]
    lane = lax.broadcasted_iota(jnp.int32, route.shape, 1)
    gate1 = jnp.sum(jnp.where(lane == 0, route, 0.0), axis=1, keepdims=True)
    gate2 = jnp.sum(jnp.where(lane == 1, route, 0.0), axis=1, keepdims=True)
    f = ybuf[slot, 0] * gate1 + ybuf[slot, 1] * gate2
    o_ref[...] = _layer_norm(DN_ALPHA * h_ref[...] + f, g_ref[...], b_ref[...])


def _final_ln(h_all, route, dest, ys, g, b, row0, rows):
    D = h_all.shape[1]
    tile = max(t for t in (ROUTE_TILE, 2 * ROUTE_TILE) if rows % t == 0 and row0 % t == 0)
    tile0 = row0 // tile
    row = lambda i, *_: (tile0 + i, 0)
    fix = lambda i, *_: (0, 0)
    grid_spec = pltpu.PrefetchScalarGridSpec(
        num_scalar_prefetch=1,
        grid=(rows // tile,),
        in_specs=[pl.BlockSpec((tile, D), row), pl.BlockSpec((tile, LANES), row),
                  pl.BlockSpec((1, D), fix), pl.BlockSpec((1, D), fix), pl.BlockSpec(memory_space=pl.ANY)],
        out_specs=pl.BlockSpec((tile, D), lambda i, *_: (i, 0)),
        scratch_shapes=[pltpu.VMEM((2, TOP_K, tile, D), F32), pltpu.SemaphoreType.DMA((2,))],
    )
    return pl.pallas_call(
        functools.partial(_final_ln_kernel, tile0=tile0),
        grid_spec=grid_spec,
        out_shape=jax.ShapeDtypeStruct((rows, D), F32),
        compiler_params=_params(("arbitrary",)),
        name="final_ln",
    )(dest, h_all, route, g, b, ys)


def kernel(x_prompt, x_sample, cache_a_k, cache_a_v, state_b_ssm, state_b_conv, w_in, rel_bias, conv_w, a_log, dt_bias, o_norm_g, w_out, ln1_g, ln1_b, w_group, b_group, w_router, b_router, w_gate, w_up, w_down, ln2_g, ln2_b):
    B, S, D = x_prompt.shape
    N, T = x_sample.shape[0], x_sample.shape[1]
    depth = w_in.shape[0]
    assert depth == 1 and T == 1 and S % ATT_TILE == 0 and N % ROUTE_TILE == 0 and cache_a_k.shape[2] % LANES == 0
    l = 0
    win_p = min(BRANCHES[-1][0], S)

    w_pad32 = jnp.pad(w_in[l], ((0, 0), (0, IN_COLS_PAD - IN_COLS)))
    w_pad = w_pad32.astype(BF16)
    wo_b = w_out[l].astype(BF16)
    wr = jnp.pad(jnp.concatenate([w_group[l], w_router[l]], axis=1), ((0, 0), (0, LANES - N_GROUPS - N_EXPERTS)))
    br = jnp.pad(jnp.concatenate([b_group[l], b_router[l].reshape(-1)]), (0, LANES - N_GROUPS - N_EXPERTS))[None, :]
    g1, b1 = ln1_g[l][None, :], ln1_b[l][None, :]
    g2, b2 = ln2_g[l][None, :], ln2_b[l][None, :]

    xp = x_prompt.reshape(B * S, D)
    qkv_p, ub_p, zb_p, ab_p, k_win, v_win = _proj(xp, w_pad, 512, seq=S, win=win_p)
    oa_p = _attn(qkv_p, _band_bias(rel_bias), B, S)
    ob_p, st_p = _delta(ub_p, zb_p, ab_p, conv_w[l], a_log[l], dt_bias[l], o_norm_g[l], B, S)
    rows_all = B * S + N
    routed_p = _mix_ln(oa_p.reshape(B * S, A_WIDTH), ob_p.reshape(B * S, B_WIDTH), xp, wo_b, g1, b1, wr, br,
                       512, rows_all, 0)

    xs = x_sample.reshape(N, D)
    qkv_s, ub_s, zb_s, ab_s = _proj(xs, w_pad32, N)
    nb = 2
    qkv_t = jnp.transpose(qkv_s.reshape(N, 3, A_HEADS, A_HEAD_DIM), (1, 2, 3, 0))
    oa_s = _attn_dec(qkv_t, jnp.transpose(cache_a_k[l], (0, 2, 3, 1)), jnp.transpose(cache_a_v[l], (0, 2, 3, 1)),
                     _cache_bias(rel_bias, cache_a_k.shape[2]), nb)
    oa_s = jnp.transpose(oa_s, (0, 3, 1, 2)).reshape(N, A_WIDTH)
    ob_s, st_s = _delta_dec(ub_s, state_b_conv[l], zb_s, ab_s, conv_w[l], a_log[l], dt_bias[l], o_norm_g[l],
                            state_b_ssm[l], 16)
    h_all, route, cnt = _mix_ln(oa_s, ob_s, xs, w_out[l], g1, b1, wr, br, N, rows_all, B * S, prev=routed_p)

    base, blk_e, blk_n, blk_x, pad_end, n_blocks = _slot_layout(cnt)
    dest = _slots(route, base)[:, 0:TOP_K].reshape(-1)
    xs = _dispatch(h_all, dest, pad_end, n_blocks * MOE_BLK)
    ys = _moe(xs, blk_e, blk_n, blk_x, w_gate[l], w_up[l], w_down[l])
    y_p = _final_ln(h_all, route, dest, ys, g2, b2, 0, B * S)
    y_s = _final_ln(h_all, route, dest, ys, g2, b2, B * S, N)

    to_rows = lambda t: jnp.transpose(t, (0, 3, 1, 2))[None]
    conv_p = ub_p.reshape(B, S, CONV_DIM)[:, S - (CONV_WIDTH - 1):]
    conv_s = jnp.concatenate([state_b_conv[l], ub_s[:, None, :]], axis=1)[:, T:]
    new_kv = lambda t: jnp.transpose(t, (2, 0, 1))[None, :, None]
    return (y_p.reshape(B, S, D), y_s.reshape(N, T, D), to_rows(k_win), to_rows(v_win),
            new_kv(qkv_t[1]), new_kv(qkv_t[2]), st_p[None], st_s[None], conv_p[None], conv_s[None])
```

```python
import functools
import math

import jax
import jax.numpy as jnp
from jax import lax
from jax.experimental import pallas as pl
from jax.experimental.pallas import tpu as pltpu

F32 = jnp.float32
BF16 = jnp.bfloat16
HIGHEST = lax.Precision.HIGHEST

LANES = 128
A_HEADS = 8
A_HEAD_DIM = 64
A_WIDTH = A_HEADS * A_HEAD_DIM
BRANCHES = ((128, 1), (512, 4), (2048, 16))
BAND = 128
ATT_TILE = BAND * 16
REL_BUCKETS = 32
REL_MAX_DIST = 2048
B_HEADS = 4
B_HEAD_DIM = 128
B_WIDTH = B_HEADS * B_HEAD_DIM
CONV_WIDTH = 4
CONV_DIM = 3 * B_WIDTH
CHUNK = 64
COL_UB = 3 * A_WIDTH
COL_ZB = COL_UB + CONV_DIM
COL_AB = COL_ZB + B_WIDTH
IN_COLS = COL_AB + 2 * B_HEADS
IN_COLS_PAD = COL_AB + LANES
N_GROUPS = 4
EXPERTS_PER_GROUP = 8
N_EXPERTS = N_GROUPS * EXPERTS_PER_GROUP
TOP_K = 2
DN_ALPHA = 2.0 ** 0.25
LN_EPS = 1e-5
RMS_EPS = 1e-6
MASKED = -1e30
MOE_BLK = 512
ROUTE_TILE = 128
DELTA_TILE = 1024
ATT_GROUP = 16
VMEM_LIMIT = 56 * 1024 * 1024


def _bdot(a, b):
    return jnp.dot(a.astype(BF16), b.astype(BF16), preferred_element_type=F32)


def _bdot_nt(a, b):
    return lax.dot_general(a.astype(BF16), b.astype(BF16), (((1,), (1,)), ((), ())), preferred_element_type=F32)


def _bdot_tn(a, b):
    return lax.dot_general(a.astype(BF16), b.astype(BF16), (((0,), (0,)), ((), ())), preferred_element_type=F32)


def _fdot(a, b):
    return jnp.dot(a, b, precision=HIGHEST, preferred_element_type=F32)


def _sigmoid(x):
    return 1.0 / (1.0 + jnp.exp(-x))


def _silu(x):
    return x * _sigmoid(x)


def _softplus(x):
    return jnp.maximum(x, 0.0) + jnp.log(1.0 + jnp.exp(-jnp.abs(x)))


def _params(sem):
    return pltpu.CompilerParams(dimension_semantics=sem, vmem_limit_bytes=VMEM_LIMIT)


def _proj_kernel(x_ref, w_ref, qkv_ref, ub_ref, zb_ref, ab_ref, *win_refs, tiles_per_seq=0, first_win_tile=0):
    if w_ref.dtype == BF16:
        xb = x_ref[...].astype(BF16)
        dot = lambda w: jnp.dot(xb, w, preferred_element_type=F32)
    else:
        dot = lambda w: _fdot(x_ref[...], w)
    qkv = dot(w_ref[:, 0:COL_UB])
    qkv_ref[...] = qkv
    ub_ref[...] = dot(w_ref[:, COL_UB:COL_ZB])
    zb_ref[...] = dot(w_ref[:, COL_ZB:COL_AB])
    ab_ref[...] = dot(w_ref[:, COL_AB:IN_COLS_PAD])
    if win_refs:
        @pl.when(pl.program_id(0) % tiles_per_seq >= first_win_tile)
        def _():
            tm = qkv.shape[0]
            for j, ref in enumerate(win_refs):
                cols = qkv[:, (1 + j) * A_WIDTH:(2 + j) * A_WIDTH]
                ref[0] = jnp.transpose(cols).reshape(A_HEADS, A_HEAD_DIM, tm)


def _proj(x2d, w_pad, tm, seq=None, win=None):
    T, D = x2d.shape
    row = lambda i: (i, 0)
    out_specs = [pl.BlockSpec((tm, COL_UB), row), pl.BlockSpec((tm, CONV_DIM), row),
                 pl.BlockSpec((tm, B_WIDTH), row), pl.BlockSpec((tm, LANES), row)]
    out_shape = [jax.ShapeDtypeStruct((T, COL_UB), F32), jax.ShapeDtypeStruct((T, CONV_DIM), F32),
                 jax.ShapeDtypeStruct((T, B_WIDTH), F32), jax.ShapeDtypeStruct((T, LANES), F32)]
    body = _proj_kernel
    if win:
        per_seq, first = seq // tm, (seq - win) // tm
        wspec = pl.BlockSpec((1, A_HEADS, A_HEAD_DIM, tm),
                             lambda i: (i // per_seq, 0, 0, jnp.maximum(i % per_seq - first, 0)))
        out_specs += [wspec, wspec]
        out_shape += [jax.ShapeDtypeStruct((T // seq, A_HEADS, A_HEAD_DIM, win), F32)] * 2
        body = functools.partial(_proj_kernel, tiles_per_seq=per_seq, first_win_tile=first)
    return pl.pallas_call(
        body,
        grid=(T // tm,),
        in_specs=[pl.BlockSpec((tm, D), row), pl.BlockSpec((D, IN_COLS_PAD), lambda i: (0, 0))],
        out_specs=out_specs,
        out_shape=out_shape,
        compiler_params=_params(("arbitrary",)),
        name="proj",
    )(x2d, w_pad)


def _rel_bucket(dist):
    max_exact = REL_BUCKETS // 2
    n = jnp.maximum(dist, 0)
    ratio = jnp.maximum(n, 1).astype(F32) / max_exact
    large = max_exact + (jnp.log(ratio) / math.log(REL_MAX_DIST / max_exact)
                         * (REL_BUCKETS - max_exact)).astype(jnp.int32)
    return jnp.where(n < max_exact, n, jnp.minimum(large, REL_BUCKETS - 1))


def _bias_of(rel_bias, dist):
    onehot = (_rel_bucket(dist)[None, :] == jnp.arange(REL_BUCKETS)[:, None]).astype(F32)
    return jnp.dot(rel_bias.astype(F32).T, onehot, precision=HIGHEST)


def _band_bias(rel_bias):
    period = 3 * BAND
    tabs = []
    for _, dil in BRANCHES:
        g = jnp.concatenate([_bias_of(rel_bias, (BAND - jnp.arange(BAND + 1)) * dil),
                             jnp.full((A_HEADS, period - BAND - 1), MASKED, F32)], axis=1)
        skew = jnp.tile(g, (1, BAND))[:, :BAND * (period - 1)].reshape(A_HEADS, BAND, period - 1)
        tabs.append(skew[:, :, :2 * BAND])
    return jnp.stack(tabs)


def _cache_bias(rel_bias, P):
    dist = P - jnp.arange(P + LANES)
    tabs = []
    for window, dil in BRANCHES:
        ok = (dist >= 0) & (dist <= window) & (dist % dil == 0)
        tabs.append(jnp.where(ok[None, :], _bias_of(rel_bias, dist), MASKED)[:, None, :])
    return jnp.stack(tabs)


def _attn_kernel(q_ref, k_ref, v_ref, bias_ref, o_ref, acc_ref, m_ref, l_ref):
    t = pl.program_id(2)
    tile0 = t * ATT_TILE
    lane = lax.broadcasted_iota(jnp.int32, (BAND, LANES), 1)
    head0 = lane < A_HEAD_DIM

    def rows(start, dil):
        return pl.ds(pl.multiple_of(start, BAND), BAND) if dil == 1 else pl.ds(start, BAND, stride=dil)

    for br, (_, dil) in enumerate(BRANCHES):
        span = BAND * dil

        def blocks(it, carry, br=br, dil=dil, span=span):
            gs = range(ATT_GROUP)
            idx = [it * ATT_GROUP + g for g in gs]
            start = [(i % dil) + (i // dil) * span for i in idx]
            cur = [tile0 + s for s in start]
            first = [c < span for c in cur]
            prev = [jnp.where(first[g], cur[g], cur[g] - span) for g in gs]
            q = [q_ref[0, rows(start[g], dil), :] * (A_HEAD_DIM ** -0.5) for g in gs]
            kk = [jnp.concatenate([k_ref[0, rows(prev[g], dil), :], k_ref[0, rows(cur[g], dil), :]],
                                  axis=0).astype(BF16) for g in gs]
            vv = [jnp.concatenate([v_ref[0, rows(prev[g], dil), :], v_ref[0, rows(cur[g], dil), :]],
                                  axis=0).astype(BF16) for g in gs]
            pen = [jnp.where(first[g], MASKED, 0.0) for g in gs]
            gh = [(g, hh) for g in gs for hh in range(2)]
            qh = [jnp.where(head0 if hh == 0 else ~head0, q[g], 0.0).astype(BF16) for g, hh in gh]
            s = [lax.dot_general(qh[j], kk[g], (((1,), (1,)), ((), ())), preferred_element_type=F32)
                 for j, (g, hh) in enumerate(gh)]
            s = [s[j] + bias_ref[br, hh] for j, (g, hh) in enumerate(gh)]
            s = [jnp.concatenate([s[j][:, 0:BAND] + pen[g], s[j][:, BAND:2 * BAND]], axis=1)
                 for j, (g, hh) in enumerate(gh)]
            m = [jnp.max(t, axis=1, keepdims=True) for t in s]
            p = [jnp.exp(s[j] - m[j]) for j in range(len(gh))]
            l = [jnp.sum(t, axis=1, keepdims=True) for t in p]
            pv = [jnp.dot(p[j].astype(BF16), vv[g], preferred_element_type=F32) for j, (g, hh) in enumerate(gh)]
            for g in gs:
                acc_ref[br, rows(start[g], dil), :] = jnp.where(head0, pv[2 * g], pv[2 * g + 1])
                m_ref[br, rows(start[g], dil), :] = jnp.where(head0, m[2 * g], m[2 * g + 1])
                l_ref[br, rows(start[g], dil), :] = jnp.where(head0, l[2 * g], l[2 * g + 1])
            return carry

        lax.fori_loop(0, ATT_TILE // BAND // ATT_GROUP, blocks, 0)

    def merge(c, carry):
        r = pl.ds(pl.multiple_of(c * 256, 256), 256)
        m0, m1, m2 = m_ref[0, r, :], m_ref[1, r, :], m_ref[2, r, :]
        mx = jnp.maximum(jnp.maximum(m0, m1), m2)
        w0, w1, w2 = jnp.exp(m0 - mx), jnp.exp(m1 - mx), jnp.exp(m2 - mx)
        num = w0 * acc_ref[0, r, :] + w1 * acc_ref[1, r, :] + w2 * acc_ref[2, r, :]
        den = w0 * l_ref[0, r, :] + w1 * l_ref[1, r, :] + w2 * l_ref[2, r, :]
        o_ref[0, r, :] = num / den
        return carry

    lax.fori_loop(0, ATT_TILE // 256, merge, 0)


def _attn(qkv, bias, B, S):
    n_pairs = A_HEADS // 2
    qkv3 = qkv.reshape(B, S, 3 * A_WIDTH)
    return pl.pallas_call(
        _attn_kernel,
        grid=(B, n_pairs, S // ATT_TILE),
        in_specs=[pl.BlockSpec((1, ATT_TILE, LANES), lambda b, hp, t: (b, t, hp)),
                  pl.BlockSpec((1, S, LANES), lambda b, hp, t: (b, 0, n_pairs + hp)),
                  pl.BlockSpec((1, S, LANES), lambda b, hp, t: (b, 0, 2 * n_pairs + hp)),
                  pl.BlockSpec((3, 2, BAND, 2 * BAND), lambda b, hp, t: (0, hp, 0, 0))],
        out_specs=pl.BlockSpec((1, ATT_TILE, LANES), lambda b, hp, t: (b, t, hp)),
        out_shape=jax.ShapeDtypeStruct((B, S, A_WIDTH), F32),
        scratch_shapes=[pltpu.VMEM((3, ATT_TILE, LANES), F32)] * 3,
        compiler_params=_params(("parallel", "parallel", "arbitrary")),
        name="attn",
    )(qkv3, qkv3, qkv3, bias)


def _attn_dec_kernel(qkv_ref, kt_ref, vt_ref, bias_ref, o_ref, *, nb):
    i = pl.program_id(0)
    N = qkv_ref.shape[-1]
    P = kt_ref.shape[-1]
    lane_n = lax.broadcasted_iota(jnp.int32, (A_HEAD_DIM, N), 1)
    lane_o = lax.broadcasted_iota(jnp.int32, (A_HEAD_DIM, nb), 1)
    lane_t = lax.broadcasted_iota(jnp.int32, (1, LANES), 1)

    def head(h, carry):
        slab = jnp.zeros((A_HEAD_DIM, nb), F32)
        for j in range(nb):
            pick = lane_n == i * nb + j
            col = lambda t: jnp.sum(jnp.where(pick, t, 0.0), axis=1, keepdims=True)
            q = col(qkv_ref[0, h]) * (A_HEAD_DIM ** -0.5)
            k_new, v_new = col(qkv_ref[1, h]), col(qkv_ref[2, h])
            s_new = jnp.sum(q * k_new, axis=0, keepdims=True)
            s = jnp.concatenate([jnp.sum(kt_ref[j, h] * q, axis=0, keepdims=True),
                                 jnp.where(lane_t == 0, s_new, 0.0)], axis=1)
            ps, ms, ls = [], [], []
            for br in range(3):
                sb = s + bias_ref[br, h]
                m = jnp.max(sb, axis=1, keepdims=True)
                p = jnp.exp(sb - m)
                ps.append(p)
                ms.append(m)
                ls.append(jnp.sum(p, axis=1, keepdims=True))
            mx = jnp.maximum(jnp.maximum(ms[0], ms[1]), ms[2])
            w = jnp.zeros((1, P + LANES), F32)
            den = jnp.zeros((1, 1), F32)
            for p, m, l in zip(ps, ms, ls):
                e = jnp.exp(m - mx)
                w = w + e * p
                den = den + e * l
            o = jnp.sum(vt_ref[j, h] * w[:, 0:P], axis=1, keepdims=True) + v_new * w[:, P:P + 1]
            slab = jnp.where(lane_o == j, o / den, slab)
        o_ref[0, h] = slab
        return carry

    lax.fori_loop(0, A_HEADS, head, 0, unroll=2)


def _attn_dec(qkv_t, cache_kt, cache_vt, bias, nb):
    N, P = cache_kt.shape[0], cache_kt.shape[-1]
    cache_spec = pl.BlockSpec((nb, A_HEADS, A_HEAD_DIM, P), lambda i: (i, 0, 0, 0))
    return pl.pallas_call(
        functools.partial(_attn_dec_kernel, nb=nb),
        grid=(N // nb,),
        in_specs=[pl.BlockSpec((3, A_HEADS, A_HEAD_DIM, N), lambda i: (0, 0, 0, 0)), cache_spec, cache_spec,
                  pl.BlockSpec((3, A_HEADS, 1, P + LANES), lambda i: (0, 0, 0, 0))],
        out_specs=pl.BlockSpec((1, A_HEADS, A_HEAD_DIM, nb), lambda i: (i, 0, 0, 0)),
        out_shape=jax.ShapeDtypeStruct((N // nb, A_HEADS, A_HEAD_DIM, nb), F32),
        compiler_params=_params(("parallel",)),
        name="attn_dec",
    )(qkv_t, cache_kt, cache_vt, bias)


def _split3(x):
    hi = x.astype(BF16)
    r = x - hi.astype(F32)
    mid = r.astype(BF16)
    return hi, mid, (r - mid.astype(F32)).astype(BF16)


def _tril_dot(tril_b, g):
    return sum(jnp.dot(tril_b, piece, preferred_element_type=F32) for piece in _split3(g))


def _dot3(a, b):
    ah, bh = a.astype(BF16), b.astype(BF16)
    al, bl = (a - ah.astype(F32)).astype(BF16), (b - bh.astype(F32)).astype(BF16)
    d = lambda x, y: jnp.dot(x, y, preferred_element_type=F32)
    return d(ah, bh) + (d(ah, bl) + d(al, bh))


def _delta_kernel(alog_ref, dtb_ref, ub_ref, z_ref, ab_ref, cw_ref, og_ref, o_ref, st_ref,
                  pad_ref, u_s, wq_s, kt_s, qk_s, gl_s, a_s, rhs_s):
    t = pl.program_id(1)
    TS = ub_ref.shape[1]
    nch = TS // CHUNK
    hdr = 8

    slabs = range(CONV_DIM // LANES)
    seq_rows = lambda first, n: pl.ds(2 * first, n, stride=2)

    @pl.when(t == 0)
    def _():
        for s in slabs:
            pad_ref[s, seq_rows(0, hdr), :] = jnp.zeros((hdr, LANES), F32)
        st_ref[...] = jnp.zeros(st_ref.shape, F32)

    @pl.when(t > 0)
    def _():
        for s in slabs:
            pad_ref[s, seq_rows(0, hdr), :] = pad_ref[s, seq_rows(TS, hdr), :]

    for s in slabs:
        pad_ref[s, seq_rows(hdr, TS), :] = ub_ref[0, :, s * LANES:(s + 1) * LANES]

    ri = lax.broadcasted_iota(jnp.int32, (CHUNK, CHUNK), 0)
    ci = lax.broadcasted_iota(jnp.int32, (CHUNK, CHUNK), 1)
    incl = ri >= ci
    strict = ri > ci
    tril_b = incl.astype(BF16)
    eye = (ri == ci).astype(F32)
    lane = lax.broadcasted_iota(jnp.int32, (CHUNK, LANES), 1)

    def local(c, carry):
        base = c * CHUNK if isinstance(c, int) else pl.multiple_of(c * CHUNK, CHUNK)
        ab = ab_ref[0, pl.ds(base, CHUNK), :]
        heads = []
        for h in range(B_HEADS):
            def conv(col):
                first = c * CHUNK + (hdr - (CONV_WIDTH - 1))
                acc = pad_ref[col // LANES, seq_rows(first, CHUNK), :] * cw_ref[0:1, col:col + LANES]
                for i in range(1, CONV_WIDTH):
                    acc = acc + pad_ref[col // LANES, seq_rows(first + i, CHUNK), :] * cw_ref[i:i + 1, col:col + LANES]
                return _silu(acc)

            cq, ck, v = conv(h * LANES), conv(B_WIDTH + h * LANES), conv(2 * B_WIDTH + h * LANES)
            q = cq * lax.rsqrt(jnp.sum(cq * cq, axis=1, keepdims=True) + 1e-6) * (B_HEAD_DIM ** -0.5)
            k = ck * lax.rsqrt(jnp.sum(ck * ck, axis=1, keepdims=True) + 1e-6)
            a_raw = jnp.sum(jnp.where(lane == h, ab, 0.0), axis=1, keepdims=True)
            b_raw = jnp.sum(jnp.where(lane == h + B_HEADS, ab, 0.0), axis=1, keepdims=True)
            neg_a = -jnp.exp(jnp.full((1, LANES), alog_ref[h], F32))
            g = neg_a * _softplus(a_raw + dtb_ref[h])
            beta = _sigmoid(b_raw)
            heads.append((q, k, v, g, beta))
        hs = range(B_HEADS)
        q, k, v, g, beta = zip(*heads)
        gc = [_tril_dot(tril_b, g[h]) for h in hs]
        dmat = [_tril_dot(tril_b, jnp.where(strict, g[h][:, 0:CHUNK], 0.0)) for h in hs]
        kq = [_bdot_nt(jnp.concatenate([k[h], q[h]], axis=0), k[h]) for h in hs]
        decay = [jnp.where(incl, jnp.exp(dmat[h]), 0.0) for h in hs]
        e_gc = [jnp.exp(gc[h]) for h in hs]
        for h in hs:
            gc_last = gc[h][CHUNK - 1:CHUNK, :]
            a_s[h, c] = jnp.where(strict, beta[h] * kq[h][0:CHUNK] * decay[h], 0.0)
            rhs_s[h, c] = jnp.concatenate([v[h] * beta[h], k[h] * (beta[h] * e_gc[h])], axis=1)
            wq_s[h, c, CHUNK:2 * CHUNK] = q[h] * e_gc[h]
            kt_s[h, c] = k[h] * jnp.exp(gc_last - gc[h])
            qk_s[h, c] = kq[h][CHUNK:2 * CHUNK] * decay[h]
            gl_s[h, c] = jnp.broadcast_to(jnp.exp(gc_last), (8, LANES))
        return carry

    def solve(c, carry):
        hs = range(B_HEADS)
        a = [a_s[h, c] for h in hs]
        x = [eye - a[h] for h in hs]
        p = [_bdot(a[h], a[h]) for h in hs]
        for _ in range(int(math.log2(CHUNK)) - 2):
            r = [_bdot(jnp.concatenate([x[h], p[h]], axis=0), p[h]) for h in hs]
            x = [x[h] + r[h][0:CHUNK] for h in hs]
            p = [r[h][CHUNK:2 * CHUNK] for h in hs]
        x = [x[h] + _bdot(x[h], p[h]) for h in hs]
        sol = [_bdot(x[h], rhs_s[h, c]) for h in hs]
        for h in hs:
            u_s[h, c] = sol[h][:, 0:LANES]
            wq_s[h, c, 0:CHUNK] = sol[h][:, LANES:2 * LANES]
        return carry

    def scan(c, carry):
        base = c * CHUNK if isinstance(c, int) else pl.multiple_of(c * CHUNK, CHUNK)
        hs = range(B_HEADS)
        state = [st_ref[0, h] for h in hs]
        r = [_bdot(wq_s[h, c], state[h]) for h in hs]
        v_new = [u_s[h, c] - r[h][0:CHUNK] for h in hs]
        upd = [_bdot_tn(kt_s[h, c], v_new[h]) for h in hs]
        out = [r[h][CHUNK:2 * CHUNK] + _bdot(qk_s[h, c], v_new[h]) for h in hs]
        for h in hs:
            st_ref[0, h] = state[h] * gl_s[h, c][0:1, :] + upd[h]
            o = out[h] * lax.rsqrt(jnp.mean(out[h] * out[h], axis=1, keepdims=True) + RMS_EPS) * og_ref[...]
            cols = slice(h * LANES, (h + 1) * LANES)
            o_ref[0, pl.ds(base, CHUNK), cols] = o * _silu(z_ref[0, pl.ds(base, CHUNK), cols])
        return carry

    def stages(c, carry):
        scan(c - 1, carry)
        solve(c, carry)
        return local(c + 1, carry)

    assert nch >= 3
    local(0, 0)
    solve(0, 0)
    local(1, 0)
    lax.fori_loop(1, nch - 1, stages, 0)
    scan(nch - 2, 0)
    solve(nch - 1, 0)
    scan(nch - 1, 0)


def _delta(ub, zb, ab, conv_w, a_log, dt_bias, o_norm_g, B, S):
    ts = min(S, DELTA_TILE)
    nch = ts // CHUNK
    seq = lambda width: pl.BlockSpec((1, ts, width), lambda b, t, *_: (b, t, 0))
    fix = lambda shape: pl.BlockSpec(shape, lambda b, t, *_: (0, 0))
    per_chunk = lambda rows, width: pltpu.VMEM((B_HEADS, nch, rows, width), F32)
    grid_spec = pltpu.PrefetchScalarGridSpec(
        num_scalar_prefetch=2,
        grid=(B, S // ts),
        in_specs=[seq(CONV_DIM), seq(B_WIDTH), seq(LANES), fix((CONV_WIDTH, CONV_DIM)), fix((1, LANES))],
        out_specs=[seq(B_WIDTH),
                   pl.BlockSpec((1, B_HEADS, B_HEAD_DIM, B_HEAD_DIM), lambda b, t, *_: (b, 0, 0, 0))],
        scratch_shapes=[pltpu.VMEM((CONV_DIM // LANES, 2 * (ts + 8), LANES), F32),
                        per_chunk(CHUNK, LANES), per_chunk(2 * CHUNK, LANES),
                        per_chunk(CHUNK, LANES), per_chunk(CHUNK, CHUNK), per_chunk(8, LANES),
                        per_chunk(CHUNK, CHUNK), per_chunk(CHUNK, 2 * LANES)],
    )
    return pl.pallas_call(
        _delta_kernel,
        grid_spec=grid_spec,
        out_shape=[jax.ShapeDtypeStruct((B, S, B_WIDTH), F32),
                   jax.ShapeDtypeStruct((B, B_HEADS, B_HEAD_DIM, B_HEAD_DIM), F32)],
        compiler_params=_params(("parallel", "arbitrary")),
        name="delta",
    )(a_log, dt_bias, ub.reshape(B, S, CONV_DIM), zb.reshape(B, S, B_WIDTH), ab.reshape(B, S, LANES),
      conv_w, o_norm_g.reshape(1, LANES))


def _delta_dec_kernel(alog_ref, dtb_ref, ub_ref, cs_ref, zb_ref, ab_ref, w_ref, og_ref, st_ref,
                      o_ref, so_ref, *, nb):
    i = pl.program_id(0)
    N = ub_ref.shape[0]
    acc = ub_ref[...] * w_ref[CONV_WIDTH - 1:CONV_WIDTH, :]
    for t in range(CONV_WIDTH - 1):
        acc = acc + cs_ref[t] * w_ref[t:t + 1, :]
    c = _silu(acc)
    ab = ab_ref[...]
    lane = lax.broadcasted_iota(jnp.int32, (N, LANES), 1)
    samp = lax.broadcasted_iota(jnp.int32, (B_HEAD_DIM, N), 1)
    row_id = lax.broadcasted_iota(jnp.int32, (N, LANES), 0)
    out_row = lax.broadcasted_iota(jnp.int32, (nb, LANES), 0)
    heads = []
    for h in range(B_HEADS):
        cq = c[:, h * LANES:(h + 1) * LANES]
        ck = c[:, B_WIDTH + h * LANES:B_WIDTH + (h + 1) * LANES]
        v = c[:, 2 * B_WIDTH + h * LANES:2 * B_WIDTH + (h + 1) * LANES]
        q = cq * lax.rsqrt(jnp.sum(cq * cq, axis=1, keepdims=True) + 1e-6) * (B_HEAD_DIM ** -0.5)
        k = ck * lax.rsqrt(jnp.sum(ck * ck, axis=1, keepdims=True) + 1e-6)
        a_raw = jnp.sum(jnp.where(lane == h, ab, 0.0), axis=1, keepdims=True)
        b_raw = jnp.sum(jnp.where(lane == h + B_HEADS, ab, 0.0), axis=1, keepdims=True)
        neg_a = -jnp.exp(jnp.full((1, 1), alog_ref[h], F32))
        dec = jnp.exp(neg_a * _softplus(a_raw + dtb_ref[h]))
        beta = _sigmoid(b_raw)
        heads.append((q.T, k.T, v, jnp.broadcast_to(dec, (N, LANES)), jnp.broadcast_to(beta, (N, LANES)),
                      zb_ref[:, h * LANES:(h + 1) * LANES]))

    hs = range(B_HEADS)
    q_t, k_t, v, dec, beta, z = zip(*heads)

    def sample(j, o_acc):
        n = i * nb + j
        pick = samp == n
        k_col = [jnp.sum(jnp.where(pick, k_t[h], 0.0), axis=1, keepdims=True) for h in hs]
        q_col = [jnp.sum(jnp.where(pick, q_t[h], 0.0), axis=1, keepdims=True) for h in hs]
        pick_r = row_id == n
        row = lambda t: jnp.sum(jnp.where(pick_r, t, 0.0), axis=0, keepdims=True)
        st = [st_ref[j, h] * row(dec[h]) for h in hs]
        mem = [jnp.sum(k_col[h] * st[h], axis=0, keepdims=True) for h in hs]
        st = [st[h] + k_col[h] * ((row(v[h]) - mem[h]) * row(beta[h])) for h in hs]
        o = [jnp.sum(q_col[h] * st[h], axis=0, keepdims=True) for h in hs]
        o = [o[h] * lax.rsqrt(jnp.mean(o[h] * o[h], axis=1, keepdims=True) + RMS_EPS) * og_ref[...] for h in hs]
        for h in hs:
            so_ref[j, h] = st[h]
        return tuple(jnp.where(out_row == j, o[h] * _silu(row(z[h])), o_acc[h]) for h in hs)

    outs = lax.fori_loop(0, nb, sample, tuple(jnp.zeros((nb, LANES), F32) for _ in hs), unroll=2)
    for h in hs:
        o_ref[:, h * LANES:(h + 1) * LANES] = outs[h]


def _delta_dec(ub_s, conv_state, zb_s, ab_s, conv_w, a_log, dt_bias, o_norm_g, state, nb):
    N = ub_s.shape[0]
    full2 = lambda shape: pl.BlockSpec(shape, lambda i, *_: (0, 0))
    grid_spec = pltpu.PrefetchScalarGridSpec(
        num_scalar_prefetch=2,
        grid=(N // nb,),
        in_specs=[full2((N, CONV_DIM)),
                  pl.BlockSpec((CONV_WIDTH - 1, N, CONV_DIM), lambda i, *_: (0, 0, 0)),
                  full2((N, B_WIDTH)), full2((N, LANES)), full2((CONV_WIDTH, CONV_DIM)), full2((1, LANES)),
                  pl.BlockSpec((nb, B_HEADS, B_HEAD_DIM, B_HEAD_DIM), lambda i, *_: (i, 0, 0, 0))],
        out_specs=[pl.BlockSpec((nb, B_WIDTH), lambda i, *_: (i, 0)),
                   pl.BlockSpec((nb, B_HEADS, B_HEAD_DIM, B_HEAD_DIM), lambda i, *_: (i, 0, 0, 0))],
    )
    return pl.pallas_call(
        functools.partial(_delta_dec_kernel, nb=nb),
        grid_spec=grid_spec,
        out_shape=[jax.ShapeDtypeStruct((N, B_WIDTH), F32), jax.ShapeDtypeStruct(state.shape, F32)],
        compiler_params=_params(("parallel",)),
        name="delta_dec",
    )(a_log, dt_bias, ub_s, jnp.swapaxes(conv_state, 0, 1), zb_s, ab_s, conv_w, o_norm_g.reshape(1, LANES), state)


def _layer_norm(r, g, b):
    mu = jnp.mean(r, axis=1, keepdims=True)
    d = r - mu
    var = jnp.mean(d * d, axis=1, keepdims=True)
    return d * lax.rsqrt(var + LN_EPS) * g + b


def _mix_ln_kernel(*refs, steps):
    outs = refs[-3:]

    @pl.when(pl.program_id(0) < steps)
    def _():
        _mix_ln_rows(*refs[:8], *outs)

    @pl.when(pl.program_id(0) >= steps)
    def _():
        for o in outs:
            o[...] = jnp.zeros(o.shape, F32)


def _mix_ln_rows(oa_ref, ob_ref, x_ref, wo_ref, g_ref, b_ref, wr_ref, br_ref, h_ref, route_ref, cnt_ref):
    if wo_ref.dtype == BF16:
        dot = lambda a, w: jnp.dot(a.astype(BF16), w, preferred_element_type=F32)
    else:
        dot = _fdot
    y = dot(oa_ref[...], wo_ref[0:A_WIDTH, :]) + dot(ob_ref[...], wo_ref[A_WIDTH:A_WIDTH + B_WIDTH, :])
    hcur = _layer_norm(DN_ALPHA * x_ref[...] + y, g_ref[...], b_ref[...])
    h_ref[...] = hcur
    logits = (_bdot if wo_ref.dtype == BF16 else _dot3)(hcur, wr_ref[...]) + br_ref[...]
    lane = lax.broadcasted_iota(jnp.int32, logits.shape, 1)
    lane_f = lane.astype(F32)
    ninf = -jnp.inf
    big = 1e9
    gl = jnp.where(lane < N_GROUPS, logits, ninf)
    gmax = jnp.max(gl, axis=1, keepdims=True)
    g_idx = jnp.min(jnp.where(gl == gmax, lane_f, big), axis=1, keepdims=True)
    p_group = 1.0 / jnp.sum(jnp.exp(gl - gmax), axis=1, keepdims=True)
    grp_of_lane = ((lane - N_GROUPS) >> 3).astype(F32)
    sel = (lane >= N_GROUPS) & (lane < N_GROUPS + N_EXPERTS) & (grp_of_lane == g_idx)
    el = jnp.where(sel, logits, ninf)
    v1 = jnp.max(el, axis=1, keepdims=True)
    i1 = jnp.min(jnp.where(el == v1, lane_f, big), axis=1, keepdims=True)
    el2 = jnp.where(lane_f == i1, ninf, el)
    v2 = jnp.max(el2, axis=1, keepdims=True)
    i2 = jnp.min(jnp.where(el2 == v2, lane_f, big), axis=1, keepdims=True)
    t = jnp.exp(v2 - v1)
    gate1 = p_group / (1.0 + t)
    gate2 = p_group * t / (1.0 + t)
    e1, e2 = i1 - N_GROUPS, i2 - N_GROUPS
    route_ref[...] = jnp.where(lane == 0, gate1, jnp.where(lane == 1, gate2, jnp.where(
        lane == 2, e1, jnp.where(lane == 3, e2, 0.0))))
    chosen = ((lane_f == e1) | (lane_f == e2)).astype(F32)
    tm = chosen.shape[0]
    cnt_ref[...] = jnp.sum(chosen.reshape(tm // ROUTE_TILE, ROUTE_TILE, LANES), axis=1)[:, None, :]


def _mix_ln(oa, ob, x2d, wo_b, g, b, wr, br, tm, total, row0, prev=()):
    T, D = x2d.shape
    off = row0 // tm
    steps = T // tm
    tail = 1 if row0 + T < total else 0
    assert total - (row0 + T) <= tm
    row = lambda i: (jnp.minimum(i, steps - 1), 0)
    out_row = lambda i: (off + i, 0)
    fix = lambda i: (0, 0)
    sub = tm // ROUTE_TILE
    return pl.pallas_call(
        functools.partial(_mix_ln_kernel, steps=steps),
        grid=(steps + tail,),
        in_specs=[pl.BlockSpec((tm, A_WIDTH), row), pl.BlockSpec((tm, B_WIDTH), row), pl.BlockSpec((tm, D), row),
                  pl.BlockSpec((A_WIDTH + B_WIDTH, D), fix), pl.BlockSpec((1, D), fix), pl.BlockSpec((1, D), fix),
                  pl.BlockSpec((D, LANES), fix), pl.BlockSpec((1, LANES), fix)]
                 + [pl.BlockSpec(memory_space=pl.ANY)] * len(prev),
        out_specs=[pl.BlockSpec((tm, D), out_row), pl.BlockSpec((tm, LANES), out_row),
                   pl.BlockSpec((sub, 1, LANES), lambda i: (off + i, 0, 0))],
        out_shape=[jax.ShapeDtypeStruct((total, D), F32), jax.ShapeDtypeStruct((total, LANES), F32),
                   jax.ShapeDtypeStruct((total // ROUTE_TILE, 1, LANES), F32)],
        input_output_aliases={8 + j: j for j in range(len(prev))},
        compiler_params=_params(("parallel",)),
        name="mix_ln",
    )(oa, ob, x2d, wo_b, g, b, wr, br, *prev)


def _slot_layout(counts):
    tiles = counts.shape[0]
    n_assign = tiles * ROUTE_TILE * TOP_K
    n_blocks = -(-(n_assign + N_EXPERTS * (MOE_BLK - 1)) // MOE_BLK)
    per_tile = counts.reshape(tiles, LANES)
    earlier = (jnp.arange(tiles)[:, None] > jnp.arange(tiles)[None, :]).astype(F32)
    before = jnp.dot(earlier, per_tile, precision=HIGHEST)
    total = jnp.sum(per_tile, axis=0)[:N_EXPERTS]
    padded = jnp.ceil(total / MOE_BLK) * MOE_BLK
    upto = (jnp.arange(N_EXPERTS)[:, None] <= jnp.arange(N_EXPERTS)[None, :]).astype(F32)
    pad_end = jnp.dot(padded, upto, precision=HIGHEST)
    pad_start = pad_end - padded
    base = (before + jnp.pad(pad_start, (0, LANES - N_EXPERTS))[None, :]).reshape(tiles, 1, LANES)
    blk_start = (jnp.arange(n_blocks) * MOE_BLK).astype(F32)
    in_e = ((pad_start[None, :] <= blk_start[:, None]) & (blk_start[:, None] < pad_end[None, :])).astype(F32)
    used = blk_start < pad_end[-1]
    n_used = jnp.sum(used.astype(jnp.int32))
    last_e = jnp.max(jnp.where(padded > 0, jnp.arange(N_EXPERTS), 0)).astype(F32)
    blk_e = jnp.where(used, jnp.dot(in_e, jnp.arange(N_EXPERTS, dtype=F32), precision=HIGHEST), last_e)
    blk_n = jnp.clip(jnp.dot(in_e, pad_start + total, precision=HIGHEST) - blk_start, 0, MOE_BLK)
    blk_x = jnp.minimum(jnp.arange(n_blocks), jnp.maximum(n_used - 1, 0))
    i32 = lambda t: t.astype(jnp.int32)
    return base, i32(blk_e), i32(blk_n), i32(blk_x), i32(pad_end), n_blocks


def _slot_kernel(route_ref, base_ref, dest_ref):
    lane = lax.broadcasted_iota(jnp.int32, (ROUTE_TILE, LANES), 1)
    lane_f = lane.astype(F32)
    ri = lax.broadcasted_iota(jnp.int32, (ROUTE_TILE, ROUTE_TILE), 0)
    ci = lax.broadcasted_iota(jnp.int32, (ROUTE_TILE, ROUTE_TILE), 1)
    before = (ri > ci).astype(BF16)
    for t in range(base_ref.shape[0]):
        rows = slice(t * ROUTE_TILE, (t + 1) * ROUTE_TILE)
        route = route_ref[rows, :]
        e1 = jnp.sum(jnp.where(lane == 2, route, 0.0), axis=1, keepdims=True)
        e2 = jnp.sum(jnp.where(lane == 3, route, 0.0), axis=1, keepdims=True)
        oh1, oh2 = lane_f == e1, lane_f == e2
        earlier = _bdot(before, (oh1 | oh2).astype(F32))
        slot = base_ref[t] + earlier
        d1 = jnp.sum(jnp.where(oh1, slot, 0.0), axis=1, keepdims=True)
        d2 = jnp.sum(jnp.where(oh2, slot, 0.0), axis=1, keepdims=True)
        dest_ref[rows, :] = jnp.where(lane == 0, d1, jnp.where(lane == 1, d2, 0.0)).astype(jnp.int32)


def _slots(route, base):
    tiles = route.shape[0] // ROUTE_TILE
    sub = max(d for d in range(1, 9) if tiles % d == 0)
    return pl.pallas_call(
        _slot_kernel,
        grid=(tiles // sub,),
        in_specs=[pl.BlockSpec((sub * ROUTE_TILE, LANES), lambda i: (i, 0)),
                  pl.BlockSpec((sub, 1, LANES), lambda i: (i, 0, 0))],
        out_specs=pl.BlockSpec((sub * ROUTE_TILE, LANES), lambda i: (i, 0)),
        out_shape=jax.ShapeDtypeStruct(route.shape, jnp.int32),
        compiler_params=_params(("parallel",)),
        name="slots",
    )(route, base)


def _pack_bf16_pairs(x):
    half = x.shape[1] // 2
    bits = pltpu.bitcast(x.astype(BF16).astype(F32), jnp.uint32)
    return (bits[:, 0:half] >> 16) | bits[:, half:2 * half]


def _unpack_bf16_pairs(w):
    lo = pltpu.bitcast(w << 16, F32).astype(BF16)
    hi = pltpu.bitcast(w & jnp.uint32(0xFFFF0000), F32).astype(BF16)
    return jnp.concatenate([lo, hi], axis=1)


def _dispatch_kernel(dest_ref, pad_end_ref, h_ref, xs_hbm, hbuf, zbuf, sem, zsem):
    i = pl.program_id(0)
    slot = i % 2
    tile = h_ref.shape[0]
    a0 = i * (tile * TOP_K)

    @pl.when(i == 0)
    def _():
        zbuf[...] = jnp.zeros(zbuf.shape, zbuf.dtype)
        fills =[pltpu.make_async_copy(zbuf, xs_hbm.at[pl.ds(pl.multiple_of(pad_end_ref[e] - MOE_BLK, MOE_BLK), MOE_BLK)],
                                       zsem) for e in range(N_EXPERTS)]
        has_rows = [pad_end_ref[e] > (pad_end_ref[e - 1] if e else 0) for e in range(N_EXPERTS)]
        for e in range(N_EXPERTS):
            @pl.when(has_rows[e])
            def _(e=e):
                fills[e].start()
        for e in range(N_EXPERTS):
            @pl.when(has_rows[e])
            def _(e=e):
                fills[e].wait()

        def tail(b):
            return pltpu.make_async_copy(zbuf, xs_hbm.at[pl.ds(pl.multiple_of(b * MOE_BLK, MOE_BLK), MOE_BLK)], zsem)

        def tail_start(b, c):
            tail(b).start()
            return c

        def tail_wait(b, c):
            tail(b).wait()
            return c

        first_unused = pad_end_ref[N_EXPERTS - 1] // MOE_BLK
        lax.fori_loop(first_unused, xs_hbm.shape[0] // MOE_BLK, tail_start, 0)
        lax.fori_loop(first_unused, xs_hbm.shape[0] // MOE_BLK, tail_wait, 0)

    hbuf[slot] = _pack_bf16_pairs(h_ref[...])

    def start(r, c):
        for k in range(TOP_K):
            pltpu.make_async_copy(hbuf.at[slot, pl.ds(r, 1)], xs_hbm.at[pl.ds(dest_ref[a0 + r * TOP_K + k], 1)],
                                  sem.at[slot]).start()
        return c

    lax.fori_loop(0, tile, start, 0, unroll=8)

    def drain(s):
        for k in range(TOP_K):
            pltpu.make_async_copy(hbuf.at[s], xs_hbm.at[pl.ds(0, tile)], sem.at[s]).wait()

    @pl.when(i > 0)
    def _():
        drain(1 - slot)

    @pl.when(i == pl.num_programs(0) - 1)
    def _():
        drain(slot)


def _dispatch(h_all, dest, pad_end, slots):
    T, D = h_all.shape
    tile = max(t for t in range(ROUTE_TILE, 4 * ROUTE_TILE + 1, 8) if T % t == 0)
    grid_spec = pltpu.PrefetchScalarGridSpec(
        num_scalar_prefetch=2,
        grid=(T // tile,),
        in_specs=[pl.BlockSpec((tile, D), lambda i, *_: (i, 0))],
        out_specs=pl.BlockSpec(memory_space=pl.ANY),
        scratch_shapes=[pltpu.VMEM((2, tile, D // 2), jnp.uint32), pltpu.VMEM((MOE_BLK, D // 2), jnp.uint32),
                        pltpu.SemaphoreType.DMA((2,)), pltpu.SemaphoreType.DMA],
    )
    return pl.pallas_call(
        _dispatch_kernel,
        grid_spec=grid_spec,
        out_shape=jax.ShapeDtypeStruct((slots, D // 2), jnp.uint32),
        compiler_params=_params(("arbitrary",)),
        name="dispatch",
    )(dest, pad_end, h_all)


def _moe_kernel(blk_e_ref, blk_n_ref, blk_x_ref, x_ref, wg_ref, wu_ref, wd_ref, y_ref, wg_b, wu_b, wd_b):
    del blk_x_ref
    i = pl.program_id(0)
    n_valid = blk_n_ref[i]

    @pl.when((i == 0) | (blk_e_ref[i] != blk_e_ref[jnp.maximum(i - 1, 0)]))
    def _():
        wg_b[...] = wg_ref[0].astype(BF16)
        wu_b[...] = wu_ref[0].astype(BF16)
        wd_b[...] = wd_ref[0].astype(BF16)

    @pl.when(n_valid > 0)
    def _():
        x = _unpack_bf16_pairs(x_ref[...])
        a = jnp.dot(x, wg_b[...], preferred_element_type=F32)
        u = jnp.dot(x, wu_b[...], preferred_element_type=F32)
        y_ref[...] = jnp.dot((_silu(a) * u).astype(BF16), wd_b[...], preferred_element_type=F32)

    @pl.when(n_valid == 0)
    def _():
        y_ref[...] = jnp.zeros(y_ref.shape, F32)


def _moe(xs, blk_e, blk_n, blk_x, w_gate, w_up, w_down):
    slots = xs.shape[0]
    D, De = w_gate.shape[-2:]
    wspec = lambda shape: pl.BlockSpec((1,) + shape, lambda i, be, *_: (be[i], 0, 0))
    grid_spec = pltpu.PrefetchScalarGridSpec(
        num_scalar_prefetch=3,
        grid=(slots // MOE_BLK,),
        in_specs=[pl.BlockSpec((MOE_BLK, xs.shape[1]), lambda i, be, bn, bx: (bx[i], 0)),
                  wspec((D, De)), wspec((D, De)), wspec((De, D))],
        out_specs=pl.BlockSpec((MOE_BLK, D), lambda i, *_: (i, 0)),
        scratch_shapes=[pltpu.VMEM((D, De), BF16), pltpu.VMEM((D, De), BF16), pltpu.VMEM((De, D), BF16)],
    )
    return pl.pallas_call(
        _moe_kernel,
        grid_spec=grid_spec,
        out_shape=jax.ShapeDtypeStruct((slots, D), F32),
        compiler_params=_params(("arbitrary",)),
        name="moe",
    )(blk_e, blk_n, blk_x, xs, w_gate, w_up, w_down)


def _final_ln_kernel(dest_ref, h_ref, route_ref, g_ref, b_ref, ys_hbm, o_ref, ybuf, sem, *, tile0):
    i = pl.program_id(0)
    slot = i % 2
    rows = h_ref.shape[0]

    def gather(tile, s):
        a0 = (tile0 + tile) * (rows * TOP_K)

        def start(r, c):
            for k in range(TOP_K):
                pltpu.make_async_copy(ys_hbm.at[pl.ds(dest_ref[a0 + r * TOP_K + k], 1)],
                                      ybuf.at[s, k, pl.ds(r, 1)], sem.at[s]).start()
            return c

        lax.fori_loop(0, rows, start, 0, unroll=8)

    @pl.when(i == 0)
    def _():
        gather(0, 0)

    @pl.when(i + 1 < pl.num_programs(0))
    def _():
        gather(i + 1, 1 - slot)

    for k in range(TOP_K):
        pltpu.make_async_copy(ys_hbm.at[pl.ds(0, rows)], ybuf.at[slot, k], sem.at[slot]).wait()
    route = route_ref[...]
    lane = lax.broadcasted_iota(jnp.int32, route.shape, 1)
    gate1 = jnp.sum(jnp.where(lane == 0, route, 0.0), axis=1, keepdims=True)
    gate2 = jnp.sum(jnp.where(lane == 1, route, 0.0), axis=1, keepdims=True)
    f = ybuf[slot, 0] * gate1 + ybuf[slot, 1] * gate2
    o_ref[...] = _layer_norm(DN_ALPHA * h_ref[...] + f, g_ref[...], b_ref[...])


def _final_ln(h_all, route, dest, ys, g, b, row0, rows):
    D = h_all.shape[1]
    tile = max(t for t in (ROUTE_TILE, 2 * ROUTE_TILE, 4 * ROUTE_TILE) if rows % t == 0 and row0 % t == 0)
    tile0 = row0 // tile
    row = lambda i, *_: (tile0 + i, 0)
    fix = lambda i, *_: (0, 0)
    grid_spec = pltpu.PrefetchScalarGridSpec(
        num_scalar_prefetch=1,
        grid=(rows // tile,),
        in_specs=[pl.BlockSpec((tile, D), row), pl.BlockSpec((tile, LANES), row),
                  pl.BlockSpec((1, D), fix), pl.BlockSpec((1, D), fix), pl.BlockSpec(memory_space=pl.ANY)],
        out_specs=pl.BlockSpec((tile, D), lambda i, *_: (i, 0)),
        scratch_shapes=[pltpu.VMEM((2, TOP_K, tile, D), F32), pltpu.SemaphoreType.DMA((2,))],
    )
    return pl.pallas_call(
        functools.partial(_final_ln_kernel, tile0=tile0),
        grid_spec=grid_spec,
        out_shape=jax.ShapeDtypeStruct((rows, D), F32),
        compiler_params=_params(("arbitrary",)),
        name="final_ln",
    )(dest, h_all, route, g, b, ys)


def kernel(x_prompt, x_sample, cache_a_k, cache_a_v, state_b_ssm, state_b_conv, w_in, rel_bias, conv_w, a_log, dt_bias, o_norm_g, w_out, ln1_g, ln1_b, w_group, b_group, w_router, b_router, w_gate, w_up, w_down, ln2_g, ln2_b):
    B, S, D = x_prompt.shape
    N, T = x_sample.shape[0], x_sample.shape[1]
    depth = w_in.shape[0]
    assert depth == 1 and T == 1 and S % ATT_TILE == 0 and N % ROUTE_TILE == 0 and cache_a_k.shape[2] % LANES == 0
    l = 0
    win_p = min(BRANCHES[-1][0], S)

    w_pad32 = jnp.pad(w_in[l], ((0, 0), (0, IN_COLS_PAD - IN_COLS)))
    w_pad = w_pad32.astype(BF16)
    wo_b = w_out[l].astype(BF16)
    wr = jnp.pad(jnp.concatenate([w_group[l], w_router[l]], axis=1), ((0, 0), (0, LANES - N_GROUPS - N_EXPERTS)))
    br = jnp.pad(jnp.concatenate([b_group[l], b_router[l].reshape(-1)]), (0, LANES - N_GROUPS - N_EXPERTS))[None, :]
    g1, b1 = ln1_g[l][None, :], ln1_b[l][None, :]
    g2, b2 = ln2_g[l][None, :], ln2_b[l][None, :]

    xp = x_prompt.reshape(B * S, D)
    qkv_p, ub_p, zb_p, ab_p, k_win, v_win = _proj(xp, w_pad, 512, seq=S, win=win_p)
    oa_p = _attn(qkv_p, _band_bias(rel_bias), B, S)
    ob_p, st_p = _delta(ub_p, zb_p, ab_p, conv_w[l], a_log[l], dt_bias[l], o_norm_g[l], B, S)
    rows_all = B * S + N
    routed_p = _mix_ln(oa_p.reshape(B * S, A_WIDTH), ob_p.reshape(B * S, B_WIDTH), xp, wo_b, g1, b1, wr, br,
                       512, rows_all, 0)

    xs = x_sample.reshape(N, D)
    qkv_s, ub_s, zb_s, ab_s = _proj(xs, w_pad32, N)
    nb = 2
    qkv_t = jnp.transpose(qkv_s.reshape(N, 3, A_HEADS, A_HEAD_DIM), (1, 2, 3, 0))
    oa_s = _attn_dec(qkv_t, jnp.transpose(cache_a_k[l], (0, 2, 3, 1)), jnp.transpose(cache_a_v[l], (0, 2, 3, 1)),
                     _cache_bias(rel_bias, cache_a_k.shape[2]), nb)
    oa_s = jnp.transpose(oa_s, (0, 3, 1, 2)).reshape(N, A_WIDTH)
    ob_s, st_s = _delta_dec(ub_s, state_b_conv[l], zb_s, ab_s, conv_w[l], a_log[l], dt_bias[l], o_norm_g[l],
                            state_b_ssm[l], 16)
    h_all, route, cnt = _mix_ln(oa_s, ob_s, xs, w_out[l], g1, b1, wr, br, N, rows_all, B * S, prev=routed_p)

    base, blk_e, blk_n, blk_x, pad_end, n_blocks = _slot_layout(cnt)
    dest = _slots(route, base)[:, 0:TOP_K].reshape(-1)
    xs = _dispatch(h_all, dest, pad_end, n_blocks * MOE_BLK)
    ys = _moe(xs, blk_e, blk_n, blk_x, w_gate[l], w_up[l], w_down[l])
    y_p = _final_ln(h_all, route, dest, ys, g2, b2, 0, B * S)
    y_s = _final_ln(h_all, route, dest, ys, g2, b2, B * S, N)

    to_rows = lambda t: jnp.transpose(t, (0, 3, 1, 2))[None]
    conv_p = ub_p.reshape(B, S, CONV_DIM)[:, S - (CONV_WIDTH - 1):]
    conv_s = jnp.concatenate([state_b_conv[l], ub_s[:, None, :]], axis=1)[:, T:]
    new_kv = lambda t: jnp.transpose(t, (2, 0, 1))[None, :, None]
    return (y_p.reshape(B, S, D), y_s.reshape(N, T, D), to_rows(k_win), to_rows(v_win),
            new_kv(qkv_t[1]), new_kv(qkv_t[2]), st_p[None], st_s[None], conv_p[None], conv_s[None])
```

```python
import functools
import math

import jax
import jax.numpy as jnp
from jax import lax
from jax.experimental import pallas as pl
from jax.experimental.pallas import tpu as pltpu

F32 = jnp.float32
BF16 = jnp.bfloat16
HIGHEST = lax.Precision.HIGHEST

LANES = 128
A_HEADS = 8
A_HEAD_DIM = 64
A_WIDTH = A_HEADS * A_HEAD_DIM
BRANCHES = ((128, 1), (512, 4), (2048, 16))
BAND = 128
ATT_TILE = BAND * 16
REL_BUCKETS = 32
REL_MAX_DIST = 2048
B_HEADS = 4
B_HEAD_DIM = 128
B_WIDTH = B_HEADS * B_HEAD_DIM
CONV_WIDTH = 4
CONV_DIM = 3 * B_WIDTH
CHUNK = 64
COL_UB = 3 * A_WIDTH
COL_ZB = COL_UB + CONV_DIM
COL_AB = COL_ZB + B_WIDTH
IN_COLS = COL_AB + 2 * B_HEADS
IN_COLS_PAD = COL_AB + LANES
N_GROUPS = 4
EXPERTS_PER_GROUP = 8
N_EXPERTS = N_GROUPS * EXPERTS_PER_GROUP
TOP_K = 2
DN_ALPHA = 2.0 ** 0.25
LN_EPS = 1e-5
RMS_EPS = 1e-6
MASKED = -1e30
MOE_BLK = 512
ROUTE_TILE = 128
DELTA_TILE = 1024
ATT_GROUP = 16
VMEM_LIMIT = 56 * 1024 * 1024


def _bdot(a, b):
    return jnp.dot(a.astype(BF16), b.astype(BF16), preferred_element_type=F32)


def _bdot_nt(a, b):
    return lax.dot_general(a.astype(BF16), b.astype(BF16), (((1,), (1,)), ((), ())), preferred_element_type=F32)


def _bdot_tn(a, b):
    return lax.dot_general(a.astype(BF16), b.astype(BF16), (((0,), (0,)), ((), ())), preferred_element_type=F32)


def _fdot(a, b):
    return jnp.dot(a, b, precision=HIGHEST, preferred_element_type=F32)


def _sigmoid(x):
    return 1.0 / (1.0 + jnp.exp(-x))


def _silu(x):
    return x * _sigmoid(x)


def _softplus(x):
    return jnp.maximum(x, 0.0) + jnp.log(1.0 + jnp.exp(-jnp.abs(x)))


def _params(sem):
    return pltpu.CompilerParams(dimension_semantics=sem, vmem_limit_bytes=VMEM_LIMIT)


def _proj_kernel(x_ref, w_ref, qkv_ref, ub_ref, zb_ref, ab_ref, *win_refs, tiles_per_seq=0, first_win_tile=0):
    if w_ref.dtype == BF16:
        xb = x_ref[...].astype(BF16)
        dot = lambda w: jnp.dot(xb, w, preferred_element_type=F32)
    else:
        dot = lambda w: _fdot(x_ref[...], w)
    qkv = dot(w_ref[:, 0:COL_UB])
    qkv_ref[...] = qkv
    ub_ref[...] = dot(w_ref[:, COL_UB:COL_ZB])
    zb_ref[...] = dot(w_ref[:, COL_ZB:COL_AB])
    ab_ref[...] = dot(w_ref[:, COL_AB:IN_COLS_PAD])
    if win_refs:
        @pl.when(pl.program_id(0) % tiles_per_seq >= first_win_tile)
        def _():
            tm = qkv.shape[0]
            for j, ref in enumerate(win_refs):
                cols = qkv[:, (1 + j) * A_WIDTH:(2 + j) * A_WIDTH]
                ref[0] = jnp.transpose(cols).reshape(A_HEADS, A_HEAD_DIM, tm)


def _proj(x2d, w_pad, tm, seq=None, win=None):
    T, D = x2d.shape
    row = lambda i: (i, 0)
    out_specs = [pl.BlockSpec((tm, COL_UB), row), pl.BlockSpec((tm, CONV_DIM), row),
                 pl.BlockSpec((tm, B_WIDTH), row), pl.BlockSpec((tm, LANES), row)]
    out_shape = [jax.ShapeDtypeStruct((T, COL_UB), F32), jax.ShapeDtypeStruct((T, CONV_DIM), F32),
                 jax.ShapeDtypeStruct((T, B_WIDTH), F32), jax.ShapeDtypeStruct((T, LANES), F32)]
    body = _proj_kernel
    if win:
        per_seq, first = seq // tm, (seq - win) // tm
        wspec = pl.BlockSpec((1, A_HEADS, A_HEAD_DIM, tm),
                             lambda i: (i // per_seq, 0, 0, jnp.maximum(i % per_seq - first, 0)))
        out_specs += [wspec, wspec]
        out_shape += [jax.ShapeDtypeStruct((T // seq, A_HEADS, A_HEAD_DIM, win), F32)] * 2
        body = functools.partial(_proj_kernel, tiles_per_seq=per_seq, first_win_tile=first)
    return pl.pallas_call(
        body,
        grid=(T // tm,),
        in_specs=[pl.BlockSpec((tm, D), row), pl.BlockSpec((D, IN_COLS_PAD), lambda i: (0, 0))],
        out_specs=out_specs,
        out_shape=out_shape,
        compiler_params=_params(("arbitrary",)),
        name="proj",
    )(x2d, w_pad)


def _rel_bucket(dist):
    max_exact = REL_BUCKETS // 2
    n = jnp.maximum(dist, 0)
    ratio = jnp.maximum(n, 1).astype(F32) / max_exact
    large = max_exact + (jnp.log(ratio) / math.log(REL_MAX_DIST / max_exact)
                         * (REL_BUCKETS - max_exact)).astype(jnp.int32)
    return jnp.where(n < max_exact, n, jnp.minimum(large, REL_BUCKETS - 1))


def _bias_of(rel_bias, dist):
    onehot = (_rel_bucket(dist)[None, :] == jnp.arange(REL_BUCKETS)[:, None]).astype(F32)
    return jnp.dot(rel_bias.astype(F32).T, onehot, precision=HIGHEST)


def _band_bias(rel_bias):
    period = 3 * BAND
    tabs = []
    for _, dil in BRANCHES:
        g = jnp.concatenate([_bias_of(rel_bias, (BAND - jnp.arange(BAND + 1)) * dil),
                             jnp.full((A_HEADS, period - BAND - 1), MASKED, F32)], axis=1)
        skew = jnp.tile(g, (1, BAND))[:, :BAND * (period - 1)].reshape(A_HEADS, BAND, period - 1)
        tabs.append(skew[:, :, :2 * BAND])
    return jnp.stack(tabs)


def _cache_bias(rel_bias, P):
    dist = P - jnp.arange(P + LANES)
    tabs = []
    for window, dil in BRANCHES:
        ok = (dist >= 0) & (dist <= window) & (dist % dil == 0)
        tabs.append(jnp.where(ok[None, :], _bias_of(rel_bias, dist), MASKED)[:, None, :])
    return jnp.stack(tabs)


def _attn_kernel(q_ref, k_ref, v_ref, bias_ref, o_ref, acc_ref, m_ref, l_ref):
    t = pl.program_id(2)
    tile0 = t * ATT_TILE
    lane = lax.broadcasted_iota(jnp.int32, (BAND, LANES), 1)
    head0 = lane < A_HEAD_DIM

    def rows(start, dil):
        return pl.ds(pl.multiple_of(start, BAND), BAND) if dil == 1 else pl.ds(start, BAND, stride=dil)

    for br, (_, dil) in enumerate(BRANCHES):
        span = BAND * dil

        def blocks(it, carry, br=br, dil=dil, span=span):
            gs = range(ATT_GROUP)
            idx = [it * ATT_GROUP + g for g in gs]
            start = [(i % dil) + (i // dil) * span for i in idx]
            cur = [tile0 + s for s in start]
            first = [c < span for c in cur]
            prev = [jnp.where(first[g], cur[g], cur[g] - span) for g in gs]
            q = [q_ref[0, rows(start[g], dil), :] * (A_HEAD_DIM ** -0.5) for g in gs]
            kk = [jnp.concatenate([k_ref[0, rows(prev[g], dil), :], k_ref[0, rows(cur[g], dil), :]],
                                  axis=0).astype(BF16) for g in gs]
            vv = [jnp.concatenate([v_ref[0, rows(prev[g], dil), :], v_ref[0, rows(cur[g], dil), :]],
                                  axis=0).astype(BF16) for g in gs]
            pen = [jnp.where(first[g], MASKED, 0.0) for g in gs]
            gh = [(g, hh) for g in gs for hh in range(2)]
            qh = [jnp.where(head0 if hh == 0 else ~head0, q[g], 0.0).astype(BF16) for g, hh in gh]
            s = [lax.dot_general(qh[j], kk[g], (((1,), (1,)), ((), ())), preferred_element_type=F32)
                 for j, (g, hh) in enumerate(gh)]
            s = [s[j] + bias_ref[br, hh] for j, (g, hh) in enumerate(gh)]
            s = [jnp.concatenate([s[j][:, 0:BAND] + pen[g], s[j][:, BAND:2 * BAND]], axis=1)
                 for j, (g, hh) in enumerate(gh)]
            m = [jnp.max(t, axis=1, keepdims=True) for t in s]
            p = [jnp.exp(s[j] - m[j]) for j in range(len(gh))]
            l = [jnp.sum(t, axis=1, keepdims=True) for t in p]
            pv = [jnp.dot(p[j].astype(BF16), vv[g], preferred_element_type=F32) for j, (g, hh) in enumerate(gh)]
            for g in gs:
                acc_ref[br, rows(start[g], dil), :] = jnp.where(head0, pv[2 * g], pv[2 * g + 1])
                m_ref[br, rows(start[g], dil), :] = jnp.where(head0, m[2 * g], m[2 * g + 1])
                l_ref[br, rows(start[g], dil), :] = jnp.where(head0, l[2 * g], l[2 * g + 1])
            return carry

        lax.fori_loop(0, ATT_TILE // BAND // ATT_GROUP, blocks, 0)

    def merge(c, carry):
        r = pl.ds(pl.multiple_of(c * 256, 256), 256)
        m0, m1, m2 = m_ref[0, r, :], m_ref[1, r, :], m_ref[2, r, :]
        mx = jnp.maximum(jnp.maximum(m0, m1), m2)
        w0, w1, w2 = jnp.exp(m0 - mx), jnp.exp(m1 - mx), jnp.exp(m2 - mx)
        num = w0 * acc_ref[0, r, :] + w1 * acc_ref[1, r, :] + w2 * acc_ref[2, r, :]
        den = w0 * l_ref[0, r, :] + w1 * l_ref[1, r, :] + w2 * l_ref[2, r, :]
        o_ref[0, r, :] = num / den
        return carry

    lax.fori_loop(0, ATT_TILE // 256, merge, 0)


def _attn(qkv, bias, B, S):
    n_pairs = A_HEADS // 2
    qkv3 = qkv.reshape(B, S, 3 * A_WIDTH)
    return pl.pallas_call(
        _attn_kernel,
        grid=(B, n_pairs, S // ATT_TILE),
        in_specs=[pl.BlockSpec((1, ATT_TILE, LANES), lambda b, hp, t: (b, t, hp)),
                  pl.BlockSpec((1, S, LANES), lambda b, hp, t: (b, 0, n_pairs + hp)),
                  pl.BlockSpec((1, S, LANES), lambda b, hp, t: (b, 0, 2 * n_pairs + hp)),
                  pl.BlockSpec((3, 2, BAND, 2 * BAND), lambda b, hp, t: (0, hp, 0, 0))],
        out_specs=pl.BlockSpec((1, ATT_TILE, LANES), lambda b, hp, t: (b, t, hp)),
        out_shape=jax.ShapeDtypeStruct((B, S, A_WIDTH), F32),
        scratch_shapes=[pltpu.VMEM((3, ATT_TILE, LANES), F32)] * 3,
        compiler_params=_params(("parallel", "parallel", "arbitrary")),
        name="attn",
    )(qkv3, qkv3, qkv3, bias)


def _attn_dec_kernel(qkv_ref, kt_ref, vt_ref, bias_ref, o_ref, *, nb):
    i = pl.program_id(0)
    N = qkv_ref.shape[-1]
    P = kt_ref.shape[-1]
    lane_n = lax.broadcasted_iota(jnp.int32, (A_HEAD_DIM, N), 1)
    lane_o = lax.broadcasted_iota(jnp.int32, (A_HEAD_DIM, nb), 1)
    lane_t = lax.broadcasted_iota(jnp.int32, (1, LANES), 1)

    def head(h, carry):
        slab = jnp.zeros((A_HEAD_DIM, nb), F32)
        for j in range(nb):
            pick = lane_n == i * nb + j
            col = lambda t: jnp.sum(jnp.where(pick, t, 0.0), axis=1, keepdims=True)
            q = col(qkv_ref[0, h]) * (A_HEAD_DIM ** -0.5)
            k_new, v_new = col(qkv_ref[1, h]), col(qkv_ref[2, h])
            s_new = jnp.sum(q * k_new, axis=0, keepdims=True)
            s = jnp.concatenate([jnp.sum(kt_ref[j, h] * q, axis=0, keepdims=True),
                                 jnp.where(lane_t == 0, s_new, 0.0)], axis=1)
            ps, ms, ls = [], [], []
            for br in range(3):
                sb = s + bias_ref[br, h]
                m = jnp.max(sb, axis=1, keepdims=True)
                p = jnp.exp(sb - m)
                ps.append(p)
                ms.append(m)
                ls.append(jnp.sum(p, axis=1, keepdims=True))
            mx = jnp.maximum(jnp.maximum(ms[0], ms[1]), ms[2])
            w = jnp.zeros((1, P + LANES), F32)
            den = jnp.zeros((1, 1), F32)
            for p, m, l in zip(ps, ms, ls):
                e = jnp.exp(m - mx)
                w = w + e * p
                den = den + e * l
            o = jnp.sum(vt_ref[j, h] * w[:, 0:P], axis=1, keepdims=True) + v_new * w[:, P:P + 1]
            slab = jnp.where(lane_o == j, o / den, slab)
        o_ref[0, h] = slab
        return carry

    lax.fori_loop(0, A_HEADS, head, 0, unroll=2)


def _attn_dec(qkv_t, cache_kt, cache_vt, bias, nb):
    N, P = cache_kt.shape[0], cache_kt.shape[-1]
    cache_spec = pl.BlockSpec((nb, A_HEADS, A_HEAD_DIM, P), lambda i: (i, 0, 0, 0))
    return pl.pallas_call(
        functools.partial(_attn_dec_kernel, nb=nb),
        grid=(N // nb,),
        in_specs=[pl.BlockSpec((3, A_HEADS, A_HEAD_DIM, N), lambda i: (0, 0, 0, 0)), cache_spec, cache_spec,
                  pl.BlockSpec((3, A_HEADS, 1, P + LANES), lambda i: (0, 0, 0, 0))],
        out_specs=pl.BlockSpec((1, A_HEADS, A_HEAD_DIM, nb), lambda i: (i, 0, 0, 0)),
        out_shape=jax.ShapeDtypeStruct((N // nb, A_HEADS, A_HEAD_DIM, nb), F32),
        compiler_params=_params(("parallel",)),
        name="attn_dec",
    )(qkv_t, cache_kt, cache_vt, bias)


def _split3(x):
    hi = x.astype(BF16)
    r = x - hi.astype(F32)
    mid = r.astype(BF16)
    return hi, mid, (r - mid.astype(F32)).astype(BF16)


def _tril_dot(tril_b, g):
    return sum(jnp.dot(tril_b, piece, preferred_element_type=F32) for piece in _split3(g))


def _dot3(a, b):
    ah, bh = a.astype(BF16), b.astype(BF16)
    al, bl = (a - ah.astype(F32)).astype(BF16), (b - bh.astype(F32)).astype(BF16)
    d = lambda x, y: jnp.dot(x, y, preferred_element_type=F32)
    return d(ah, bh) + (d(ah, bl) + d(al, bh))


def _delta_kernel(alog_ref, dtb_ref, ub_ref, z_ref, ab_ref, cw_ref, og_ref, o_ref, st_ref,
                  pad_ref, u_s, wq_s, kt_s, qk_s, gl_s, a_s, rhs_s):
    t = pl.program_id(1)
    TS = ub_ref.shape[1]
    nch = TS // CHUNK
    hdr = 8

    slabs = range(CONV_DIM // LANES)
    seq_rows = lambda first, n: pl.ds(2 * first, n, stride=2)

    @pl.when(t == 0)
    def _():
        for s in slabs:
            pad_ref[s, seq_rows(0, hdr), :] = jnp.zeros((hdr, LANES), F32)
        st_ref[...] = jnp.zeros(st_ref.shape, F32)

    @pl.when(t > 0)
    def _():
        for s in slabs:
            pad_ref[s, seq_rows(0, hdr), :] = pad_ref[s, seq_rows(TS, hdr), :]

    for s in slabs:
        pad_ref[s, seq_rows(hdr, TS), :] = ub_ref[0, :, s * LANES:(s + 1) * LANES]

    ri = lax.broadcasted_iota(jnp.int32, (CHUNK, CHUNK), 0)
    ci = lax.broadcasted_iota(jnp.int32, (CHUNK, CHUNK), 1)
    incl = ri >= ci
    strict = ri > ci
    tril_b = incl.astype(BF16)
    eye = (ri == ci).astype(F32)
    lane = lax.broadcasted_iota(jnp.int32, (CHUNK, LANES), 1)

    def local(c, carry):
        base = c * CHUNK if isinstance(c, int) else pl.multiple_of(c * CHUNK, CHUNK)
        ab = ab_ref[0, pl.ds(base, CHUNK), :]
        heads = []
        for h in range(B_HEADS):
            def conv(col):
                first = c * CHUNK + (hdr - (CONV_WIDTH - 1))
                acc = pad_ref[col // LANES, seq_rows(first, CHUNK), :] * cw_ref[0:1, col:col + LANES]
                for i in range(1, CONV_WIDTH):
                    acc = acc + pad_ref[col // LANES, seq_rows(first + i, CHUNK), :] * cw_ref[i:i + 1, col:col + LANES]
                return _silu(acc)

            cq, ck, v = conv(h * LANES), conv(B_WIDTH + h * LANES), conv(2 * B_WIDTH + h * LANES)
            q = cq * lax.rsqrt(jnp.sum(cq * cq, axis=1, keepdims=True) + 1e-6) * (B_HEAD_DIM ** -0.5)
            k = ck * lax.rsqrt(jnp.sum(ck * ck, axis=1, keepdims=True) + 1e-6)
            a_raw = jnp.sum(jnp.where(lane == h, ab, 0.0), axis=1, keepdims=True)
            b_raw = jnp.sum(jnp.where(lane == h + B_HEADS, ab, 0.0), axis=1, keepdims=True)
            neg_a = -jnp.exp(jnp.full((1, LANES), alog_ref[h], F32))
            g = neg_a * _softplus(a_raw + dtb_ref[h])
            beta = _sigmoid(b_raw)
            heads.append((q, k, v, g, beta))
        hs = range(B_HEADS)
        q, k, v, g, beta = zip(*heads)
        gc = [_tril_dot(tril_b, g[h]) for h in hs]
        dmat = [_tril_dot(tril_b, jnp.where(strict, g[h][:, 0:CHUNK], 0.0)) for h in hs]
        kq = [_bdot_nt(jnp.concatenate([k[h], q[h]], axis=0), k[h]) for h in hs]
        decay = [jnp.where(incl, jnp.exp(dmat[h]), 0.0) for h in hs]
        e_gc = [jnp.exp(gc[h]) for h in hs]
        for h in hs:
            gc_last = gc[h][CHUNK - 1:CHUNK, :]
            a_s[h, c] = jnp.where(strict, beta[h] * kq[h][0:CHUNK] * decay[h], 0.0)
            rhs_s[h, c] = jnp.concatenate([v[h] * beta[h], k[h] * (beta[h] * e_gc[h])], axis=1)
            wq_s[h, c, CHUNK:2 * CHUNK] = q[h] * e_gc[h]
            kt_s[h, c] = k[h] * jnp.exp(gc_last - gc[h])
            qk_s[h, c] = kq[h][CHUNK:2 * CHUNK] * decay[h]
            gl_s[h, c] = jnp.broadcast_to(jnp.exp(gc_last), (8, LANES))
        return carry

    def solve(c, carry):
        hs = range(B_HEADS)
        a = [a_s[h, c] for h in hs]
        x = [eye - a[h] for h in hs]
        p = [_bdot(a[h], a[h]) for h in hs]
        for _ in range(int(math.log2(CHUNK)) - 2):
            r = [_bdot(jnp.concatenate([x[h], p[h]], axis=0), p[h]) for h in hs]
            x = [x[h] + r[h][0:CHUNK] for h in hs]
            p = [r[h][CHUNK:2 * CHUNK] for h in hs]
        x = [x[h] + _bdot(x[h], p[h]) for h in hs]
        sol = [_bdot(x[h], rhs_s[h, c]) for h in hs]
        for h in hs:
            u_s[h, c] = sol[h][:, 0:LANES]
            wq_s[h, c, 0:CHUNK] = sol[h][:, LANES:2 * LANES]
        return carry

    def scan(c, carry):
        base = c * CHUNK if isinstance(c, int) else pl.multiple_of(c * CHUNK, CHUNK)
        hs = range(B_HEADS)
        state = [st_ref[0, h] for h in hs]
        r = [_bdot(wq_s[h, c], state[h]) for h in hs]
        v_new = [u_s[h, c] - r[h][0:CHUNK] for h in hs]
        upd = [_bdot_tn(kt_s[h, c], v_new[h]) for h in hs]
        out = [r[h][CHUNK:2 * CHUNK] + _bdot(qk_s[h, c], v_new[h]) for h in hs]
        for h in hs:
            st_ref[0, h] = state[h] * gl_s[h, c][0:1, :] + upd[h]
            o = out[h] * lax.rsqrt(jnp.mean(out[h] * out[h], axis=1, keepdims=True) + RMS_EPS) * og_ref[...]
            cols = slice(h * LANES, (h + 1) * LANES)
            o_ref[0, pl.ds(base, CHUNK), cols] = o * _silu(z_ref[0, pl.ds(base, CHUNK), cols])
        return carry

    def stages(c, carry):
        scan(c - 1, carry)
        solve(c, carry)
        return local(c + 1, carry)

    assert nch >= 3
    local(0, 0)
    solve(0, 0)
    local(1, 0)
    lax.fori_loop(1, nch - 1, stages, 0)
    scan(nch - 2, 0)
    solve(nch - 1, 0)
    scan(nch - 1, 0)


def _delta(ub, zb, ab, conv_w, a_log, dt_bias, o_norm_g, B, S):
    ts = min(S, DELTA_TILE)
    nch = ts // CHUNK
    seq = lambda width: pl.BlockSpec((1, ts, width), lambda b, t, *_: (b, t, 0))
    fix = lambda shape: pl.BlockSpec(shape, lambda b, t, *_: (0, 0))
    per_chunk = lambda rows, width: pltpu.VMEM((B_HEADS, nch, rows, width), F32)
    grid_spec = pltpu.PrefetchScalarGridSpec(
        num_scalar_prefetch=2,
        grid=(B, S // ts),
        in_specs=[seq(CONV_DIM), seq(B_WIDTH), seq(LANES), fix((CONV_WIDTH, CONV_DIM)), fix((1, LANES))],
        out_specs=[seq(B_WIDTH),
                   pl.BlockSpec((1, B_HEADS, B_HEAD_DIM, B_HEAD_DIM), lambda b, t, *_: (b, 0, 0, 0))],
        scratch_shapes=[pltpu.VMEM((CONV_DIM // LANES, 2 * (ts + 8), LANES), F32),
                        per_chunk(CHUNK, LANES), per_chunk(2 * CHUNK, LANES),
                        per_chunk(CHUNK, LANES), per_chunk(CHUNK, CHUNK), per_chunk(8, LANES),
                        per_chunk(CHUNK, CHUNK), per_chunk(CHUNK, 2 * LANES)],
    )
    return pl.pallas_call(
        _delta_kernel,
        grid_spec=grid_spec,
        out_shape=[jax.ShapeDtypeStruct((B, S, B_WIDTH), F32),
                   jax.ShapeDtypeStruct((B, B_HEADS, B_HEAD_DIM, B_HEAD_DIM), F32)],
        compiler_params=_params(("parallel", "arbitrary")),
        name="delta",
    )(a_log, dt_bias, ub.reshape(B, S, CONV_DIM), zb.reshape(B, S, B_WIDTH), ab.reshape(B, S, LANES),
      conv_w, o_norm_g.reshape(1, LANES))


def _delta_dec_kernel(alog_ref, dtb_ref, ub_ref, cs_ref, zb_ref, ab_ref, w_ref, og_ref, st_ref,
                      o_ref, so_ref, *, nb):
    i = pl.program_id(0)
    N = ub_ref.shape[0]
    acc = ub_ref[...] * w_ref[CONV_WIDTH - 1:CONV_WIDTH, :]
    for t in range(CONV_WIDTH - 1):
        acc = acc + cs_ref[t] * w_ref[t:t + 1, :]
    c = _silu(acc)
    ab = ab_ref[...]
    lane = lax.broadcasted_iota(jnp.int32, (N, LANES), 1)
    samp = lax.broadcasted_iota(jnp.int32, (B_HEAD_DIM, N), 1)
    row_id = lax.broadcasted_iota(jnp.int32, (N, LANES), 0)
    out_row = lax.broadcasted_iota(jnp.int32, (nb, LANES), 0)
    heads = []
    for h in range(B_HEADS):
        cq = c[:, h * LANES:(h + 1) * LANES]
        ck = c[:, B_WIDTH + h * LANES:B_WIDTH + (h + 1) * LANES]
        v = c[:, 2 * B_WIDTH + h * LANES:2 * B_WIDTH + (h + 1) * LANES]
        q = cq * lax.rsqrt(jnp.sum(cq * cq, axis=1, keepdims=True) + 1e-6) * (B_HEAD_DIM ** -0.5)
        k = ck * lax.rsqrt(jnp.sum(ck * ck, axis=1, keepdims=True) + 1e-6)
        a_raw = jnp.sum(jnp.where(lane == h, ab, 0.0), axis=1, keepdims=True)
        b_raw = jnp.sum(jnp.where(lane == h + B_HEADS, ab, 0.0), axis=1, keepdims=True)
        neg_a = -jnp.exp(jnp.full((1, 1), alog_ref[h], F32))
        dec = jnp.exp(neg_a * _softplus(a_raw + dtb_ref[h]))
        beta = _sigmoid(b_raw)
        heads.append((q.T, k.T, v, jnp.broadcast_to(dec, (N, LANES)), jnp.broadcast_to(beta, (N, LANES)),
                      zb_ref[:, h * LANES:(h + 1) * LANES]))

    hs = range(B_HEADS)
    q_t, k_t, v, dec, beta, z = zip(*heads)

    def sample(j, o_acc):
        n = i * nb + j
        pick = samp == n
        k_col = [jnp.sum(jnp.where(pick, k_t[h], 0.0), axis=1, keepdims=True) for h in hs]
        q_col = [jnp.sum(jnp.where(pick, q_t[h], 0.0), axis=1, keepdims=True) for h in hs]
        pick_r = row_id == n
        row = lambda t: jnp.sum(jnp.where(pick_r, t, 0.0), axis=0, keepdims=True)
        st = [st_ref[j, h] * row(dec[h]) for h in hs]
        mem = [jnp.sum(k_col[h] * st[h], axis=0, keepdims=True) for h in hs]
        st = [st[h] + k_col[h] * ((row(v[h]) - mem[h]) * row(beta[h])) for h in hs]
        o = [jnp.sum(q_col[h] * st[h], axis=0, keepdims=True) for h in hs]
        o = [o[h] * lax.rsqrt(jnp.mean(o[h] * o[h], axis=1, keepdims=True) + RMS_EPS) * og_ref[...] for h in hs]
        for h in hs:
            so_ref[j, h] = st[h]
        return tuple(jnp.where(out_row == j, o[h] * _silu(row(z[h])), o_acc[h]) for h in hs)

    outs = lax.fori_loop(0, nb, sample, tuple(jnp.zeros((nb, LANES), F32) for _ in hs), unroll=2)
    for h in hs:
        o_ref[:, h * LANES:(h + 1) * LANES] = outs[h]


def _delta_dec(ub_s, conv_state, zb_s, ab_s, conv_w, a_log, dt_bias, o_norm_g, state, nb):
    N = ub_s.shape[0]
    full2 = lambda shape: pl.BlockSpec(shape, lambda i, *_: (0, 0))
    grid_spec = pltpu.PrefetchScalarGridSpec(
        num_scalar_prefetch=2,
        grid=(N // nb,),
        in_specs=[full2((N, CONV_DIM)),
                  pl.BlockSpec((CONV_WIDTH - 1, N, CONV_DIM), lambda i, *_: (0, 0, 0)),
                  full2((N, B_WIDTH)), full2((N, LANES)), full2((CONV_WIDTH, CONV_DIM)), full2((1, LANES)),
                  pl.BlockSpec((nb, B_HEADS, B_HEAD_DIM, B_HEAD_DIM), lambda i, *_: (i, 0, 0, 0))],
        out_specs=[pl.BlockSpec((nb, B_WIDTH), lambda i, *_: (i, 0)),
                   pl.BlockSpec((nb, B_HEADS, B_HEAD_DIM, B_HEAD_DIM), lambda i, *_: (i, 0, 0, 0))],
    )
    return pl.pallas_call(
        functools.partial(_delta_dec_kernel, nb=nb),
        grid_spec=grid_spec,
        out_shape=[jax.ShapeDtypeStruct((N, B_WIDTH), F32), jax.ShapeDtypeStruct(state.shape, F32)],
        compiler_params=_params(("parallel",)),
        name="delta_dec",
    )(a_log, dt_bias, ub_s, jnp.swapaxes(conv_state, 0, 1), zb_s, ab_s, conv_w, o_norm_g.reshape(1, LANES), state)


def _layer_norm(r, g, b):
    mu = jnp.mean(r, axis=1, keepdims=True)
    d = r - mu
    var = jnp.mean(d * d, axis=1, keepdims=True)
    return d * lax.rsqrt(var + LN_EPS) * g + b


def _mix_ln_kernel(*refs, steps):
    outs = refs[-3:]

    @pl.when(pl.program_id(0) < steps)
    def _():
        _mix_ln_rows(*refs[:8], *outs)

    @pl.when(pl.program_id(0) >= steps)
    def _():
        for o in outs:
            o[...] = jnp.zeros(o.shape, F32)


def _mix_ln_rows(oa_ref, ob_ref, x_ref, wo_ref, g_ref, b_ref, wr_ref, br_ref, h_ref, route_ref, cnt_ref):
    if wo_ref.dtype == BF16:
        dot = lambda a, w: jnp.dot(a.astype(BF16), w, preferred_element_type=F32)
    else:
        dot = _fdot
    y = dot(oa_ref[...], wo_ref[0:A_WIDTH, :]) + dot(ob_ref[...], wo_ref[A_WIDTH:A_WIDTH + B_WIDTH, :])
    hcur = _layer_norm(DN_ALPHA * x_ref[...] + y, g_ref[...], b_ref[...])
    h_ref[...] = hcur
    logits = (_bdot if wo_ref.dtype == BF16 else _dot3)(hcur, wr_ref[...]) + br_ref[...]
    lane = lax.broadcasted_iota(jnp.int32, logits.shape, 1)
    lane_f = lane.astype(F32)
    ninf = -jnp.inf
    big = 1e9
    gl = jnp.where(lane < N_GROUPS, logits, ninf)
    gmax = jnp.max(gl, axis=1, keepdims=True)
    g_idx = jnp.min(jnp.where(gl == gmax, lane_f, big), axis=1, keepdims=True)
    p_group = 1.0 / jnp.sum(jnp.exp(gl - gmax), axis=1, keepdims=True)
    grp_of_lane = ((lane - N_GROUPS) >> 3).astype(F32)
    sel = (lane >= N_GROUPS) & (lane < N_GROUPS + N_EXPERTS) & (grp_of_lane == g_idx)
    el = jnp.where(sel, logits, ninf)
    v1 = jnp.max(el, axis=1, keepdims=True)
    i1 = jnp.min(jnp.where(el == v1, lane_f, big), axis=1, keepdims=True)
    el2 = jnp.where(lane_f == i1, ninf, el)
    v2 = jnp.max(el2, axis=1, keepdims=True)
    i2 = jnp.min(jnp.where(el2 == v2, lane_f, big), axis=1, keepdims=True)
    t = jnp.exp(v2 - v1)
    gate1 = p_group / (1.0 + t)
    gate2 = p_group * t / (1.0 + t)
    e1, e2 = i1 - N_GROUPS, i2 - N_GROUPS
    route_ref[...] = jnp.where(lane == 0, gate1, jnp.where(lane == 1, gate2, jnp.where(
        lane == 2, e1, jnp.where(lane == 3, e2, 0.0))))
    chosen = ((lane_f == e1) | (lane_f == e2)).astype(F32)
    tm = chosen.shape[0]
    cnt_ref[...] = jnp.sum(chosen.reshape(tm // ROUTE_TILE, ROUTE_TILE, LANES), axis=1)[:, None, :]


def _mix_ln(oa, ob, x2d, wo_b, g, b, wr, br, tm, total, row0, prev=()):
    T, D = x2d.shape
    off = row0 // tm
    steps = T // tm
    tail = 1 if row0 + T < total else 0
    assert total - (row0 + T) <= tm
    row = lambda i: (jnp.minimum(i, steps - 1), 0)
    out_row = lambda i: (off + i, 0)
    fix = lambda i: (0, 0)
    sub = tm // ROUTE_TILE
    return pl.pallas_call(
        functools.partial(_mix_ln_kernel, steps=steps),
        grid=(steps + tail,),
        in_specs=[pl.BlockSpec((tm, A_WIDTH), row), pl.BlockSpec((tm, B_WIDTH), row), pl.BlockSpec((tm, D), row),
                  pl.BlockSpec((A_WIDTH + B_WIDTH, D), fix), pl.BlockSpec((1, D), fix), pl.BlockSpec((1, D), fix),
                  pl.BlockSpec((D, LANES), fix), pl.BlockSpec((1, LANES), fix)]
                 + [pl.BlockSpec(memory_space=pl.ANY)] * len(prev),
        out_specs=[pl.BlockSpec((tm, D), out_row), pl.BlockSpec((tm, LANES), out_row),
                   pl.BlockSpec((sub, 1, LANES), lambda i: (off + i, 0, 0))],
        out_shape=[jax.ShapeDtypeStruct((total, D), F32), jax.ShapeDtypeStruct((total, LANES), F32),
                   jax.ShapeDtypeStruct((total // ROUTE_TILE, 1, LANES), F32)],
        input_output_aliases={8 + j: j for j in range(len(prev))},
        compiler_params=_params(("parallel",)),
        name="mix_ln",
    )(oa, ob, x2d, wo_b, g, b, wr, br, *prev)


def _slot_layout(counts):
    tiles = counts.shape[0]
    n_assign = tiles * ROUTE_TILE * TOP_K
    n_blocks = -(-(n_assign + N_EXPERTS * (MOE_BLK - 1)) // MOE_BLK)
    per_tile = counts.reshape(tiles, LANES)
    earlier = (jnp.arange(tiles)[:, None] > jnp.arange(tiles)[None, :]).astype(F32)
    before = jnp.dot(earlier, per_tile, precision=HIGHEST)
    total = jnp.sum(per_tile, axis=0)[:N_EXPERTS]
    padded = jnp.ceil(total / MOE_BLK) * MOE_BLK
    upto = (jnp.arange(N_EXPERTS)[:, None] <= jnp.arange(N_EXPERTS)[None, :]).astype(F32)
    pad_end = jnp.dot(padded, upto, precision=HIGHEST)
    pad_start = pad_end - padded
    base = (before + jnp.pad(pad_start, (0, LANES - N_EXPERTS))[None, :]).reshape(tiles, 1, LANES)
    blk_start = (jnp.arange(n_blocks) * MOE_BLK).astype(F32)
    in_e = ((pad_start[None, :] <= blk_start[:, None]) & (blk_start[:, None] < pad_end[None, :])).astype(F32)
    used = blk_start < pad_end[-1]
    n_used = jnp.sum(used.astype(jnp.int32))
    last_e = jnp.max(jnp.where(padded > 0, jnp.arange(N_EXPERTS), 0)).astype(F32)
    blk_e = jnp.where(used, jnp.dot(in_e, jnp.arange(N_EXPERTS, dtype=F32), precision=HIGHEST), last_e)
    blk_n = jnp.clip(jnp.dot(in_e, pad_start + total, precision=HIGHEST) - blk_start, 0, MOE_BLK)
    blk_x = jnp.minimum(jnp.arange(n_blocks), jnp.maximum(n_used - 1, 0))
    i32 = lambda t: t.astype(jnp.int32)
    return base, i32(blk_e), i32(blk_n), i32(blk_x), i32(pad_end), n_blocks


def _slot_kernel(route_ref, base_ref, dest_ref):
    lane = lax.broadcasted_iota(jnp.int32, (ROUTE_TILE, LANES), 1)
    lane_f = lane.astype(F32)
    ri = lax.broadcasted_iota(jnp.int32, (ROUTE_TILE, ROUTE_TILE), 0)
    ci = lax.broadcasted_iota(jnp.int32, (ROUTE_TILE, ROUTE_TILE), 1)
    before = (ri > ci).astype(BF16)
    for t in range(base_ref.shape[0]):
        rows = slice(t * ROUTE_TILE, (t + 1) * ROUTE_TILE)
        route = route_ref[rows, :]
        e1 = jnp.sum(jnp.where(lane == 2, route, 0.0), axis=1, keepdims=True)
        e2 = jnp.sum(jnp.where(lane == 3, route, 0.0), axis=1, keepdims=True)
        oh1, oh2 = lane_f == e1, lane_f == e2
        earlier = _bdot(before, (oh1 | oh2).astype(F32))
        slot = base_ref[t] + earlier
        d1 = jnp.sum(jnp.where(oh1, slot, 0.0), axis=1, keepdims=True)
        d2 = jnp.sum(jnp.where(oh2, slot, 0.0), axis=1, keepdims=True)
        dest_ref[rows, :] = jnp.where(lane == 0, d1, jnp.where(lane == 1, d2, 0.0)).astype(jnp.int32)


def _slots(route, base):
    tiles = route.shape[0] // ROUTE_TILE
    sub = max(d for d in range(1, 9) if tiles % d == 0)
    return pl.pallas_call(
        _slot_kernel,
        grid=(tiles // sub,),
        in_specs=[pl.BlockSpec((sub * ROUTE_TILE, LANES), lambda i: (i, 0)),
                  pl.BlockSpec((sub, 1, LANES), lambda i: (i, 0, 0))],
        out_specs=pl.BlockSpec((sub * ROUTE_TILE, LANES), lambda i: (i, 0)),
        out_shape=jax.ShapeDtypeStruct(route.shape, jnp.int32),
        compiler_params=_params(("parallel",)),
        name="slots",
    )(route, base)


def _pack_bf16_pairs(x):
    half = x.shape[1] // 2
    bits = pltpu.bitcast(x.astype(BF16).astype(F32), jnp.uint32)
    return (bits[:, 0:half] >> 16) | bits[:, half:2 * half]


def _unpack_bf16_pairs(w):
    lo = pltpu.bitcast(w << 16, F32).astype(BF16)
    hi = pltpu.bitcast(w & jnp.uint32(0xFFFF0000), F32).astype(BF16)
    return jnp.concatenate([lo, hi], axis=1)


def _dispatch_kernel(dest_ref, pad_end_ref, h_ref, xs_hbm, hbuf, zbuf, sem, zsem):
    i = pl.program_id(0)
    slot = i % 2
    tile = h_ref.shape[0]
    a0 = i * (tile * TOP_K)

    @pl.when(i == 0)
    def _():
        zbuf[...] = jnp.zeros(zbuf.shape, zbuf.dtype)
        fills =[pltpu.make_async_copy(zbuf, xs_hbm.at[pl.ds(pl.multiple_of(pad_end_ref[e] - MOE_BLK, MOE_BLK), MOE_BLK)],
                                       zsem) for e in range(N_EXPERTS)]
        has_rows = [pad_end_ref[e] > (pad_end_ref[e - 1] if e else 0) for e in range(N_EXPERTS)]
        for e in range(N_EXPERTS):
            @pl.when(has_rows[e])
            def _(e=e):
                fills[e].start()
        for e in range(N_EXPERTS):
            @pl.when(has_rows[e])
            def _(e=e):
                fills[e].wait()

        def tail(b):
            return pltpu.make_async_copy(zbuf, xs_hbm.at[pl.ds(pl.multiple_of(b * MOE_BLK, MOE_BLK), MOE_BLK)], zsem)

        def tail_start(b, c):
            tail(b).start()
            return c

        def tail_wait(b, c):
            tail(b).wait()
            return c

        first_unused = pad_end_ref[N_EXPERTS - 1] // MOE_BLK
        lax.fori_loop(first_unused, xs_hbm.shape[0] // MOE_BLK, tail_start, 0)
        lax.fori_loop(first_unused, xs_hbm.shape[0] // MOE_BLK, tail_wait, 0)

    hbuf[slot] = _pack_bf16_pairs(h_ref[...])

    def start(r, c):
        for k in range(TOP_K):
            pltpu.make_async_copy(hbuf.at[slot, pl.ds(r, 1)], xs_hbm.at[pl.ds(dest_ref[a0 + r * TOP_K + k], 1)],
                                  sem.at[slot]).start()
        return c

    lax.fori_loop(0, tile, start, 0, unroll=8)

    def drain(s):
        for k in range(TOP_K):
            pltpu.make_async_copy(hbuf.at[s], xs_hbm.at[pl.ds(0, tile)], sem.at[s]).wait()

    @pl.when(i > 0)
    def _():
        drain(1 - slot)

    @pl.when(i == pl.num_programs(0) - 1)
    def _():
        drain(slot)


def _dispatch(h_all, dest, pad_end, slots):
    T, D = h_all.shape
    tile = max(t for t in range(ROUTE_TILE, 8 * ROUTE_TILE + 1, 8) if T % t == 0)
    grid_spec = pltpu.PrefetchScalarGridSpec(
        num_scalar_prefetch=2,
        grid=(T // tile,),
        in_specs=[pl.BlockSpec((tile, D), lambda i, *_: (i, 0))],
        out_specs=pl.BlockSpec(memory_space=pl.ANY),
        scratch_shapes=[pltpu.VMEM((2, tile, D // 2), jnp.uint32), pltpu.VMEM((MOE_BLK, D // 2), jnp.uint32),
                        pltpu.SemaphoreType.DMA((2,)), pltpu.SemaphoreType.DMA],
    )
    return pl.pallas_call(
        _dispatch_kernel,
        grid_spec=grid_spec,
        out_shape=jax.ShapeDtypeStruct((slots, D // 2), jnp.uint32),
        compiler_params=_params(("arbitrary",)),
        name="dispatch",
    )(dest, pad_end, h_all)


def _moe_kernel(blk_e_ref, blk_n_ref, blk_x_ref, x_ref, wg_ref, wu_ref, wd_ref, y_ref, wg_b, wu_b, wd_b):
    del blk_x_ref
    i = pl.program_id(0)
    n_valid = blk_n_ref[i]

    @pl.when((i == 0) | (blk_e_ref[i] != blk_e_ref[jnp.maximum(i - 1, 0)]))
    def _():
        wg_b[...] = wg_ref[0].astype(BF16)
        wu_b[...] = wu_ref[0].astype(BF16)
        wd_b[...] = wd_ref[0].astype(BF16)

    @pl.when(n_valid > 0)
    def _():
        x = _unpack_bf16_pairs(x_ref[...])
        a = jnp.dot(x, wg_b[...], preferred_element_type=F32)
        u = jnp.dot(x, wu_b[...], preferred_element_type=F32)
        y_ref[...] = jnp.dot((_silu(a) * u).astype(BF16), wd_b[...], preferred_element_type=F32)

    @pl.when(n_valid == 0)
    def _():
        y_ref[...] = jnp.zeros(y_ref.shape, F32)


def _moe(xs, blk_e, blk_n, blk_x, w_gate, w_up, w_down):
    slots = xs.shape[0]
    D, De = w_gate.shape[-2:]
    wspec = lambda shape: pl.BlockSpec((1,) + shape, lambda i, be, *_: (be[i], 0, 0))
    grid_spec = pltpu.PrefetchScalarGridSpec(
        num_scalar_prefetch=3,
        grid=(slots // MOE_BLK,),
        in_specs=[pl.BlockSpec((MOE_BLK, xs.shape[1]), lambda i, be, bn, bx: (bx[i], 0)),
                  wspec((D, De)), wspec((D, De)), wspec((De, D))],
        out_specs=pl.BlockSpec((MOE_BLK, D), lambda i, *_: (i, 0)),
        scratch_shapes=[pltpu.VMEM((D, De), BF16), pltpu.VMEM((D, De), BF16), pltpu.VMEM((De, D), BF16)],
    )
    return pl.pallas_call(
        _moe_kernel,
        grid_spec=grid_spec,
        out_shape=jax.ShapeDtypeStruct((slots, D), F32),
        compiler_params=_params(("arbitrary",)),
        name="moe",
    )(blk_e, blk_n, blk_x, xs, w_gate, w_up, w_down)


def _final_ln_kernel(dest_ref, h_ref, route_ref, g_ref, b_ref, ys_hbm, o_ref, ybuf, sem, *, tile0):
    i = pl.program_id(0)
    slot = i % 2
    rows = h_ref.shape[0]

    def gather(tile, s):
        a0 = (tile0 + tile) * (rows * TOP_K)

        def start(r, c):
            for k in range(TOP_K):
                pltpu.make_async_copy(ys_hbm.at[pl.ds(dest_ref[a0 + r * TOP_K + k], 1)],
                                      ybuf.at[s, k, pl.ds(r, 1)], sem.at[s]).start()
            return c

        lax.fori_loop(0, rows, start, 0, unroll=8)

    @pl.when(i == 0)
    def _():
        gather(0, 0)

    @pl.when(i + 1 < pl.num_programs(0))
    def _():
        gather(i + 1, 1 - slot)

    for k in range(TOP_K):
        pltpu.make_async_copy(ys_hbm.at[pl.ds(0, rows)], ybuf.at[slot, k], sem.at[slot]).wait()
    route = route_ref[...]
    lane = lax.broadcasted_iota(jnp.int32, route.shape, 1)
    gate1 = jnp.sum(jnp.where(lane == 0, route, 0.0), axis=1, keepdims=True)
    gate2 = jnp.sum(jnp.where(lane == 1, route, 0.0), axis=1, keepdims=True)
    f = ybuf[slot, 0] * gate1 + ybuf[slot, 1] * gate2
    o_ref[...] = _layer_norm(DN_ALPHA * h_ref[...] + f, g_ref[...], b_ref[...])


def _final_ln(h_all, route, dest, ys, g, b, row0, rows):
    D = h_all.shape[1]
    tile = max(t for t in (ROUTE_TILE, 2 * ROUTE_TILE, 4 * ROUTE_TILE) if rows % t == 0 and row0 % t == 0)
    tile0 = row0 // tile
    row = lambda i, *_: (tile0 + i, 0)
    fix = lambda i, *_: (0, 0)
    grid_spec = pltpu.PrefetchScalarGridSpec(
        num_scalar_prefetch=1,
        grid=(rows // tile,),
        in_specs=[pl.BlockSpec((tile, D), row), pl.BlockSpec((tile, LANES), row),
                  pl.BlockSpec((1, D), fix), pl.BlockSpec((1, D), fix), pl.BlockSpec(memory_space=pl.ANY)],
        out_specs=pl.BlockSpec((tile, D), lambda i, *_: (i, 0)),
        scratch_shapes=[pltpu.VMEM((2, TOP_K, tile, D), F32), pltpu.SemaphoreType.DMA((2,))],
    )
    return pl.pallas_call(
        functools.partial(_final_ln_kernel, tile0=tile0),
        grid_spec=grid_spec,
        out_shape=jax.ShapeDtypeStruct((rows, D), F32),
        compiler_params=_params(("arbitrary",)),
        name="final_ln",
    )(dest, h_all, route, g, b, ys)


def kernel(x_prompt, x_sample, cache_a_k, cache_a_v, state_b_ssm, state_b_conv, w_in, rel_bias, conv_w, a_log, dt_bias, o_norm_g, w_out, ln1_g, ln1_b, w_group, b_group, w_router, b_router, w_gate, w_up, w_down, ln2_g, ln2_b):
    B, S, D = x_prompt.shape
    N, T = x_sample.shape[0], x_sample.shape[1]
    depth = w_in.shape[0]
    assert depth == 1 and T == 1 and S % ATT_TILE == 0 and N % ROUTE_TILE == 0 and cache_a_k.shape[2] % LANES == 0
    l = 0
    win_p = min(BRANCHES[-1][0], S)

    w_pad32 = jnp.pad(w_in[l], ((0, 0), (0, IN_COLS_PAD - IN_COLS)))
    w_pad = w_pad32.astype(BF16)
    wo_b = w_out[l].astype(BF16)
    wr = jnp.pad(jnp.concatenate([w_group[l], w_router[l]], axis=1), ((0, 0), (0, LANES - N_GROUPS - N_EXPERTS)))
    br = jnp.pad(jnp.concatenate([b_group[l], b_router[l].reshape(-1)]), (0, LANES - N_GROUPS - N_EXPERTS))[None, :]
    g1, b1 = ln1_g[l][None, :], ln1_b[l][None, :]
    g2, b2 = ln2_g[l][None, :], ln2_b[l][None, :]

    xp = x_prompt.reshape(B * S, D)
    qkv_p, ub_p, zb_p, ab_p, k_win, v_win = _proj(xp, w_pad, 512, seq=S, win=win_p)
    oa_p = _attn(qkv_p, _band_bias(rel_bias), B, S)
    ob_p, st_p = _delta(ub_p, zb_p, ab_p, conv_w[l], a_log[l], dt_bias[l], o_norm_g[l], B, S)
    rows_all = B * S + N
    routed_p = _mix_ln(oa_p.reshape(B * S, A_WIDTH), ob_p.reshape(B * S, B_WIDTH), xp, wo_b, g1, b1, wr, br,
                       512, rows_all, 0)

    xs = x_sample.reshape(N, D)
    qkv_s, ub_s, zb_s, ab_s = _proj(xs, w_pad32, N)
    nb = 2
    qkv_t = jnp.transpose(qkv_s.reshape(N, 3, A_HEADS, A_HEAD_DIM), (1, 2, 3, 0))
    oa_s = _attn_dec(qkv_t, jnp.transpose(cache_a_k[l], (0, 2, 3, 1)), jnp.transpose(cache_a_v[l], (0, 2, 3, 1)),
                     _cache_bias(rel_bias, cache_a_k.shape[2]), nb)
    oa_s = jnp.transpose(oa_s, (0, 3, 1, 2)).reshape(N, A_WIDTH)
    ob_s, st_s = _delta_dec(ub_s, state_b_conv[l], zb_s, ab_s, conv_w[l], a_log[l], dt_bias[l], o_norm_g[l],
                            state_b_ssm[l], 32)
    h_all, route, cnt = _mix_ln(oa_s, ob_s, xs, w_out[l], g1, b1, wr, br, N, rows_all, B * S, prev=routed_p)

    base, blk_e, blk_n, blk_x, pad_end, n_blocks = _slot_layout(cnt)
    dest = _slots(route, base)[:, 0:TOP_K].reshape(-1)
    xs = _dispatch(h_all, dest, pad_end, n_blocks * MOE_BLK)
    ys = _moe(xs, blk_e, blk_n, blk_x, w_gate[l], w_up[l], w_down[l])
    y_p = _final_ln(h_all, route, dest, ys, g2, b2, 0, B * S)
    y_s = _final_ln(h_all, route, dest, ys, g2, b2, B * S, N)

    to_rows = lambda t: jnp.transpose(t, (0, 3, 1, 2))[None]
    conv_p = ub_p.reshape(B, S, CONV_DIM)[:, S - (CONV_WIDTH - 1):]
    conv_s = jnp.concatenate([state_b_conv[l], ub_s[:, None, :]], axis=1)[:, T:]
    new_kv = lambda t: jnp.transpose(t, (2, 0, 1))[None, :, None]
    return (y_p.reshape(B, S, D), y_s.reshape(N, T, D), to_rows(k_win), to_rows(v_win),
            new_kv(qkv_t[1]), new_kv(qkv_t[2]), st_p[None], st_s[None], conv_p[None], conv_s[None])
```

```python
import functools
import math

import jax
import jax.numpy as jnp
from jax import lax
from jax.experimental import pallas as pl
from jax.experimental.pallas import tpu as pltpu

F32 = jnp.float32
BF16 = jnp.bfloat16
HIGHEST = lax.Precision.HIGHEST

LANES = 128
A_HEADS = 8
A_HEAD_DIM = 64
A_WIDTH = A_HEADS * A_HEAD_DIM
BRANCHES = ((128, 1), (512, 4), (2048, 16))
BAND = 128
ATT_TILE = BAND * 16
REL_BUCKETS = 32
REL_MAX_DIST = 2048
B_HEADS = 4
B_HEAD_DIM = 128
B_WIDTH = B_HEADS * B_HEAD_DIM
CONV_WIDTH = 4
CONV_DIM = 3 * B_WIDTH
CHUNK = 64
COL_UB = 3 * A_WIDTH
COL_ZB = COL_UB + CONV_DIM
COL_AB = COL_ZB + B_WIDTH
IN_COLS = COL_AB + 2 * B_HEADS
IN_COLS_PAD = COL_AB + LANES
N_GROUPS = 4
EXPERTS_PER_GROUP = 8
N_EXPERTS = N_GROUPS * EXPERTS_PER_GROUP
TOP_K = 2
DN_ALPHA = 2.0 ** 0.25
LN_EPS = 1e-5
RMS_EPS = 1e-6
MASKED = -1e30
MOE_BLK = 512
ROUTE_TILE = 128
DELTA_TILE = 1024
ATT_GROUP = 16
VMEM_LIMIT = 56 * 1024 * 1024


def _bdot(a, b):
    return jnp.dot(a.astype(BF16), b.astype(BF16), preferred_element_type=F32)


def _bdot_nt(a, b):
    return lax.dot_general(a.astype(BF16), b.astype(BF16), (((1,), (1,)), ((), ())), preferred_element_type=F32)


def _bdot_tn(a, b):
    return lax.dot_general(a.astype(BF16), b.astype(BF16), (((0,), (0,)), ((), ())), preferred_element_type=F32)


def _fdot(a, b):
    return jnp.dot(a, b, precision=HIGHEST, preferred_element_type=F32)


def _sigmoid(x):
    return 1.0 / (1.0 + jnp.exp(-x))


def _silu(x):
    return x * _sigmoid(x)


def _softplus(x):
    return jnp.maximum(x, 0.0) + jnp.log(1.0 + jnp.exp(-jnp.abs(x)))


def _params(sem):
    return pltpu.CompilerParams(dimension_semantics=sem, vmem_limit_bytes=VMEM_LIMIT)


def _proj_kernel(x_ref, w_ref, qkv_ref, ub_ref, zb_ref, ab_ref, *win_refs, tiles_per_seq=0, first_win_tile=0):
    if w_ref.dtype == BF16:
        xb = x_ref[...].astype(BF16)
        dot = lambda w: jnp.dot(xb, w, preferred_element_type=F32)
    else:
        dot = lambda w: _fdot(x_ref[...], w)
    qkv = dot(w_ref[:, 0:COL_UB])
    qkv_ref[...] = qkv
    ub_ref[...] = dot(w_ref[:, COL_UB:COL_ZB])
    zb_ref[...] = dot(w_ref[:, COL_ZB:COL_AB])
    ab_ref[...] = dot(w_ref[:, COL_AB:IN_COLS_PAD])
    if win_refs:
        @pl.when(pl.program_id(0) % tiles_per_seq >= first_win_tile)
        def _():
            tm = qkv.shape[0]
            for j, ref in enumerate(win_refs):
                cols = qkv[:, (1 + j) * A_WIDTH:(2 + j) * A_WIDTH]
                ref[0] = jnp.transpose(cols).reshape(A_HEADS, A_HEAD_DIM, tm)


def _proj(x2d, w_pad, tm, seq=None, win=None):
    T, D = x2d.shape
    row = lambda i: (i, 0)
    out_specs = [pl.BlockSpec((tm, COL_UB), row), pl.BlockSpec((tm, CONV_DIM), row),
                 pl.BlockSpec((tm, B_WIDTH), row), pl.BlockSpec((tm, LANES), row)]
    out_shape = [jax.ShapeDtypeStruct((T, COL_UB), F32), jax.ShapeDtypeStruct((T, CONV_DIM), F32),
                 jax.ShapeDtypeStruct((T, B_WIDTH), F32), jax.ShapeDtypeStruct((T, LANES), F32)]
    body = _proj_kernel
    if win:
        per_seq, first = seq // tm, (seq - win) // tm
        wspec = pl.BlockSpec((1, A_HEADS, A_HEAD_DIM, tm),
                             lambda i: (i // per_seq, 0, 0, jnp.maximum(i % per_seq - first, 0)))
        out_specs += [wspec, wspec]
        out_shape += [jax.ShapeDtypeStruct((T // seq, A_HEADS, A_HEAD_DIM, win), F32)] * 2
        body = functools.partial(_proj_kernel, tiles_per_seq=per_seq, first_win_tile=first)
    return pl.pallas_call(
        body,
        grid=(T // tm,),
        in_specs=[pl.BlockSpec((tm, D), row), pl.BlockSpec((D, IN_COLS_PAD), lambda i: (0, 0))],
        out_specs=out_specs,
        out_shape=out_shape,
        compiler_params=_params(("arbitrary",)),
        name="proj",
    )(x2d, w_pad)


def _rel_bucket(dist):
    max_exact = REL_BUCKETS // 2
    n = jnp.maximum(dist, 0)
    ratio = jnp.maximum(n, 1).astype(F32) / max_exact
    large = max_exact + (jnp.log(ratio) / math.log(REL_MAX_DIST / max_exact)
                         * (REL_BUCKETS - max_exact)).astype(jnp.int32)
    return jnp.where(n < max_exact, n, jnp.minimum(large, REL_BUCKETS - 1))


def _bias_of(rel_bias, dist):
    onehot = (_rel_bucket(dist)[None, :] == jnp.arange(REL_BUCKETS)[:, None]).astype(F32)
    return jnp.dot(rel_bias.astype(F32).T, onehot, precision=HIGHEST)


def _band_bias(rel_bias):
    period = 3 * BAND
    tabs = []
    for _, dil in BRANCHES:
        g = jnp.concatenate([_bias_of(rel_bias, (BAND - jnp.arange(BAND + 1)) * dil),
                             jnp.full((A_HEADS, period - BAND - 1), MASKED, F32)], axis=1)
        skew = jnp.tile(g, (1, BAND))[:, :BAND * (period - 1)].reshape(A_HEADS, BAND, period - 1)
        tabs.append(skew[:, :, :2 * BAND])
    return jnp.stack(tabs)


def _cache_bias(rel_bias, P):
    dist = P - jnp.arange(P + LANES)
    tabs = []
    for window, dil in BRANCHES:
        ok = (dist >= 0) & (dist <= window) & (dist % dil == 0)
        tabs.append(jnp.where(ok[None, :], _bias_of(rel_bias, dist), MASKED)[:, None, :])
    return jnp.stack(tabs)


def _attn_kernel(q_ref, k_ref, v_ref, bias_ref, o_ref, acc_ref, m_ref, l_ref):
    t = pl.program_id(2)
    tile0 = t * ATT_TILE
    lane = lax.broadcasted_iota(jnp.int32, (BAND, LANES), 1)
    head0 = lane < A_HEAD_DIM

    def rows(start, dil):
        return pl.ds(pl.multiple_of(start, BAND), BAND) if dil == 1 else pl.ds(start, BAND, stride=dil)

    for br, (_, dil) in enumerate(BRANCHES):
        span = BAND * dil

        def blocks(it, carry, br=br, dil=dil, span=span):
            gs = range(ATT_GROUP)
            idx = [it * ATT_GROUP + g for g in gs]
            start = [(i % dil) + (i // dil) * span for i in idx]
            cur = [tile0 + s for s in start]
            first = [c < span for c in cur]
            prev = [jnp.where(first[g], cur[g], cur[g] - span) for g in gs]
            q = [q_ref[0, rows(start[g], dil), :] * (A_HEAD_DIM ** -0.5) for g in gs]
            kk = [jnp.concatenate([k_ref[0, rows(prev[g], dil), :], k_ref[0, rows(cur[g], dil), :]],
                                  axis=0).astype(BF16) for g in gs]
            vv = [jnp.concatenate([v_ref[0, rows(prev[g], dil), :], v_ref[0, rows(cur[g], dil), :]],
                                  axis=0).astype(BF16) for g in gs]
            pen = [jnp.where(first[g], MASKED, 0.0) for g in gs]
            gh = [(g, hh) for g in gs for hh in range(2)]
            qh = [jnp.where(head0 if hh == 0 else ~head0, q[g], 0.0).astype(BF16) for g, hh in gh]
            s = [lax.dot_general(qh[j], kk[g], (((1,), (1,)), ((), ())), preferred_element_type=F32)
                 for j, (g, hh) in enumerate(gh)]
            s = [s[j] + bias_ref[br, hh] for j, (g, hh) in enumerate(gh)]
            s = [jnp.concatenate([s[j][:, 0:BAND] + pen[g], s[j][:, BAND:2 * BAND]], axis=1)
                 for j, (g, hh) in enumerate(gh)]
            m = [jnp.max(t, axis=1, keepdims=True) for t in s]
            p = [jnp.exp(s[j] - m[j]) for j in range(len(gh))]
            l = [jnp.sum(t, axis=1, keepdims=True) for t in p]
            pv = [jnp.dot(p[j].astype(BF16), vv[g], preferred_element_type=F32) for j, (g, hh) in enumerate(gh)]
            for g in gs:
                acc_ref[br, rows(start[g], dil), :] = jnp.where(head0, pv[2 * g], pv[2 * g + 1])
                m_ref[br, rows(start[g], dil), :] = jnp.where(head0, m[2 * g], m[2 * g + 1])
                l_ref[br, rows(start[g], dil), :] = jnp.where(head0, l[2 * g], l[2 * g + 1])
            return carry

        lax.fori_loop(0, ATT_TILE // BAND // ATT_GROUP, blocks, 0)

    def merge(c, carry):
        r = pl.ds(pl.multiple_of(c * 256, 256), 256)
        m0, m1, m2 = m_ref[0, r, :], m_ref[1, r, :], m_ref[2, r, :]
        mx = jnp.maximum(jnp.maximum(m0, m1), m2)
        w0, w1, w2 = jnp.exp(m0 - mx), jnp.exp(m1 - mx), jnp.exp(m2 - mx)
        num = w0 * acc_ref[0, r, :] + w1 * acc_ref[1, r, :] + w2 * acc_ref[2, r, :]
        den = w0 * l_ref[0, r, :] + w1 * l_ref[1, r, :] + w2 * l_ref[2, r, :]
        o_ref[0, r, :] = num / den
        return carry

    lax.fori_loop(0, ATT_TILE // 256, merge, 0)


def _attn(qkv, bias, B, S):
    n_pairs = A_HEADS // 2
    qkv3 = qkv.reshape(B, S, 3 * A_WIDTH)
    return pl.pallas_call(
        _attn_kernel,
        grid=(B, n_pairs, S // ATT_TILE),
        in_specs=[pl.BlockSpec((1, ATT_TILE, LANES), lambda b, hp, t: (b, t, hp)),
                  pl.BlockSpec((1, S, LANES), lambda b, hp, t: (b, 0, n_pairs + hp)),
                  pl.BlockSpec((1, S, LANES), lambda b, hp, t: (b, 0, 2 * n_pairs + hp)),
                  pl.BlockSpec((3, 2, BAND, 2 * BAND), lambda b, hp, t: (0, hp, 0, 0))],
        out_specs=pl.BlockSpec((1, ATT_TILE, LANES), lambda b, hp, t: (b, t, hp)),
        out_shape=jax.ShapeDtypeStruct((B, S, A_WIDTH), F32),
        scratch_shapes=[pltpu.VMEM((3, ATT_TILE, LANES), F32)] * 3,
        compiler_params=_params(("parallel", "parallel", "arbitrary")),
        name="attn",
    )(qkv3, qkv3, qkv3, bias)


def _attn_dec_kernel(qkv_ref, kt_ref, vt_ref, bias_ref, o_ref, *, nb):
    i = pl.program_id(0)
    N = qkv_ref.shape[-1]
    P = kt_ref.shape[-1]
    lane_n = lax.broadcasted_iota(jnp.int32, (A_HEAD_DIM, N), 1)
    lane_o = lax.broadcasted_iota(jnp.int32, (A_HEAD_DIM, nb), 1)
    lane_t = lax.broadcasted_iota(jnp.int32, (1, LANES), 1)

    def head(h, carry):
        slab = jnp.zeros((A_HEAD_DIM, nb), F32)
        for j in range(nb):
            pick = lane_n == i * nb + j
            col = lambda t: jnp.sum(jnp.where(pick, t, 0.0), axis=1, keepdims=True)
            q = col(qkv_ref[0, h]) * (A_HEAD_DIM ** -0.5)
            k_new, v_new = col(qkv_ref[1, h]), col(qkv_ref[2, h])
            s_new = jnp.sum(q * k_new, axis=0, keepdims=True)
            s = jnp.concatenate([jnp.sum(kt_ref[j, h] * q, axis=0, keepdims=True),
                                 jnp.where(lane_t == 0, s_new, 0.0)], axis=1)
            ps, ms, ls = [], [], []
            for br in range(3):
                sb = s + bias_ref[br, h]
                m = jnp.max(sb, axis=1, keepdims=True)
                p = jnp.exp(sb - m)
                ps.append(p)
                ms.append(m)
                ls.append(jnp.sum(p, axis=1, keepdims=True))
            mx = jnp.maximum(jnp.maximum(ms[0], ms[1]), ms[2])
            w = jnp.zeros((1, P + LANES), F32)
            den = jnp.zeros((1, 1), F32)
            for p, m, l in zip(ps, ms, ls):
                e = jnp.exp(m - mx)
                w = w + e * p
                den = den + e * l
            o = jnp.sum(vt_ref[j, h] * w[:, 0:P], axis=1, keepdims=True) + v_new * w[:, P:P + 1]
            slab = jnp.where(lane_o == j, o / den, slab)
        o_ref[0, h] = slab
        return carry

    lax.fori_loop(0, A_HEADS, head, 0, unroll=2)


def _attn_dec(qkv_t, cache_kt, cache_vt, bias, nb):
    N, P = cache_kt.shape[0], cache_kt.shape[-1]
    cache_spec = pl.BlockSpec((nb, A_HEADS, A_HEAD_DIM, P), lambda i: (i, 0, 0, 0))
    return pl.pallas_call(
        functools.partial(_attn_dec_kernel, nb=nb),
        grid=(N // nb,),
        in_specs=[pl.BlockSpec((3, A_HEADS, A_HEAD_DIM, N), lambda i: (0, 0, 0, 0)), cache_spec, cache_spec,
                  pl.BlockSpec((3, A_HEADS, 1, P + LANES), lambda i: (0, 0, 0, 0))],
        out_specs=pl.BlockSpec((1, A_HEADS, A_HEAD_DIM, nb), lambda i: (i, 0, 0, 0)),
        out_shape=jax.ShapeDtypeStruct((N // nb, A_HEADS, A_HEAD_DIM, nb), F32),
        compiler_params=_params(("parallel",)),
        name="attn_dec",
    )(qkv_t, cache_kt, cache_vt, bias)


def _split3(x):
    hi = x.astype(BF16)
    r = x - hi.astype(F32)
    mid = r.astype(BF16)
    return hi, mid, (r - mid.astype(F32)).astype(BF16)


def _tril_dot(tril_b, g):
    return sum(jnp.dot(tril_b, piece, preferred_element_type=F32) for piece in _split3(g))


def _dot3(a, b):
    ah, bh = a.astype(BF16), b.astype(BF16)
    al, bl = (a - ah.astype(F32)).astype(BF16), (b - bh.astype(F32)).astype(BF16)
    d = lambda x, y: jnp.dot(x, y, preferred_element_type=F32)
    return d(ah, bh) + (d(ah, bl) + d(al, bh))


def _delta_kernel(alog_ref, dtb_ref, ub_ref, z_ref, ab_ref, cw_ref, og_ref, o_ref, st_ref,
                  pad_ref, u_s, wq_s, kt_s, qk_s, gl_s, a_s, rhs_s):
    t = pl.program_id(1)
    TS = ub_ref.shape[1]
    nch = TS // CHUNK
    hdr = 8

    slabs = range(CONV_DIM // LANES)
    seq_rows = lambda first, n: pl.ds(2 * first, n, stride=2)

    @pl.when(t == 0)
    def _():
        for s in slabs:
            pad_ref[s, seq_rows(0, hdr), :] = jnp.zeros((hdr, LANES), F32)
        st_ref[...] = jnp.zeros(st_ref.shape, F32)

    @pl.when(t > 0)
    def _():
        for s in slabs:
            pad_ref[s, seq_rows(0, hdr), :] = pad_ref[s, seq_rows(TS, hdr), :]

    for s in slabs:
        pad_ref[s, seq_rows(hdr, TS), :] = ub_ref[0, :, s * LANES:(s + 1) * LANES]

    ri = lax.broadcasted_iota(jnp.int32, (CHUNK, CHUNK), 0)
    ci = lax.broadcasted_iota(jnp.int32, (CHUNK, CHUNK), 1)
    incl = ri >= ci
    strict = ri > ci
    tril_b = incl.astype(BF16)
    eye = (ri == ci).astype(F32)
    lane = lax.broadcasted_iota(jnp.int32, (CHUNK, LANES), 1)

    def local(c, carry):
        base = c * CHUNK if isinstance(c, int) else pl.multiple_of(c * CHUNK, CHUNK)
        ab = ab_ref[0, pl.ds(base, CHUNK), :]
        heads = []
        for h in range(B_HEADS):
            def conv(col):
                first = c * CHUNK + (hdr - (CONV_WIDTH - 1))
                acc = pad_ref[col // LANES, seq_rows(first, CHUNK), :] * cw_ref[0:1, col:col + LANES]
                for i in range(1, CONV_WIDTH):
                    acc = acc + pad_ref[col // LANES, seq_rows(first + i, CHUNK), :] * cw_ref[i:i + 1, col:col + LANES]
                return _silu(acc)

            cq, ck, v = conv(h * LANES), conv(B_WIDTH + h * LANES), conv(2 * B_WIDTH + h * LANES)
            q = cq * lax.rsqrt(jnp.sum(cq * cq, axis=1, keepdims=True) + 1e-6) * (B_HEAD_DIM ** -0.5)
            k = ck * lax.rsqrt(jnp.sum(ck * ck, axis=1, keepdims=True) + 1e-6)
            a_raw = jnp.sum(jnp.where(lane == h, ab, 0.0), axis=1, keepdims=True)
            b_raw = jnp.sum(jnp.where(lane == h + B_HEADS, ab, 0.0), axis=1, keepdims=True)
            neg_a = -jnp.exp(jnp.full((1, LANES), alog_ref[h], F32))
            g = neg_a * _softplus(a_raw + dtb_ref[h])
            beta = _sigmoid(b_raw)
            heads.append((q, k, v, g, beta))
        hs = range(B_HEADS)
        q, k, v, g, beta = zip(*heads)
        gc = [_tril_dot(tril_b, g[h]) for h in hs]
        dmat = [_tril_dot(tril_b, jnp.where(strict, g[h][:, 0:CHUNK], 0.0)) for h in hs]
        kq = [_bdot_nt(jnp.concatenate([k[h], q[h]], axis=0), k[h]) for h in hs]
        decay = [jnp.where(incl, jnp.exp(dmat[h]), 0.0) for h in hs]
        e_gc = [jnp.exp(gc[h]) for h in hs]
        for h in hs:
            gc_last = gc[h][CHUNK - 1:CHUNK, :]
            a_s[h, c] = jnp.where(strict, beta[h] * kq[h][0:CHUNK] * decay[h], 0.0)
            rhs_s[h, c] = jnp.concatenate([v[h] * beta[h], k[h] * (beta[h] * e_gc[h])], axis=1)
            wq_s[h, c, CHUNK:2 * CHUNK] = q[h] * e_gc[h]
            kt_s[h, c] = k[h] * jnp.exp(gc_last - gc[h])
            qk_s[h, c] = kq[h][CHUNK:2 * CHUNK] * decay[h]
            gl_s[h, c] = jnp.broadcast_to(jnp.exp(gc_last), (8, LANES))
        return carry

    def solve(c, carry):
        hs = range(B_HEADS)
        a = [a_s[h, c] for h in hs]
        x = [eye - a[h] for h in hs]
        p = [_bdot(a[h], a[h]) for h in hs]
        for _ in range(int(math.log2(CHUNK)) - 2):
            r = [_bdot(jnp.concatenate([x[h], p[h]], axis=0), p[h]) for h in hs]
            x = [x[h] + r[h][0:CHUNK] for h in hs]
            p = [r[h][CHUNK:2 * CHUNK] for h in hs]
        x = [x[h] + _bdot(x[h], p[h]) for h in hs]
        sol = [_bdot(x[h], rhs_s[h, c]) for h in hs]
        for h in hs:
            u_s[h, c] = sol[h][:, 0:LANES]
            wq_s[h, c, 0:CHUNK] = sol[h][:, LANES:2 * LANES]
        return carry

    def scan(c, carry):
        base = c * CHUNK if isinstance(c, int) else pl.multiple_of(c * CHUNK, CHUNK)
        hs = range(B_HEADS)
        state = [st_ref[0, h] for h in hs]
        r = [_bdot(wq_s[h, c], state[h]) for h in hs]
        v_new = [u_s[h, c] - r[h][0:CHUNK] for h in hs]
        upd = [_bdot_tn(kt_s[h, c], v_new[h]) for h in hs]
        out = [r[h][CHUNK:2 * CHUNK] + _bdot(qk_s[h, c], v_new[h]) for h in hs]
        for h in hs:
            st_ref[0, h] = state[h] * gl_s[h, c][0:1, :] + upd[h]
            o = out[h] * lax.rsqrt(jnp.mean(out[h] * out[h], axis=1, keepdims=True) + RMS_EPS) * og_ref[...]
            cols = slice(h * LANES, (h + 1) * LANES)
            o_ref[0, pl.ds(base, CHUNK), cols] = o * _silu(z_ref[0, pl.ds(base, CHUNK), cols])
        return carry

    def stages(c, carry):
        scan(c - 1, carry)
        solve(c, carry)
        return local(c + 1, carry)

    assert nch >= 3
    local(0, 0)
    solve(0, 0)
    local(1, 0)
    lax.fori_loop(1, nch - 1, stages, 0)
    scan(nch - 2, 0)
    solve(nch - 1, 0)
    scan(nch - 1, 0)


def _delta(ub, zb, ab, conv_w, a_log, dt_bias, o_norm_g, B, S):
    ts = min(S, DELTA_TILE)
    nch = ts // CHUNK
    seq = lambda width: pl.BlockSpec((1, ts, width), lambda b, t, *_: (b, t, 0))
    fix = lambda shape: pl.BlockSpec(shape, lambda b, t, *_: (0, 0))
    per_chunk = lambda rows, width: pltpu.VMEM((B_HEADS, nch, rows, width), F32)
    grid_spec = pltpu.PrefetchScalarGridSpec(
        num_scalar_prefetch=2,
        grid=(B, S // ts),
        in_specs=[seq(CONV_DIM), seq(B_WIDTH), seq(LANES), fix((CONV_WIDTH, CONV_DIM)), fix((1, LANES))],
        out_specs=[seq(B_WIDTH),
                   pl.BlockSpec((1, B_HEADS, B_HEAD_DIM, B_HEAD_DIM), lambda b, t, *_: (b, 0, 0, 0))],
        scratch_shapes=[pltpu.VMEM((CONV_DIM // LANES, 2 * (ts + 8), LANES), F32),
                        per_chunk(CHUNK, LANES), per_chunk(2 * CHUNK, LANES),
                        per_chunk(CHUNK, LANES), per_chunk(CHUNK, CHUNK), per_chunk(8, LANES),
                        per_chunk(CHUNK, CHUNK), per_chunk(CHUNK, 2 * LANES)],
    )
    return pl.pallas_call(
        _delta_kernel,
        grid_spec=grid_spec,
        out_shape=[jax.ShapeDtypeStruct((B, S, B_WIDTH), F32),
                   jax.ShapeDtypeStruct((B, B_HEADS, B_HEAD_DIM, B_HEAD_DIM), F32)],
        compiler_params=_params(("parallel", "arbitrary")),
        name="delta",
    )(a_log, dt_bias, ub.reshape(B, S, CONV_DIM), zb.reshape(B, S, B_WIDTH), ab.reshape(B, S, LANES),
      conv_w, o_norm_g.reshape(1, LANES))


def _delta_dec_kernel(alog_ref, dtb_ref, ub_ref, cs_ref, zb_ref, ab_ref, w_ref, og_ref, st_ref,
                      o_ref, so_ref, *, nb):
    i = pl.program_id(0)
    N = ub_ref.shape[0]
    acc = ub_ref[...] * w_ref[CONV_WIDTH - 1:CONV_WIDTH, :]
    for t in range(CONV_WIDTH - 1):
        acc = acc + cs_ref[t] * w_ref[t:t + 1, :]
    c = _silu(acc)
    ab = ab_ref[...]
    lane = lax.broadcasted_iota(jnp.int32, (N, LANES), 1)
    samp = lax.broadcasted_iota(jnp.int32, (B_HEAD_DIM, N), 1)
    row_id = lax.broadcasted_iota(jnp.int32, (N, LANES), 0)
    out_row = lax.broadcasted_iota(jnp.int32, (nb, LANES), 0)
    heads = []
    for h in range(B_HEADS):
        cq = c[:, h * LANES:(h + 1) * LANES]
        ck = c[:, B_WIDTH + h * LANES:B_WIDTH + (h + 1) * LANES]
        v = c[:, 2 * B_WIDTH + h * LANES:2 * B_WIDTH + (h + 1) * LANES]
        q = cq * lax.rsqrt(jnp.sum(cq * cq, axis=1, keepdims=True) + 1e-6) * (B_HEAD_DIM ** -0.5)
        k = ck * lax.rsqrt(jnp.sum(ck * ck, axis=1, keepdims=True) + 1e-6)
        a_raw = jnp.sum(jnp.where(lane == h, ab, 0.0), axis=1, keepdims=True)
        b_raw = jnp.sum(jnp.where(lane == h + B_HEADS, ab, 0.0), axis=1, keepdims=True)
        neg_a = -jnp.exp(jnp.full((1, 1), alog_ref[h], F32))
        dec = jnp.exp(neg_a * _softplus(a_raw + dtb_ref[h]))
        beta = _sigmoid(b_raw)
        heads.append((q.T, k.T, v, jnp.broadcast_to(dec, (N, LANES)), jnp.broadcast_to(beta, (N, LANES)),
                      zb_ref[:, h * LANES:(h + 1) * LANES]))

    hs = range(B_HEADS)
    q_t, k_t, v, dec, beta, z = zip(*heads)

    def sample(j, o_acc):
        n = i * nb + j
        pick = samp == n
        k_col = [jnp.sum(jnp.where(pick, k_t[h], 0.0), axis=1, keepdims=True) for h in hs]
        q_col = [jnp.sum(jnp.where(pick, q_t[h], 0.0), axis=1, keepdims=True) for h in hs]
        pick_r = row_id == n
        row = lambda t: jnp.sum(jnp.where(pick_r, t, 0.0), axis=0, keepdims=True)
        st = [st_ref[j, h] * row(dec[h]) for h in hs]
        mem = [jnp.sum(k_col[h] * st[h], axis=0, keepdims=True) for h in hs]
        st = [st[h] + k_col[h] * ((row(v[h]) - mem[h]) * row(beta[h])) for h in hs]
        o = [jnp.sum(q_col[h] * st[h], axis=0, keepdims=True) for h in hs]
        o = [o[h] * lax.rsqrt(jnp.mean(o[h] * o[h], axis=1, keepdims=True) + RMS_EPS) * og_ref[...] for h in hs]
        for h in hs:
            so_ref[j, h] = st[h]
        return tuple(jnp.where(out_row == j, o[h] * _silu(row(z[h])), o_acc[h]) for h in hs)

    outs = lax.fori_loop(0, nb, sample, tuple(jnp.zeros((nb, LANES), F32) for _ in hs), unroll=2)
    for h in hs:
        o_ref[:, h * LANES:(h + 1) * LANES] = outs[h]


def _delta_dec(ub_s, conv_state, zb_s, ab_s, conv_w, a_log, dt_bias, o_norm_g, state, nb):
    N = ub_s.shape[0]
    full2 = lambda shape: pl.BlockSpec(shape, lambda i, *_: (0, 0))
    grid_spec = pltpu.PrefetchScalarGridSpec(
        num_scalar_prefetch=2,
        grid=(N // nb,),
        in_specs=[full2((N, CONV_DIM)),
                  pl.BlockSpec((CONV_WIDTH - 1, N, CONV_DIM), lambda i, *_: (0, 0, 0)),
                  full2((N, B_WIDTH)), full2((N, LANES)), full2((CONV_WIDTH, CONV_DIM)), full2((1, LANES)),
                  pl.BlockSpec((nb, B_HEADS, B_HEAD_DIM, B_HEAD_DIM), lambda i, *_: (i, 0, 0, 0))],
        out_specs=[pl.BlockSpec((nb, B_WIDTH), lambda i, *_: (i, 0)),
                   pl.BlockSpec((nb, B_HEADS, B_HEAD_DIM, B_HEAD_DIM), lambda i, *_: (i, 0, 0, 0))],
    )
    return pl.pallas_call(
        functools.partial(_delta_dec_kernel, nb=nb),
        grid_spec=grid_spec,
        out_shape=[jax.ShapeDtypeStruct((N, B_WIDTH), F32), jax.ShapeDtypeStruct(state.shape, F32)],
        compiler_params=_params(("parallel",)),
        name="delta_dec",
    )(a_log, dt_bias, ub_s, jnp.swapaxes(conv_state, 0, 1), zb_s, ab_s, conv_w, o_norm_g.reshape(1, LANES), state)


def _layer_norm(r, g, b):
    mu = jnp.mean(r, axis=1, keepdims=True)
    d = r - mu
    var = jnp.mean(d * d, axis=1, keepdims=True)
    return d * lax.rsqrt(var + LN_EPS) * g + b


def _mix_ln_kernel(*refs, steps):
    outs = refs[-3:]

    @pl.when(pl.program_id(0) < steps)
    def _():
        _mix_ln_rows(*refs[:8], *outs)

    @pl.when(pl.program_id(0) >= steps)
    def _():
        for o in outs:
            o[...] = jnp.zeros(o.shape, F32)


def _mix_ln_rows(oa_ref, ob_ref, x_ref, wo_ref, g_ref, b_ref, wr_ref, br_ref, h_ref, route_ref, cnt_ref):
    if wo_ref.dtype == BF16:
        dot = lambda a, w: jnp.dot(a.astype(BF16), w, preferred_element_type=F32)
    else:
        dot = _fdot
    y = dot(oa_ref[...], wo_ref[0:A_WIDTH, :]) + dot(ob_ref[...], wo_ref[A_WIDTH:A_WIDTH + B_WIDTH, :])
    hcur = _layer_norm(DN_ALPHA * x_ref[...] + y, g_ref[...], b_ref[...])
    h_ref[...] = hcur
    logits = (_bdot if wo_ref.dtype == BF16 else _dot3)(hcur, wr_ref[...]) + br_ref[...]
    lane = lax.broadcasted_iota(jnp.int32, logits.shape, 1)
    lane_f = lane.astype(F32)
    ninf = -jnp.inf
    big = 1e9
    gl = jnp.where(lane < N_GROUPS, logits, ninf)
    gmax = jnp.max(gl, axis=1, keepdims=True)
    g_idx = jnp.min(jnp.where(gl == gmax, lane_f, big), axis=1, keepdims=True)
    p_group = 1.0 / jnp.sum(jnp.exp(gl - gmax), axis=1, keepdims=True)
    grp_of_lane = ((lane - N_GROUPS) >> 3).astype(F32)
    sel = (lane >= N_GROUPS) & (lane < N_GROUPS + N_EXPERTS) & (grp_of_lane == g_idx)
    el = jnp.where(sel, logits, ninf)
    v1 = jnp.max(el, axis=1, keepdims=True)
    i1 = jnp.min(jnp.where(el == v1, lane_f, big), axis=1, keepdims=True)
    el2 = jnp.where(lane_f == i1, ninf, el)
    v2 = jnp.max(el2, axis=1, keepdims=True)
    i2 = jnp.min(jnp.where(el2 == v2, lane_f, big), axis=1, keepdims=True)
    t = jnp.exp(v2 - v1)
    gate1 = p_group / (1.0 + t)
    gate2 = p_group * t / (1.0 + t)
    e1, e2 = i1 - N_GROUPS, i2 - N_GROUPS
    route_ref[...] = jnp.where(lane == 0, gate1, jnp.where(lane == 1, gate2, jnp.where(
        lane == 2, e1, jnp.where(lane == 3, e2, 0.0))))
    chosen = ((lane_f == e1) | (lane_f == e2)).astype(F32)
    tm = chosen.shape[0]
    cnt_ref[...] = jnp.sum(chosen.reshape(tm // ROUTE_TILE, ROUTE_TILE, LANES), axis=1)[:, None, :]


def _mix_ln(oa, ob, x2d, wo_b, g, b, wr, br, tm, total, row0, prev=()):
    T, D = x2d.shape
    off = row0 // tm
    steps = T // tm
    tail = 1 if row0 + T < total else 0
    assert total - (row0 + T) <= tm
    row = lambda i: (jnp.minimum(i, steps - 1), 0)
    out_row = lambda i: (off + i, 0)
    fix = lambda i: (0, 0)
    sub = tm // ROUTE_TILE
    return pl.pallas_call(
        functools.partial(_mix_ln_kernel, steps=steps),
        grid=(steps + tail,),
        in_specs=[pl.BlockSpec((tm, A_WIDTH), row), pl.BlockSpec((tm, B_WIDTH), row), pl.BlockSpec((tm, D), row),
                  pl.BlockSpec((A_WIDTH + B_WIDTH, D), fix), pl.BlockSpec((1, D), fix), pl.BlockSpec((1, D), fix),
                  pl.BlockSpec((D, LANES), fix), pl.BlockSpec((1, LANES), fix)]
                 + [pl.BlockSpec(memory_space=pl.ANY)] * len(prev),
        out_specs=[pl.BlockSpec((tm, D), out_row), pl.BlockSpec((tm, LANES), out_row),
                   pl.BlockSpec((sub, 1, LANES), lambda i: (off + i, 0, 0))],
        out_shape=[jax.ShapeDtypeStruct((total, D), F32), jax.ShapeDtypeStruct((total, LANES), F32),
                   jax.ShapeDtypeStruct((total // ROUTE_TILE, 1, LANES), F32)],
        input_output_aliases={8 + j: j for j in range(len(prev))},
        compiler_params=_params(("parallel",)),
        name="mix_ln",
    )(oa, ob, x2d, wo_b, g, b, wr, br, *prev)


def _slot_layout(counts):
    tiles = counts.shape[0]
    n_assign = tiles * ROUTE_TILE * TOP_K
    n_blocks = -(-(n_assign + N_EXPERTS * (MOE_BLK - 1)) // MOE_BLK)
    per_tile = counts.reshape(tiles, LANES)
    earlier = (jnp.arange(tiles)[:, None] > jnp.arange(tiles)[None, :]).astype(F32)
    before = jnp.dot(earlier, per_tile, precision=HIGHEST)
    total = jnp.sum(per_tile, axis=0)[:N_EXPERTS]
    padded = jnp.ceil(total / MOE_BLK) * MOE_BLK
    upto = (jnp.arange(N_EXPERTS)[:, None] <= jnp.arange(N_EXPERTS)[None, :]).astype(F32)
    pad_end = jnp.dot(padded, upto, precision=HIGHEST)
    pad_start = pad_end - padded
    base = (before + jnp.pad(pad_start, (0, LANES - N_EXPERTS))[None, :]).reshape(tiles, 1, LANES)
    blk_start = (jnp.arange(n_blocks) * MOE_BLK).astype(F32)
    in_e = ((pad_start[None, :] <= blk_start[:, None]) & (blk_start[:, None] < pad_end[None, :])).astype(F32)
    used = blk_start < pad_end[-1]
    n_used = jnp.sum(used.astype(jnp.int32))
    last_e = jnp.max(jnp.where(padded > 0, jnp.arange(N_EXPERTS), 0)).astype(F32)
    blk_e = jnp.where(used, jnp.dot(in_e, jnp.arange(N_EXPERTS, dtype=F32), precision=HIGHEST), last_e)
    blk_n = jnp.clip(jnp.dot(in_e, pad_start + total, precision=HIGHEST) - blk_start, 0, MOE_BLK)
    blk_x = jnp.minimum(jnp.arange(n_blocks), jnp.maximum(n_used - 1, 0))
    i32 = lambda t: t.astype(jnp.int32)
    return base, i32(blk_e), i32(blk_n), i32(blk_x), i32(pad_end), n_blocks


def _slot_kernel(route_ref, base_ref, dest_ref):
    lane = lax.broadcasted_iota(jnp.int32, (ROUTE_TILE, LANES), 1)
    lane_f = lane.astype(F32)
    ri = lax.broadcasted_iota(jnp.int32, (ROUTE_TILE, ROUTE_TILE), 0)
    ci = lax.broadcasted_iota(jnp.int32, (ROUTE_TILE, ROUTE_TILE), 1)
    before = (ri > ci).astype(BF16)
    for t in range(base_ref.shape[0]):
        rows = slice(t * ROUTE_TILE, (t + 1) * ROUTE_TILE)
        route = route_ref[rows, :]
        e1 = jnp.sum(jnp.where(lane == 2, route, 0.0), axis=1, keepdims=True)
        e2 = jnp.sum(jnp.where(lane == 3, route, 0.0), axis=1, keepdims=True)
        oh1, oh2 = lane_f == e1, lane_f == e2
        earlier = _bdot(before, (oh1 | oh2).astype(F32))
        slot = base_ref[t] + earlier
        d1 = jnp.sum(jnp.where(oh1, slot, 0.0), axis=1, keepdims=True)
        d2 = jnp.sum(jnp.where(oh2, slot, 0.0), axis=1, keepdims=True)
        dest_ref[rows, :] = jnp.where(lane == 0, d1, jnp.where(lane == 1, d2, 0.0)).astype(jnp.int32)


def _slots(route, base):
    tiles = route.shape[0] // ROUTE_TILE
    sub = max(d for d in range(1, 9) if tiles % d == 0)
    return pl.pallas_call(
        _slot_kernel,
        grid=(tiles // sub,),
        in_specs=[pl.BlockSpec((sub * ROUTE_TILE, LANES), lambda i: (i, 0)),
                  pl.BlockSpec((sub, 1, LANES), lambda i: (i, 0, 0))],
        out_specs=pl.BlockSpec((sub * ROUTE_TILE, LANES), lambda i: (i, 0)),
        out_shape=jax.ShapeDtypeStruct(route.shape, jnp.int32),
        compiler_params=_params(("parallel",)),
        name="slots",
    )(route, base)


def _pack_bf16_pairs(x):
    half = x.shape[1] // 2
    bits = pltpu.bitcast(x.astype(BF16).astype(F32), jnp.uint32)
    return (bits[:, 0:half] >> 16) | bits[:, half:2 * half]


def _unpack_bf16_pairs(w):
    lo = pltpu.bitcast(w << 16, F32).astype(BF16)
    hi = pltpu.bitcast(w & jnp.uint32(0xFFFF0000), F32).astype(BF16)
    return jnp.concatenate([lo, hi], axis=1)


def _dispatch_kernel(dest_ref, pad_end_ref, h_ref, xs_hbm, hbuf, zbuf, sem, zsem):
    i = pl.program_id(0)
    slot = i % 2
    tile = h_ref.shape[0]
    a0 = i * (tile * TOP_K)

    @pl.when(i == 0)
    def _():
        zbuf[...] = jnp.zeros(zbuf.shape, zbuf.dtype)
        fills =[pltpu.make_async_copy(zbuf, xs_hbm.at[pl.ds(pl.multiple_of(pad_end_ref[e] - MOE_BLK, MOE_BLK), MOE_BLK)],
                                       zsem) for e in range(N_EXPERTS)]
        has_rows = [pad_end_ref[e] > (pad_end_ref[e - 1] if e else 0) for e in range(N_EXPERTS)]
        for e in range(N_EXPERTS):
            @pl.when(has_rows[e])
            def _(e=e):
                fills[e].start()
        for e in range(N_EXPERTS):
            @pl.when(has_rows[e])
            def _(e=e):
                fills[e].wait()

        def tail(b):
            return pltpu.make_async_copy(zbuf, xs_hbm.at[pl.ds(pl.multiple_of(b * MOE_BLK, MOE_BLK), MOE_BLK)], zsem)

        def tail_start(b, c):
            tail(b).start()
            return c

        def tail_wait(b, c):
            tail(b).wait()
            return c

        first_unused = pad_end_ref[N_EXPERTS - 1] // MOE_BLK
        lax.fori_loop(first_unused, xs_hbm.shape[0] // MOE_BLK, tail_start, 0)
        lax.fori_loop(first_unused, xs_hbm.shape[0] // MOE_BLK, tail_wait, 0)

    hbuf[slot] = _pack_bf16_pairs(h_ref[...])

    def start(r, c):
        for k in range(TOP_K):
            pltpu.make_async_copy(hbuf.at[slot, pl.ds(r, 1)], xs_hbm.at[pl.ds(dest_ref[a0 + r * TOP_K + k], 1)],
                                  sem.at[slot]).start(priority=k)
        return c

    lax.fori_loop(0, tile, start, 0, unroll=8)

    def drain(s):
        for k in range(TOP_K):
            pltpu.make_async_copy(hbuf.at[s], xs_hbm.at[pl.ds(0, tile)], sem.at[s]).wait()

    @pl.when(i > 0)
    def _():
        drain(1 - slot)

    @pl.when(i == pl.num_programs(0) - 1)
    def _():
        drain(slot)


def _dispatch(h_all, dest, pad_end, slots):
    T, D = h_all.shape
    tile = max(t for t in range(ROUTE_TILE, 8 * ROUTE_TILE + 1, 8) if T % t == 0)
    grid_spec = pltpu.PrefetchScalarGridSpec(
        num_scalar_prefetch=2,
        grid=(T // tile,),
        in_specs=[pl.BlockSpec((tile, D), lambda i, *_: (i, 0))],
        out_specs=pl.BlockSpec(memory_space=pl.ANY),
        scratch_shapes=[pltpu.VMEM((2, tile, D // 2), jnp.uint32), pltpu.VMEM((MOE_BLK, D // 2), jnp.uint32),
                        pltpu.SemaphoreType.DMA((2,)), pltpu.SemaphoreType.DMA],
    )
    return pl.pallas_call(
        _dispatch_kernel,
        grid_spec=grid_spec,
        out_shape=jax.ShapeDtypeStruct((slots, D // 2), jnp.uint32),
        compiler_params=_params(("arbitrary",)),
        name="dispatch",
    )(dest, pad_end, h_all)


def _moe_kernel(blk_e_ref, blk_n_ref, blk_x_ref, x_ref, wg_ref, wu_ref, wd_ref, y_ref, wg_b, wu_b, wd_b):
    del blk_x_ref
    i = pl.program_id(0)
    n_valid = blk_n_ref[i]

    @pl.when((i == 0) | (blk_e_ref[i] != blk_e_ref[jnp.maximum(i - 1, 0)]))
    def _():
        wg_b[...] = wg_ref[0].astype(BF16)
        wu_b[...] = wu_ref[0].astype(BF16)
        wd_b[...] = wd_ref[0].astype(BF16)

    @pl.when(n_valid > 0)
    def _():
        x = _unpack_bf16_pairs(x_ref[...])
        a = jnp.dot(x, wg_b[...], preferred_element_type=F32)
        u = jnp.dot(x, wu_b[...], preferred_element_type=F32)
        y_ref[...] = jnp.dot((_silu(a) * u).astype(BF16), wd_b[...], preferred_element_type=F32)

    @pl.when(n_valid == 0)
    def _():
        y_ref[...] = jnp.zeros(y_ref.shape, F32)


def _moe(xs, blk_e, blk_n, blk_x, w_gate, w_up, w_down):
    slots = xs.shape[0]
    D, De = w_gate.shape[-2:]
    wspec = lambda shape: pl.BlockSpec((1,) + shape, lambda i, be, *_: (be[i], 0, 0))
    grid_spec = pltpu.PrefetchScalarGridSpec(
        num_scalar_prefetch=3,
        grid=(slots // MOE_BLK,),
        in_specs=[pl.BlockSpec((MOE_BLK, xs.shape[1]), lambda i, be, bn, bx: (bx[i], 0)),
                  wspec((D, De)), wspec((D, De)), wspec((De, D))],
        out_specs=pl.BlockSpec((MOE_BLK, D), lambda i, *_: (i, 0)),
        scratch_shapes=[pltpu.VMEM((D, De), BF16), pltpu.VMEM((D, De), BF16), pltpu.VMEM((De, D), BF16)],
    )
    return pl.pallas_call(
        _moe_kernel,
        grid_spec=grid_spec,
        out_shape=jax.ShapeDtypeStruct((slots, D), F32),
        compiler_params=_params(("arbitrary",)),
        name="moe",
    )(blk_e, blk_n, blk_x, xs, w_gate, w_up, w_down)


def _final_ln_kernel(dest_ref, h_ref, route_ref, g_ref, b_ref, ys_hbm, o_ref, ybuf, sem, *, tile0):
    i = pl.program_id(0)
    slot = i % 2
    rows = h_ref.shape[0]

    def gather(tile, s):
        a0 = (tile0 + tile) * (rows * TOP_K)

        def start(r, c):
            for k in range(TOP_K):
                pltpu.make_async_copy(ys_hbm.at[pl.ds(dest_ref[a0 + r * TOP_K + k], 1)],
                                      ybuf.at[s, k, pl.ds(r, 1)], sem.at[s]).start(priority=k)
            return c

        lax.fori_loop(0, rows, start, 0, unroll=8)

    @pl.when(i == 0)
    def _():
        gather(0, 0)

    @pl.when(i + 1 < pl.num_programs(0))
    def _():
        gather(i + 1, 1 - slot)

    for k in range(TOP_K):
        pltpu.make_async_copy(ys_hbm.at[pl.ds(0, rows)], ybuf.at[slot, k], sem.at[slot]).wait()
    route = route_ref[...]
    lane = lax.broadcasted_iota(jnp.int32, route.shape, 1)
    gate1 = jnp.sum(jnp.where(lane == 0, route, 0.0), axis=1, keepdims=True)
    gate2 = jnp.sum(jnp.where(lane == 1, route, 0.0), axis=1, keepdims=True)
    f = ybuf[slot, 0] * gate1 + ybuf[slot, 1] * gate2
    o_ref[...] = _layer_norm(DN_ALPHA * h_ref[...] + f, g_ref[...], b_ref[...])


def _final_ln(h_all, route, dest, ys, g, b, row0, rows):
    D = h_all.shape[1]
    tile = max(t for t in (ROUTE_TILE, 2 * ROUTE_TILE, 4 * ROUTE_TILE) if rows % t == 0 and row0 % t == 0)
    tile0 = row0 // tile
    row = lambda i, *_: (tile0 + i, 0)
    fix = lambda i, *_: (0, 0)
    grid_spec = pltpu.PrefetchScalarGridSpec(
        num_scalar_prefetch=1,
        grid=(rows // tile,),
        in_specs=[pl.BlockSpec((tile, D), row), pl.BlockSpec((tile, LANES), row),
                  pl.BlockSpec((1, D), fix), pl.BlockSpec((1, D), fix), pl.BlockSpec(memory_space=pl.ANY)],
        out_specs=pl.BlockSpec((tile, D), lambda i, *_: (i, 0)),
        scratch_shapes=[pltpu.VMEM((2, TOP_K, tile, D), F32), pltpu.SemaphoreType.DMA((2,))],
    )
    return pl.pallas_call(
        functools.partial(_final_ln_kernel, tile0=tile0),
        grid_spec=grid_spec,
        out_shape=jax.ShapeDtypeStruct((rows, D), F32),
        compiler_params=_params(("arbitrary",)),
        name="final_ln",
    )(dest, h_all, route, g, b, ys)


def kernel(x_prompt, x_sample, cache_a_k, cache_a_v, state_b_ssm, state_b_conv, w_in, rel_bias, conv_w, a_log, dt_bias, o_norm_g, w_out, ln1_g, ln1_b, w_group, b_group, w_router, b_router, w_gate, w_up, w_down, ln2_g, ln2_b):
    B, S, D = x_prompt.shape
    N, T = x_sample.shape[0], x_sample.shape[1]
    depth = w_in.shape[0]
    assert depth == 1 and T == 1 and S % ATT_TILE == 0 and N % ROUTE_TILE == 0 and cache_a_k.shape[2] % LANES == 0
    l = 0
    win_p = min(BRANCHES[-1][0], S)

    w_pad32 = jnp.pad(w_in[l], ((0, 0), (0, IN_COLS_PAD - IN_COLS)))
    w_pad = w_pad32.astype(BF16)
    wo_b = w_out[l].astype(BF16)
    wr = jnp.pad(jnp.concatenate([w_group[l], w_router[l]], axis=1), ((0, 0), (0, LANES - N_GROUPS - N_EXPERTS)))
    br = jnp.pad(jnp.concatenate([b_group[l], b_router[l].reshape(-1)]), (0, LANES - N_GROUPS - N_EXPERTS))[None, :]
    g1, b1 = ln1_g[l][None, :], ln1_b[l][None, :]
    g2, b2 = ln2_g[l][None, :], ln2_b[l][None, :]

    xp = x_prompt.reshape(B * S, D)
    qkv_p, ub_p, zb_p, ab_p, k_win, v_win = _proj(xp, w_pad, 512, seq=S, win=win_p)
    oa_p = _attn(qkv_p, _band_bias(rel_bias), B, S)
    ob_p, st_p = _delta(ub_p, zb_p, ab_p, conv_w[l], a_log[l], dt_bias[l], o_norm_g[l], B, S)
    rows_all = B * S + N
    routed_p = _mix_ln(oa_p.reshape(B * S, A_WIDTH), ob_p.reshape(B * S, B_WIDTH), xp, wo_b, g1, b1, wr, br,
                       512, rows_all, 0)

    xs = x_sample.reshape(N, D)
    qkv_s, ub_s, zb_s, ab_s = _proj(xs, w_pad32, N)
    nb = 2
    qkv_t = jnp.transpose(qkv_s.reshape(N, 3, A_HEADS, A_HEAD_DIM), (1, 2, 3, 0))
    oa_s = _attn_dec(qkv_t, jnp.transpose(cache_a_k[l], (0, 2, 3, 1)), jnp.transpose(cache_a_v[l], (0, 2, 3, 1)),
                     _cache_bias(rel_bias, cache_a_k.shape[2]), nb)
    oa_s = jnp.transpose(oa_s, (0, 3, 1, 2)).reshape(N, A_WIDTH)
    ob_s, st_s = _delta_dec(ub_s, state_b_conv[l], zb_s, ab_s, conv_w[l], a_log[l], dt_bias[l], o_norm_g[l],
                            state_b_ssm[l], 32)
    h_all, route, cnt = _mix_ln(oa_s, ob_s, xs, w_out[l], g1, b1, wr, br, N, rows_all, B * S, prev=routed_p)

    base, blk_e, blk_n, blk_x, pad_end, n_blocks = _slot_layout(cnt)
    dest = _slots(route, base)[:, 0:TOP_K].reshape(-1)
    xs = _dispatch(h_all, dest, pad_end, n_blocks * MOE_BLK)
    ys = _moe(xs, blk_e, blk_n, blk_x, w_gate[l], w_up[l], w_down[l])
    y_p = _final_ln(h_all, route, dest, ys, g2, b2, 0, B * S)
    y_s = _final_ln(h_all, route, dest, ys, g2, b2, B * S, N)

    to_rows = lambda t: jnp.transpose(t, (0, 3, 1, 2))[None]
    conv_p = ub_p.reshape(B, S, CONV_DIM)[:, S - (CONV_WIDTH - 1):]
    conv_s = jnp.concatenate([state_b_conv[l], ub_s[:, None, :]], axis=1)[:, T:]
    new_kv = lambda t: jnp.transpose(t, (2, 0, 1))[None, :, None]
    return (y_p.reshape(B, S, D), y_s.reshape(N, T, D), to_rows(k_win), to_rows(v_win),
            new_kv(qkv_t[1]), new_kv(qkv_t[2]), st_p[None], st_s[None], conv_p[None], conv_s[None])
```
